```python
import math, functools
import jax, jax.numpy as jnp
from jax import lax
import numpy as np

D_MODEL = 1024
BATCH = 8
SEQ = 2048
DEPTH = 4
DEC_BATCH = 128
DEC_SEQ = 8
PAST_LEN = 8192
PAGE_SIZE = 128

HEAD_DIM = 64
ATTN_W = D_MODEL // 2
N_Q_HEADS = ATTN_W // HEAD_DIM
N_KV_HEADS = N_Q_HEADS // 4
Q_GROUP = N_Q_HEADS // N_KV_HEADS
KV_W = N_KV_HEADS * HEAD_DIM
WINDOW = 128
BLOCK = 128
ROT_DIM = HEAD_DIM // 4
ROPE_THETA = 500000.0
SSM_W = D_MODEL - ATTN_W
SSM_GROUP_CH = 16
N_SSM_GROUPS = SSM_W // SSM_GROUP_CH
SSM_STATE = 64
DT_MIN = 1e-3
DT_MAX = 1e-1
MIX_W = ATTN_W + SSM_W
IN_W = ATTN_W + 2 * KV_W + ATTN_W + 2 * SSM_W
SPLIT_IDX = (ATTN_W, ATTN_W + KV_W, ATTN_W + 2 * KV_W, 2 * ATTN_W + 2 * KV_W,
             2 * ATTN_W + 2 * KV_W + SSM_W)
EPS = 1e-6
NEG_INF = -1e30

kernel_name = "hymba_swa_sink_s5_adaln_step"


def rms_norm(x, g):
    xf = x.astype(jnp.float32)
    y = xf * lax.rsqrt(jnp.mean(xf * xf, axis=-1, keepdims=True) + EPS)
    return (y * g.astype(jnp.float32)).astype(x.dtype)


def rope_partial(x, pos):
    half = ROT_DIM // 2
    inv = ROPE_THETA ** (-jnp.arange(half, dtype=jnp.float32) / half)
    ang = pos.astype(jnp.float32)[:, None] * inv[None, :]
    cos = jnp.cos(ang)[:, None, :].astype(x.dtype)
    sin = jnp.sin(ang)[:, None, :].astype(x.dtype)
    x1 = x[..., :half]
    x2 = x[..., half:ROT_DIM]
    return jnp.concatenate([x1 * cos - x2 * sin, x2 * cos + x1 * sin, x[..., ROT_DIM:]], axis=-1)


def sink_attention(q, k, v, mask, sinks):
    s = jnp.einsum('...qhgd,...khd->...hgqk', q, k).astype(jnp.float32) * (HEAD_DIM ** -0.5)
    s = jnp.where(mask, s, NEG_INF)
    sink = sinks.astype(jnp.float32).reshape(N_KV_HEADS, Q_GROUP)[:, :, None, None]
    m = jnp.maximum(jnp.max(s, axis=-1, keepdims=True), sink)
    p = jnp.exp(s - m)
    p = p / (jnp.sum(p, axis=-1, keepdims=True) + jnp.exp(sink - m))
    return jnp.einsum('...hgqk,...khd->...qhgd', p.astype(v.dtype), v)


def attend_prompt(q, k, v, sinks):
    Bn, L = q.shape[0], q.shape[1]
    nb = L // BLOCK
    qb = q.reshape(Bn, nb, BLOCK, N_KV_HEADS, Q_GROUP, HEAD_DIM)
    kb = k.reshape(Bn, nb, BLOCK, N_KV_HEADS, HEAD_DIM)
    vb = v.reshape(Bn, nb, BLOCK, N_KV_HEADS, HEAD_DIM)
    kk = jnp.concatenate([jnp.concatenate([jnp.zeros_like(kb[:, :1]), kb[:, :-1]], axis=1), kb], axis=2)
    vv = jnp.concatenate([jnp.concatenate([jnp.zeros_like(vb[:, :1]), vb[:, :-1]], axis=1), vb], axis=2)
    blk = jnp.arange(nb)[:, None]
    qpos = blk * BLOCK + jnp.arange(BLOCK)[None, :]
    kpos = (blk - 1) * BLOCK + jnp.arange(2 * BLOCK)[None, :]
    diff = qpos[:, :, None] - kpos[:, None, :]
    mask = (diff >= 0) & (diff <= WINDOW) & (kpos[:, None, :] >= 0)
    out = sink_attention(qb, kk, vv, mask[:, None, None], sinks)
    return out.reshape(Bn, L, ATTN_W)


def attend_sample(buf_k, buf_v, q, k, v, sinks):
    Bn, T = q.shape[0], q.shape[1]
    w = buf_k.shape[1]
    kk = jnp.concatenate([buf_k.astype(k.dtype), k], axis=1)
    vv = jnp.concatenate([buf_v.astype(v.dtype), v], axis=1)
    qpos = PAST_LEN + jnp.arange(T)
    kpos = PAST_LEN - w + jnp.arange(w + T)
    diff = qpos[:, None] - kpos[None, :]
    mask = (diff >= 0) & (diff <= WINDOW)
    qg = q.reshape(Bn, T, N_KV_HEADS, Q_GROUP, HEAD_DIM)
    out = sink_attention(qg, kk, vv, mask, sinks)
    return out.reshape(Bn, T, ATTN_W)


def complex_scan_combine(e1, e2):
    a1r, a1i, b1r, b1i = e1
    a2r, a2i, b2r, b2i = e2
    ar = a1r * a2r - a1i * a2i
    ai = a1r * a2i + a1i * a2r
    br = a2r * b1r - a2i * b1i + b2r
    bi = a2r * b1i + a2i * b1r + b2i
    return (ar, ai, br, bi)


def s5_branch(u, h0_re, h0_im, a_re, a_im, log_dt, b_re, b_im, c_re, c_im, d, w_glu, b_glu):
    f32 = jnp.float32
    Bn, L = u.shape[0], u.shape[1]
    uf = u.astype(f32).reshape(Bn, L, N_SSM_GROUPS, SSM_GROUP_CH)
    a_re = a_re.astype(f32)
    a_im = a_im.astype(f32)
    dt = jnp.exp(log_dt.astype(f32))[:, None]
    mag = jnp.exp(a_re * dt)
    abar_re = mag * jnp.cos(a_im * dt)
    abar_im = mag * jnp.sin(a_im * dt)
    den = a_re * a_re + a_im * a_im
    nr = abar_re - 1.0
    coef_re = (nr * a_re + abar_im * a_im) / den
    coef_im = (abar_im * a_re - nr * a_im) / den
    br = b_re.astype(f32)
    bi = b_im.astype(f32)
    bb_re = coef_re[..., None] * br - coef_im[..., None] * bi
    bb_im = coef_re[..., None] * bi + coef_im[..., None] * br
    bu_re = jnp.einsum('gpc,blgc->blgp', bb_re, uf)
    bu_im = jnp.einsum('gpc,blgc->blgp', bb_im, uf)
    a_seq_re = jnp.broadcast_to(abar_re, (1, L, N_SSM_GROUPS, SSM_STATE))
    a_seq_im = jnp.broadcast_to(abar_im, (1, L, N_SSM_GROUPS, SSM_STATE))
    acum_re, acum_im, hz_re, hz_im = lax.associative_scan(
        complex_scan_combine, (a_seq_re, a_seq_im, bu_re, bu_im), axis=1)
    s_re = h0_re.astype(f32)[:, None]
    s_im = h0_im.astype(f32)[:, None]
    h_re = hz_re + acum_re * s_re - acum_im * s_im
    h_im = hz_im + acum_re * s_im + acum_im * s_re
    y = (jnp.einsum('gcp,blgp->blgc', c_re.astype(f32), h_re)
         - jnp.einsum('gcp,blgp->blgc', c_im.astype(f32), h_im)
         + d.astype(f32) * uf)
    y = jax.nn.gelu(y.reshape(Bn, L, SSM_W))
    y = y * jax.nn.sigmoid(y @ w_glu.astype(f32) + b_glu.astype(f32))
    return y.astype(u.dtype), h_re[:, -1].astype(h0_re.dtype), h_im[:, -1].astype(h0_im.dtype)


def decoder_layer(x, c, pos, attend, h0_re, h0_im, norm_g, w_ada, b_ada, w_in, sinks,
                  a_re, a_im, log_dt, b_re, b_im, c_re, c_im, d, w_glu, b_glu, w_out):
    Bn, L = x.shape[0], x.shape[1]
    mod = jax.nn.silu(c) @ w_ada + b_ada
    shift, scale, gate = jnp.split(mod[:, None, :], 3, axis=-1)
    h = rms_norm(x, norm_g) * (1.0 + scale) + shift
    proj = h @ w_in
    q, k, v, za, u, zs = jnp.split(proj, SPLIT_IDX, axis=-1)
    q = rope_partial(q.reshape(Bn, L, N_Q_HEADS, HEAD_DIM), pos)
    k = rope_partial(k.reshape(Bn, L, N_KV_HEADS, HEAD_DIM), pos)
    v = v.reshape(Bn, L, N_KV_HEADS, HEAD_DIM)
    ya = attend(q, k, v, sinks)
    ys, hT_re, hT_im = s5_branch(u, h0_re, h0_im, a_re, a_im, log_dt, b_re, b_im,
                                 c_re, c_im, d, w_glu, b_glu)
    mix = jnp.concatenate([ya * jax.nn.silu(za), ys * jax.nn.silu(zs)], axis=-1)
    x = x + gate * (mix @ w_out)
    return x, k, v, hT_re, hT_im


def setup_inputs(seed: int = 0) -> dict:
    key = jax.random.key(seed)
    ks = jax.random.split(key, 26)
    f32 = jnp.float32
    win_buf = min(WINDOW, PAST_LEN)
    nrm = lambda k, s: jax.random.normal(k, s, f32)
    return {
        "x_prompt": nrm(ks[0], (BATCH, SEQ, D_MODEL)),
        "x_sample": nrm(ks[1], (DEC_BATCH, DEC_SEQ, D_MODEL)),
        "cache_k": nrm(ks[2], (DEPTH, DEC_BATCH, win_buf, N_KV_HEADS, HEAD_DIM)),
        "cache_v": nrm(ks[3], (DEPTH, DEC_BATCH, win_buf, N_KV_HEADS, HEAD_DIM)),
        "state_ssm_re": 0.3 * nrm(ks[4], (DEPTH, DEC_BATCH, N_SSM_GROUPS, SSM_STATE)),
        "state_ssm_im": 0.3 * nrm(ks[5], (DEPTH, DEC_BATCH, N_SSM_GROUPS, SSM_STATE)),
        "c_prompt": nrm(ks[6], (BATCH, D_MODEL)),
        "c_sample": nrm(ks[7], (DEC_BATCH, D_MODEL)),
        "norm_g": 1.0 + 0.05 * nrm(ks[8], (DEPTH, D_MODEL)),
        "w_ada": 0.3 * D_MODEL ** -0.5 * nrm(ks[9], (DEPTH, D_MODEL, 3 * D_MODEL)),
        "b_ada": 0.01 * nrm(ks[10], (DEPTH, 3 * D_MODEL)),
        "w_in": D_MODEL ** -0.5 * nrm(ks[11], (DEPTH, D_MODEL, IN_W)),
        "attn_sinks": 0.5 * nrm(ks[12], (DEPTH, N_Q_HEADS)),
        "ssm_a_re": -0.5 + 0.01 * nrm(ks[13], (DEPTH, N_SSM_GROUPS, SSM_STATE)),
        "ssm_a_im": math.pi * jnp.arange(SSM_STATE, dtype=f32)
                    + 0.01 * nrm(ks[14], (DEPTH, N_SSM_GROUPS, SSM_STATE)),
        "ssm_log_dt": jax.random.uniform(ks[15], (DEPTH, N_SSM_GROUPS), f32,
                                         math.log(DT_MIN), math.log(DT_MAX)),
        "ssm_b_re": (2 * SSM_GROUP_CH) ** -0.5 * nrm(ks[16], (DEPTH, N_SSM_GROUPS, SSM_STATE, SSM_GROUP_CH)),
        "ssm_b_im": (2 * SSM_GROUP_CH) ** -0.5 * nrm(ks[17], (DEPTH, N_SSM_GROUPS, SSM_STATE, SSM_GROUP_CH)),
        "ssm_c_re": SSM_STATE ** -0.5 * nrm(ks[18], (DEPTH, N_SSM_GROUPS, SSM_GROUP_CH, SSM_STATE)),
        "ssm_c_im": SSM_STATE ** -0.5 * nrm(ks[19], (DEPTH, N_SSM_GROUPS, SSM_GROUP_CH, SSM_STATE)),
        "ssm_d": nrm(ks[20], (DEPTH, N_SSM_GROUPS, SSM_GROUP_CH)),
        "w_glu": SSM_W ** -0.5 * nrm(ks[21], (DEPTH, SSM_W, SSM_W)),
        "b_glu": 0.01 * nrm(ks[22], (DEPTH, SSM_W)),
        "w_out": MIX_W ** -0.5 * nrm(ks[23], (DEPTH, MIX_W, D_MODEL)),
        "final_g": 1.0 + 0.05 * nrm(ks[24], (D_MODEL,)),
    }


def reference(x_prompt, x_sample, cache_k, cache_v, state_ssm_re, state_ssm_im, c_prompt, c_sample,
              norm_g, w_ada, b_ada, w_in, attn_sinks, ssm_a_re, ssm_a_im, ssm_log_dt,
              ssm_b_re, ssm_b_im, ssm_c_re, ssm_c_im, ssm_d, w_glu, b_glu, w_out, final_g):
    win_buf = cache_k.shape[2]
    pos_p = jnp.arange(x_prompt.shape[1])
    pos_s = PAST_LEN + jnp.arange(x_sample.shape[1])
    zeros_state = jnp.zeros((x_prompt.shape[0], N_SSM_GROUPS, SSM_STATE), x_prompt.dtype)
    xp, xs = x_prompt, x_sample
    kp_l, vp_l, srp_l, sip_l = [], [], [], []
    ks_l, vs_l, srs_l, sis_l = [], [], [], []
    for l in range(DEPTH):
        params = (norm_g[l], w_ada[l], b_ada[l], w_in[l], attn_sinks[l], ssm_a_re[l], ssm_a_im[l],
                  ssm_log_dt[l], ssm_b_re[l], ssm_b_im[l], ssm_c_re[l], ssm_c_im[l], ssm_d[l],
                  w_glu[l], b_glu[l], w_out[l])
        xp, kp, vp, hr, hi = decoder_layer(xp, c_prompt, pos_p, attend_prompt,
                                           zeros_state, zeros_state, *params)
        kp_l.append(kp[:, -win_buf:])
        vp_l.append(vp[:, -win_buf:])
        srp_l.append(hr)
        sip_l.append(hi)
        att_s = functools.partial(attend_sample, cache_k[l], cache_v[l])
        xs, kn, vn, hr_s, hi_s = decoder_layer(xs, c_sample, pos_s, att_s,
                                               state_ssm_re[l], state_ssm_im[l], *params)
        ks_l.append(jnp.concatenate([cache_k[l].astype(kn.dtype), kn], axis=1)[:, -win_buf:])
        vs_l.append(jnp.concatenate([cache_v[l].astype(vn.dtype), vn], axis=1)[:, -win_buf:])
        srs_l.append(hr_s)
        sis_l.append(hi_s)
    y_prompt = rms_norm(xp, final_g)
    y_sample = rms_norm(xs, final_g)
    new_k_prompt = jnp.stack(kp_l)
    new_v_prompt = jnp.stack(vp_l)
    new_ssm_re_prompt = jnp.stack(srp_l)
    new_ssm_im_prompt = jnp.stack(sip_l)
    new_k_sample = jnp.stack(ks_l)
    new_v_sample = jnp.stack(vs_l)
    new_ssm_re_sample = jnp.stack(srs_l)
    new_ssm_im_sample = jnp.stack(sis_l)
    return (y_prompt, y_sample, new_k_prompt, new_v_prompt, new_ssm_re_prompt, new_ssm_im_prompt,
            new_k_sample, new_v_sample, new_ssm_re_sample, new_ssm_im_sample)
```

```python
import functools
import math

import jax
import jax.numpy as jnp
import numpy as np
from jax import lax
from jax.experimental import pallas as pl
from jax.experimental.pallas import tpu as pltpu

F32 = jnp.float32
BF16 = jnp.bfloat16

D_MODEL = 1024
HEAD_DIM = 64
ATTN_W = 512
N_Q_HEADS = 8
N_KV_HEADS = 2
KV_W = 128
WINDOW = 128
ROT_DIM = 16
ROPE_THETA = 500000.0
SSM_W = 512
SSM_GROUP_CH = 16
N_SSM_GROUPS = 32
SSM_STATE = 64
N_STATE = N_SSM_GROUPS * SSM_STATE
HALF_STATE = N_STATE // 2
EPS = 1e-6
NEG_INF = -1e30
PAST_LEN = 8192

LANES = 128
N_CHUNKS = ATTN_W // LANES
SUBLANES = 8
VMEM_LIMIT = 56 * 1024 * 1024

HEAD_PERM = (0, 4, 1, 5, 2, 6, 3, 7)


def _params(sem):
    return pltpu.CompilerParams(dimension_semantics=sem, vmem_limit_bytes=VMEM_LIMIT)


def _full(shape):
    return pl.BlockSpec(shape, lambda *_: (0,) * len(shape))


def _mod_kernel(c_ref, w_ref, b_ref, o_ref):
    c = c_ref[...]
    a = (c * jax.nn.sigmoid(c)).astype(BF16)
    o_ref[...] = jnp.dot(a, w_ref[...].astype(BF16), preferred_element_type=F32) + b_ref[...]


def adaln_mod(c_all, w_ada, b_ada):
    depth = w_ada.shape[0]
    n = c_all.shape[0]
    nj = 3 * D_MODEL // D_MODEL
    return pl.pallas_call(
        _mod_kernel,
        grid=(depth, nj),
        in_specs=[
            pl.BlockSpec((n, D_MODEL), lambda l, j: (0, 0)),
            pl.BlockSpec((None, D_MODEL, D_MODEL), lambda l, j: (l, 0, j)),
            pl.BlockSpec((None, 1, D_MODEL), lambda l, j: (l, 0, j)),
        ],
        out_specs=pl.BlockSpec((None, n, D_MODEL), lambda l, j: (l, 0, j)),
        out_shape=jax.ShapeDtypeStruct((depth, n, 3 * D_MODEL), F32),
        compiler_params=_params(("arbitrary", "arbitrary")),
        name="adaln_mod",
    )(c_all, w_ada, b_ada.reshape(depth, 1, 3 * D_MODEL))


def _disc_kernel(are_ref, aim_ref, ldt_ref, bre_ref, bim_ref, abre_ref, abim_ref, bbre_ref, bbim_ref):
    a_re = are_ref[...]
    a_im = aim_ref[...]
    dt = jnp.exp(ldt_ref[...])
    mag = jnp.exp(a_re * dt)
    abar_re = mag * jnp.cos(a_im * dt)
    abar_im = mag * jnp.sin(a_im * dt)
    den = a_re * a_re + a_im * a_im
    nr = abar_re - 1.0
    coef_re = (nr * a_re + abar_im * a_im) / den
    coef_im = (abar_im * a_re - nr * a_im) / den
    br = bre_ref[...]
    bi = bim_ref[...]
    abre_ref[...] = abar_re
    abim_ref[...] = abar_im
    bbre_ref[...] = coef_re * br - coef_im * bi
    bbim_ref[...] = coef_re * bi + coef_im * br


def ssm_discretise(a_re, a_im, log_dt, b_re, b_im):
    depth = a_re.shape[0]
    a_re = a_re.reshape(depth, 1, N_STATE)
    a_im = a_im.reshape(depth, 1, N_STATE)
    ldt = jnp.repeat(log_dt, SSM_STATE, axis=1).reshape(depth, 1, N_STATE)
    bt_re = b_re.reshape(depth, N_STATE, SSM_GROUP_CH).transpose(0, 2, 1)
    bt_im = b_im.reshape(depth, N_STATE, SSM_GROUP_CH).transpose(0, 2, 1)
    row = pl.BlockSpec((None, 1, N_STATE), lambda l: (l, 0, 0))
    mat = pl.BlockSpec((None, SSM_GROUP_CH, N_STATE), lambda l: (l, 0, 0))
    return pl.pallas_call(
        _disc_kernel,
        grid=(depth,),
        in_specs=[row, row, row, mat, mat],
        out_specs=[row, row, mat, mat],
        out_shape=[jax.ShapeDtypeStruct((depth, 1, N_STATE), F32)] * 2
        + [jax.ShapeDtypeStruct((depth, SSM_GROUP_CH, N_STATE), F32)] * 2,
        compiler_params=_params(("arbitrary",)),
        name="ssm_discretise",
    )(a_re, a_im, ldt, bt_re, bt_im)


def _rope(x, cos, sm):
    lane = lax.broadcasted_iota(jnp.int32, x.shape, 1) % HEAD_DIM
    partner = jnp.where(lane < ROT_DIM // 2, pltpu.roll(x, LANES - ROT_DIM // 2, 1), pltpu.roll(x, ROT_DIM // 2, 1))
    return x * cos + partner * sm


def _inproj_kernel(x_ref, g_ref, scale_ref, shift_ref, w_ref, cos_ref, sm_ref,
                   q_ref, k_ref, v_ref, za_ref, u_ref, zs_ref, *, n_batch):
    x = x_ref[...]
    rows = x.shape[0]
    ms = jnp.mean(x * x, axis=-1, keepdims=True)
    y = x * lax.rsqrt(ms + EPS) * g_ref[...]
    y = y.reshape(rows // n_batch, n_batch, D_MODEL) * (1.0 + scale_ref[...])[None] + shift_ref[...][None]
    h = y.reshape(rows, D_MODEL).astype(BF16)
    cos = cos_ref[...]
    sm = sm_ref[...]

    def proj(lo, width):
        return jnp.dot(h, w_ref[:, lo:lo + width], preferred_element_type=F32)

    q = proj(0, ATTN_W)
    za = proj(ATTN_W, ATTN_W)
    for c in range(N_CHUNKS):
        q_ref[c] = _rope(q[:, c * LANES:(c + 1) * LANES], cos, sm)
        za_ref[c] = za[:, c * LANES:(c + 1) * LANES]
    u_ref[...] = proj(2 * ATTN_W, SSM_W)
    zs_ref[...] = proj(2 * ATTN_W + SSM_W, SSM_W)
    k_ref[...] = _rope(proj(2 * ATTN_W + 2 * SSM_W, KV_W), cos, sm)
    v_ref[...] = proj(2 * ATTN_W + 2 * SSM_W + KV_W, KV_W)


def in_projection(x, norm_g, scale, shift, w_in, cos_t, sm_t, n_batch, block_rows):
    t_rows = x.shape[0]
    in_w = w_in.shape[1]
    rows = lambda w: pl.BlockSpec((block_rows, w), lambda i: (i, 0))
    chunked = pl.BlockSpec((N_CHUNKS, block_rows, LANES), lambda i: (0, i, 0))
    flat = lambda w: jax.ShapeDtypeStruct((t_rows, w), F32)
    chunked_shape = jax.ShapeDtypeStruct((N_CHUNKS, t_rows, LANES), F32)
    return pl.pallas_call(
        functools.partial(_inproj_kernel, n_batch=n_batch),
        grid=(t_rows // block_rows,),
        in_specs=[rows(D_MODEL), _full((1, D_MODEL)), _full((n_batch, D_MODEL)), _full((n_batch, D_MODEL)),
                  _full((D_MODEL, in_w)), rows(LANES), rows(LANES)],
        out_specs=[chunked, rows(KV_W), rows(KV_W), chunked, rows(SSM_W), rows(SSM_W)],
        out_shape=[chunked_shape, flat(KV_W), flat(KV_W), chunked_shape, flat(SSM_W), flat(SSM_W)],
        compiler_params=_params(("parallel",)),
        name="in_projection",
    )(x, norm_g, scale, shift, w_in, cos_t, sm_t)


def _pad_queries(chunks):
    lane = lax.broadcasted_iota(jnp.int32, chunks[0].shape, 1)
    pieces = []
    for chunk in chunks:
        pieces.append(jnp.where(lane < HEAD_DIM, chunk, 0.0))
        pieces.append(jnp.where(lane >= HEAD_DIM, chunk, 0.0))
    return jnp.concatenate(pieces, axis=0).astype(BF16)


def _unpad_outputs(o, t):
    lane = lax.broadcasted_iota(jnp.int32, (t, LANES), 1)
    chunks = []
    for c in range(N_CHUNKS):
        lo = o[(2 * c) * t:(2 * c + 1) * t]
        hi = o[(2 * c + 1) * t:(2 * c + 2) * t]
        chunks.append(jnp.where(lane < HEAD_DIM, lo, hi))
    return chunks


def _gated_store(o_ref, za_ref, sel, ya_chunks):
    for c, ya in enumerate(ya_chunks):
        za = za_ref[c, sel, :]
        o_ref[c, sel, :] = ya * (za * jax.nn.sigmoid(za))


def _scores(qp, k):
    return lax.dot_general(qp, k.astype(BF16), (((1,), (1,)), ((), ())),
                           preferred_element_type=F32) * (HEAD_DIM ** -0.5)


def _attn_prompt_kernel(q_ref, kp_ref, kc_ref, vp_ref, vc_ref, za_ref, sink_ref, o_ref, *, n_batch, blk):
    n = pl.program_id(0)
    rows = N_Q_HEADS * blk
    tq = lax.broadcasted_iota(jnp.int32, (rows, 2 * blk), 0) % blk
    j = lax.broadcasted_iota(jnp.int32, (rows, 2 * blk), 1)
    first_key = jnp.where(n == 0, blk, 0)
    mask = (j >= tq) & (j <= tq + WINDOW) & (j >= first_key)
    sink = sink_ref[...][:, :1]

    def body(b, carry):
        sel = pl.ds(b, blk, stride=n_batch)
        qp = _pad_queries([q_ref[c, sel, :] for c in range(N_CHUNKS)])
        k = jnp.concatenate([kp_ref[sel, :], kc_ref[sel, :]], axis=0)
        v = jnp.concatenate([vp_ref[sel, :], vc_ref[sel, :]], axis=0)
        s = jnp.where(mask, _scores(qp, k), NEG_INF)
        m = jnp.maximum(jnp.max(s, axis=-1, keepdims=True), sink)
        p = jnp.exp(s - m)
        den = jnp.sum(p, axis=-1, keepdims=True) + jnp.exp(sink - m)
        o = jnp.dot(p.astype(BF16), v.astype(BF16), preferred_element_type=F32) / den
        _gated_store(o_ref, za_ref, sel, _unpad_outputs(o, blk))
        return carry

    lax.fori_loop(0, n_batch, body, 0)


def attention_prompt(q, k, v, za, sink_rows, n_batch, blk):
    t_rows = k.shape[0]
    r = blk * n_batch
    cur = lambda w: pl.BlockSpec((r, w), lambda i: (i, 0))
    prev = lambda w: pl.BlockSpec((r, w), lambda i: (jnp.maximum(i - 1, 0), 0))
    chunked = pl.BlockSpec((N_CHUNKS, r, LANES), lambda i: (0, i, 0))
    return pl.pallas_call(
        functools.partial(_attn_prompt_kernel, n_batch=n_batch, blk=blk),
        grid=(t_rows // r,),
        in_specs=[chunked, prev(KV_W), cur(KV_W), prev(KV_W), cur(KV_W), chunked,
                  _full(sink_rows.shape)],
        out_specs=chunked,
        out_shape=jax.ShapeDtypeStruct((N_CHUNKS, t_rows, LANES), F32),
        compiler_params=_params(("parallel",)),
        name="attention_prompt",
    )(q, k, k, v, v, za, sink_rows)


def _attn_sample_kernel(q_ref, k_ref, v_ref, za_ref, ck_ref, cv_ref, sink_ref, o_ref, nk_ref, nv_ref,
                        *, n_batch, t_new, group):
    g = pl.program_id(0)
    win = ck_ref.shape[1]
    rows = N_Q_HEADS * t_new
    tq = lax.broadcasted_iota(jnp.int32, (rows, win), 0) % t_new
    j = lax.broadcasted_iota(jnp.int32, (rows, win), 1)
    mask_c = (tq + win - j >= 0) & (tq + win - j <= WINDOW)
    mask_n = (j <= tq) & (j < t_new)
    sink = sink_ref[...][:, :1]
    pad = jnp.zeros((win - t_new, KV_W), F32)

    def body(i, carry):
        sel = pl.ds(g * group + i, t_new, stride=n_batch)
        qp = _pad_queries([q_ref[c, sel, :] for c in range(N_CHUNKS)])
        kn = k_ref[sel, :]
        vn = v_ref[sel, :]
        kc = ck_ref[i]
        vc = cv_ref[i]
        s_c = jnp.where(mask_c, _scores(qp, kc), NEG_INF)
        s_n = jnp.where(mask_n, _scores(qp, jnp.concatenate([kn, pad], axis=0)), NEG_INF)
        m = jnp.maximum(jnp.maximum(jnp.max(s_c, axis=-1, keepdims=True), jnp.max(s_n, axis=-1, keepdims=True)), sink)
        p_c = jnp.exp(s_c - m)
        p_n = jnp.exp(s_n - m)
        den = jnp.sum(p_c, axis=-1, keepdims=True) + jnp.sum(p_n, axis=-1, keepdims=True) + jnp.exp(sink - m)
        o = (jnp.dot(p_c.astype(BF16), vc.astype(BF16), preferred_element_type=F32)
             + jnp.dot(p_n.astype(BF16), jnp.concatenate([vn, pad], axis=0).astype(BF16),
                       preferred_element_type=F32)) / den
        _gated_store(o_ref, za_ref, sel, _unpad_outputs(o, t_new))
        nk_ref[i, 0:win - t_new, :] = kc[t_new:win]
        nk_ref[i, win - t_new:win, :] = kn
        nv_ref[i, 0:win - t_new, :] = vc[t_new:win]
        nv_ref[i, win - t_new:win, :] = vn
        return carry

    lax.fori_loop(0, group, body, 0)


def attention_sample(q, k, v, za, cache_k, cache_v, sink_rows, n_batch, t_new, group):
    t_rows = k.shape[0]
    win = cache_k.shape[1]
    cache = pl.BlockSpec((group, win, KV_W), lambda i: (i, 0, 0))
    chunked = _full((N_CHUNKS, t_rows, LANES))
    return pl.pallas_call(
        functools.partial(_attn_sample_kernel, n_batch=n_batch, t_new=t_new, group=group),
        grid=(n_batch // group,),
        in_specs=[chunked, _full((t_rows, KV_W)), _full((t_rows, KV_W)), chunked,
                  cache, cache, _full(sink_rows.shape)],
        out_specs=[chunked, cache, cache],
        out_shape=[jax.ShapeDtypeStruct((N_CHUNKS, t_rows, LANES), F32),
                   jax.ShapeDtypeStruct(cache_k.shape, F32), jax.ShapeDtypeStruct(cache_v.shape, F32)],
        compiler_params=_params(("arbitrary",)),
        name="attention_sample",
    )(q, k, v, za, cache_k, cache_v, sink_rows)


def _ssm_kernel(u_ref, zs_ref, h0re_ref, h0im_ref, are_ref, aim_ref, wb_ref, wc_ref, d_ref, wg_ref, bg_ref,
                o_ref, hre_ref, him_ref, x_scr, *, n_batch, t_chunk, slab):
    step = pl.program_id(0)

    @pl.when(step == 0)
    def _():
        hre_ref[...] = h0re_ref[...]
        him_ref[...] = h0im_ref[...]

    u = u_ref[...]
    ub = u.astype(BF16)
    half_in = SSM_W // 2
    ys = []
    for s in range(2):
        x_scr[...] = jnp.dot(ub[:, s * half_in:(s + 1) * half_in], wb_ref[s], preferred_element_type=F32)
        for jb in range(HALF_STATE // slab):
            n_lo = s * HALF_STATE + jb * slab
            ar = jnp.broadcast_to(are_ref[:, n_lo:n_lo + slab], (SUBLANES, slab))
            ai = jnp.broadcast_to(aim_ref[:, n_lo:n_lo + slab], (SUBLANES, slab))
            re_lo = jb * slab
            im_lo = HALF_STATE + jb * slab

            def tile_body(i, carry):
                r0 = pl.multiple_of(i * SUBLANES, SUBLANES)

                def t_body(t, h):
                    hr, hi = h
                    row = pl.multiple_of(t * n_batch + r0, SUBLANES)
                    xr = x_scr[pl.ds(row, SUBLANES), re_lo:re_lo + slab]
                    xi = x_scr[pl.ds(row, SUBLANES), im_lo:im_lo + slab]
                    nr = ar * hr - ai * hi + xr
                    ni = ar * hi + ai * hr + xi
                    x_scr[pl.ds(row, SUBLANES), re_lo:re_lo + slab] = nr
                    x_scr[pl.ds(row, SUBLANES), im_lo:im_lo + slab] = ni
                    return nr, ni

                h0 = (hre_ref[pl.ds(r0, SUBLANES), n_lo:n_lo + slab], him_ref[pl.ds(r0, SUBLANES), n_lo:n_lo + slab])
                hr, hi = lax.fori_loop(0, t_chunk, t_body, h0, unroll=8)
                hre_ref[pl.ds(r0, SUBLANES), n_lo:n_lo + slab] = hr
                him_ref[pl.ds(r0, SUBLANES), n_lo:n_lo + slab] = hi
                return carry

            lax.fori_loop(0, n_batch // SUBLANES, tile_body, 0)
        ys.append(jnp.dot(x_scr[...].astype(BF16), wc_ref[s], preferred_element_type=F32))
    y = jnp.concatenate(ys, axis=1) + d_ref[...] * u
    y = jax.nn.gelu(y)
    gate = jax.nn.sigmoid(jnp.dot(y.astype(BF16), wg_ref[...], preferred_element_type=F32) + bg_ref[...])
    zs = zs_ref[...]
    o_ref[...] = y * gate * (zs * jax.nn.sigmoid(zs))


def ssm_branch(u, zs, h0_re, h0_im, abar_re, abar_im, w_b, w_c, d, w_glu, b_glu, n_batch, t_chunk):
    t_rows = u.shape[0]
    r = t_chunk * n_batch
    rows = pl.BlockSpec((r, SSM_W), lambda i: (i, 0))
    state = _full((n_batch, N_STATE))
    return pl.pallas_call(
        functools.partial(_ssm_kernel, n_batch=n_batch, t_chunk=t_chunk, slab=4 * LANES),
        grid=(t_rows // r,),
        in_specs=[rows, rows, state, state, _full((1, N_STATE)), _full((1, N_STATE)),
                  _full(w_b.shape), _full(w_c.shape), _full((1, SSM_W)), _full((SSM_W, SSM_W)), _full((1, SSM_W))],
        out_specs=[rows, state, state],
        out_shape=[jax.ShapeDtypeStruct((t_rows, SSM_W), F32),
                   jax.ShapeDtypeStruct((n_batch, N_STATE), F32), jax.ShapeDtypeStruct((n_batch, N_STATE), F32)],
        scratch_shapes=[pltpu.VMEM((r, 2 * HALF_STATE), F32)],
        compiler_params=_params(("arbitrary",)),
        name="ssm_branch",
    )(u, zs, h0_re, h0_im, abar_re, abar_im, w_b, w_c, d, w_glu, b_glu)


def _outproj_kernel(x_ref, ya_ref, ys_ref, gate_ref, w_ref, fg_ref, o_ref, *, n_batch, final):
    rows = x_ref.shape[0]
    ya = jnp.concatenate([ya_ref[c] for c in range(N_CHUNKS)], axis=1)
    mix = (jnp.dot(ya.astype(BF16), w_ref[0:ATTN_W, :], preferred_element_type=F32)
           + jnp.dot(ys_ref[...].astype(BF16), w_ref[ATTN_W:ATTN_W + SSM_W, :], preferred_element_type=F32))
    gated = mix.reshape(rows // n_batch, n_batch, D_MODEL) * gate_ref[...][None]
    x = x_ref[...] + gated.reshape(rows, D_MODEL)
    if final:
        ms = jnp.mean(x * x, axis=-1, keepdims=True)
        x = x * lax.rsqrt(ms + EPS) * fg_ref[...]
    o_ref[...] = x


def out_projection(x, ya, ys, gate, w_out, final_g, n_batch, block_rows, final):
    t_rows = x.shape[0]
    rows = lambda w: pl.BlockSpec((block_rows, w), lambda i: (i, 0))
    return pl.pallas_call(
        functools.partial(_outproj_kernel, n_batch=n_batch, final=final),
        grid=(t_rows // block_rows,),
        in_specs=[rows(D_MODEL), pl.BlockSpec((N_CHUNKS, block_rows, LANES), lambda i: (0, i, 0)), rows(SSM_W),
                  _full((n_batch, D_MODEL)),
                  _full(w_out.shape), _full((1, D_MODEL))],
        out_specs=rows(D_MODEL),
        out_shape=jax.ShapeDtypeStruct((t_rows, D_MODEL), F32),
        compiler_params=_params(("parallel",)),
        name="out_projection",
    )(x, ya, ys, gate, w_out, final_g)


def _rope_tables(pos, n_batch):
    half = ROT_DIM // 2
    inv = ROPE_THETA ** (-jnp.arange(half, dtype=F32) / half)
    ang = pos.astype(F32)[:, None] * inv[None, :]
    d = np.arange(LANES) % HEAD_DIM
    idx = d % half
    cos = jnp.where(d < ROT_DIM, jnp.cos(ang)[:, idx], 1.0)
    sin = jnp.sin(ang)[:, idx]
    sm = jnp.where(d < half, -sin, jnp.where(d < ROT_DIM, sin, 0.0))
    return jnp.repeat(cos, n_batch, axis=0), jnp.repeat(sm, n_batch, axis=0)


def _block_diag(blocks):
    g, a, b = blocks.shape
    eye = jnp.eye(g, dtype=bool)
    out = jnp.where(eye[:, None, :, None], blocks[:, :, None, :], 0.0)
    return out.reshape(g * a, g * b)


def _layer_weights(l, w_in, w_out, attn_sinks, bb_re, bb_im, ssm_c_re, ssm_c_im):
    qcols = np.concatenate([np.arange(h * HEAD_DIM, (h + 1) * HEAD_DIM) for h in HEAD_PERM])
    base_k = ATTN_W
    base_v = ATTN_W + KV_W
    base_za = ATTN_W + 2 * KV_W
    base_u = base_za + ATTN_W
    base_zs = base_u + SSM_W
    cols = np.concatenate([qcols, base_za + qcols, base_u + np.arange(SSM_W), base_zs + np.arange(SSM_W),
                           base_k + np.arange(KV_W), base_v + np.arange(KV_W)])
    w_in_l = w_in[l][:, cols].astype(BF16)
    w_out_l = w_out[l][np.concatenate([qcols, ATTN_W + np.arange(SSM_W)])].astype(BF16)
    sinks = attn_sinks[l][np.array(HEAD_PERM)]
    def b_full(bb):
        blocks = bb[l].reshape(SSM_GROUP_CH, N_SSM_GROUPS, SSM_STATE).transpose(1, 0, 2)
        return _block_diag(blocks)
    bre, bim = b_full(bb_re), b_full(bb_im)
    hin = SSM_W // 2
    w_b = jnp.stack([jnp.concatenate([bre[s * hin:(s + 1) * hin, s * HALF_STATE:(s + 1) * HALF_STATE],
                                      bim[s * hin:(s + 1) * hin, s * HALF_STATE:(s + 1) * HALF_STATE]], axis=1)
                     for s in range(2)]).astype(BF16)
    cre = _block_diag(ssm_c_re[l].transpose(0, 2, 1))
    cim = _block_diag(ssm_c_im[l].transpose(0, 2, 1))
    w_c = jnp.stack([jnp.concatenate([cre[s * HALF_STATE:(s + 1) * HALF_STATE, s * hin:(s + 1) * hin],
                                      -cim[s * HALF_STATE:(s + 1) * HALF_STATE, s * hin:(s + 1) * hin]], axis=0)
                     for s in range(2)]).astype(BF16)
    return w_in_l, w_out_l, sinks, w_b, w_c


def _sink_rows(sinks, t):
    return jnp.broadcast_to(jnp.repeat(sinks, t)[:, None], (N_Q_HEADS * t, LANES))


def kernel(x_prompt, x_sample, cache_k, cache_v, state_ssm_re, state_ssm_im, c_prompt, c_sample, norm_g, w_ada, b_ada, w_in, attn_sinks, ssm_a_re, ssm_a_im, ssm_log_dt, ssm_b_re, ssm_b_im, ssm_c_re, ssm_c_im, ssm_d, w_glu, b_glu, w_out, final_g):
    depth = w_in.shape[0]
    nb_p, seq, _ = x_prompt.shape
    nb_s, t_new, _ = x_sample.shape
    win = cache_k.shape[2]
    blk = WINDOW

    mod = adaln_mod(jnp.concatenate([c_prompt, c_sample], axis=0), w_ada, b_ada)
    abar_re, abar_im, bb_re, bb_im = ssm_discretise(ssm_a_re, ssm_a_im, ssm_log_dt, ssm_b_re, ssm_b_im)

    cos_p, sm_p = _rope_tables(jnp.arange(seq), nb_p)
    cos_s, sm_s = _rope_tables(PAST_LEN + jnp.arange(t_new), nb_s)

    xp = x_prompt.transpose(1, 0, 2).reshape(seq * nb_p, D_MODEL)
    xs = x_sample.transpose(1, 0, 2).reshape(t_new * nb_s, D_MODEL)
    zeros_state = jnp.zeros((nb_p, N_STATE), F32)
    final_g2 = final_g.reshape(1, D_MODEL)

    outs = {k: [] for k in ("kp", "vp", "rp", "ip", "ks", "vs", "rs", "is")}
    for l in range(depth):
        w_in_l, w_out_l, sinks, w_b, w_c = _layer_weights(l, w_in, w_out, attn_sinks, bb_re, bb_im, ssm_c_re, ssm_c_im)
        g_l = norm_g[l].reshape(1, D_MODEL)
        d_l = ssm_d[l].reshape(1, SSM_W)
        wg_l = w_glu[l].astype(BF16)
        bg_l = b_glu[l].reshape(1, SSM_W)
        final = l == depth - 1

        def split_mod(m):
            return m[:, :D_MODEL], m[:, D_MODEL:2 * D_MODEL], m[:, 2 * D_MODEL:]

        shift, scale, gate = split_mod(mod[l, :nb_p])
        q, k, v, za, u, zs = in_projection(xp, g_l, scale, shift, w_in_l, cos_p, sm_p, nb_p, 1024)
        ya = attention_prompt(q, k, v, za, _sink_rows(sinks, blk), nb_p, blk)
        ys, hre, him = ssm_branch(u, zs, zeros_state, zeros_state, abar_re[l], abar_im[l], w_b, w_c, d_l, wg_l, bg_l,
                                  nb_p, 64)
        xp = out_projection(xp, ya, ys, gate, w_out_l, final_g2, nb_p, 1024, final)
        tail = lambda a: a[(seq - win) * nb_p:].reshape(win, nb_p, N_KV_HEADS, HEAD_DIM).transpose(1, 0, 2, 3)
        outs["kp"].append(tail(k))
        outs["vp"].append(tail(v))
        outs["rp"].append(hre.reshape(nb_p, N_SSM_GROUPS, SSM_STATE))
        outs["ip"].append(him.reshape(nb_p, N_SSM_GROUPS, SSM_STATE))

        shift, scale, gate = split_mod(mod[l, nb_p:])
        q, k, v, za, u, zs = in_projection(xs, g_l, scale, shift, w_in_l, cos_s, sm_s, nb_s, t_new * nb_s)
        ya, nk, nv = attention_sample(q, k, v, za, cache_k[l].reshape(nb_s, win, KV_W),
                                      cache_v[l].reshape(nb_s, win, KV_W), _sink_rows(sinks, t_new), nb_s, t_new, 16)
        ys, hre, him = ssm_branch(u, zs, state_ssm_re[l].reshape(nb_s, N_STATE), state_ssm_im[l].reshape(nb_s, N_STATE),
                                  abar_re[l], abar_im[l], w_b, w_c, d_l, wg_l, bg_l, nb_s, t_new)
        xs = out_projection(xs, ya, ys, gate, w_out_l, final_g2, nb_s, t_new * nb_s, final)
        outs["ks"].append(nk.reshape(nb_s, win, N_KV_HEADS, HEAD_DIM))
        outs["vs"].append(nv.reshape(nb_s, win, N_KV_HEADS, HEAD_DIM))
        outs["rs"].append(hre.reshape(nb_s, N_SSM_GROUPS, SSM_STATE))
        outs["is"].append(him.reshape(nb_s, N_SSM_GROUPS, SSM_STATE))

    y_prompt = xp.reshape(seq, nb_p, D_MODEL).transpose(1, 0, 2)
    y_sample = xs.reshape(t_new, nb_s, D_MODEL).transpose(1, 0, 2)
    st = lambda key: jnp.stack(outs[key])
    return (y_prompt, y_sample, st("kp"), st("vp"), st("rp"), st("ip"), st("ks"), st("vs"), st("rs"), st("is"))
```

```python
import functools
import math

import jax
import jax.numpy as jnp
import numpy as np
from jax import lax
from jax.experimental import pallas as pl
from jax.experimental.pallas import tpu as pltpu

F32 = jnp.float32
BF16 = jnp.bfloat16

D_MODEL = 1024
HEAD_DIM = 64
ATTN_W = 512
N_Q_HEADS = 8
N_KV_HEADS = 2
KV_W = 128
WINDOW = 128
ROT_DIM = 16
ROPE_THETA = 500000.0
SSM_W = 512
SSM_GROUP_CH = 16
N_SSM_GROUPS = 32
SSM_STATE = 64
N_STATE = N_SSM_GROUPS * SSM_STATE
HALF_STATE = N_STATE // 2
EPS = 1e-6
NEG_INF = -1e30
PAST_LEN = 8192

LANES = 128
N_CHUNKS = ATTN_W // LANES
SUBLANES = 8
VMEM_LIMIT = 56 * 1024 * 1024

HEAD_PERM = (0, 4, 1, 5, 2, 6, 3, 7)


def _params(sem):
    return pltpu.CompilerParams(dimension_semantics=sem, vmem_limit_bytes=VMEM_LIMIT)


def _full(shape):
    return pl.BlockSpec(shape, lambda *_: (0,) * len(shape))


def _mod_kernel(c_ref, w_ref, b_ref, o_ref):
    c = c_ref[...]
    a = (c * jax.nn.sigmoid(c)).astype(BF16)
    o_ref[...] = jnp.dot(a, w_ref[...].astype(BF16), preferred_element_type=F32) + b_ref[...]


def adaln_mod(c_all, w_ada, b_ada):
    depth = w_ada.shape[0]
    n = c_all.shape[0]
    nj = 3 * D_MODEL // D_MODEL
    return pl.pallas_call(
        _mod_kernel,
        grid=(depth, nj),
        in_specs=[
            pl.BlockSpec((n, D_MODEL), lambda l, j: (0, 0)),
            pl.BlockSpec((None, D_MODEL, D_MODEL), lambda l, j: (l, 0, j)),
            pl.BlockSpec((None, 1, D_MODEL), lambda l, j: (l, 0, j)),
        ],
        out_specs=pl.BlockSpec((None, n, D_MODEL), lambda l, j: (l, 0, j)),
        out_shape=jax.ShapeDtypeStruct((depth, n, 3 * D_MODEL), F32),
        compiler_params=_params(("arbitrary", "arbitrary")),
        name="adaln_mod",
    )(c_all, w_ada, b_ada.reshape(depth, 1, 3 * D_MODEL))


def _disc_kernel(are_ref, aim_ref, ldt_ref, bre_ref, bim_ref, abre_ref, abim_ref, bbre_ref, bbim_ref):
    a_re = are_ref[...]
    a_im = aim_ref[...]
    dt = jnp.exp(ldt_ref[...])
    mag = jnp.exp(a_re * dt)
    abar_re = mag * jnp.cos(a_im * dt)
    abar_im = mag * jnp.sin(a_im * dt)
    den = a_re * a_re + a_im * a_im
    nr = abar_re - 1.0
    coef_re = (nr * a_re + abar_im * a_im) / den
    coef_im = (abar_im * a_re - nr * a_im) / den
    br = bre_ref[...]
    bi = bim_ref[...]
    abre_ref[...] = abar_re
    abim_ref[...] = abar_im
    bbre_ref[...] = coef_re * br - coef_im * bi
    bbim_ref[...] = coef_re * bi + coef_im * br


def ssm_discretise(a_re, a_im, log_dt, b_re, b_im):
    depth = a_re.shape[0]
    a_re = a_re.reshape(depth, 1, N_STATE)
    a_im = a_im.reshape(depth, 1, N_STATE)
    ldt = jnp.repeat(log_dt, SSM_STATE, axis=1).reshape(depth, 1, N_STATE)
    bt_re = b_re.reshape(depth, N_STATE, SSM_GROUP_CH).transpose(0, 2, 1)
    bt_im = b_im.reshape(depth, N_STATE, SSM_GROUP_CH).transpose(0, 2, 1)
    row = pl.BlockSpec((None, 1, N_STATE), lambda l: (l, 0, 0))
    mat = pl.BlockSpec((None, SSM_GROUP_CH, N_STATE), lambda l: (l, 0, 0))
    return pl.pallas_call(
        _disc_kernel,
        grid=(depth,),
        in_specs=[row, row, row, mat, mat],
        out_specs=[row, row, mat, mat],
        out_shape=[jax.ShapeDtypeStruct((depth, 1, N_STATE), F32)] * 2
        + [jax.ShapeDtypeStruct((depth, SSM_GROUP_CH, N_STATE), F32)] * 2,
        compiler_params=_params(("arbitrary",)),
        name="ssm_discretise",
    )(a_re, a_im, ldt, bt_re, bt_im)


def _rope(x, cos, sm):
    lane = lax.broadcasted_iota(jnp.int32, x.shape, 1) % HEAD_DIM
    partner = jnp.where(lane < ROT_DIM // 2, pltpu.roll(x, LANES - ROT_DIM // 2, 1), pltpu.roll(x, ROT_DIM // 2, 1))
    return x * cos + partner * sm


def _inproj_kernel(x_ref, g_ref, scale_ref, shift_ref, w_ref, cos_ref, sm_ref,
                   q_ref, k_ref, v_ref, za_ref, u_ref, zs_ref, *, n_batch):
    x = x_ref[...]
    rows = x.shape[0]
    ms = jnp.mean(x * x, axis=-1, keepdims=True)
    y = x * lax.rsqrt(ms + EPS) * g_ref[...]
    y = y.reshape(rows // n_batch, n_batch, D_MODEL) * (1.0 + scale_ref[...])[None] + shift_ref[...][None]
    h = y.reshape(rows, D_MODEL).astype(BF16)
    cos = cos_ref[...]
    sm = sm_ref[...]

    def proj(lo, width):
        return jnp.dot(h, w_ref[:, lo:lo + width], preferred_element_type=F32)

    q = proj(0, ATTN_W)
    za = proj(ATTN_W, ATTN_W)
    for c in range(N_CHUNKS):
        q_ref[c] = _rope(q[:, c * LANES:(c + 1) * LANES], cos, sm)
        za_ref[c] = za[:, c * LANES:(c + 1) * LANES]
    u_ref[...] = proj(2 * ATTN_W, SSM_W)
    zs_ref[...] = proj(2 * ATTN_W + SSM_W, SSM_W)
    k_ref[...] = _rope(proj(2 * ATTN_W + 2 * SSM_W, KV_W), cos, sm)
    v_ref[...] = proj(2 * ATTN_W + 2 * SSM_W + KV_W, KV_W)


def in_projection(x, norm_g, scale, shift, w_in, cos_t, sm_t, n_batch, block_rows):
    t_rows = x.shape[0]
    in_w = w_in.shape[1]
    rows = lambda w: pl.BlockSpec((block_rows, w), lambda i: (i, 0))
    chunked = pl.BlockSpec((N_CHUNKS, block_rows, LANES), lambda i: (0, i, 0))
    flat = lambda w: jax.ShapeDtypeStruct((t_rows, w), F32)
    chunked_shape = jax.ShapeDtypeStruct((N_CHUNKS, t_rows, LANES), F32)
    return pl.pallas_call(
        functools.partial(_inproj_kernel, n_batch=n_batch),
        grid=(t_rows // block_rows,),
        in_specs=[rows(D_MODEL), _full((1, D_MODEL)), _full((n_batch, D_MODEL)), _full((n_batch, D_MODEL)),
                  _full((D_MODEL, in_w)), rows(LANES), rows(LANES)],
        out_specs=[chunked, rows(KV_W), rows(KV_W), chunked, rows(SSM_W), rows(SSM_W)],
        out_shape=[chunked_shape, flat(KV_W), flat(KV_W), chunked_shape, flat(SSM_W), flat(SSM_W)],
        compiler_params=_params(("parallel",)),
        name="in_projection",
    )(x, norm_g, scale, shift, w_in, cos_t, sm_t)


def _pad_queries(chunks):
    lane = lax.broadcasted_iota(jnp.int32, chunks[0].shape, 1)
    pieces = []
    for chunk in chunks:
        pieces.append(jnp.where(lane < HEAD_DIM, chunk, 0.0))
        pieces.append(jnp.where(lane >= HEAD_DIM, chunk, 0.0))
    return jnp.concatenate(pieces, axis=0).astype(BF16)


def _unpad_outputs(o, t):
    lane = lax.broadcasted_iota(jnp.int32, (t, LANES), 1)
    chunks = []
    for c in range(N_CHUNKS):
        lo = o[(2 * c) * t:(2 * c + 1) * t]
        hi = o[(2 * c + 1) * t:(2 * c + 2) * t]
        chunks.append(jnp.where(lane < HEAD_DIM, lo, hi))
    return chunks


def _gated_store(o_ref, za_ref, sel, ya_chunks):
    for c, ya in enumerate(ya_chunks):
        za = za_ref[c, sel, :]
        o_ref[c, sel, :] = ya * (za * jax.nn.sigmoid(za))


def _scores(qp, k):
    return lax.dot_general(qp, k.astype(BF16), (((1,), (1,)), ((), ())),
                           preferred_element_type=F32) * (HEAD_DIM ** -0.5)


def _attn_prompt_kernel(q_ref, kp_ref, kc_ref, vp_ref, vc_ref, za_ref, sink_ref, o_ref, *, n_batch, blk):
    n = pl.program_id(0)
    keys = 2 * blk
    j = lax.broadcasted_iota(jnp.int32, (keys, 2 * blk), 0)
    tq = lax.broadcasted_iota(jnp.int32, (keys, 2 * blk), 1) % blk
    first_key = jnp.where(n == 0, blk, 0)
    mask = (j >= jnp.maximum(tq, first_key)) & (j <= tq + WINDOW)
    low = lax.broadcasted_iota(jnp.int32, (blk, LANES), 1) < HEAD_DIM
    low_rows = lax.broadcasted_iota(jnp.int32, (KV_W, blk), 0) < HEAD_DIM
    ones = jnp.ones((SUBLANES, keys), F32)

    chunks = range(N_CHUNKS)
    sinks = [jnp.concatenate([sink_ref[2 * c:2 * c + 1, :], sink_ref[2 * c + 1:2 * c + 2, :]], axis=1) for c in chunks]

    def rows_of(b):
        return pl.ds(b, blk, stride=n_batch)

    def score_stage(b):
        sel = rows_of(b)
        k = jnp.concatenate([kp_ref[sel, :], kc_ref[sel, :]], axis=0).astype(BF16)
        sts = []
        for c in chunks:
            qc = q_ref[c, sel, :]
            qp = jnp.concatenate([jnp.where(low, qc, 0.0), jnp.where(low, 0.0, qc)], axis=0).astype(BF16)
            sts.append(_scores(k, qp))
        return sts

    def value_stage(b, sts):
        sel = rows_of(b)
        v = jnp.concatenate([vp_ref[sel, :], vc_ref[sel, :]], axis=0)
        vt_aug = jnp.concatenate([v.T, ones], axis=0).astype(BF16)
        ms, ots = [], []
        for c in chunks:
            st = jnp.where(mask, sts[c], NEG_INF)
            m = jnp.maximum(jnp.max(st, axis=0, keepdims=True), sinks[c])
            ms.append(m)
            p = jnp.exp(st - m).astype(BF16)
            ots.append(jnp.dot(vt_aug, p, preferred_element_type=F32))
        for c in chunks:
            ot = ots[c]
            den = ot[KV_W:KV_W + 1, :] + jnp.exp(sinks[c] - ms[c])
            ot = ot[:KV_W, :] * (1.0 / den)
            ya = jnp.where(low_rows, ot[:, :blk], ot[:, blk:]).T
            za = za_ref[c, sel, :]
            o_ref[c, sel, :] = ya * (za * jax.nn.sigmoid(za))

    sts = score_stage(0)
    for b in range(n_batch):
        nxt = score_stage(b + 1) if b + 1 < n_batch else None
        value_stage(b, sts)
        sts = nxt


def attention_prompt(q, k, v, za, sink_rows, n_batch, blk):
    t_rows = k.shape[0]
    r = blk * n_batch
    cur = lambda w: pl.BlockSpec((r, w), lambda i: (i, 0))
    prev = lambda w: pl.BlockSpec((r, w), lambda i: (jnp.maximum(i - 1, 0), 0))
    chunked = pl.BlockSpec((N_CHUNKS, r, LANES), lambda i: (0, i, 0))
    return pl.pallas_call(
        functools.partial(_attn_prompt_kernel, n_batch=n_batch, blk=blk),
        grid=(t_rows // r,),
        in_specs=[chunked, prev(KV_W), cur(KV_W), prev(KV_W), cur(KV_W), chunked,
                  _full(sink_rows.shape)],
        out_specs=chunked,
        out_shape=jax.ShapeDtypeStruct((N_CHUNKS, t_rows, LANES), F32),
        compiler_params=_params(("parallel",)),
        name="attention_prompt",
    )(q, k, k, v, v, za, sink_rows)


def _attn_sample_kernel(q_ref, k_ref, v_ref, za_ref, ck_ref, cv_ref, sink_ref, o_ref, nk_ref, nv_ref,
                        *, n_batch, t_new, group):
    g = pl.program_id(0)
    win = ck_ref.shape[1]
    rows = N_Q_HEADS * t_new
    tq = lax.broadcasted_iota(jnp.int32, (rows, win), 0) % t_new
    j = lax.broadcasted_iota(jnp.int32, (rows, win), 1)
    mask_c = (tq + win - j >= 0) & (tq + win - j <= WINDOW)
    mask_n = (j <= tq) & (j < t_new)
    sink = sink_ref[...][:, :1]
    pad = jnp.zeros((win - t_new, KV_W), F32)

    def body(i, carry):
        sel = pl.ds(g * group + i, t_new, stride=n_batch)
        qp = _pad_queries([q_ref[c, sel, :] for c in range(N_CHUNKS)])
        kn = k_ref[sel, :]
        vn = v_ref[sel, :]
        kc = ck_ref[i]
        vc = cv_ref[i]
        s_c = jnp.where(mask_c, _scores(qp, kc), NEG_INF)
        s_n = jnp.where(mask_n, _scores(qp, jnp.concatenate([kn, pad], axis=0)), NEG_INF)
        m = jnp.maximum(jnp.maximum(jnp.max(s_c, axis=-1, keepdims=True), jnp.max(s_n, axis=-1, keepdims=True)), sink)
        p_c = jnp.exp(s_c - m)
        p_n = jnp.exp(s_n - m)
        den = jnp.sum(p_c, axis=-1, keepdims=True) + jnp.sum(p_n, axis=-1, keepdims=True) + jnp.exp(sink - m)
        o = (jnp.dot(p_c.astype(BF16), vc.astype(BF16), preferred_element_type=F32)
             + jnp.dot(p_n.astype(BF16), jnp.concatenate([vn, pad], axis=0).astype(BF16),
                       preferred_element_type=F32)) / den
        _gated_store(o_ref, za_ref, sel, _unpad_outputs(o, t_new))
        nk_ref[i, 0:win - t_new, :] = kc[t_new:win]
        nk_ref[i, win - t_new:win, :] = kn
        nv_ref[i, 0:win - t_new, :] = vc[t_new:win]
        nv_ref[i, win - t_new:win, :] = vn
        return carry

    lax.fori_loop(0, group, body, 0)


def attention_sample(q, k, v, za, cache_k, cache_v, sink_rows, n_batch, t_new, group):
    t_rows = k.shape[0]
    win = cache_k.shape[1]
    cache = pl.BlockSpec((group, win, KV_W), lambda i: (i, 0, 0))
    chunked = _full((N_CHUNKS, t_rows, LANES))
    return pl.pallas_call(
        functools.partial(_attn_sample_kernel, n_batch=n_batch, t_new=t_new, group=group),
        grid=(n_batch // group,),
        in_specs=[chunked, _full((t_rows, KV_W)), _full((t_rows, KV_W)), chunked,
                  cache, cache, _full(sink_rows.shape)],
        out_specs=[chunked, cache, cache],
        out_shape=[jax.ShapeDtypeStruct((N_CHUNKS, t_rows, LANES), F32),
                   jax.ShapeDtypeStruct(cache_k.shape, F32), jax.ShapeDtypeStruct(cache_v.shape, F32)],
        compiler_params=_params(("arbitrary",)),
        name="attention_sample",
    )(q, k, v, za, cache_k, cache_v, sink_rows)


def _ssm_kernel(u_ref, zs_ref, h0re_ref, h0im_ref, are_ref, aim_ref, wb_ref, wc_ref, d_ref, wg_ref, bg_ref,
                o_ref, hre_ref, him_ref, x_scr, *, n_batch, t_chunk, slab):
    step = pl.program_id(0)

    @pl.when(step == 0)
    def _():
        hre_ref[...] = h0re_ref[...]
        him_ref[...] = h0im_ref[...]

    u = u_ref[...]
    ub = u.astype(BF16)
    half_in = SSM_W // 2
    ys = []
    for s in range(2):
        x_scr[...] = jnp.dot(ub[:, s * half_in:(s + 1) * half_in], wb_ref[s], preferred_element_type=F32)
        for jb in range(HALF_STATE // slab):
            n_lo = s * HALF_STATE + jb * slab
            ar = jnp.broadcast_to(are_ref[:, n_lo:n_lo + slab], (SUBLANES, slab))
            ai = jnp.broadcast_to(aim_ref[:, n_lo:n_lo + slab], (SUBLANES, slab))
            re_lo = jb * slab
            im_lo = HALF_STATE + jb * slab

            def tile_body(i, carry):
                r0 = pl.multiple_of(i * SUBLANES, SUBLANES)

                def t_body(t, h):
                    hr, hi = h
                    row = pl.multiple_of(t * n_batch + r0, SUBLANES)
                    xr = x_scr[pl.ds(row, SUBLANES), re_lo:re_lo + slab]
                    xi = x_scr[pl.ds(row, SUBLANES), im_lo:im_lo + slab]
                    nr = ar * hr - ai * hi + xr
                    ni = ar * hi + ai * hr + xi
                    x_scr[pl.ds(row, SUBLANES), re_lo:re_lo + slab] = nr
                    x_scr[pl.ds(row, SUBLANES), im_lo:im_lo + slab] = ni
                    return nr, ni

                h0 = (hre_ref[pl.ds(r0, SUBLANES), n_lo:n_lo + slab], him_ref[pl.ds(r0, SUBLANES), n_lo:n_lo + slab])
                hr, hi = lax.fori_loop(0, t_chunk, t_body, h0, unroll=8)
                hre_ref[pl.ds(r0, SUBLANES), n_lo:n_lo + slab] = hr
                him_ref[pl.ds(r0, SUBLANES), n_lo:n_lo + slab] = hi
                return carry

            lax.fori_loop(0, n_batch // SUBLANES, tile_body, 0)
        ys.append(jnp.dot(x_scr[...].astype(BF16), wc_ref[s], preferred_element_type=F32))
    y = jnp.concatenate(ys, axis=1) + d_ref[...] * u
    y = jax.nn.gelu(y)
    gate = jax.nn.sigmoid(jnp.dot(y.astype(BF16), wg_ref[...], preferred_element_type=F32) + bg_ref[...])
    zs = zs_ref[...]
    o_ref[...] = y * gate * (zs * jax.nn.sigmoid(zs))


def ssm_branch(u, zs, h0_re, h0_im, abar_re, abar_im, w_b, w_c, d, w_glu, b_glu, n_batch, t_chunk):
    t_rows = u.shape[0]
    r = t_chunk * n_batch
    rows = pl.BlockSpec((r, SSM_W), lambda i: (i, 0))
    state = _full((n_batch, N_STATE))
    return pl.pallas_call(
        functools.partial(_ssm_kernel, n_batch=n_batch, t_chunk=t_chunk, slab=4 * LANES),
        grid=(t_rows // r,),
        in_specs=[rows, rows, state, state, _full((1, N_STATE)), _full((1, N_STATE)),
                  _full(w_b.shape), _full(w_c.shape), _full((1, SSM_W)), _full((SSM_W, SSM_W)), _full((1, SSM_W))],
        out_specs=[rows, state, state],
        out_shape=[jax.ShapeDtypeStruct((t_rows, SSM_W), F32),
                   jax.ShapeDtypeStruct((n_batch, N_STATE), F32), jax.ShapeDtypeStruct((n_batch, N_STATE), F32)],
        scratch_shapes=[pltpu.VMEM((r, 2 * HALF_STATE), F32)],
        compiler_params=_params(("arbitrary",)),
        name="ssm_branch",
    )(u, zs, h0_re, h0_im, abar_re, abar_im, w_b, w_c, d, w_glu, b_glu)


def _outproj_kernel(x_ref, ya_ref, ys_ref, gate_ref, w_ref, fg_ref, o_ref, *, n_batch, final):
    rows = x_ref.shape[0]
    ya = jnp.concatenate([ya_ref[c] for c in range(N_CHUNKS)], axis=1)
    mix = (jnp.dot(ya.astype(BF16), w_ref[0:ATTN_W, :], preferred_element_type=F32)
           + jnp.dot(ys_ref[...].astype(BF16), w_ref[ATTN_W:ATTN_W + SSM_W, :], preferred_element_type=F32))
    gated = mix.reshape(rows // n_batch, n_batch, D_MODEL) * gate_ref[...][None]
    x = x_ref[...] + gated.reshape(rows, D_MODEL)
    if final:
        ms = jnp.mean(x * x, axis=-1, keepdims=True)
        x = x * lax.rsqrt(ms + EPS) * fg_ref[...]
    o_ref[...] = x


def out_projection(x, ya, ys, gate, w_out, final_g, n_batch, block_rows, final):
    t_rows = x.shape[0]
    rows = lambda w: pl.BlockSpec((block_rows, w), lambda i: (i, 0))
    return pl.pallas_call(
        functools.partial(_outproj_kernel, n_batch=n_batch, final=final),
        grid=(t_rows // block_rows,),
        in_specs=[rows(D_MODEL), pl.BlockSpec((N_CHUNKS, block_rows, LANES), lambda i: (0, i, 0)), rows(SSM_W),
                  _full((n_batch, D_MODEL)),
                  _full(w_out.shape), _full((1, D_MODEL))],
        out_specs=rows(D_MODEL),
        out_shape=jax.ShapeDtypeStruct((t_rows, D_MODEL), F32),
        compiler_params=_params(("parallel",)),
        name="out_projection",
    )(x, ya, ys, gate, w_out, final_g)


def _rope_tables(pos, n_batch):
    half = ROT_DIM // 2
    inv = ROPE_THETA ** (-jnp.arange(half, dtype=F32) / half)
    ang = pos.astype(F32)[:, None] * inv[None, :]
    d = np.arange(LANES) % HEAD_DIM
    idx = d % half
    cos = jnp.where(d < ROT_DIM, jnp.cos(ang)[:, idx], 1.0)
    sin = jnp.sin(ang)[:, idx]
    sm = jnp.where(d < half, -sin, jnp.where(d < ROT_DIM, sin, 0.0))
    return jnp.repeat(cos, n_batch, axis=0), jnp.repeat(sm, n_batch, axis=0)


def _block_diag(blocks):
    g, a, b = blocks.shape
    eye = jnp.eye(g, dtype=bool)
    out = jnp.where(eye[:, None, :, None], blocks[:, :, None, :], 0.0)
    return out.reshape(g * a, g * b)


def _layer_weights(l, w_in, w_out, attn_sinks, bb_re, bb_im, ssm_c_re, ssm_c_im):
    qcols = np.concatenate([np.arange(h * HEAD_DIM, (h + 1) * HEAD_DIM) for h in HEAD_PERM])
    base_k = ATTN_W
    base_v = ATTN_W + KV_W
    base_za = ATTN_W + 2 * KV_W
    base_u = base_za + ATTN_W
    base_zs = base_u + SSM_W
    cols = np.concatenate([qcols, base_za + qcols, base_u + np.arange(SSM_W), base_zs + np.arange(SSM_W),
                           base_k + np.arange(KV_W), base_v + np.arange(KV_W)])
    w_in_l = w_in[l][:, cols].astype(BF16)
    w_out_l = w_out[l][np.concatenate([qcols, ATTN_W + np.arange(SSM_W)])].astype(BF16)
    sinks = attn_sinks[l][np.array(HEAD_PERM)]
    def b_full(bb):
        blocks = bb[l].reshape(SSM_GROUP_CH, N_SSM_GROUPS, SSM_STATE).transpose(1, 0, 2)
        return _block_diag(blocks)
    bre, bim = b_full(bb_re), b_full(bb_im)
    hin = SSM_W // 2
    w_b = jnp.stack([jnp.concatenate([bre[s * hin:(s + 1) * hin, s * HALF_STATE:(s + 1) * HALF_STATE],
                                      bim[s * hin:(s + 1) * hin, s * HALF_STATE:(s + 1) * HALF_STATE]], axis=1)
                     for s in range(2)]).astype(BF16)
    cre = _block_diag(ssm_c_re[l].transpose(0, 2, 1))
    cim = _block_diag(ssm_c_im[l].transpose(0, 2, 1))
    w_c = jnp.stack([jnp.concatenate([cre[s * HALF_STATE:(s + 1) * HALF_STATE, s * hin:(s + 1) * hin],
                                      -cim[s * HALF_STATE:(s + 1) * HALF_STATE, s * hin:(s + 1) * hin]], axis=0)
                     for s in range(2)]).astype(BF16)
    return w_in_l, w_out_l, sinks, w_b, w_c


def _sink_rows(sinks, t):
    return jnp.broadcast_to(jnp.repeat(sinks, t)[:, None], (N_Q_HEADS * t, LANES))


def kernel(x_prompt, x_sample, cache_k, cache_v, state_ssm_re, state_ssm_im, c_prompt, c_sample, norm_g, w_ada, b_ada, w_in, attn_sinks, ssm_a_re, ssm_a_im, ssm_log_dt, ssm_b_re, ssm_b_im, ssm_c_re, ssm_c_im, ssm_d, w_glu, b_glu, w_out, final_g):
    depth = w_in.shape[0]
    nb_p, seq, _ = x_prompt.shape
    nb_s, t_new, _ = x_sample.shape
    win = cache_k.shape[2]
    blk = WINDOW

    mod = adaln_mod(jnp.concatenate([c_prompt, c_sample], axis=0), w_ada, b_ada)
    abar_re, abar_im, bb_re, bb_im = ssm_discretise(ssm_a_re, ssm_a_im, ssm_log_dt, ssm_b_re, ssm_b_im)

    cos_p, sm_p = _rope_tables(jnp.arange(seq), nb_p)
    cos_s, sm_s = _rope_tables(PAST_LEN + jnp.arange(t_new), nb_s)

    xp = x_prompt.transpose(1, 0, 2).reshape(seq * nb_p, D_MODEL)
    xs = x_sample.transpose(1, 0, 2).reshape(t_new * nb_s, D_MODEL)
    zeros_state = jnp.zeros((nb_p, N_STATE), F32)
    final_g2 = final_g.reshape(1, D_MODEL)

    outs = {k: [] for k in ("kp", "vp", "rp", "ip", "ks", "vs", "rs", "is")}
    for l in range(depth):
        w_in_l, w_out_l, sinks, w_b, w_c = _layer_weights(l, w_in, w_out, attn_sinks, bb_re, bb_im, ssm_c_re, ssm_c_im)
        g_l = norm_g[l].reshape(1, D_MODEL)
        d_l = ssm_d[l].reshape(1, SSM_W)
        wg_l = w_glu[l].astype(BF16)
        bg_l = b_glu[l].reshape(1, SSM_W)
        final = l == depth - 1

        def split_mod(m):
            return m[:, :D_MODEL], m[:, D_MODEL:2 * D_MODEL], m[:, 2 * D_MODEL:]

        shift, scale, gate = split_mod(mod[l, :nb_p])
        q, k, v, za, u, zs = in_projection(xp, g_l, scale, shift, w_in_l, cos_p, sm_p, nb_p, 1024)
        ya = attention_prompt(q, k, v, za, _sink_rows(sinks, 1), nb_p, blk)
        ys, hre, him = ssm_branch(u, zs, zeros_state, zeros_state, abar_re[l], abar_im[l], w_b, w_c, d_l, wg_l, bg_l,
                                  nb_p, 64)
        xp = out_projection(xp, ya, ys, gate, w_out_l, final_g2, nb_p, 1024, final)
        tail = lambda a: a[(seq - win) * nb_p:].reshape(win, nb_p, N_KV_HEADS, HEAD_DIM).transpose(1, 0, 2, 3)
        outs["kp"].append(tail(k))
        outs["vp"].append(tail(v))
        outs["rp"].append(hre.reshape(nb_p, N_SSM_GROUPS, SSM_STATE))
        outs["ip"].append(him.reshape(nb_p, N_SSM_GROUPS, SSM_STATE))

        shift, scale, gate = split_mod(mod[l, nb_p:])
        q, k, v, za, u, zs = in_projection(xs, g_l, scale, shift, w_in_l, cos_s, sm_s, nb_s, t_new * nb_s)
        ya, nk, nv = attention_sample(q, k, v, za, cache_k[l].reshape(nb_s, win, KV_W),
                                      cache_v[l].reshape(nb_s, win, KV_W), _sink_rows(sinks, t_new), nb_s, t_new, 16)
        ys, hre, him = ssm_branch(u, zs, state_ssm_re[l].reshape(nb_s, N_STATE), state_ssm_im[l].reshape(nb_s, N_STATE),
                                  abar_re[l], abar_im[l], w_b, w_c, d_l, wg_l, bg_l, nb_s, t_new)
        xs = out_projection(xs, ya, ys, gate, w_out_l, final_g2, nb_s, t_new * nb_s, final)
        outs["ks"].append(nk.reshape(nb_s, win, N_KV_HEADS, HEAD_DIM))
        outs["vs"].append(nv.reshape(nb_s, win, N_KV_HEADS, HEAD_DIM))
        outs["rs"].append(hre.reshape(nb_s, N_SSM_GROUPS, SSM_STATE))
        outs["is"].append(him.reshape(nb_s, N_SSM_GROUPS, SSM_STATE))

    y_prompt = xp.reshape(seq, nb_p, D_MODEL).transpose(1, 0, 2)
    y_sample = xs.reshape(t_new, nb_s, D_MODEL).transpose(1, 0, 2)
    st = lambda key: jnp.stack(outs[key])
    return (y_prompt, y_sample, st("kp"), st("vp"), st("rp"), st("ip"), st("ks"), st("vs"), st("rs"), st("is"))
```

```python
import functools
from typing import NamedTuple

import jax
import jax.numpy as jnp
import numpy as np
from jax import lax
from jax.experimental import pallas as pl
from jax.experimental.pallas import tpu as pltpu

F32 = jnp.float32
BF16 = jnp.bfloat16

D_MODEL = 1024
HEAD_DIM = 64
ATTN_W = 512
N_Q_HEADS = 8
N_KV_HEADS = 2
KV_W = 128
WINDOW = 128
ROT_DIM = 16
ROPE_THETA = 500000.0
SSM_W = 512
SSM_GROUP_CH = 16
N_SSM_GROUPS = 32
SSM_STATE = 64
N_STATE = N_SSM_GROUPS * SSM_STATE
HALF_STATE = N_STATE // 2
EPS = 1e-6
NEG_INF = -1e30
PAST_LEN = 8192

LANES = 128
N_CHUNKS = ATTN_W // LANES
SSM_SUB_ROWS = 256
SSM_BUFFERS = 3
SUBLANES = 8
VMEM_LIMIT = 56 * 1024 * 1024

HEAD_PERM = (0, 4, 1, 5, 2, 6, 3, 7)


class Group(NamedTuple):
    n_batch: int
    mod_block: int


def _params(sem):
    return pltpu.CompilerParams(dimension_semantics=sem, vmem_limit_bytes=VMEM_LIMIT)


def _full(shape):
    return pl.BlockSpec(shape, lambda *_: (0,) * len(shape))


def _mod_kernel(c_ref, w_ref, b_ref, o_ref):
    c = c_ref[...]
    a = (c * jax.nn.sigmoid(c)).astype(BF16)
    o_ref[...] = jnp.dot(a, w_ref[...].astype(BF16), preferred_element_type=F32) + b_ref[...]


def adaln_mod(c_all, w_ada, b_ada):
    depth = w_ada.shape[0]
    n = c_all.shape[0]
    nj = 3 * D_MODEL // D_MODEL
    return pl.pallas_call(
        _mod_kernel,
        grid=(depth, nj),
        in_specs=[
            pl.BlockSpec((n, D_MODEL), lambda l, j: (0, 0)),
            pl.BlockSpec((None, D_MODEL, D_MODEL), lambda l, j: (l, 0, j)),
            pl.BlockSpec((None, 1, D_MODEL), lambda l, j: (l, 0, j)),
        ],
        out_specs=pl.BlockSpec((None, n, D_MODEL), lambda l, j: (l, 0, j)),
        out_shape=jax.ShapeDtypeStruct((depth, n, 3 * D_MODEL), F32),
        compiler_params=_params(("arbitrary", "arbitrary")),
        name="adaln_mod",
    )(c_all, w_ada, b_ada.reshape(depth, 1, 3 * D_MODEL))


def _disc_kernel(are_ref, aim_ref, ldt_ref, bre_ref, bim_ref, abre_ref, abim_ref, bbre_ref, bbim_ref):
    a_re = are_ref[...]
    a_im = aim_ref[...]
    dt = jnp.exp(ldt_ref[...])
    mag = jnp.exp(a_re * dt)
    abar_re = mag * jnp.cos(a_im * dt)
    abar_im = mag * jnp.sin(a_im * dt)
    den = a_re * a_re + a_im * a_im
    nr = abar_re - 1.0
    coef_re = (nr * a_re + abar_im * a_im) / den
    coef_im = (abar_im * a_re - nr * a_im) / den
    br = bre_ref[...]
    bi = bim_ref[...]
    abre_ref[...] = abar_re
    abim_ref[...] = abar_im
    bbre_ref[...] = coef_re * br - coef_im * bi
    bbim_ref[...] = coef_re * bi + coef_im * br


def ssm_discretise(a_re, a_im, log_dt, b_re, b_im):
    depth = a_re.shape[0]
    a_re = a_re.reshape(depth, 1, N_STATE)
    a_im = a_im.reshape(depth, 1, N_STATE)
    ldt = jnp.repeat(log_dt, SSM_STATE, axis=1).reshape(depth, 1, N_STATE)
    bt_re = b_re.reshape(depth, N_STATE, SSM_GROUP_CH).transpose(0, 2, 1)
    bt_im = b_im.reshape(depth, N_STATE, SSM_GROUP_CH).transpose(0, 2, 1)
    row = pl.BlockSpec((None, 1, N_STATE), lambda l: (l, 0, 0))
    mat = pl.BlockSpec((None, SSM_GROUP_CH, N_STATE), lambda l: (l, 0, 0))
    return pl.pallas_call(
        _disc_kernel,
        grid=(depth,),
        in_specs=[row, row, row, mat, mat],
        out_specs=[row, row, mat, mat],
        out_shape=[jax.ShapeDtypeStruct((depth, 1, N_STATE), F32)] * 2
        + [jax.ShapeDtypeStruct((depth, SSM_GROUP_CH, N_STATE), F32)] * 2,
        compiler_params=_params(("arbitrary",)),
        name="ssm_discretise",
    )(a_re, a_im, ldt, bt_re, bt_im)


def _rope(x, cos, sm):
    lane = lax.broadcasted_iota(jnp.int32, x.shape, 1) % HEAD_DIM
    partner = jnp.where(lane < ROT_DIM // 2, pltpu.roll(x, LANES - ROT_DIM // 2, 1), pltpu.roll(x, ROT_DIM // 2, 1))
    return x * cos + partner * sm


def _inproj_kernel(x_ref, g_ref, scale_ref, shift_ref, w_ref, cos_ref, sm_ref,
                   q_ref, k_ref, v_ref, za_ref, u_ref, zs_ref, *, n_batch):
    x = x_ref[...]
    rows = x.shape[0]
    ms = jnp.mean(x * x, axis=-1, keepdims=True)
    y = x * lax.rsqrt(ms + EPS) * g_ref[...]
    y = y.reshape(rows // n_batch, n_batch, D_MODEL) * (1.0 + scale_ref[...])[None] + shift_ref[...][None]
    h = y.reshape(rows, D_MODEL).astype(BF16)
    cos = cos_ref[...]
    sm = sm_ref[...]

    def proj(lo, width):
        return jnp.dot(h, w_ref[:, lo:lo + width], preferred_element_type=F32)

    q = proj(0, ATTN_W)
    za = proj(ATTN_W, ATTN_W)
    for c in range(N_CHUNKS):
        q_ref[c] = _rope(q[:, c * LANES:(c + 1) * LANES], cos, sm)
        za_ref[c] = za[:, c * LANES:(c + 1) * LANES]
    u_ref[...] = proj(2 * ATTN_W, SSM_W)
    zs_ref[...] = proj(2 * ATTN_W + SSM_W, SSM_W)
    k_ref[...] = _rope(proj(2 * ATTN_W + 2 * SSM_W, KV_W), cos, sm)
    v_ref[...] = proj(2 * ATTN_W + 2 * SSM_W + KV_W, KV_W)


def _layer_spec(arr, l):
    tail = arr.shape[1:]
    return pl.BlockSpec((None,) + tail, lambda *_: (l,) + (0,) * len(tail))


def _mod_spec(l, group, part):
    return pl.BlockSpec((None, group.n_batch, D_MODEL), lambda *_: (l, group.mod_block, part))


def in_projection(x, norm_g, mod, w_in, cos_t, sm_t, l, group, block_rows):
    t_rows = x.shape[0]
    rows = lambda w: pl.BlockSpec((block_rows, w), lambda i: (i, 0))
    chunked = pl.BlockSpec((N_CHUNKS, block_rows, LANES), lambda i: (0, i, 0))
    flat = lambda w: jax.ShapeDtypeStruct((t_rows, w), F32)
    chunked_shape = jax.ShapeDtypeStruct((N_CHUNKS, t_rows, LANES), F32)
    return pl.pallas_call(
        functools.partial(_inproj_kernel, n_batch=group.n_batch),
        grid=(t_rows // block_rows,),
        in_specs=[rows(D_MODEL), _layer_spec(norm_g, l), _mod_spec(l, group, 1), _mod_spec(l, group, 0),
                  _layer_spec(w_in, l), rows(LANES), rows(LANES)],
        out_specs=[chunked, rows(KV_W), rows(KV_W), chunked, rows(SSM_W), rows(SSM_W)],
        out_shape=[chunked_shape, flat(KV_W), flat(KV_W), chunked_shape, flat(SSM_W), flat(SSM_W)],
        compiler_params=_params(("parallel",)),
        name="in_projection",
    )(x, norm_g, mod, mod, w_in, cos_t, sm_t)


def _pad_queries(chunks):
    lane = lax.broadcasted_iota(jnp.int32, chunks[0].shape, 1)
    pieces = []
    for chunk in chunks:
        pieces.append(jnp.where(lane < HEAD_DIM, chunk, 0.0))
        pieces.append(jnp.where(lane >= HEAD_DIM, chunk, 0.0))
    return jnp.concatenate(pieces, axis=0).astype(BF16)


def _unpad_outputs(o, t):
    lane = lax.broadcasted_iota(jnp.int32, (t, LANES), 1)
    chunks = []
    for c in range(N_CHUNKS):
        lo = o[(2 * c) * t:(2 * c + 1) * t]
        hi = o[(2 * c + 1) * t:(2 * c + 2) * t]
        chunks.append(jnp.where(lane < HEAD_DIM, lo, hi))
    return chunks


def _gated_store(o_ref, za_ref, sel, ya_chunks):
    for c, ya in enumerate(ya_chunks):
        za = za_ref[c, sel, :]
        o_ref[c, sel, :] = ya * (za * jax.nn.sigmoid(za))


def _scores(qp, k):
    return lax.dot_general(qp, k.astype(BF16), (((1,), (1,)), ((), ())),
                           preferred_element_type=F32) * (HEAD_DIM ** -0.5)


def _attn_prompt_kernel(q_ref, kp_ref, kc_ref, vp_ref, vc_ref, za_ref, sink_ref, o_ref, *, n_batch, blk):
    n = pl.program_id(0)
    keys = 2 * blk
    j = lax.broadcasted_iota(jnp.int32, (keys, 2 * blk), 0)
    tq = lax.broadcasted_iota(jnp.int32, (keys, 2 * blk), 1) % blk
    first_key = jnp.where(n == 0, blk, 0)
    mask = (j >= jnp.maximum(tq, first_key)) & (j <= tq + WINDOW)
    low = lax.broadcasted_iota(jnp.int32, (blk, LANES), 1) < HEAD_DIM
    low_rows = lax.broadcasted_iota(jnp.int32, (KV_W, blk), 0) < HEAD_DIM
    ones = jnp.ones((SUBLANES, keys), F32)

    chunks = range(N_CHUNKS)
    sinks = [jnp.concatenate([sink_ref[2 * c:2 * c + 1, :], sink_ref[2 * c + 1:2 * c + 2, :]], axis=1) for c in chunks]

    def rows_of(b):
        return pl.ds(b, blk, stride=n_batch)

    def score_stage(b):
        sel = rows_of(b)
        k = jnp.concatenate([kp_ref[sel, :], kc_ref[sel, :]], axis=0).astype(BF16)
        sts = []
        for c in chunks:
            qc = q_ref[c, sel, :]
            qp = jnp.concatenate([jnp.where(low, qc, 0.0), jnp.where(low, 0.0, qc)], axis=0).astype(BF16)
            sts.append(_scores(k, qp))
        return sts

    def value_stage(b, sts):
        sel = rows_of(b)
        v = jnp.concatenate([vp_ref[sel, :], vc_ref[sel, :]], axis=0)
        vt_aug = jnp.concatenate([v.T, ones], axis=0).astype(BF16)
        ms, ots = [], []
        for c in chunks:
            st = jnp.where(mask, sts[c], NEG_INF)
            m = jnp.maximum(jnp.max(st, axis=0, keepdims=True), sinks[c])
            ms.append(m)
            p = jnp.exp(st - m).astype(BF16)
            ots.append(jnp.dot(vt_aug, p, preferred_element_type=F32))
        for c in chunks:
            ot = ots[c]
            den = ot[KV_W:KV_W + 1, :] + jnp.exp(sinks[c] - ms[c])
            ot = ot[:KV_W, :] * (1.0 / den)
            ya = jnp.where(low_rows, ot[:, :blk], ot[:, blk:]).T
            za = za_ref[c, sel, :]
            o_ref[c, sel, :] = ya * (za * jax.nn.sigmoid(za))

    sts = score_stage(0)
    for b in range(n_batch):
        nxt = score_stage(b + 1) if b + 1 < n_batch else None
        value_stage(b, sts)
        sts = nxt


def attention_prompt(q, k, v, za, sinks, l, n_batch, blk):
    t_rows = k.shape[0]
    r = blk * n_batch
    cur = lambda w: pl.BlockSpec((r, w), lambda i: (i, 0))
    prev = lambda w: pl.BlockSpec((r, w), lambda i: (jnp.maximum(i - 1, 0), 0))
    chunked = pl.BlockSpec((N_CHUNKS, r, LANES), lambda i: (0, i, 0))
    return pl.pallas_call(
        functools.partial(_attn_prompt_kernel, n_batch=n_batch, blk=blk),
        grid=(t_rows // r,),
        in_specs=[chunked, prev(KV_W), cur(KV_W), prev(KV_W), cur(KV_W), chunked, _layer_spec(sinks, l)],
        out_specs=chunked,
        out_shape=jax.ShapeDtypeStruct((N_CHUNKS, t_rows, LANES), F32),
        compiler_params=_params(("parallel",)),
        name="attention_prompt",
    )(q, k, k, v, v, za, sinks)


def _attn_sample_kernel(q_ref, k_ref, v_ref, za_ref, ck_ref, cv_ref, sink_ref, nk_all_ref, nv_all_ref,
                        o_ref, nk_ref, nv_ref, *, n_batch, t_new, group):
    del nk_all_ref, nv_all_ref
    g = pl.program_id(0)
    win = ck_ref.shape[1]
    rows = N_Q_HEADS * t_new
    tq = lax.broadcasted_iota(jnp.int32, (rows, 2 * win), 0) % t_new
    j = lax.broadcasted_iota(jnp.int32, (rows, 2 * win), 1)
    dist_c = tq + win - j
    mask = ((j < win) & (dist_c >= 0) & (dist_c <= WINDOW)) | ((j >= win) & (j - win <= tq) & (j - win < t_new))
    sink = sink_ref[...]
    pad = jnp.zeros((win - t_new, KV_W), F32)
    ones = jnp.ones((2 * win, LANES), BF16)
    batch = range(group)
    sels = [pl.ds(g * group + i, t_new, stride=n_batch) for i in batch]

    scores = []
    for i in batch:
        qp = _pad_queries([q_ref[c, sels[i], :] for c in range(N_CHUNKS)])
        keys = jnp.concatenate([ck_ref[i], k_ref[sels[i], :], pad], axis=0)
        scores.append(jnp.where(mask, _scores(qp, keys), NEG_INF))
    outs = []
    for i in batch:
        values = jnp.concatenate([cv_ref[i], v_ref[sels[i], :], pad], axis=0).astype(BF16)
        s = scores[i]
        m = jnp.maximum(jnp.max(s, axis=-1, keepdims=True), sink[:, :1])
        p = jnp.exp(s - m).astype(BF16)
        o = jnp.dot(p, jnp.concatenate([values, ones], axis=1), preferred_element_type=F32)
        outs.append(o[:, :KV_W] / (o[:, KV_W:] + jnp.exp(sink - m)))
    for i in batch:
        _gated_store(o_ref, za_ref, sels[i], _unpad_outputs(outs[i], t_new))
        nk_ref[i, 0:win - t_new, :] = ck_ref[i, t_new:win, :]
        nk_ref[i, win - t_new:win, :] = k_ref[sels[i], :]
        nv_ref[i, 0:win - t_new, :] = cv_ref[i, t_new:win, :]
        nv_ref[i, win - t_new:win, :] = v_ref[sels[i], :]


def attention_sample(q, k, v, za, cache_k, cache_v, sink_rows, new_k, new_v, l, n_batch, t_new, group):
    t_rows = k.shape[0]
    win = cache_k.shape[2]
    cache = pl.BlockSpec((None, group, win, KV_W), lambda i: (l, i, 0, 0))
    chunked = _full((N_CHUNKS, t_rows, LANES))
    in_place = pl.BlockSpec(memory_space=pl.ANY)
    return pl.pallas_call(
        functools.partial(_attn_sample_kernel, n_batch=n_batch, t_new=t_new, group=group),
        grid=(n_batch // group,),
        in_specs=[chunked, _full((t_rows, KV_W)), _full((t_rows, KV_W)), chunked,
                  cache, cache, _layer_spec(sink_rows, l), in_place, in_place],
        out_specs=[chunked, cache, cache],
        out_shape=[jax.ShapeDtypeStruct((N_CHUNKS, t_rows, LANES), F32),
                   jax.ShapeDtypeStruct(new_k.shape, F32), jax.ShapeDtypeStruct(new_v.shape, F32)],
        input_output_aliases={7: 1, 8: 2},
        compiler_params=_params(("arbitrary",)),
        name="attention_sample",
    )(q, k, v, za, cache_k, cache_v, sink_rows, new_k, new_v)


def _ssm_kernel(u_ref, zs_ref, h0re_ref, h0im_ref, are_ref, aim_ref, wb_ref, wc_ref, d_ref, wg_ref, bg_ref,
                o_ref, hre_ref, him_ref, x_scr, *, n_batch, sub_rows, slab):
    step = pl.program_id(0)

    @pl.when(step == 0)
    def _():
        hre_ref[...] = h0re_ref[...]
        him_ref[...] = h0im_ref[...]

    rows = u_ref.shape[0]
    half_in = SSM_W // 2
    n_sub = rows // sub_rows
    n_tiles = n_batch // SUBLANES
    t_steps = sub_rows // n_batch
    n_buf = x_scr.shape[0]
    items = [(sc, s) for sc in range(n_sub) for s in range(2)]

    def expand(i):
        sc, s = items[i]
        ub = u_ref[sc * sub_rows:(sc + 1) * sub_rows, s * half_in:(s + 1) * half_in].astype(BF16)
        x_scr[i % n_buf] = jnp.dot(ub, wb_ref[s], preferred_element_type=F32)

    def recur(i):
        sc, s = items[i]
        buf = i % n_buf
        for jb in range(HALF_STATE // slab):
            n_lo = s * HALF_STATE + jb * slab
            ar = jnp.broadcast_to(are_ref[:, n_lo:n_lo + slab], (SUBLANES, slab))
            ai = jnp.broadcast_to(aim_ref[:, n_lo:n_lo + slab], (SUBLANES, slab))
            re = slice(jb * slab, (jb + 1) * slab)
            im = slice(HALF_STATE + jb * slab, HALF_STATE + (jb + 1) * slab)
            for tile in range(n_tiles):
                r0 = tile * SUBLANES
                hr = hre_ref[r0:r0 + SUBLANES, n_lo:n_lo + slab]
                hi = him_ref[r0:r0 + SUBLANES, n_lo:n_lo + slab]
                for t in range(t_steps):
                    row = slice(t * n_batch + r0, t * n_batch + r0 + SUBLANES)
                    hr, hi = (ar * hr - ai * hi + x_scr[buf, row, re], ar * hi + ai * hr + x_scr[buf, row, im])
                    x_scr[buf, row, re] = hr
                    x_scr[buf, row, im] = hi
                hre_ref[r0:r0 + SUBLANES, n_lo:n_lo + slab] = hr
                him_ref[r0:r0 + SUBLANES, n_lo:n_lo + slab] = hi

    def contract(i):
        s = items[i][1]
        return jnp.dot(x_scr[i % n_buf].astype(BF16), wc_ref[s], preferred_element_type=F32)

    def finish(sc, y_halves):
        r = slice(sc * sub_rows, (sc + 1) * sub_rows)
        y = jnp.concatenate(y_halves, axis=1) + d_ref[...] * u_ref[r, :]
        y = jax.nn.gelu(y)
        gate = jax.nn.sigmoid(jnp.dot(y.astype(BF16), wg_ref[...], preferred_element_type=F32) + bg_ref[...])
        zs = zs_ref[r, :]
        o_ref[r, :] = y * gate * (zs * jax.nn.sigmoid(zs))

    n_items = len(items)
    ys = {}
    expand(0)
    for i in range(n_items + 1):
        if i + 1 < n_items:
            expand(i + 1)
        if i < n_items:
            recur(i)
        if i >= 1:
            sc, s = items[i - 1]
            ys[s] = contract(i - 1)
            if s == 1:
                finish(sc, [ys[0], ys[1]])


def ssm_branch(u, zs, h0_re, h0_im, abar_re, abar_im, w_b, w_c, d, w_glu, b_glu, l, l_state, n_batch, t_chunk):
    t_rows = u.shape[0]
    r = t_chunk * n_batch
    sub_rows = max(SSM_SUB_ROWS, n_batch)
    rows = pl.BlockSpec((r, SSM_W), lambda i: (i, 0))
    state = _full((n_batch, N_STATE))
    layer = lambda a: _layer_spec(a, l)
    return pl.pallas_call(
        functools.partial(_ssm_kernel, n_batch=n_batch, sub_rows=sub_rows, slab=4 * LANES),
        grid=(t_rows // r,),
        in_specs=[rows, rows, _layer_spec(h0_re, l_state), _layer_spec(h0_im, l_state), layer(abar_re), layer(abar_im),
                  layer(w_b), layer(w_c), layer(d), layer(w_glu), layer(b_glu)],
        out_specs=[rows, state, state],
        out_shape=[jax.ShapeDtypeStruct((t_rows, SSM_W), F32),
                   jax.ShapeDtypeStruct((n_batch, N_STATE), F32), jax.ShapeDtypeStruct((n_batch, N_STATE), F32)],
        scratch_shapes=[pltpu.VMEM((SSM_BUFFERS, sub_rows, 2 * HALF_STATE), F32)],
        compiler_params=_params(("arbitrary",)),
        name="ssm_branch",
    )(u, zs, h0_re, h0_im, abar_re, abar_im, w_b, w_c, d, w_glu, b_glu)


def _outproj_kernel(x_ref, ya_ref, ys_ref, gate_ref, w_ref, fg_ref, o_ref, *, n_batch, final):
    rows = x_ref.shape[0]
    ya = jnp.concatenate([ya_ref[c] for c in range(N_CHUNKS)], axis=1)
    mix = (jnp.dot(ya.astype(BF16), w_ref[0:ATTN_W, :], preferred_element_type=F32)
           + jnp.dot(ys_ref[...].astype(BF16), w_ref[ATTN_W:ATTN_W + SSM_W, :], preferred_element_type=F32))
    gated = mix.reshape(rows // n_batch, n_batch, D_MODEL) * gate_ref[...][None]
    x = x_ref[...] + gated.reshape(rows, D_MODEL)
    if final:
        ms = jnp.mean(x * x, axis=-1, keepdims=True)
        x = x * lax.rsqrt(ms + EPS) * fg_ref[...]
    o_ref[...] = x


def out_projection(x, ya, ys, mod, w_out, final_g, l, group, block_rows, final):
    t_rows = x.shape[0]
    rows = lambda w: pl.BlockSpec((block_rows, w), lambda i: (i, 0))
    return pl.pallas_call(
        functools.partial(_outproj_kernel, n_batch=group.n_batch, final=final),
        grid=(t_rows // block_rows,),
        in_specs=[rows(D_MODEL), pl.BlockSpec((N_CHUNKS, block_rows, LANES), lambda i: (0, i, 0)), rows(SSM_W),
                  _mod_spec(l, group, 2), _layer_spec(w_out, l), _full((1, D_MODEL))],
        out_specs=rows(D_MODEL),
        out_shape=jax.ShapeDtypeStruct((t_rows, D_MODEL), F32),
        compiler_params=_params(("parallel",)),
        name="out_projection",
    )(x, ya, ys, mod, w_out, final_g)


def _rope_tables(pos, n_batch):
    half = ROT_DIM // 2
    inv = ROPE_THETA ** (-jnp.arange(half, dtype=F32) / half)
    ang = pos.astype(F32)[:, None] * inv[None, :]
    d = np.arange(LANES) % HEAD_DIM
    idx = d % half
    cos = jnp.where(d < ROT_DIM, jnp.cos(ang)[:, idx], 1.0)
    sin = jnp.sin(ang)[:, idx]
    sm = jnp.where(d < half, -sin, jnp.where(d < ROT_DIM, sin, 0.0))
    return jnp.repeat(cos, n_batch, axis=0), jnp.repeat(sm, n_batch, axis=0)


def _half_block_diag(blocks):
    nl, g, a, b = blocks.shape
    hg = g // 2
    eye = jnp.eye(hg, dtype=bool)
    halves = blocks.reshape(nl, 2, hg, a, b)
    out = jnp.where(eye[None, None, :, None, :, None], halves[:, :, :, :, None, :], 0.0)
    return out.reshape(nl, 2, hg * a, hg * b)


def _prepare_weights(w_in, w_out, attn_sinks, bb_re, bb_im, ssm_c_re, ssm_c_im, t_new):
    depth = w_in.shape[0]
    qcols = np.concatenate([np.arange(h * HEAD_DIM, (h + 1) * HEAD_DIM) for h in HEAD_PERM])
    base_k = ATTN_W
    base_v = ATTN_W + KV_W
    base_za = ATTN_W + 2 * KV_W
    base_u = base_za + ATTN_W
    base_zs = base_u + SSM_W
    cols = np.concatenate([qcols, base_za + qcols, base_u + np.arange(SSM_W), base_zs + np.arange(SSM_W),
                           base_k + np.arange(KV_W), base_v + np.arange(KV_W)])
    w_in_p = w_in[:, :, cols].astype(BF16)
    w_out_p = w_out[:, np.concatenate([qcols, ATTN_W + np.arange(SSM_W)]), :].astype(BF16)
    sinks = attn_sinks[:, np.array(HEAD_PERM)]
    sink_lanes = jnp.broadcast_to(sinks[:, :, None], (depth, N_Q_HEADS, LANES))
    sink_rows = jnp.broadcast_to(jnp.repeat(sinks, t_new, axis=1)[:, :, None], (depth, N_Q_HEADS * t_new, LANES))
    to_blocks = lambda bb: bb.reshape(depth, SSM_GROUP_CH, N_SSM_GROUPS, SSM_STATE).transpose(0, 2, 1, 3)
    w_b = jnp.concatenate([_half_block_diag(to_blocks(bb_re)), _half_block_diag(to_blocks(bb_im))], axis=3).astype(BF16)
    w_c = jnp.concatenate([_half_block_diag(ssm_c_re.transpose(0, 1, 3, 2)),
                           -_half_block_diag(ssm_c_im.transpose(0, 1, 3, 2))], axis=2).astype(BF16)
    return w_in_p, w_out_p, sink_lanes, sink_rows, w_b, w_c


def kernel(x_prompt, x_sample, cache_k, cache_v, state_ssm_re, state_ssm_im, c_prompt, c_sample, norm_g, w_ada, b_ada, w_in, attn_sinks, ssm_a_re, ssm_a_im, ssm_log_dt, ssm_b_re, ssm_b_im, ssm_c_re, ssm_c_im, ssm_d, w_glu, b_glu, w_out, final_g):
    depth = w_in.shape[0]
    nb_p, seq, _ = x_prompt.shape
    nb_s, t_new, _ = x_sample.shape
    win = cache_k.shape[2]
    blk = WINDOW

    sample = Group(n_batch=nb_s, mod_block=0)
    prompt = Group(n_batch=nb_p, mod_block=nb_s // nb_p)
    mod = adaln_mod(jnp.concatenate([c_sample, c_prompt], axis=0), w_ada, b_ada)
    abar_re, abar_im, bb_re, bb_im = ssm_discretise(ssm_a_re, ssm_a_im, ssm_log_dt, ssm_b_re, ssm_b_im)
    w_in_p, w_out_p, sink_lanes, sink_rows, w_b, w_c = _prepare_weights(
        w_in, w_out, attn_sinks, bb_re, bb_im, ssm_c_re, ssm_c_im, t_new)
    norm_g3 = norm_g.reshape(depth, 1, D_MODEL)
    d3 = ssm_d.reshape(depth, 1, SSM_W)
    w_glu_b = w_glu.astype(BF16)
    b_glu3 = b_glu.reshape(depth, 1, SSM_W)
    final_g2 = final_g.reshape(1, D_MODEL)
    ssm_weights = (abar_re, abar_im, w_b, w_c, d3, w_glu_b, b_glu3)

    cos_p, sm_p = _rope_tables(jnp.arange(seq), nb_p)
    cos_s, sm_s = _rope_tables(PAST_LEN + jnp.arange(t_new), nb_s)

    xp = x_prompt.transpose(1, 0, 2).reshape(seq * nb_p, D_MODEL)
    xs = x_sample.transpose(1, 0, 2).reshape(t_new * nb_s, D_MODEL)
    zeros_state = jnp.zeros((1, nb_p, N_STATE), F32)
    cache_k4 = cache_k.reshape(depth, nb_s, win, KV_W)
    cache_v4 = cache_v.reshape(depth, nb_s, win, KV_W)
    state_re3 = state_ssm_re.reshape(depth, nb_s, N_STATE)
    state_im3 = state_ssm_im.reshape(depth, nb_s, N_STATE)
    new_k = jnp.zeros_like(cache_k4)
    new_v = jnp.zeros_like(cache_v4)

    outs = {k: [] for k in ("kp", "vp", "rp", "ip", "rs", "is")}
    for l in range(depth):
        final = l == depth - 1

        q, k, v, za, u, zs = in_projection(xp, norm_g3, mod, w_in_p, cos_p, sm_p, l, prompt, 1024)
        ya = attention_prompt(q, k, v, za, sink_lanes, l, nb_p, blk)
        ys, hre, him = ssm_branch(u, zs, zeros_state, zeros_state, *ssm_weights, l, 0, nb_p, 1024 // nb_p)
        xp = out_projection(xp, ya, ys, mod, w_out_p, final_g2, l, prompt, 1024, final)
        outs["kp"].append(k[(seq - win) * nb_p:])
        outs["vp"].append(v[(seq - win) * nb_p:])
        outs["rp"].append(hre)
        outs["ip"].append(him)

        q, k, v, za, u, zs = in_projection(xs, norm_g3, mod, w_in_p, cos_s, sm_s, l, sample, t_new * nb_s)
        ya, new_k, new_v = attention_sample(q, k, v, za, cache_k4, cache_v4, sink_rows, new_k, new_v, l,
                                            nb_s, t_new, 8)
        ys, hre, him = ssm_branch(u, zs, state_re3, state_im3, *ssm_weights, l, l, nb_s, t_new)
        xs = out_projection(xs, ya, ys, mod, w_out_p, final_g2, l, sample, t_new * nb_s, final)
        outs["rs"].append(hre)
        outs["is"].append(him)

    y_prompt = xp.reshape(seq, nb_p, D_MODEL).transpose(1, 0, 2)
    y_sample = xs.reshape(t_new, nb_s, D_MODEL).transpose(1, 0, 2)
    st = lambda key: jnp.stack(outs[key])
    window = lambda key: st(key).reshape(depth, win, nb_p, N_KV_HEADS, HEAD_DIM).transpose(0, 2, 1, 3, 4)
    state = lambda key, nb: st(key).reshape(depth, nb, N_SSM_GROUPS, SSM_STATE)
    return (y_prompt, y_sample, window("kp"), window("vp"), state("rp", nb_p), state("ip", nb_p),
            new_k.reshape(cache_k.shape), new_v.reshape(cache_v.shape), state("rs", nb_s), state("is", nb_s))
```

```python
import functools
from typing import NamedTuple

import jax
import jax.numpy as jnp
import numpy as np
from jax import lax
from jax.experimental import pallas as pl
from jax.experimental.pallas import tpu as pltpu

F32 = jnp.float32
BF16 = jnp.bfloat16

D_MODEL = 1024
HEAD_DIM = 64
ATTN_W = 512
N_Q_HEADS = 8
N_KV_HEADS = 2
Q_GROUP = N_Q_HEADS // N_KV_HEADS
KV_W = 128
WINDOW = 128
ROT_DIM = 16
ROPE_THETA = 500000.0
SSM_W = 512
SSM_GROUP_CH = 16
N_SSM_GROUPS = 32
SSM_STATE = 64
N_STATE = N_SSM_GROUPS * SSM_STATE
HALF_STATE = N_STATE // 2
EPS = 1e-6
NEG_INF = -1e30
PAST_LEN = 8192

LANES = 128
N_CHUNKS = ATTN_W // LANES
BOUNDARY_ROWS = 1024
SSM_ROWS = 1024
SAMPLE_GROUP = 8
SSM_SUB_ROWS = 256
SSM_BUFFERS = 3
SUBLANES = 8
VMEM_LIMIT = 56 * 1024 * 1024

HEAD_PERM = (0, 4, 1, 5, 2, 6, 3, 7)


class Group(NamedTuple):
    n_batch: int
    mod_block: int


def _params(sem):
    return pltpu.CompilerParams(dimension_semantics=sem, vmem_limit_bytes=VMEM_LIMIT)


def _full(shape):
    return pl.BlockSpec(shape, lambda *_: (0,) * len(shape))


def _mod_kernel(c_ref, w_ref, b_ref, o_ref):
    c = c_ref[...]
    a = (c * jax.nn.sigmoid(c)).astype(BF16)
    o_ref[...] = jnp.dot(a, w_ref[...].astype(BF16), preferred_element_type=F32) + b_ref[...]


def adaln_mod(c_all, w_ada, b_ada):
    depth = w_ada.shape[0]
    n = c_all.shape[0]
    nj = 3 * D_MODEL // D_MODEL
    return pl.pallas_call(
        _mod_kernel,
        grid=(depth, nj),
        in_specs=[
            pl.BlockSpec((n, D_MODEL), lambda l, j: (0, 0)),
            pl.BlockSpec((None, D_MODEL, D_MODEL), lambda l, j: (l, 0, j)),
            pl.BlockSpec((None, 1, D_MODEL), lambda l, j: (l, 0, j)),
        ],
        out_specs=pl.BlockSpec((None, n, D_MODEL), lambda l, j: (l, 0, j)),
        out_shape=jax.ShapeDtypeStruct((depth, n, 3 * D_MODEL), F32),
        compiler_params=_params(("arbitrary", "arbitrary")),
        name="adaln_mod",
    )(c_all, w_ada, b_ada.reshape(depth, 1, 3 * D_MODEL))


def _disc_kernel(are_ref, aim_ref, ldt_ref, bre_ref, bim_ref, abre_ref, abim_ref, bbre_ref, bbim_ref):
    a_re = are_ref[...]
    a_im = aim_ref[...]
    dt = jnp.exp(ldt_ref[...])
    mag = jnp.exp(a_re * dt)
    abar_re = mag * jnp.cos(a_im * dt)
    abar_im = mag * jnp.sin(a_im * dt)
    den = a_re * a_re + a_im * a_im
    nr = abar_re - 1.0
    coef_re = (nr * a_re + abar_im * a_im) / den
    coef_im = (abar_im * a_re - nr * a_im) / den
    br = bre_ref[...]
    bi = bim_ref[...]
    abre_ref[...] = abar_re
    abim_ref[...] = abar_im
    bbre_ref[...] = coef_re * br - coef_im * bi
    bbim_ref[...] = coef_re * bi + coef_im * br


def ssm_discretise(a_re, a_im, log_dt, b_re, b_im):
    depth = a_re.shape[0]
    a_re = a_re.reshape(depth, 1, N_STATE)
    a_im = a_im.reshape(depth, 1, N_STATE)
    ldt = jnp.repeat(log_dt, SSM_STATE, axis=1).reshape(depth, 1, N_STATE)
    bt_re = b_re.reshape(depth, N_STATE, SSM_GROUP_CH).transpose(0, 2, 1)
    bt_im = b_im.reshape(depth, N_STATE, SSM_GROUP_CH).transpose(0, 2, 1)
    row = pl.BlockSpec((None, 1, N_STATE), lambda l: (l, 0, 0))
    mat = pl.BlockSpec((None, SSM_GROUP_CH, N_STATE), lambda l: (l, 0, 0))
    return pl.pallas_call(
        _disc_kernel,
        grid=(depth,),
        in_specs=[row, row, row, mat, mat],
        out_specs=[row, row, mat, mat],
        out_shape=[jax.ShapeDtypeStruct((depth, 1, N_STATE), F32)] * 2
        + [jax.ShapeDtypeStruct((depth, SSM_GROUP_CH, N_STATE), F32)] * 2,
        compiler_params=_params(("arbitrary",)),
        name="ssm_discretise",
    )(a_re, a_im, ldt, bt_re, bt_im)


def _rope(x, cos, sm):
    lane = lax.broadcasted_iota(jnp.int32, x.shape, 1) % HEAD_DIM
    partner = jnp.where(lane < ROT_DIM // 2, pltpu.roll(x, LANES - ROT_DIM // 2, 1), pltpu.roll(x, ROT_DIM // 2, 1))
    return x * cos + partner * sm


def _to_time_major(x_ref, scr, n_batch):
    steps = x_ref.shape[1]
    for b in range(n_batch):
        for c in range(D_MODEL // LANES):
            scr[c, pl.ds(b, steps, stride=n_batch), :] = x_ref[b, :, c * LANES:(c + 1) * LANES]
    return jnp.concatenate([scr[c] for c in range(D_MODEL // LANES)], axis=1)


def _from_time_major(x, o_ref, scr, n_batch):
    steps = o_ref.shape[1]
    for c in range(D_MODEL // LANES):
        scr[c] = x[:, c * LANES:(c + 1) * LANES]
    for b in range(n_batch):
        for c in range(D_MODEL // LANES):
            o_ref[b, :, c * LANES:(c + 1) * LANES] = scr[c, pl.ds(b, steps, stride=n_batch), :]


def _rms(x, g_ref):
    ms = jnp.mean(x * x, axis=-1, keepdims=True)
    return x * lax.rsqrt(ms + EPS) * g_ref[...]


def _per_sequence(y, n_batch, fn):
    rows = y.shape[0]
    return fn(y.reshape(rows // n_batch, n_batch, D_MODEL)).reshape(rows, D_MODEL)


def _residual_update(x, ya_ref, ys_ref, gate_ref, w_ref, n_batch):
    mixed = jnp.concatenate([ya_ref[c].astype(BF16) for c in range(N_CHUNKS)] + [ys_ref[...]], axis=1)
    mix = jnp.dot(mixed, w_ref[...], preferred_element_type=F32)
    return x + _per_sequence(mix, n_batch, lambda m: m * gate_ref[...][None])


def _project(x, g_ref, scale_ref, shift_ref, w_ref, cos_ref, sm_ref, q_ref, k_ref, v_ref, za_ref, u_ref, zs_ref,
             n_batch):
    y = _per_sequence(_rms(x, g_ref), n_batch, lambda y3: y3 * (1.0 + scale_ref[...])[None] + shift_ref[...][None])
    h = y.astype(BF16)
    cos = cos_ref[...]
    sm = sm_ref[...]

    def proj(lo, width):
        return jnp.dot(h, w_ref[:, lo:lo + width], preferred_element_type=F32)

    q = proj(0, ATTN_W)
    za = proj(ATTN_W, ATTN_W)
    for c in range(N_CHUNKS):
        q_ref[c] = _rope(q[:, c * LANES:(c + 1) * LANES], cos, sm)
        za_ref[c] = za[:, c * LANES:(c + 1) * LANES]
    u_ref[...] = proj(2 * ATTN_W, SSM_W)
    zs_ref[...] = proj(2 * ATTN_W + SSM_W, SSM_W)
    kv = proj(2 * ATTN_W + 2 * SSM_W, 2 * KV_W)
    k_ref[...] = _rope(kv[:, :KV_W], cos, sm)
    v_ref[...] = kv[:, KV_W:]


def _boundary_kernel(*refs, n_batch, closes, opens, batch_major):
    refs = list(refs)
    x_ref = refs.pop(0)
    close_refs = [refs.pop(0) for _ in range(4)] if closes else None
    final_g_ref = refs.pop(0) if closes and not opens else None
    open_refs = [refs.pop(0) for _ in range(6)] if opens else None
    x_out_ref = refs.pop(0) if closes or batch_major else None
    proj_out_refs = [refs.pop(0) for _ in range(6)] if opens else None
    scr = refs.pop(0) if batch_major else None

    x = _to_time_major(x_ref, scr, n_batch) if (batch_major and not closes) else x_ref[...]
    if closes:
        x = _residual_update(x, *close_refs, n_batch)
    if opens:
        if x_out_ref is not None:
            x_out_ref[...] = x
        _project(x, *open_refs, *proj_out_refs, n_batch)
    else:
        x = _rms(x, final_g_ref)
        if batch_major:
            _from_time_major(x, x_out_ref, scr, n_batch)
        else:
            x_out_ref[...] = x


def _layer_spec(arr, l):
    tail = arr.shape[1:]
    return pl.BlockSpec((None,) + tail, lambda *_: (l,) + (0,) * len(tail))


def _mod_spec(l, group, part):
    return pl.BlockSpec((None, group.n_batch, D_MODEL), lambda *_: (l, group.mod_block, part))


def layer_boundary(x, mod, group, block_rows, close=None, open_=None, final_g=None, batch_major_out=False):
    closes, opens = close is not None, open_ is not None
    batch_major_in = x.ndim == 3
    assert not (batch_major_in and closes) and not (batch_major_out and opens)
    batch_major = batch_major_in or batch_major_out
    t_rows = x.shape[0] * x.shape[1] if batch_major_in else x.shape[0]
    rows = lambda w: pl.BlockSpec((block_rows, w), lambda i: (i, 0))
    chunked = pl.BlockSpec((N_CHUNKS, block_rows, LANES), lambda i: (0, i, 0))
    blocked3 = pl.BlockSpec((group.n_batch, block_rows // group.n_batch, D_MODEL), lambda i: (0, i, 0))
    flat = lambda w: jax.ShapeDtypeStruct((t_rows, w), F32)
    chunked_shape = jax.ShapeDtypeStruct((N_CHUNKS, t_rows, LANES), F32)

    args, in_specs, out_specs, out_shape = [x], [blocked3 if batch_major_in else rows(D_MODEL)], [], []
    if closes:
        ya, ys, w_out, l = close
        args += [ya, ys, mod, w_out]
        in_specs += [chunked, rows(SSM_W), _mod_spec(l, group, 2), _layer_spec(w_out, l)]
        if not opens:
            args.append(final_g)
            in_specs.append(_full(final_g.shape))
    if opens:
        norm_g, w_in, cos_t, sm_t, l = open_
        args += [norm_g, mod, mod, w_in, cos_t, sm_t]
        in_specs += [_layer_spec(norm_g, l), _mod_spec(l, group, 1), _mod_spec(l, group, 0), _layer_spec(w_in, l),
                     rows(LANES), rows(LANES)]
    if closes or batch_major_in:
        out_specs.append(blocked3 if batch_major_out else rows(D_MODEL))
        out_shape.append(jax.ShapeDtypeStruct((group.n_batch, t_rows // group.n_batch, D_MODEL), F32)
                         if batch_major_out else flat(D_MODEL))
    if opens:
        out_specs += [chunked, rows(KV_W), rows(KV_W), chunked, rows(SSM_W), rows(SSM_W)]
        out_shape += [chunked_shape, flat(KV_W), flat(KV_W), chunked_shape, flat(SSM_W), flat(SSM_W)]
    scratch = [pltpu.VMEM((D_MODEL // LANES, block_rows, LANES), F32)] if batch_major else []
    outs = pl.pallas_call(
        functools.partial(_boundary_kernel, n_batch=group.n_batch, closes=closes, opens=opens,
                          batch_major=batch_major),
        grid=(t_rows // block_rows,),
        in_specs=in_specs,
        out_specs=out_specs,
        out_shape=out_shape,
        scratch_shapes=scratch,
        compiler_params=_params(("parallel",)),
        name="layer_boundary",
    )(*args)
    x_rows = outs[0] if (closes or batch_major_in) else x
    return x_rows, (tuple(outs[-6:]) if opens else None)


def _pad_queries(chunks):
    lane = lax.broadcasted_iota(jnp.int32, chunks[0].shape, 1)
    pieces = []
    for chunk in chunks:
        pieces.append(jnp.where(lane < HEAD_DIM, chunk, 0.0))
        pieces.append(jnp.where(lane >= HEAD_DIM, chunk, 0.0))
    return jnp.concatenate(pieces, axis=0).astype(BF16)


def _unpad_outputs(o, t):
    lane = lax.broadcasted_iota(jnp.int32, (t, LANES), 1)
    chunks = []
    for c in range(N_CHUNKS):
        lo = o[(2 * c) * t:(2 * c + 1) * t]
        hi = o[(2 * c + 1) * t:(2 * c + 2) * t]
        chunks.append(jnp.where(lane < HEAD_DIM, lo, hi))
    return chunks


def _gated_store(o_ref, za_ref, sel, ya_chunks):
    for c, ya in enumerate(ya_chunks):
        za = za_ref[c, sel, :]
        o_ref[c, sel, :] = ya * (za * jax.nn.sigmoid(za))


def _scores(qp, k):
    return lax.dot_general(qp, k.astype(BF16), (((1,), (1,)), ((), ())),
                           preferred_element_type=F32) * (HEAD_DIM ** -0.5)


def _attn_prompt_kernel(q_ref, kp_ref, kc_ref, vp_ref, vc_ref, za_ref, sink_ref, o_ref, *, n_batch, blk):
    n = pl.program_id(0)
    keys = 2 * blk
    j = lax.broadcasted_iota(jnp.int32, (keys, 2 * blk), 0)
    tq = lax.broadcasted_iota(jnp.int32, (keys, 2 * blk), 1) % blk
    first_key = jnp.where(n == 0, blk, 0)
    mask = (j >= jnp.maximum(tq, first_key)) & (j <= tq + WINDOW)
    low = lax.broadcasted_iota(jnp.int32, (blk, LANES), 1) < HEAD_DIM
    low_rows = lax.broadcasted_iota(jnp.int32, (KV_W, blk), 0) < HEAD_DIM
    ones = jnp.ones((SUBLANES, keys), F32)

    chunks = range(N_CHUNKS)
    sinks = [jnp.concatenate([sink_ref[2 * c:2 * c + 1, :], sink_ref[2 * c + 1:2 * c + 2, :]], axis=1) for c in chunks]

    def rows_of(b):
        return pl.ds(b, blk, stride=n_batch)

    def score_stage(b):
        sel = rows_of(b)
        k = jnp.concatenate([kp_ref[sel, :], kc_ref[sel, :]], axis=0).astype(BF16)
        sts = []
        for c in chunks:
            qc = q_ref[c, sel, :]
            qp = jnp.concatenate([jnp.where(low, qc, 0.0), jnp.where(low, 0.0, qc)], axis=0).astype(BF16)
            sts.append(_scores(k, qp))
        return sts

    def value_stage(b, sts):
        sel = rows_of(b)
        v = jnp.concatenate([vp_ref[sel, :], vc_ref[sel, :]], axis=0)
        vt_aug = jnp.concatenate([v.T, ones], axis=0).astype(BF16)
        ms, ots = [], []
        for c in chunks:
            st = jnp.where(mask, sts[c], NEG_INF)
            m = jnp.maximum(jnp.max(st, axis=0, keepdims=True), sinks[c])
            ms.append(m)
            p = jnp.exp(st - m).astype(BF16)
            ots.append(jnp.dot(vt_aug, p, preferred_element_type=F32))
        for c in chunks:
            ot = ots[c]
            den = ot[KV_W:KV_W + 1, :] + jnp.exp(sinks[c] - ms[c])
            ot = ot[:KV_W, :] * (1.0 / den)
            ya = jnp.where(low_rows, ot[:, :blk], ot[:, blk:]).T
            za = za_ref[c, sel, :]
            o_ref[c, sel, :] = ya * (za * jax.nn.sigmoid(za))

    sts = score_stage(0)
    for b in range(n_batch):
        nxt = score_stage(b + 1) if b + 1 < n_batch else None
        value_stage(b, sts)
        sts = nxt


def attention_prompt(q, k, v, za, sinks, l, n_batch, blk):
    t_rows = k.shape[0]
    r = blk * n_batch
    cur = lambda w: pl.BlockSpec((r, w), lambda i: (i, 0))
    prev = lambda w: pl.BlockSpec((r, w), lambda i: (jnp.maximum(i - 1, 0), 0))
    chunked = pl.BlockSpec((N_CHUNKS, r, LANES), lambda i: (0, i, 0))
    return pl.pallas_call(
        functools.partial(_attn_prompt_kernel, n_batch=n_batch, blk=blk),
        grid=(t_rows // r,),
        in_specs=[chunked, prev(KV_W), cur(KV_W), prev(KV_W), cur(KV_W), chunked, _layer_spec(sinks, l)],
        out_specs=chunked,
        out_shape=jax.ShapeDtypeStruct((N_CHUNKS, t_rows, LANES), F32),
        compiler_params=_params(("parallel",)),
        name="attention_prompt",
    )(q, k, k, v, v, za, sinks)


def _attn_sample_kernel(q_ref, k_ref, v_ref, za_ref, ck_ref, cv_ref, sink_ref, nk_all_ref, nv_all_ref,
                        o_ref, nk_ref, nv_ref, *, n_batch, t_new, group):
    del nk_all_ref, nv_all_ref
    g = pl.program_id(0)
    win = ck_ref.shape[1]
    rows = N_Q_HEADS * t_new
    tq = lax.broadcasted_iota(jnp.int32, (rows, 2 * win), 0) % t_new
    j = lax.broadcasted_iota(jnp.int32, (rows, 2 * win), 1)
    dist_c = tq + win - j
    mask = ((j < win) & (dist_c >= 0) & (dist_c <= WINDOW)) | ((j >= win) & (j - win <= tq) & (j - win < t_new))
    sink = sink_ref[...]
    pad = jnp.zeros((win - t_new, KV_W), F32)
    ones = jnp.ones((2 * win, LANES), BF16)
    batch = range(group)
    sels = [pl.ds(g * group + i, t_new, stride=n_batch) for i in batch]

    scores = []
    for i in batch:
        qp = _pad_queries([q_ref[c, sels[i], :] for c in range(N_CHUNKS)])
        keys = jnp.concatenate([ck_ref[i], k_ref[sels[i], :], pad], axis=0)
        scores.append(jnp.where(mask, _scores(qp, keys), NEG_INF))
    outs = []
    for i in batch:
        values = jnp.concatenate([cv_ref[i], v_ref[sels[i], :], pad], axis=0).astype(BF16)
        s = scores[i]
        m = jnp.maximum(jnp.max(s, axis=-1, keepdims=True), sink[:, :1])
        p = jnp.exp(s - m).astype(BF16)
        o = jnp.dot(p, jnp.concatenate([values, ones], axis=1), preferred_element_type=F32)
        outs.append(o[:, :KV_W] / (o[:, KV_W:] + jnp.exp(sink - m)))
    for i in batch:
        _gated_store(o_ref, za_ref, sels[i], _unpad_outputs(outs[i], t_new))
        nk_ref[i, 0:win - t_new, :] = ck_ref[i, t_new:win, :]
        nk_ref[i, win - t_new:win, :] = k_ref[sels[i], :]
        nv_ref[i, 0:win - t_new, :] = cv_ref[i, t_new:win, :]
        nv_ref[i, win - t_new:win, :] = v_ref[sels[i], :]


def attention_sample(q, k, v, za, cache_k, cache_v, sink_rows, new_k, new_v, l, n_batch, t_new, group):
    t_rows = k.shape[0]
    win = cache_k.shape[2]
    cache = pl.BlockSpec((None, group, win, KV_W), lambda i: (l, i, 0, 0))
    chunked = _full((N_CHUNKS, t_rows, LANES))
    in_place = pl.BlockSpec(memory_space=pl.ANY)
    return pl.pallas_call(
        functools.partial(_attn_sample_kernel, n_batch=n_batch, t_new=t_new, group=group),
        grid=(n_batch // group,),
        in_specs=[chunked, _full((t_rows, KV_W)), _full((t_rows, KV_W)), chunked,
                  cache, cache, _layer_spec(sink_rows, l), in_place, in_place],
        out_specs=[chunked, cache, cache],
        out_shape=[jax.ShapeDtypeStruct((N_CHUNKS, t_rows, LANES), F32),
                   jax.ShapeDtypeStruct(new_k.shape, F32), jax.ShapeDtypeStruct(new_v.shape, F32)],
        input_output_aliases={7: 1, 8: 2},
        compiler_params=_params(("arbitrary",)),
        name="attention_sample",
    )(q, k, v, za, cache_k, cache_v, sink_rows, new_k, new_v)


def _ssm_kernel(u_ref, zs_ref, h0re_ref, h0im_ref, are_ref, aim_ref, wb_ref, wc_ref, d_ref, wg_ref, bg_ref,
                o_ref, hre_ref, him_ref, x_scr, *, n_batch, sub_rows, slab):
    step = pl.program_id(0)

    @pl.when(step == 0)
    def _():
        hre_ref[...] = h0re_ref[...]
        him_ref[...] = h0im_ref[...]

    rows = u_ref.shape[0]
    half_in = SSM_W // 2
    n_sub = rows // sub_rows
    n_tiles = n_batch // SUBLANES
    t_steps = sub_rows // n_batch
    n_buf = x_scr.shape[0]
    items = [(sc, s) for sc in range(n_sub) for s in range(2)]

    def expand(i):
        sc, s = items[i]
        ub = u_ref[sc * sub_rows:(sc + 1) * sub_rows, s * half_in:(s + 1) * half_in].astype(BF16)
        x_scr[i % n_buf] = jnp.dot(ub, wb_ref[s], preferred_element_type=F32)

    def recur(i):
        sc, s = items[i]
        buf = i % n_buf
        for jb in range(HALF_STATE // slab):
            n_lo = s * HALF_STATE + jb * slab
            ar = jnp.broadcast_to(are_ref[:, n_lo:n_lo + slab], (SUBLANES, slab))
            ai = jnp.broadcast_to(aim_ref[:, n_lo:n_lo + slab], (SUBLANES, slab))
            re = slice(jb * slab, (jb + 1) * slab)
            im = slice(HALF_STATE + jb * slab, HALF_STATE + (jb + 1) * slab)
            for tile in range(n_tiles):
                r0 = tile * SUBLANES
                hr = hre_ref[r0:r0 + SUBLANES, n_lo:n_lo + slab]
                hi = him_ref[r0:r0 + SUBLANES, n_lo:n_lo + slab]
                for t in range(t_steps):
                    row = slice(t * n_batch + r0, t * n_batch + r0 + SUBLANES)
                    hr, hi = (ar * hr - ai * hi + x_scr[buf, row, re], ar * hi + ai * hr + x_scr[buf, row, im])
                    x_scr[buf, row, re] = hr
                    x_scr[buf, row, im] = hi
                hre_ref[r0:r0 + SUBLANES, n_lo:n_lo + slab] = hr
                him_ref[r0:r0 + SUBLANES, n_lo:n_lo + slab] = hi

    def contract(i):
        s = items[i][1]
        return jnp.dot(x_scr[i % n_buf].astype(BF16), wc_ref[s], preferred_element_type=F32)

    def finish(sc, y_halves):
        r = slice(sc * sub_rows, (sc + 1) * sub_rows)
        y = jnp.concatenate(y_halves, axis=1) + d_ref[...] * u_ref[r, :]
        y = jax.nn.gelu(y)
        gate = jax.nn.sigmoid(jnp.dot(y.astype(BF16), wg_ref[...], preferred_element_type=F32) + bg_ref[...])
        zs = zs_ref[r, :]
        o_ref[r, :] = (y * gate * (zs * jax.nn.sigmoid(zs))).astype(o_ref.dtype)

    n_items = len(items)
    ys = {}
    expand(0)
    for i in range(n_items + 1):
        if i + 1 < n_items:
            expand(i + 1)
        if i < n_items:
            recur(i)
        if i >= 1:
            sc, s = items[i - 1]
            ys[s] = contract(i - 1)
            if s == 1:
                finish(sc, [ys[0], ys[1]])


def ssm_branch(u, zs, h0_re, h0_im, abar_re, abar_im, w_b, w_c, d, w_glu, b_glu, l, l_state, n_batch, t_chunk):
    t_rows = u.shape[0]
    r = t_chunk * n_batch
    sub_rows = max(SSM_SUB_ROWS, n_batch)
    rows = pl.BlockSpec((r, SSM_W), lambda i: (i, 0))
    state = _full((n_batch, N_STATE))
    layer = lambda a: _layer_spec(a, l)
    return pl.pallas_call(
        functools.partial(_ssm_kernel, n_batch=n_batch, sub_rows=sub_rows, slab=4 * LANES),
        grid=(t_rows // r,),
        in_specs=[rows, rows, _layer_spec(h0_re, l_state), _layer_spec(h0_im, l_state), layer(abar_re), layer(abar_im),
                  layer(w_b), layer(w_c), layer(d), layer(w_glu), layer(b_glu)],
        out_specs=[rows, state, state],
        out_shape=[jax.ShapeDtypeStruct((t_rows, SSM_W), BF16),
                   jax.ShapeDtypeStruct((n_batch, N_STATE), F32), jax.ShapeDtypeStruct((n_batch, N_STATE), F32)],
        scratch_shapes=[pltpu.VMEM((SSM_BUFFERS, sub_rows, 2 * HALF_STATE), F32)],
        compiler_params=_params(("arbitrary",)),
        name="ssm_branch",
    )(u, zs, h0_re, h0_im, abar_re, abar_im, w_b, w_c, d, w_glu, b_glu)


def _rope_tables(pos, n_batch):
    half = ROT_DIM // 2
    inv = ROPE_THETA ** (-jnp.arange(half, dtype=F32) / half)
    ang = pos.astype(F32)[:, None] * inv[None, :]
    d = np.arange(LANES) % HEAD_DIM
    idx = d % half
    cos = jnp.where(d < ROT_DIM, jnp.cos(ang)[:, idx], 1.0)
    sin = jnp.sin(ang)[:, idx]
    sm = jnp.where(d < half, -sin, jnp.where(d < ROT_DIM, sin, 0.0))
    return jnp.repeat(cos, n_batch, axis=0), jnp.repeat(sm, n_batch, axis=0)


def _half_block_diag(blocks):
    nl, g, a, b = blocks.shape
    hg = g // 2
    eye = jnp.eye(hg, dtype=bool)
    halves = blocks.reshape(nl, 2, hg, a, b)
    out = jnp.where(eye[None, None, :, None, :, None], halves[:, :, :, :, None, :], 0.0)
    return out.reshape(nl, 2, hg * a, hg * b)


def _prepare_weights(w_in, w_out, attn_sinks, bb_re, bb_im, ssm_c_re, ssm_c_im, t_new):
    depth = w_in.shape[0]
    assert HEAD_PERM == tuple(kv * Q_GROUP + g for g in range(Q_GROUP) for kv in range(N_KV_HEADS))

    def permute_heads(w, axis):
        shape = w.shape
        w = w.reshape(shape[:axis] + (N_KV_HEADS, Q_GROUP, HEAD_DIM) + shape[axis + 1:])
        return jnp.swapaxes(w, axis, axis + 1).reshape(shape)

    base_za = ATTN_W + 2 * KV_W
    w_in_p = jnp.concatenate([permute_heads(w_in[:, :, :ATTN_W], 2),
                              permute_heads(w_in[:, :, base_za:base_za + ATTN_W], 2),
                              w_in[:, :, base_za + ATTN_W:], w_in[:, :, ATTN_W:base_za]], axis=2).astype(BF16)
    w_out_p = jnp.concatenate([permute_heads(w_out[:, :ATTN_W], 1), w_out[:, ATTN_W:]], axis=1).astype(BF16)
    sinks = jnp.swapaxes(attn_sinks.reshape(depth, N_KV_HEADS, Q_GROUP), 1, 2).reshape(depth, N_Q_HEADS)
    sink_lanes = jnp.broadcast_to(sinks[:, :, None], (depth, N_Q_HEADS, LANES))
    sink_rows = jnp.broadcast_to(jnp.repeat(sinks, t_new, axis=1)[:, :, None], (depth, N_Q_HEADS * t_new, LANES))
    to_blocks = lambda bb: bb.reshape(depth, SSM_GROUP_CH, N_SSM_GROUPS, SSM_STATE).transpose(0, 2, 1, 3)
    w_b = jnp.concatenate([_half_block_diag(to_blocks(bb_re)), _half_block_diag(to_blocks(bb_im))], axis=3).astype(BF16)
    w_c = jnp.concatenate([_half_block_diag(ssm_c_re.transpose(0, 1, 3, 2)),
                           -_half_block_diag(ssm_c_im.transpose(0, 1, 3, 2))], axis=2).astype(BF16)
    return w_in_p, w_out_p, sink_lanes, sink_rows, w_b, w_c


def kernel(x_prompt, x_sample, cache_k, cache_v, state_ssm_re, state_ssm_im, c_prompt, c_sample, norm_g, w_ada, b_ada, w_in, attn_sinks, ssm_a_re, ssm_a_im, ssm_log_dt, ssm_b_re, ssm_b_im, ssm_c_re, ssm_c_im, ssm_d, w_glu, b_glu, w_out, final_g):
    depth = w_in.shape[0]
    nb_p, seq, _ = x_prompt.shape
    nb_s, t_new, _ = x_sample.shape
    win = cache_k.shape[2]
    blk = WINDOW

    sample = Group(n_batch=nb_s, mod_block=0)
    prompt = Group(n_batch=nb_p, mod_block=nb_s // nb_p)
    mod = adaln_mod(jnp.concatenate([c_sample, c_prompt], axis=0), w_ada, b_ada)
    abar_re, abar_im, bb_re, bb_im = ssm_discretise(ssm_a_re, ssm_a_im, ssm_log_dt, ssm_b_re, ssm_b_im)
    w_in_p, w_out_p, sink_lanes, sink_rows, w_b, w_c = _prepare_weights(
        w_in, w_out, attn_sinks, bb_re, bb_im, ssm_c_re, ssm_c_im, t_new)
    norm_g3 = norm_g.reshape(depth, 1, D_MODEL)
    d3 = ssm_d.reshape(depth, 1, SSM_W)
    w_glu_b = w_glu.astype(BF16)
    b_glu3 = b_glu.reshape(depth, 1, SSM_W)
    final_g2 = final_g.reshape(1, D_MODEL)
    ssm_weights = (abar_re, abar_im, w_b, w_c, d3, w_glu_b, b_glu3)

    cos_p, sm_p = _rope_tables(jnp.arange(seq), nb_p)
    cos_s, sm_s = _rope_tables(PAST_LEN + jnp.arange(t_new), nb_s)

    xp = x_prompt
    xs = x_sample.transpose(1, 0, 2).reshape(t_new * nb_s, D_MODEL)
    zeros_state = jnp.zeros((1, nb_p, N_STATE), F32)
    cache_k4 = cache_k.reshape(depth, nb_s, win, KV_W)
    cache_v4 = cache_v.reshape(depth, nb_s, win, KV_W)
    state_re3 = state_ssm_re.reshape(depth, nb_s, N_STATE)
    state_im3 = state_ssm_im.reshape(depth, nb_s, N_STATE)
    new_k = jnp.zeros_like(cache_k4)
    new_v = jnp.zeros_like(cache_v4)

    outs = {k: [] for k in ("kp", "vp", "rp", "ip", "rs", "is")}
    rows_p, rows_s = BOUNDARY_ROWS, t_new * nb_s
    opening = lambda l, cos_t, sm_t: (norm_g3, w_in_p, cos_t, sm_t, l)
    xp, proj_p = layer_boundary(xp, mod, prompt, rows_p, open_=opening(0, cos_p, sm_p))
    xs, proj_s = layer_boundary(xs, mod, sample, rows_s, open_=opening(0, cos_s, sm_s))
    for l in range(depth):
        last = l == depth - 1

        q, k, v, za, u, zs = proj_p
        ya = attention_prompt(q, k, v, za, sink_lanes, l, nb_p, blk)
        ys, hre, him = ssm_branch(u, zs, zeros_state, zeros_state, *ssm_weights, l, 0, nb_p, SSM_ROWS // nb_p)
        xp, proj_p = layer_boundary(xp, mod, prompt, rows_p, close=(ya, ys, w_out_p, l),
                                    open_=None if last else opening(l + 1, cos_p, sm_p),
                                    final_g=final_g2, batch_major_out=last)
        outs["kp"].append(k[(seq - win) * nb_p:])
        outs["vp"].append(v[(seq - win) * nb_p:])
        outs["rp"].append(hre)
        outs["ip"].append(him)

        q, k, v, za, u, zs = proj_s
        ya, new_k, new_v = attention_sample(q, k, v, za, cache_k4, cache_v4, sink_rows, new_k, new_v, l,
                                            nb_s, t_new, SAMPLE_GROUP)
        ys, hre, him = ssm_branch(u, zs, state_re3, state_im3, *ssm_weights, l, l, nb_s, t_new)
        xs, proj_s = layer_boundary(xs, mod, sample, rows_s, close=(ya, ys, w_out_p, l),
                                    open_=None if last else opening(l + 1, cos_s, sm_s), final_g=final_g2)
        outs["rs"].append(hre)
        outs["is"].append(him)

    y_prompt = xp
    y_sample = xs.reshape(t_new, nb_s, D_MODEL).transpose(1, 0, 2)
    st = lambda key: jnp.stack(outs[key])
    window = lambda key: st(key).reshape(depth, win, nb_p, N_KV_HEADS, HEAD_DIM).transpose(0, 2, 1, 3, 4)
    state = lambda key, nb: st(key).reshape(depth, nb, N_SSM_GROUPS, SSM_STATE)
    return (y_prompt, y_sample, window("kp"), window("vp"), state("rp", nb_p), state("ip", nb_p),
            new_k.reshape(cache_k.shape), new_v.reshape(cache_v.shape), state("rs", nb_s), state("is", nb_s))
```

```python
import functools
from typing import NamedTuple

import jax
import jax.numpy as jnp
import numpy as np
from jax import lax
from jax.experimental import pallas as pl
from jax.experimental.pallas import tpu as pltpu

F32 = jnp.float32
BF16 = jnp.bfloat16

D_MODEL = 1024
HEAD_DIM = 64
ATTN_W = 512
N_Q_HEADS = 8
N_KV_HEADS = 2
Q_GROUP = N_Q_HEADS // N_KV_HEADS
KV_W = 128
WINDOW = 128
ROT_DIM = 16
ROPE_THETA = 500000.0
SSM_W = 512
SSM_GROUP_CH = 16
N_SSM_GROUPS = 32
SSM_STATE = 64
N_STATE = N_SSM_GROUPS * SSM_STATE
HALF_STATE = N_STATE // 2
EPS = 1e-6
NEG_INF = -1e30
PAST_LEN = 8192

LANES = 128
N_CHUNKS = ATTN_W // LANES
BOUNDARY_ROWS = 1024
SSM_SLAB = 4 * LANES
SAMPLE_GROUP = 8
SSM_SUB_ROWS = 256
SSM_BUFFERS = 3
SUBLANES = 8
VMEM_LIMIT = 56 * 1024 * 1024

HEAD_PERM = (0, 4, 1, 5, 2, 6, 3, 7)


class Group(NamedTuple):
    n_batch: int
    mod_block: int


def _params(sem):
    return pltpu.CompilerParams(dimension_semantics=sem, vmem_limit_bytes=VMEM_LIMIT)


def _full(shape):
    return pl.BlockSpec(shape, lambda *_: (0,) * len(shape))


def _mod_kernel(c_ref, w_ref, b_ref, o_ref):
    c = c_ref[...]
    a = (c * jax.nn.sigmoid(c)).astype(BF16)
    o_ref[...] = jnp.dot(a, w_ref[...].astype(BF16), preferred_element_type=F32) + b_ref[...]


def adaln_mod(c_all, w_ada, b_ada):
    depth = w_ada.shape[0]
    n = c_all.shape[0]
    nj = 3 * D_MODEL // D_MODEL
    return pl.pallas_call(
        _mod_kernel,
        grid=(depth, nj),
        in_specs=[
            pl.BlockSpec((n, D_MODEL), lambda l, j: (0, 0)),
            pl.BlockSpec((None, D_MODEL, D_MODEL), lambda l, j: (l, 0, j)),
            pl.BlockSpec((None, 1, D_MODEL), lambda l, j: (l, 0, j)),
        ],
        out_specs=pl.BlockSpec((None, n, D_MODEL), lambda l, j: (l, 0, j)),
        out_shape=jax.ShapeDtypeStruct((depth, n, 3 * D_MODEL), F32),
        compiler_params=_params(("arbitrary", "arbitrary")),
        name="adaln_mod",
    )(c_all, w_ada, b_ada.reshape(depth, 1, 3 * D_MODEL))


def _disc_kernel(are_ref, aim_ref, ldt_ref, bre_ref, bim_ref, abre_ref, abim_ref, bbre_ref, bbim_ref):
    a_re = are_ref[...]
    a_im = aim_ref[...]
    dt = jnp.exp(ldt_ref[...])
    mag = jnp.exp(a_re * dt)
    abar_re = mag * jnp.cos(a_im * dt)
    abar_im = mag * jnp.sin(a_im * dt)
    den = a_re * a_re + a_im * a_im
    nr = abar_re - 1.0
    coef_re = (nr * a_re + abar_im * a_im) / den
    coef_im = (abar_im * a_re - nr * a_im) / den
    br = bre_ref[...]
    bi = bim_ref[...]
    abre_ref[...] = abar_re
    abim_ref[...] = abar_im
    bbre_ref[...] = coef_re * br - coef_im * bi
    bbim_ref[...] = coef_re * bi + coef_im * br


def ssm_discretise(a_re, a_im, log_dt, b_re, b_im):
    depth = a_re.shape[0]
    a_re = a_re.reshape(depth, 1, N_STATE)
    a_im = a_im.reshape(depth, 1, N_STATE)
    ldt = jnp.repeat(log_dt, SSM_STATE, axis=1).reshape(depth, 1, N_STATE)
    bt_re = b_re.reshape(depth, N_STATE, SSM_GROUP_CH).transpose(0, 2, 1)
    bt_im = b_im.reshape(depth, N_STATE, SSM_GROUP_CH).transpose(0, 2, 1)
    row = pl.BlockSpec((None, 1, N_STATE), lambda l: (l, 0, 0))
    mat = pl.BlockSpec((None, SSM_GROUP_CH, N_STATE), lambda l: (l, 0, 0))
    return pl.pallas_call(
        _disc_kernel,
        grid=(depth,),
        in_specs=[row, row, row, mat, mat],
        out_specs=[row, row, mat, mat],
        out_shape=[jax.ShapeDtypeStruct((depth, 1, N_STATE), F32)] * 2
        + [jax.ShapeDtypeStruct((depth, SSM_GROUP_CH, N_STATE), F32)] * 2,
        compiler_params=_params(("arbitrary",)),
        name="ssm_discretise",
    )(a_re, a_im, ldt, bt_re, bt_im)


def _rope(x, cos, sm):
    lane = lax.broadcasted_iota(jnp.int32, x.shape, 1) % HEAD_DIM
    partner = jnp.where(lane < ROT_DIM // 2, pltpu.roll(x, LANES - ROT_DIM // 2, 1), pltpu.roll(x, ROT_DIM // 2, 1))
    return x * cos + partner * sm


def _to_time_major(x_ref, scr, n_batch):
    steps = x_ref.shape[1]
    for b in range(n_batch):
        for c in range(D_MODEL // LANES):
            scr[c, pl.ds(b, steps, stride=n_batch), :] = x_ref[b, :, c * LANES:(c + 1) * LANES]
    return jnp.concatenate([scr[c] for c in range(D_MODEL // LANES)], axis=1)


def _from_time_major(x, o_ref, scr, n_batch):
    steps = o_ref.shape[1]
    for c in range(D_MODEL // LANES):
        scr[c] = x[:, c * LANES:(c + 1) * LANES]
    for b in range(n_batch):
        for c in range(D_MODEL // LANES):
            o_ref[b, :, c * LANES:(c + 1) * LANES] = scr[c, pl.ds(b, steps, stride=n_batch), :]


def _rms(x, g_ref):
    ms = jnp.mean(x * x, axis=-1, keepdims=True)
    return x * lax.rsqrt(ms + EPS) * g_ref[...]


def _per_sequence(y, n_batch, fn):
    rows = y.shape[0]
    return fn(y.reshape(rows // n_batch, n_batch, D_MODEL)).reshape(rows, D_MODEL)


def _residual_update(x, ya_ref, ys_ref, gate_ref, w_ref, n_batch):
    mixed = jnp.concatenate([ya_ref[c].astype(BF16) for c in range(N_CHUNKS)] + [ys_ref[...]], axis=1)
    mix = jnp.dot(mixed, w_ref[...], preferred_element_type=F32)
    return x + _per_sequence(mix, n_batch, lambda m: m * gate_ref[...][None])


def _project(x, g_ref, scale_ref, shift_ref, w_ref, cos_ref, sm_ref, q_ref, k_ref, v_ref, za_ref, u_ref, zs_ref,
             n_batch):
    y = _per_sequence(_rms(x, g_ref), n_batch, lambda y3: y3 * (1.0 + scale_ref[...])[None] + shift_ref[...][None])
    h = y.astype(BF16)
    cos = cos_ref[...]
    sm = sm_ref[...]

    def proj(lo, width):
        return jnp.dot(h, w_ref[:, lo:lo + width], preferred_element_type=F32)

    q = proj(0, ATTN_W)
    za = proj(ATTN_W, ATTN_W)
    for c in range(N_CHUNKS):
        q_ref[c] = _rope(q[:, c * LANES:(c + 1) * LANES], cos, sm)
        za_ref[c] = za[:, c * LANES:(c + 1) * LANES]
    u_ref[...] = proj(2 * ATTN_W, SSM_W)
    zs_ref[...] = proj(2 * ATTN_W + SSM_W, SSM_W)
    kv = proj(2 * ATTN_W + 2 * SSM_W, 2 * KV_W)
    k_ref[...] = _rope(kv[:, :KV_W], cos, sm)
    v_ref[...] = kv[:, KV_W:]


def _boundary_kernel(*refs, n_batch, closes, opens, batch_major):
    refs = list(refs)
    x_ref = refs.pop(0)
    close_refs = [refs.pop(0) for _ in range(4)] if closes else None
    final_g_ref = refs.pop(0) if closes and not opens else None
    open_refs = [refs.pop(0) for _ in range(6)] if opens else None
    x_out_ref = refs.pop(0) if closes or batch_major else None
    proj_out_refs = [refs.pop(0) for _ in range(6)] if opens else None
    scr = refs.pop(0) if batch_major else None

    x = _to_time_major(x_ref, scr, n_batch) if (batch_major and not closes) else x_ref[...]
    if closes:
        x = _residual_update(x, *close_refs, n_batch)
    if opens:
        if x_out_ref is not None:
            x_out_ref[...] = x
        _project(x, *open_refs, *proj_out_refs, n_batch)
    else:
        x = _rms(x, final_g_ref)
        if batch_major:
            _from_time_major(x, x_out_ref, scr, n_batch)
        else:
            x_out_ref[...] = x


def _layer_spec(arr, l):
    tail = arr.shape[1:]
    return pl.BlockSpec((None,) + tail, lambda *_: (l,) + (0,) * len(tail))


def _mod_spec(l, group, part):
    return pl.BlockSpec((None, group.n_batch, D_MODEL), lambda *_: (l, group.mod_block, part))


def layer_boundary(x, mod, group, block_rows, close=None, open_=None, final_g=None, batch_major_out=False):
    closes, opens = close is not None, open_ is not None
    batch_major_in = x.ndim == 3
    assert not (batch_major_in and closes) and not (batch_major_out and opens)
    batch_major = batch_major_in or batch_major_out
    t_rows = x.shape[0] * x.shape[1] if batch_major_in else x.shape[0]
    rows = lambda w: pl.BlockSpec((block_rows, w), lambda i: (i, 0))
    chunked = pl.BlockSpec((N_CHUNKS, block_rows, LANES), lambda i: (0, i, 0))
    blocked3 = pl.BlockSpec((group.n_batch, block_rows // group.n_batch, D_MODEL), lambda i: (0, i, 0))
    flat = lambda w: jax.ShapeDtypeStruct((t_rows, w), F32)
    chunked_shape = jax.ShapeDtypeStruct((N_CHUNKS, t_rows, LANES), F32)

    args, in_specs, out_specs, out_shape = [x], [blocked3 if batch_major_in else rows(D_MODEL)], [], []
    if closes:
        ya, ys, w_out, l = close
        args += [ya, ys, mod, w_out]
        in_specs += [chunked, rows(SSM_W), _mod_spec(l, group, 2), _layer_spec(w_out, l)]
        if not opens:
            args.append(final_g)
            in_specs.append(_full(final_g.shape))
    if opens:
        norm_g, w_in, cos_t, sm_t, l = open_
        args += [norm_g, mod, mod, w_in, cos_t, sm_t]
        in_specs += [_layer_spec(norm_g, l), _mod_spec(l, group, 1), _mod_spec(l, group, 0), _layer_spec(w_in, l),
                     rows(LANES), rows(LANES)]
    if closes or batch_major_in:
        out_specs.append(blocked3 if batch_major_out else rows(D_MODEL))
        out_shape.append(jax.ShapeDtypeStruct((group.n_batch, t_rows // group.n_batch, D_MODEL), F32)
                         if batch_major_out else flat(D_MODEL))
    if opens:
        out_specs += [chunked, rows(KV_W), rows(KV_W), chunked, rows(SSM_W), rows(SSM_W)]
        out_shape += [chunked_shape, flat(KV_W), flat(KV_W), chunked_shape, flat(SSM_W), flat(SSM_W)]
    scratch = [pltpu.VMEM((D_MODEL // LANES, block_rows, LANES), F32)] if batch_major else []
    outs = pl.pallas_call(
        functools.partial(_boundary_kernel, n_batch=group.n_batch, closes=closes, opens=opens,
                          batch_major=batch_major),
        grid=(t_rows // block_rows,),
        in_specs=in_specs,
        out_specs=out_specs,
        out_shape=out_shape,
        scratch_shapes=scratch,
        compiler_params=_params(("parallel",)),
        name="layer_boundary",
    )(*args)
    x_rows = outs[0] if (closes or batch_major_in) else x
    return x_rows, (tuple(outs[-6:]) if opens else None)


def _pad_queries(chunks):
    lane = lax.broadcasted_iota(jnp.int32, chunks[0].shape, 1)
    pieces = []
    for chunk in chunks:
        pieces.append(jnp.where(lane < HEAD_DIM, chunk, 0.0))
        pieces.append(jnp.where(lane >= HEAD_DIM, chunk, 0.0))
    return jnp.concatenate(pieces, axis=0).astype(BF16)


def _unpad_outputs(o, t):
    lane = lax.broadcasted_iota(jnp.int32, (t, LANES), 1)
    chunks = []
    for c in range(N_CHUNKS):
        lo = o[(2 * c) * t:(2 * c + 1) * t]
        hi = o[(2 * c + 1) * t:(2 * c + 2) * t]
        chunks.append(jnp.where(lane < HEAD_DIM, lo, hi))
    return chunks


def _gated_store(o_ref, za_ref, sel, ya_chunks):
    for c, ya in enumerate(ya_chunks):
        za = za_ref[c, sel, :]
        o_ref[c, sel, :] = ya * (za * jax.nn.sigmoid(za))


def _scores(qp, k):
    return lax.dot_general(qp, k.astype(BF16), (((1,), (1,)), ((), ())),
                           preferred_element_type=F32) * (HEAD_DIM ** -0.5)


def _prompt_attention_stages(q_ref, kp_ref, kc_ref, vp_ref, vc_ref, za_ref, sink_ref, o_ref, n_batch, blk):
    n = pl.program_id(0)
    keys = 2 * blk
    j = lax.broadcasted_iota(jnp.int32, (keys, 2 * blk), 0)
    tq = lax.broadcasted_iota(jnp.int32, (keys, 2 * blk), 1) % blk
    first_key = jnp.where(n == 0, blk, 0)
    mask = (j >= jnp.maximum(tq, first_key)) & (j <= tq + WINDOW)
    low = lax.broadcasted_iota(jnp.int32, (blk, LANES), 1) < HEAD_DIM
    low_rows = lax.broadcasted_iota(jnp.int32, (KV_W, blk), 0) < HEAD_DIM
    ones = jnp.ones((SUBLANES, keys), F32)

    chunks = range(N_CHUNKS)
    sinks = [jnp.concatenate([sink_ref[2 * c:2 * c + 1, :], sink_ref[2 * c + 1:2 * c + 2, :]], axis=1) for c in chunks]

    def rows_of(b):
        return pl.ds(b, blk, stride=n_batch)

    def score_stage(b):
        sel = rows_of(b)
        k = jnp.concatenate([kp_ref[sel, :], kc_ref[sel, :]], axis=0).astype(BF16)
        sts = []
        for c in chunks:
            qc = q_ref[c, sel, :]
            qp = jnp.concatenate([jnp.where(low, qc, 0.0), jnp.where(low, 0.0, qc)], axis=0).astype(BF16)
            sts.append(_scores(k, qp))
        return sts

    def value_stage(b, sts):
        sel = rows_of(b)
        v = jnp.concatenate([vp_ref[sel, :], vc_ref[sel, :]], axis=0)
        vt_aug = jnp.concatenate([v.T, ones], axis=0).astype(BF16)
        ms, ots = [], []
        for c in chunks:
            st = jnp.where(mask, sts[c], NEG_INF)
            m = jnp.maximum(jnp.max(st, axis=0, keepdims=True), sinks[c])
            ms.append(m)
            p = jnp.exp(st - m).astype(BF16)
            ots.append(jnp.dot(vt_aug, p, preferred_element_type=F32))
        for c in chunks:
            ot = ots[c]
            den = ot[KV_W:KV_W + 1, :] + jnp.exp(sinks[c] - ms[c])
            ot = ot[:KV_W, :] * (1.0 / den)
            ya = jnp.where(low_rows, ot[:, :blk], ot[:, blk:]).T
            za = za_ref[c, sel, :]
            o_ref[c, sel, :] = ya * (za * jax.nn.sigmoid(za))

    return score_stage, value_stage


def _attn_sample_kernel(q_ref, k_ref, v_ref, za_ref, ck_ref, cv_ref, sink_ref, nk_all_ref, nv_all_ref,
                        o_ref, nk_ref, nv_ref, *, n_batch, t_new, group):
    del nk_all_ref, nv_all_ref
    g = pl.program_id(0)
    win = ck_ref.shape[1]
    rows = N_Q_HEADS * t_new
    tq = lax.broadcasted_iota(jnp.int32, (rows, 2 * win), 0) % t_new
    j = lax.broadcasted_iota(jnp.int32, (rows, 2 * win), 1)
    dist_c = tq + win - j
    mask = ((j < win) & (dist_c >= 0) & (dist_c <= WINDOW)) | ((j >= win) & (j - win <= tq) & (j - win < t_new))
    sink = sink_ref[...]
    pad = jnp.zeros((win - t_new, KV_W), F32)
    ones = jnp.ones((2 * win, LANES), BF16)
    batch = range(group)
    sels = [pl.ds(g * group + i, t_new, stride=n_batch) for i in batch]

    scores = []
    for i in batch:
        qp = _pad_queries([q_ref[c, sels[i], :] for c in range(N_CHUNKS)])
        keys = jnp.concatenate([ck_ref[i], k_ref[sels[i], :], pad], axis=0)
        scores.append(jnp.where(mask, _scores(qp, keys), NEG_INF))
    outs = []
    for i in batch:
        values = jnp.concatenate([cv_ref[i], v_ref[sels[i], :], pad], axis=0).astype(BF16)
        s = scores[i]
        m = jnp.maximum(jnp.max(s, axis=-1, keepdims=True), sink[:, :1])
        p = jnp.exp(s - m).astype(BF16)
        o = jnp.dot(p, jnp.concatenate([values, ones], axis=1), preferred_element_type=F32)
        outs.append(o[:, :KV_W] / (o[:, KV_W:] + jnp.exp(sink - m)))
    for i in batch:
        _gated_store(o_ref, za_ref, sels[i], _unpad_outputs(outs[i], t_new))
        nk_ref[i, 0:win - t_new, :] = ck_ref[i, t_new:win, :]
        nk_ref[i, win - t_new:win, :] = k_ref[sels[i], :]
        nv_ref[i, 0:win - t_new, :] = cv_ref[i, t_new:win, :]
        nv_ref[i, win - t_new:win, :] = v_ref[sels[i], :]


def attention_sample(q, k, v, za, cache_k, cache_v, sink_rows, new_k, new_v, l, n_batch, t_new, group):
    t_rows = k.shape[0]
    win = cache_k.shape[2]
    cache = pl.BlockSpec((None, group, win, KV_W), lambda i: (l, i, 0, 0))
    chunked = _full((N_CHUNKS, t_rows, LANES))
    in_place = pl.BlockSpec(memory_space=pl.ANY)
    return pl.pallas_call(
        functools.partial(_attn_sample_kernel, n_batch=n_batch, t_new=t_new, group=group),
        grid=(n_batch // group,),
        in_specs=[chunked, _full((t_rows, KV_W)), _full((t_rows, KV_W)), chunked,
                  cache, cache, _layer_spec(sink_rows, l), in_place, in_place],
        out_specs=[chunked, cache, cache],
        out_shape=[jax.ShapeDtypeStruct((N_CHUNKS, t_rows, LANES), F32),
                   jax.ShapeDtypeStruct(new_k.shape, F32), jax.ShapeDtypeStruct(new_v.shape, F32)],
        input_output_aliases={7: 1, 8: 2},
        compiler_params=_params(("arbitrary",)),
        name="attention_sample",
    )(q, k, v, za, cache_k, cache_v, sink_rows, new_k, new_v)


def _ssm_stages(u_ref, zs_ref, h0re_ref, h0im_ref, are_ref, aim_ref, wb_ref, wc_ref, d_ref, wg_ref, bg_ref,
                o_ref, hre_ref, him_ref, x_scr, n_batch, sub_rows, slab):
    step = pl.program_id(0)

    @pl.when(step == 0)
    def _():
        hre_ref[...] = h0re_ref[...]
        him_ref[...] = h0im_ref[...]

    rows = u_ref.shape[0]
    half_in = SSM_W // 2
    n_sub = rows // sub_rows
    n_tiles = n_batch // SUBLANES
    t_steps = sub_rows // n_batch
    n_buf = x_scr.shape[0]
    items = [(sc, s) for sc in range(n_sub) for s in range(2)]

    def expand(i):
        sc, s = items[i]
        ub = u_ref[sc * sub_rows:(sc + 1) * sub_rows, s * half_in:(s + 1) * half_in].astype(BF16)
        x_scr[i % n_buf] = jnp.dot(ub, wb_ref[s], preferred_element_type=F32)

    def recur(i):
        sc, s = items[i]
        buf = i % n_buf
        for jb in range(HALF_STATE // slab):
            n_lo = s * HALF_STATE + jb * slab
            ar = jnp.broadcast_to(are_ref[:, n_lo:n_lo + slab], (SUBLANES, slab))
            ai = jnp.broadcast_to(aim_ref[:, n_lo:n_lo + slab], (SUBLANES, slab))
            re = slice(jb * slab, (jb + 1) * slab)
            im = slice(HALF_STATE + jb * slab, HALF_STATE + (jb + 1) * slab)
            for tile in range(n_tiles):
                r0 = tile * SUBLANES
                hr = hre_ref[r0:r0 + SUBLANES, n_lo:n_lo + slab]
                hi = him_ref[r0:r0 + SUBLANES, n_lo:n_lo + slab]
                for t in range(t_steps):
                    row = slice(t * n_batch + r0, t * n_batch + r0 + SUBLANES)
                    hr, hi = (ar * hr - ai * hi + x_scr[buf, row, re], ar * hi + ai * hr + x_scr[buf, row, im])
                    x_scr[buf, row, re] = hr
                    x_scr[buf, row, im] = hi
                hre_ref[r0:r0 + SUBLANES, n_lo:n_lo + slab] = hr
                him_ref[r0:r0 + SUBLANES, n_lo:n_lo + slab] = hi

    y_halves = {}

    def contract(i):
        sc, s = items[i]
        y_halves[s] = jnp.dot(x_scr[i % n_buf].astype(BF16), wc_ref[s], preferred_element_type=F32)
        if s == 1:
            r = slice(sc * sub_rows, (sc + 1) * sub_rows)
            y = jnp.concatenate([y_halves[0], y_halves[1]], axis=1) + d_ref[...] * u_ref[r, :]
            y = jax.nn.gelu(y)
            gate = jax.nn.sigmoid(jnp.dot(y.astype(BF16), wg_ref[...], preferred_element_type=F32) + bg_ref[...])
            zs = zs_ref[r, :]
            o_ref[r, :] = (y * gate * (zs * jax.nn.sigmoid(zs))).astype(o_ref.dtype)

    return len(items), expand, recur, contract


def _ssm_kernel(*refs, n_batch, sub_rows, slab):
    n_items, expand, recur, contract = _ssm_stages(*refs, n_batch, sub_rows, slab)
    expand(0)
    for i in range(n_items + 1):
        if i + 1 < n_items:
            expand(i + 1)
        if i < n_items:
            recur(i)
        if i >= 1:
            contract(i - 1)


N_ATTN_INPUTS = 7
N_SSM_INPUTS = 11


def _prompt_branches_kernel(*refs, n_batch, blk, sub_rows, slab):
    n_in = N_ATTN_INPUTS + N_SSM_INPUTS
    attn_refs = refs[:N_ATTN_INPUTS] + refs[n_in:n_in + 1]
    ssm_refs = refs[N_ATTN_INPUTS:n_in] + refs[n_in + 1:]
    score_stage, value_stage = _prompt_attention_stages(*attn_refs, n_batch, blk)
    n_items, expand, recur, contract = _ssm_stages(*ssm_refs, n_batch, sub_rows, slab)
    assert n_items == n_batch
    expand(0)
    scores = score_stage(0)
    for i in range(n_items + 1):
        if i + 1 < n_items:
            expand(i + 1)
            next_scores = score_stage(i + 1)
        if i < n_items:
            recur(i)
            value_stage(i, scores)
            scores = next_scores
        if i >= 1:
            contract(i - 1)


def _ssm_call_parts(u, h0_re, h0_im, abar_re, abar_im, w_b, w_c, d, w_glu, b_glu, l, l_state, n_batch, r):
    t_rows = u.shape[0]
    sub_rows = max(SSM_SUB_ROWS, n_batch)
    rows = pl.BlockSpec((r, SSM_W), lambda i: (i, 0))
    state = _full((n_batch, N_STATE))
    layer = lambda a: _layer_spec(a, l)
    in_specs = [rows, rows, _layer_spec(h0_re, l_state), _layer_spec(h0_im, l_state), layer(abar_re), layer(abar_im),
                layer(w_b), layer(w_c), layer(d), layer(w_glu), layer(b_glu)]
    out_shape = [jax.ShapeDtypeStruct((t_rows, SSM_W), BF16),
                 jax.ShapeDtypeStruct((n_batch, N_STATE), F32), jax.ShapeDtypeStruct((n_batch, N_STATE), F32)]
    scratch = [pltpu.VMEM((SSM_BUFFERS, sub_rows, 2 * HALF_STATE), F32)]
    return in_specs, [rows, state, state], out_shape, scratch, sub_rows


def ssm_branch(u, zs, h0_re, h0_im, abar_re, abar_im, w_b, w_c, d, w_glu, b_glu, l, l_state, n_batch, t_chunk):
    r = t_chunk * n_batch
    in_specs, out_specs, out_shape, scratch, sub_rows = _ssm_call_parts(
        u, h0_re, h0_im, abar_re, abar_im, w_b, w_c, d, w_glu, b_glu, l, l_state, n_batch, r)
    return pl.pallas_call(
        functools.partial(_ssm_kernel, n_batch=n_batch, sub_rows=sub_rows, slab=SSM_SLAB),
        grid=(u.shape[0] // r,),
        in_specs=in_specs,
        out_specs=out_specs,
        out_shape=out_shape,
        scratch_shapes=scratch,
        compiler_params=_params(("arbitrary",)),
        name="ssm_branch",
    )(u, zs, h0_re, h0_im, abar_re, abar_im, w_b, w_c, d, w_glu, b_glu)


def prompt_branches(q, k, v, za, sinks, u, zs, h0_re, h0_im, abar_re, abar_im, w_b, w_c, d, w_glu, b_glu,
                    l, l_state, n_batch, blk):
    t_rows = k.shape[0]
    r = blk * n_batch
    cur = lambda w: pl.BlockSpec((r, w), lambda i: (i, 0))
    prev = lambda w: pl.BlockSpec((r, w), lambda i: (jnp.maximum(i - 1, 0), 0))
    chunked = pl.BlockSpec((N_CHUNKS, r, LANES), lambda i: (0, i, 0))
    ssm_in, ssm_out, ssm_shape, scratch, sub_rows = _ssm_call_parts(
        u, h0_re, h0_im, abar_re, abar_im, w_b, w_c, d, w_glu, b_glu, l, l_state, n_batch, r)
    attn_in = [chunked, prev(KV_W), cur(KV_W), prev(KV_W), cur(KV_W), chunked, _layer_spec(sinks, l)]
    assert len(attn_in) == N_ATTN_INPUTS and len(ssm_in) == N_SSM_INPUTS
    return pl.pallas_call(
        functools.partial(_prompt_branches_kernel, n_batch=n_batch, blk=blk, sub_rows=sub_rows, slab=SSM_SLAB),
        grid=(t_rows // r,),
        in_specs=attn_in + ssm_in,
        out_specs=[chunked] + ssm_out,
        out_shape=[jax.ShapeDtypeStruct((N_CHUNKS, t_rows, LANES), F32)] + ssm_shape,
        scratch_shapes=scratch,
        compiler_params=_params(("arbitrary",)),
        name="prompt_branches",
    )(q, k, k, v, v, za, sinks, u, zs, h0_re, h0_im, abar_re, abar_im, w_b, w_c, d, w_glu, b_glu)


def _rope_tables(pos, n_batch):
    half = ROT_DIM // 2
    inv = ROPE_THETA ** (-jnp.arange(half, dtype=F32) / half)
    ang = pos.astype(F32)[:, None] * inv[None, :]
    d = np.arange(LANES) % HEAD_DIM
    idx = d % half
    cos = jnp.where(d < ROT_DIM, jnp.cos(ang)[:, idx], 1.0)
    sin = jnp.sin(ang)[:, idx]
    sm = jnp.where(d < half, -sin, jnp.where(d < ROT_DIM, sin, 0.0))
    return jnp.repeat(cos, n_batch, axis=0), jnp.repeat(sm, n_batch, axis=0)


def _keep_diagonal_blocks(w, row_block, col_block):
    rows, cols = w.shape[-2:]
    keep = (np.arange(rows)[:, None] // row_block) == (np.arange(cols)[None, :] // col_block)
    return jnp.where(keep, w, 0.0)


def _expand_weights(bb):
    depth = bb.shape[0]
    hg = N_SSM_GROUPS // 2
    halves = bb.reshape(depth, SSM_GROUP_CH, 2, HALF_STATE).transpose(0, 2, 1, 3)
    tiled = jnp.broadcast_to(halves[:, :, None], (depth, 2, hg, SSM_GROUP_CH, HALF_STATE))
    return _keep_diagonal_blocks(tiled.reshape(depth, 2, hg * SSM_GROUP_CH, HALF_STATE), SSM_GROUP_CH, SSM_STATE)


def _contract_weights(c):
    depth = c.shape[0]
    hg = N_SSM_GROUPS // 2
    rows = c.reshape(depth, 2, hg, SSM_GROUP_CH, SSM_STATE).transpose(0, 1, 2, 4, 3).reshape(
        depth, 2, HALF_STATE, SSM_GROUP_CH)
    return _keep_diagonal_blocks(jnp.tile(rows, (1, 1, 1, hg)), SSM_STATE, SSM_GROUP_CH)


def _prepare_weights(w_in, w_out, attn_sinks, bb_re, bb_im, ssm_c_re, ssm_c_im, t_new):
    depth = w_in.shape[0]
    assert HEAD_PERM == tuple(kv * Q_GROUP + g for g in range(Q_GROUP) for kv in range(N_KV_HEADS))

    def permute_heads(w, axis):
        shape = w.shape
        w = w.reshape(shape[:axis] + (N_KV_HEADS, Q_GROUP, HEAD_DIM) + shape[axis + 1:])
        return jnp.swapaxes(w, axis, axis + 1).reshape(shape)

    base_za = ATTN_W + 2 * KV_W
    w_in_p = jnp.concatenate([permute_heads(w_in[:, :, :ATTN_W], 2),
                              permute_heads(w_in[:, :, base_za:base_za + ATTN_W], 2),
                              w_in[:, :, base_za + ATTN_W:], w_in[:, :, ATTN_W:base_za]], axis=2).astype(BF16)
    w_out_p = jnp.concatenate([permute_heads(w_out[:, :ATTN_W], 1), w_out[:, ATTN_W:]], axis=1).astype(BF16)
    sinks = jnp.swapaxes(attn_sinks.reshape(depth, N_KV_HEADS, Q_GROUP), 1, 2).reshape(depth, N_Q_HEADS)
    sink_lanes = jnp.broadcast_to(sinks[:, :, None], (depth, N_Q_HEADS, LANES))
    sink_rows = jnp.broadcast_to(jnp.repeat(sinks, t_new, axis=1)[:, :, None], (depth, N_Q_HEADS * t_new, LANES))
    w_b = jnp.concatenate([_expand_weights(bb_re), _expand_weights(bb_im)], axis=3).astype(BF16)
    w_c = jnp.concatenate([_contract_weights(ssm_c_re), -_contract_weights(ssm_c_im)], axis=2).astype(BF16)
    return w_in_p, w_out_p, sink_lanes, sink_rows, w_b, w_c


def kernel(x_prompt, x_sample, cache_k, cache_v, state_ssm_re, state_ssm_im, c_prompt, c_sample, norm_g, w_ada, b_ada, w_in, attn_sinks, ssm_a_re, ssm_a_im, ssm_log_dt, ssm_b_re, ssm_b_im, ssm_c_re, ssm_c_im, ssm_d, w_glu, b_glu, w_out, final_g):
    depth = w_in.shape[0]
    nb_p, seq, _ = x_prompt.shape
    nb_s, t_new, _ = x_sample.shape
    win = cache_k.shape[2]
    blk = WINDOW

    sample = Group(n_batch=nb_s, mod_block=0)
    prompt = Group(n_batch=nb_p, mod_block=nb_s // nb_p)
    mod = adaln_mod(jnp.concatenate([c_sample, c_prompt], axis=0), w_ada, b_ada)
    abar_re, abar_im, bb_re, bb_im = ssm_discretise(ssm_a_re, ssm_a_im, ssm_log_dt, ssm_b_re, ssm_b_im)
    w_in_p, w_out_p, sink_lanes, sink_rows, w_b, w_c = _prepare_weights(
        w_in, w_out, attn_sinks, bb_re, bb_im, ssm_c_re, ssm_c_im, t_new)
    norm_g3 = norm_g.reshape(depth, 1, D_MODEL)
    d3 = ssm_d.reshape(depth, 1, SSM_W)
    w_glu_b = w_glu.astype(BF16)
    b_glu3 = b_glu.reshape(depth, 1, SSM_W)
    final_g2 = final_g.reshape(1, D_MODEL)
    ssm_weights = (abar_re, abar_im, w_b, w_c, d3, w_glu_b, b_glu3)

    cos_p, sm_p = _rope_tables(jnp.arange(seq), nb_p)
    cos_s, sm_s = _rope_tables(PAST_LEN + jnp.arange(t_new), nb_s)

    xp = x_prompt
    xs = x_sample.transpose(1, 0, 2).reshape(t_new * nb_s, D_MODEL)
    zeros_state = jnp.zeros((1, nb_p, N_STATE), F32)
    cache_k4 = cache_k.reshape(depth, nb_s, win, KV_W)
    cache_v4 = cache_v.reshape(depth, nb_s, win, KV_W)
    state_re3 = state_ssm_re.reshape(depth, nb_s, N_STATE)
    state_im3 = state_ssm_im.reshape(depth, nb_s, N_STATE)
    new_k = jnp.zeros_like(cache_k4)
    new_v = jnp.zeros_like(cache_v4)

    outs = {k: [] for k in ("kp", "vp", "rp", "ip", "rs", "is")}
    rows_p, rows_s = BOUNDARY_ROWS, t_new * nb_s
    opening = lambda l, cos_t, sm_t: (norm_g3, w_in_p, cos_t, sm_t, l)
    xp, proj_p = layer_boundary(xp, mod, prompt, rows_p, open_=opening(0, cos_p, sm_p))
    xs, proj_s = layer_boundary(xs, mod, sample, rows_s, open_=opening(0, cos_s, sm_s))
    for l in range(depth):
        last = l == depth - 1

        q, k, v, za, u, zs = proj_p
        ya, ys, hre, him = prompt_branches(q, k, v, za, sink_lanes, u, zs, zeros_state, zeros_state, *ssm_weights,
                                           l, 0, nb_p, blk)
        xp, proj_p = layer_boundary(xp, mod, prompt, rows_p, close=(ya, ys, w_out_p, l),
                                    open_=None if last else opening(l + 1, cos_p, sm_p),
                                    final_g=final_g2, batch_major_out=last)
        outs["kp"].append(k[(seq - win) * nb_p:])
        outs["vp"].append(v[(seq - win) * nb_p:])
        outs["rp"].append(hre)
        outs["ip"].append(him)

        q, k, v, za, u, zs = proj_s
        ya, new_k, new_v = attention_sample(q, k, v, za, cache_k4, cache_v4, sink_rows, new_k, new_v, l,
                                            nb_s, t_new, SAMPLE_GROUP)
        ys, hre, him = ssm_branch(u, zs, state_re3, state_im3, *ssm_weights, l, l, nb_s, t_new)
        xs, proj_s = layer_boundary(xs, mod, sample, rows_s, close=(ya, ys, w_out_p, l),
                                    open_=None if last else opening(l + 1, cos_s, sm_s), final_g=final_g2)
        outs["rs"].append(hre)
        outs["is"].append(him)

    y_prompt = xp
    y_sample = xs.reshape(t_new, nb_s, D_MODEL).transpose(1, 0, 2)
    st = lambda key: jnp.stack(outs[key])
    window = lambda key: st(key).reshape(depth, win, nb_p, N_KV_HEADS, HEAD_DIM).transpose(0, 2, 1, 3, 4)
    state = lambda key, nb: st(key).reshape(depth, nb, N_SSM_GROUPS, SSM_STATE)
    return (y_prompt, y_sample, window("kp"), window("vp"), state("rp", nb_p), state("ip", nb_p),
            new_k.reshape(cache_k.shape), new_v.reshape(cache_v.shape), state("rs", nb_s), state("is", nb_s))
```

```python
import functools
from typing import NamedTuple

import jax
import jax.numpy as jnp
import numpy as np
from jax import lax
from jax.experimental import pallas as pl
from jax.experimental.pallas import tpu as pltpu

F32 = jnp.float32
BF16 = jnp.bfloat16

D_MODEL = 1024
HEAD_DIM = 64
ATTN_W = 512
N_Q_HEADS = 8
N_KV_HEADS = 2
Q_GROUP = N_Q_HEADS // N_KV_HEADS
KV_W = 128
WINDOW = 128
ROT_DIM = 16
ROPE_THETA = 500000.0
SSM_W = 512
SSM_GROUP_CH = 16
N_SSM_GROUPS = 32
SSM_STATE = 64
N_STATE = N_SSM_GROUPS * SSM_STATE
HALF_STATE = N_STATE // 2
EPS = 1e-6
NEG_INF = -1e30
LOG2_E = 1.4426950408889634
PAST_LEN = 8192

LANES = 128
N_CHUNKS = ATTN_W // LANES
BOUNDARY_ROWS = 1024
SSM_SLAB = 4 * LANES
SAMPLE_GROUP = 8
SSM_SUB_ROWS = 256
SSM_BUFFERS = 3
SUBLANES = 8
VMEM_LIMIT = 56 * 1024 * 1024

HEAD_PERM = (0, 4, 1, 5, 2, 6, 3, 7)


class Group(NamedTuple):
    n_batch: int
    mod_block: int


def _params(sem):
    return pltpu.CompilerParams(dimension_semantics=sem, vmem_limit_bytes=VMEM_LIMIT)


def _full(shape):
    return pl.BlockSpec(shape, lambda *_: (0,) * len(shape))


def _mod_kernel(c_ref, w_ref, b_ref, o_ref):
    c = c_ref[...]
    a = (c * jax.nn.sigmoid(c)).astype(BF16)
    o_ref[...] = jnp.dot(a, w_ref[...].astype(BF16), preferred_element_type=F32) + b_ref[...]


def adaln_mod(c_all, w_ada, b_ada):
    depth = w_ada.shape[0]
    n = c_all.shape[0]
    nj = 3 * D_MODEL // D_MODEL
    return pl.pallas_call(
        _mod_kernel,
        grid=(depth, nj),
        in_specs=[
            pl.BlockSpec((n, D_MODEL), lambda l, j: (0, 0)),
            pl.BlockSpec((None, D_MODEL, D_MODEL), lambda l, j: (l, 0, j)),
            pl.BlockSpec((None, 1, D_MODEL), lambda l, j: (l, 0, j)),
        ],
        out_specs=pl.BlockSpec((None, n, D_MODEL), lambda l, j: (l, 0, j)),
        out_shape=jax.ShapeDtypeStruct((depth, n, 3 * D_MODEL), F32),
        compiler_params=_params(("arbitrary", "arbitrary")),
        name="adaln_mod",
    )(c_all, w_ada, b_ada.reshape(depth, 1, 3 * D_MODEL))


def _disc_kernel(are_ref, aim_ref, ldt_ref, bre_ref, bim_ref, abre_ref, abim_ref, bbre_ref, bbim_ref):
    a_re = are_ref[...]
    a_im = aim_ref[...]
    dt = jnp.exp(ldt_ref[...])
    mag = jnp.exp(a_re * dt)
    abar_re = mag * jnp.cos(a_im * dt)
    abar_im = mag * jnp.sin(a_im * dt)
    den = a_re * a_re + a_im * a_im
    nr = abar_re - 1.0
    coef_re = (nr * a_re + abar_im * a_im) / den
    coef_im = (abar_im * a_re - nr * a_im) / den
    br = bre_ref[...]
    bi = bim_ref[...]
    abre_ref[...] = abar_re
    abim_ref[...] = abar_im
    bbre_ref[...] = coef_re * br - coef_im * bi
    bbim_ref[...] = coef_re * bi + coef_im * br


def ssm_discretise(a_re, a_im, log_dt, b_re, b_im):
    depth = a_re.shape[0]
    a_re = a_re.reshape(depth, 1, N_STATE)
    a_im = a_im.reshape(depth, 1, N_STATE)
    ldt = jnp.repeat(log_dt, SSM_STATE, axis=1).reshape(depth, 1, N_STATE)
    bt_re = b_re.reshape(depth, N_STATE, SSM_GROUP_CH).transpose(0, 2, 1)
    bt_im = b_im.reshape(depth, N_STATE, SSM_GROUP_CH).transpose(0, 2, 1)
    row = pl.BlockSpec((None, 1, N_STATE), lambda l: (l, 0, 0))
    mat = pl.BlockSpec((None, SSM_GROUP_CH, N_STATE), lambda l: (l, 0, 0))
    return pl.pallas_call(
        _disc_kernel,
        grid=(depth,),
        in_specs=[row, row, row, mat, mat],
        out_specs=[row, row, mat, mat],
        out_shape=[jax.ShapeDtypeStruct((depth, 1, N_STATE), F32)] * 2
        + [jax.ShapeDtypeStruct((depth, SSM_GROUP_CH, N_STATE), F32)] * 2,
        compiler_params=_params(("arbitrary",)),
        name="ssm_discretise",
    )(a_re, a_im, ldt, bt_re, bt_im)


def _rope(x, cos, sm):
    lane = lax.broadcasted_iota(jnp.int32, x.shape, 1) % HEAD_DIM
    partner = jnp.where(lane < ROT_DIM // 2, pltpu.roll(x, LANES - ROT_DIM // 2, 1), pltpu.roll(x, ROT_DIM // 2, 1))
    return x * cos + partner * sm


def _to_time_major(x_ref, scr, n_batch):
    steps = x_ref.shape[1]
    for b in range(n_batch):
        for c in range(D_MODEL // LANES):
            scr[c, pl.ds(b, steps, stride=n_batch), :] = x_ref[b, :, c * LANES:(c + 1) * LANES]
    return jnp.concatenate([scr[c] for c in range(D_MODEL // LANES)], axis=1)


def _from_time_major(x, o_ref, scr, n_batch):
    steps = o_ref.shape[1]
    for c in range(D_MODEL // LANES):
        scr[c] = x[:, c * LANES:(c + 1) * LANES]
    for b in range(n_batch):
        for c in range(D_MODEL // LANES):
            o_ref[b, :, c * LANES:(c + 1) * LANES] = scr[c, pl.ds(b, steps, stride=n_batch), :]


def _rms(x, g_ref):
    ms = jnp.mean(x * x, axis=-1, keepdims=True)
    return x * lax.rsqrt(ms + EPS) * g_ref[...]


def _per_sequence(y, n_batch, fn):
    rows = y.shape[0]
    return fn(y.reshape(rows // n_batch, n_batch, D_MODEL)).reshape(rows, D_MODEL)


def _residual_update(x, ya_ref, ys_ref, gate_ref, w_ref, n_batch):
    mixed = jnp.concatenate([ya_ref[c].astype(BF16) for c in range(N_CHUNKS)] + [ys_ref[...]], axis=1)
    mix = jnp.dot(mixed, w_ref[...], preferred_element_type=F32)
    return x + _per_sequence(mix, n_batch, lambda m: m * gate_ref[...][None])


def _project(x, g_ref, scale_ref, shift_ref, w_ref, cos_ref, sm_ref, q_ref, k_ref, v_ref, za_ref, u_ref, zs_ref,
             rope_scr, n_batch):
    y = _per_sequence(_rms(x, g_ref), n_batch, lambda y3: y3 * (1.0 + scale_ref[...])[None] + shift_ref[...][None])
    h = y.astype(BF16)
    steps = cos_ref.shape[0]
    for b in range(n_batch):
        rope_scr[0, pl.ds(b, steps, stride=n_batch), :] = cos_ref[...]
        rope_scr[1, pl.ds(b, steps, stride=n_batch), :] = sm_ref[...]
    cos = rope_scr[0]
    sm = rope_scr[1]

    def proj(lo, width):
        return jnp.dot(h, w_ref[:, lo:lo + width], preferred_element_type=F32)

    q = proj(0, ATTN_W)
    za = proj(ATTN_W, ATTN_W)
    for c in range(N_CHUNKS):
        q_ref[c] = _rope(q[:, c * LANES:(c + 1) * LANES], cos, sm)
        za_ref[c] = za[:, c * LANES:(c + 1) * LANES]
    u_ref[...] = proj(2 * ATTN_W, SSM_W)
    zs_ref[...] = proj(2 * ATTN_W + SSM_W, SSM_W)
    kv = proj(2 * ATTN_W + 2 * SSM_W, 2 * KV_W)
    k_ref[...] = _rope(kv[:, :KV_W], cos, sm)
    v_ref[...] = kv[:, KV_W:]


def _boundary_kernel(*refs, n_batch, closes, opens, batch_major):
    refs = list(refs)
    x_ref = refs.pop(0)
    close_refs = [refs.pop(0) for _ in range(4)] if closes else None
    final_g_ref = refs.pop(0) if closes and not opens else None
    open_refs = [refs.pop(0) for _ in range(6)] if opens else None
    x_out_ref = refs.pop(0) if closes or batch_major else None
    proj_out_refs = [refs.pop(0) for _ in range(6)] if opens else None
    scr = refs.pop(0) if batch_major else None
    rope_scr = refs.pop(0) if opens else None

    x = _to_time_major(x_ref, scr, n_batch) if (batch_major and not closes) else x_ref[...]
    if closes:
        x = _residual_update(x, *close_refs, n_batch)
    if opens:
        if x_out_ref is not None:
            x_out_ref[...] = x
        _project(x, *open_refs, *proj_out_refs, rope_scr, n_batch)
    else:
        x = _rms(x, final_g_ref)
        if batch_major:
            _from_time_major(x, x_out_ref, scr, n_batch)
        else:
            x_out_ref[...] = x


def _layer_spec(arr, l):
    tail = arr.shape[1:]
    return pl.BlockSpec((None,) + tail, lambda *_: (l,) + (0,) * len(tail))


def _mod_spec(l, group, part):
    return pl.BlockSpec((None, group.n_batch, D_MODEL), lambda *_: (l, group.mod_block, part))


def layer_boundary(x, mod, group, block_rows, close=None, open_=None, final_g=None, batch_major_out=False):
    closes, opens = close is not None, open_ is not None
    batch_major_in = x.ndim == 3
    assert not (batch_major_in and closes) and not (batch_major_out and opens)
    batch_major = batch_major_in or batch_major_out
    t_rows = x.shape[0] * x.shape[1] if batch_major_in else x.shape[0]
    rows = lambda w: pl.BlockSpec((block_rows, w), lambda i: (i, 0))
    chunked = pl.BlockSpec((N_CHUNKS, block_rows, LANES), lambda i: (0, i, 0))
    blocked3 = pl.BlockSpec((group.n_batch, block_rows // group.n_batch, D_MODEL), lambda i: (0, i, 0))
    flat = lambda w: jax.ShapeDtypeStruct((t_rows, w), F32)
    chunked_shape = jax.ShapeDtypeStruct((N_CHUNKS, t_rows, LANES), F32)

    args, in_specs, out_specs, out_shape = [x], [blocked3 if batch_major_in else rows(D_MODEL)], [], []
    if closes:
        ya, ys, w_out, l = close
        args += [ya, ys, mod, w_out]
        in_specs += [chunked, rows(SSM_W), _mod_spec(l, group, 2), _layer_spec(w_out, l)]
        if not opens:
            args.append(final_g)
            in_specs.append(_full(final_g.shape))
    if opens:
        norm_g, w_in, cos_t, sm_t, l = open_
        args += [norm_g, mod, mod, w_in, cos_t, sm_t]
        per_step = pl.BlockSpec((block_rows // group.n_batch, LANES), lambda i: (i, 0))
        in_specs += [_layer_spec(norm_g, l), _mod_spec(l, group, 1), _mod_spec(l, group, 0), _layer_spec(w_in, l),
                     per_step, per_step]
    if closes or batch_major_in:
        out_specs.append(blocked3 if batch_major_out else rows(D_MODEL))
        out_shape.append(jax.ShapeDtypeStruct((group.n_batch, t_rows // group.n_batch, D_MODEL), F32)
                         if batch_major_out else flat(D_MODEL))
    if opens:
        out_specs += [chunked, rows(KV_W), rows(KV_W), chunked, rows(SSM_W), rows(SSM_W)]
        out_shape += [chunked_shape, flat(KV_W), flat(KV_W), chunked_shape, flat(SSM_W), flat(SSM_W)]
    scratch = [pltpu.VMEM((D_MODEL // LANES, block_rows, LANES), F32)] if batch_major else []
    if opens:
        scratch.append(pltpu.VMEM((2, block_rows, LANES), F32))
    outs = pl.pallas_call(
        functools.partial(_boundary_kernel, n_batch=group.n_batch, closes=closes, opens=opens,
                          batch_major=batch_major),
        grid=(t_rows // block_rows,),
        in_specs=in_specs,
        out_specs=out_specs,
        out_shape=out_shape,
        scratch_shapes=scratch,
        compiler_params=_params(("parallel",)),
        name="layer_boundary",
    )(*args)
    x_rows = outs[0] if (closes or batch_major_in) else x
    return x_rows, (tuple(outs[-6:]) if opens else None)


def _pad_queries(chunks):
    lane = lax.broadcasted_iota(jnp.int32, chunks[0].shape, 1)
    pieces = []
    for chunk in chunks:
        pieces.append(jnp.where(lane < HEAD_DIM, chunk, 0.0))
        pieces.append(jnp.where(lane >= HEAD_DIM, chunk, 0.0))
    return jnp.concatenate(pieces, axis=0).astype(BF16)


def _unpad_outputs(o, t):
    lane = lax.broadcasted_iota(jnp.int32, (t, LANES), 1)
    chunks = []
    for c in range(N_CHUNKS):
        lo = o[(2 * c) * t:(2 * c + 1) * t]
        hi = o[(2 * c + 1) * t:(2 * c + 2) * t]
        chunks.append(jnp.where(lane < HEAD_DIM, lo, hi))
    return chunks


def _gated_store(o_ref, za_ref, sel, ya_chunks):
    for c, ya in enumerate(ya_chunks):
        za = za_ref[c, sel, :]
        o_ref[c, sel, :] = ya * (za * jax.nn.sigmoid(za))


def _scores(qp, k, scale=HEAD_DIM ** -0.5):
    return lax.dot_general(qp, k.astype(BF16), (((1,), (1,)), ((), ())), preferred_element_type=F32) * scale


def _prompt_attention_stages(q_ref, kp_ref, kc_ref, vp_ref, vc_ref, za_ref, sink_ref, o_ref, n_batch, blk):
    n = pl.program_id(0)
    keys = 2 * blk
    j = lax.broadcasted_iota(jnp.int32, (keys, 2 * blk), 0)
    tq = lax.broadcasted_iota(jnp.int32, (keys, 2 * blk), 1) % blk
    first_key = jnp.where(n == 0, blk, 0)
    mask = (j >= jnp.maximum(tq, first_key)) & (j <= tq + WINDOW)
    low = lax.broadcasted_iota(jnp.int32, (blk, LANES), 1) < HEAD_DIM
    low_rows = lax.broadcasted_iota(jnp.int32, (KV_W, blk), 0) < HEAD_DIM
    ones = jnp.ones((SUBLANES, keys), F32)

    chunks = range(N_CHUNKS)
    sinks = [jnp.concatenate([sink_ref[2 * c:2 * c + 1, :], sink_ref[2 * c + 1:2 * c + 2, :]], axis=1) * LOG2_E
             for c in chunks]

    def rows_of(b):
        return pl.ds(b, blk, stride=n_batch)

    def score_stage(b):
        sel = rows_of(b)
        k = jnp.concatenate([kp_ref[sel, :], kc_ref[sel, :]], axis=0).astype(BF16)
        sts = []
        for c in chunks:
            qc = q_ref[c, sel, :]
            qp = jnp.concatenate([jnp.where(low, qc, 0.0), jnp.where(low, 0.0, qc)], axis=0).astype(BF16)
            sts.append(_scores(k, qp, HEAD_DIM ** -0.5 * LOG2_E))
        return sts

    def value_stage(b, sts):
        sel = rows_of(b)
        v = jnp.concatenate([vp_ref[sel, :], vc_ref[sel, :]], axis=0)
        vt_aug = jnp.concatenate([v.T, ones], axis=0).astype(BF16)
        def finish(c, ot, m):
            den = ot[KV_W:KV_W + 1, :] + jnp.exp2(sinks[c] - m)
            ot = ot[:KV_W, :] * (1.0 / den)
            ya = jnp.where(low_rows, ot[:, :blk], ot[:, blk:]).T
            za = za_ref[c, sel, :]
            o_ref[c, sel, :] = ya * (za * jax.nn.sigmoid(za))

        pending = None
        for c in chunks:
            st = jnp.where(mask, sts[c], NEG_INF)
            m = jnp.maximum(jnp.max(st, axis=0, keepdims=True), sinks[c])
            p = jnp.exp2(st - m).astype(BF16)
            ot = jnp.dot(vt_aug, p, preferred_element_type=F32)
            if pending is not None:
                finish(*pending)
            pending = (c, ot, m)
        finish(*pending)

    return score_stage, value_stage


def _attn_sample_kernel(q_ref, k_ref, v_ref, za_ref, ck_ref, cv_ref, sink_ref, o_ref, nk_ref, nv_ref,
                        *, n_batch, t_new, group):
    g = pl.program_id(0)
    win = ck_ref.shape[1]
    rows = N_Q_HEADS * t_new
    tq = lax.broadcasted_iota(jnp.int32, (rows, 2 * win), 0) % t_new
    j = lax.broadcasted_iota(jnp.int32, (rows, 2 * win), 1)
    dist_c = tq + win - j
    mask = ((j < win) & (dist_c >= 0) & (dist_c <= WINDOW)) | ((j >= win) & (j - win <= tq) & (j - win < t_new))
    sink = sink_ref[...]
    pad = jnp.zeros((win - t_new, KV_W), F32)
    ones = jnp.ones((2 * win, LANES), BF16)
    batch = range(group)
    sels = [pl.ds(g * group + i, t_new, stride=n_batch) for i in batch]

    scores = []
    for i in batch:
        qp = _pad_queries([q_ref[c, sels[i], :] for c in range(N_CHUNKS)])
        keys = jnp.concatenate([ck_ref[i], k_ref[sels[i], :], pad], axis=0)
        scores.append(jnp.where(mask, _scores(qp, keys), NEG_INF))
    outs = []
    for i in batch:
        values = jnp.concatenate([cv_ref[i], v_ref[sels[i], :], pad], axis=0).astype(BF16)
        s = scores[i]
        m = jnp.maximum(jnp.max(s, axis=-1, keepdims=True), sink[:, :1])
        p = jnp.exp(s - m).astype(BF16)
        o = jnp.dot(p, jnp.concatenate([values, ones], axis=1), preferred_element_type=F32)
        outs.append(o[:, :KV_W] / (o[:, KV_W:] + jnp.exp(sink - m)))
    for i in batch:
        _gated_store(o_ref, za_ref, sels[i], _unpad_outputs(outs[i], t_new))
        nk_ref[i, 0:win - t_new, :] = ck_ref[i, t_new:win, :]
        nk_ref[i, win - t_new:win, :] = k_ref[sels[i], :]
        nv_ref[i, 0:win - t_new, :] = cv_ref[i, t_new:win, :]
        nv_ref[i, win - t_new:win, :] = v_ref[sels[i], :]


def attention_sample(q, k, v, za, window_k, window_v, sink_rows, l, n_batch, t_new, group):
    t_rows = k.shape[0]
    win = window_k.shape[2]
    window = pl.BlockSpec((None, group, win, KV_W), lambda i: (l, i, 0, 0))
    chunked = _full((N_CHUNKS, t_rows, LANES))
    return pl.pallas_call(
        functools.partial(_attn_sample_kernel, n_batch=n_batch, t_new=t_new, group=group),
        grid=(n_batch // group,),
        in_specs=[chunked, _full((t_rows, KV_W)), _full((t_rows, KV_W)), chunked,
                  window, window, _layer_spec(sink_rows, l)],
        out_specs=[chunked, window, window],
        out_shape=[jax.ShapeDtypeStruct((N_CHUNKS, t_rows, LANES), F32),
                   jax.ShapeDtypeStruct(window_k.shape, F32), jax.ShapeDtypeStruct(window_v.shape, F32)],
        input_output_aliases={4: 1, 5: 2},
        compiler_params=_params(("arbitrary",)),
        name="attention_sample",
    )(q, k, v, za, window_k, window_v, sink_rows)


def _ssm_stages(u_ref, zs_ref, h0re_ref, h0im_ref, are_ref, aim_ref, wb_ref, wc_ref, d_ref, wg_ref, bg_ref,
                o_ref, hre_ref, him_ref, x_scr, n_batch, sub_rows, slab):
    step = pl.program_id(0)

    @pl.when(step == 0)
    def _():
        hre_ref[...] = h0re_ref[...]
        him_ref[...] = h0im_ref[...]

    rows = u_ref.shape[0]
    half_in = SSM_W // 2
    n_sub = rows // sub_rows
    n_tiles = n_batch // SUBLANES
    t_steps = sub_rows // n_batch
    n_buf = x_scr.shape[0]
    items = [(sc, s) for sc in range(n_sub) for s in range(2)]

    def expand(i):
        sc, s = items[i]
        ub = u_ref[sc * sub_rows:(sc + 1) * sub_rows, s * half_in:(s + 1) * half_in].astype(BF16)
        x_scr[i % n_buf] = jnp.dot(ub, wb_ref[s], preferred_element_type=F32)

    def recur(i):
        sc, s = items[i]
        buf = i % n_buf
        for jb in range(HALF_STATE // slab):
            n_lo = s * HALF_STATE + jb * slab
            ar = jnp.broadcast_to(are_ref[:, n_lo:n_lo + slab], (SUBLANES, slab))
            ai = jnp.broadcast_to(aim_ref[:, n_lo:n_lo + slab], (SUBLANES, slab))
            re = slice(jb * slab, (jb + 1) * slab)
            im = slice(HALF_STATE + jb * slab, HALF_STATE + (jb + 1) * slab)
            for tile in range(n_tiles):
                r0 = tile * SUBLANES
                hr = hre_ref[r0:r0 + SUBLANES, n_lo:n_lo + slab]
                hi = him_ref[r0:r0 + SUBLANES, n_lo:n_lo + slab]
                for t in range(t_steps):
                    row = slice(t * n_batch + r0, t * n_batch + r0 + SUBLANES)
                    hr, hi = (ar * hr - ai * hi + x_scr[buf, row, re], ar * hi + ai * hr + x_scr[buf, row, im])
                    x_scr[buf, row, re] = hr
                    x_scr[buf, row, im] = hi
                hre_ref[r0:r0 + SUBLANES, n_lo:n_lo + slab] = hr
                him_ref[r0:r0 + SUBLANES, n_lo:n_lo + slab] = hi

    y_halves = {}

    def contract(i):
        sc, s = items[i]
        y_halves[s] = jnp.dot(x_scr[i % n_buf].astype(BF16), wc_ref[s], preferred_element_type=F32)
        if s == 1:
            r = slice(sc * sub_rows, (sc + 1) * sub_rows)
            y = jnp.concatenate([y_halves[0], y_halves[1]], axis=1) + d_ref[...] * u_ref[r, :]
            y = jax.nn.gelu(y)
            gate = jax.nn.sigmoid(jnp.dot(y.astype(BF16), wg_ref[...], preferred_element_type=F32) + bg_ref[...])
            zs = zs_ref[r, :]
            o_ref[r, :] = (y * gate * (zs * jax.nn.sigmoid(zs))).astype(o_ref.dtype)

    return len(items), expand, recur, contract


def _ssm_kernel(*refs, n_batch, sub_rows, slab):
    n_items, expand, recur, contract = _ssm_stages(*refs, n_batch, sub_rows, slab)
    expand(0)
    for i in range(n_items + 1):
        if i + 1 < n_items:
            expand(i + 1)
        if i < n_items:
            recur(i)
        if i >= 1:
            contract(i - 1)


N_ATTN_INPUTS = 7
N_SSM_INPUTS = 11


def _prompt_branches_kernel(*refs, n_batch, blk, sub_rows, slab):
    n_in = N_ATTN_INPUTS + N_SSM_INPUTS
    attn_refs = refs[:N_ATTN_INPUTS] + refs[n_in:n_in + 1]
    ssm_refs = refs[N_ATTN_INPUTS:n_in] + refs[n_in + 1:]
    score_stage, value_stage = _prompt_attention_stages(*attn_refs, n_batch, blk)
    n_items, expand, recur, contract = _ssm_stages(*ssm_refs, n_batch, sub_rows, slab)
    assert n_items == n_batch
    expand(0)
    scores = score_stage(0)
    for i in range(n_items + 1):
        if i + 1 < n_items:
            expand(i + 1)
            next_scores = score_stage(i + 1)
        if i < n_items:
            recur(i)
            value_stage(i, scores)
            scores = next_scores
        if i >= 1:
            contract(i - 1)


def _ssm_call_parts(u, h0_re, h0_im, abar_re, abar_im, w_b, w_c, d, w_glu, b_glu, l, l_state, n_batch, r):
    t_rows = u.shape[0]
    sub_rows = max(SSM_SUB_ROWS, n_batch)
    rows = pl.BlockSpec((r, SSM_W), lambda i: (i, 0))
    state = _full((n_batch, N_STATE))
    layer = lambda a: _layer_spec(a, l)
    in_specs = [rows, rows, _layer_spec(h0_re, l_state), _layer_spec(h0_im, l_state), layer(abar_re), layer(abar_im),
                layer(w_b), layer(w_c), layer(d), layer(w_glu), layer(b_glu)]
    out_shape = [jax.ShapeDtypeStruct((t_rows, SSM_W), BF16),
                 jax.ShapeDtypeStruct((n_batch, N_STATE), F32), jax.ShapeDtypeStruct((n_batch, N_STATE), F32)]
    scratch = [pltpu.VMEM((SSM_BUFFERS, sub_rows, 2 * HALF_STATE), F32)]
    return in_specs, [rows, state, state], out_shape, scratch, sub_rows


def ssm_branch(u, zs, h0_re, h0_im, abar_re, abar_im, w_b, w_c, d, w_glu, b_glu, l, l_state, n_batch, t_chunk):
    r = t_chunk * n_batch
    in_specs, out_specs, out_shape, scratch, sub_rows = _ssm_call_parts(
        u, h0_re, h0_im, abar_re, abar_im, w_b, w_c, d, w_glu, b_glu, l, l_state, n_batch, r)
    return pl.pallas_call(
        functools.partial(_ssm_kernel, n_batch=n_batch, sub_rows=sub_rows, slab=SSM_SLAB),
        grid=(u.shape[0] // r,),
        in_specs=in_specs,
        out_specs=out_specs,
        out_shape=out_shape,
        scratch_shapes=scratch,
        compiler_params=_params(("arbitrary",)),
        name="ssm_branch",
    )(u, zs, h0_re, h0_im, abar_re, abar_im, w_b, w_c, d, w_glu, b_glu)


def prompt_branches(q, k, v, za, sinks, u, zs, h0_re, h0_im, abar_re, abar_im, w_b, w_c, d, w_glu, b_glu,
                    l, l_state, n_batch, blk):
    t_rows = k.shape[0]
    r = blk * n_batch
    cur = lambda w: pl.BlockSpec((r, w), lambda i: (i, 0))
    prev = lambda w: pl.BlockSpec((r, w), lambda i: (jnp.maximum(i - 1, 0), 0))
    chunked = pl.BlockSpec((N_CHUNKS, r, LANES), lambda i: (0, i, 0))
    ssm_in, ssm_out, ssm_shape, scratch, sub_rows = _ssm_call_parts(
        u, h0_re, h0_im, abar_re, abar_im, w_b, w_c, d, w_glu, b_glu, l, l_state, n_batch, r)
    attn_in = [chunked, prev(KV_W), cur(KV_W), prev(KV_W), cur(KV_W), chunked, _layer_spec(sinks, l)]
    assert len(attn_in) == N_ATTN_INPUTS and len(ssm_in) == N_SSM_INPUTS
    return pl.pallas_call(
        functools.partial(_prompt_branches_kernel, n_batch=n_batch, blk=blk, sub_rows=sub_rows, slab=SSM_SLAB),
        grid=(t_rows // r,),
        in_specs=attn_in + ssm_in,
        out_specs=[chunked] + ssm_out,
        out_shape=[jax.ShapeDtypeStruct((N_CHUNKS, t_rows, LANES), F32)] + ssm_shape,
        scratch_shapes=scratch,
        compiler_params=_params(("arbitrary",)),
        name="prompt_branches",
    )(q, k, k, v, v, za, sinks, u, zs, h0_re, h0_im, abar_re, abar_im, w_b, w_c, d, w_glu, b_glu)


def _rope_tables(pos):
    half = ROT_DIM // 2
    inv = ROPE_THETA ** (-jnp.arange(half, dtype=F32) / half)
    ang = pos.astype(F32)[:, None] * inv[None, :]
    d = np.arange(LANES) % HEAD_DIM
    idx = d % half
    cos = jnp.where(d < ROT_DIM, jnp.cos(ang)[:, idx], 1.0)
    sin = jnp.sin(ang)[:, idx]
    sm = jnp.where(d < half, -sin, jnp.where(d < ROT_DIM, sin, 0.0))
    return cos, sm


def _keep_diagonal_blocks(w, row_block, col_block):
    rows, cols = w.shape[-2:]
    keep = (np.arange(rows)[:, None] // row_block) == (np.arange(cols)[None, :] // col_block)
    return jnp.where(keep, w, 0.0)


def _expand_weights(bb):
    depth = bb.shape[0]
    hg = N_SSM_GROUPS // 2
    halves = bb.reshape(depth, SSM_GROUP_CH, 2, HALF_STATE).transpose(0, 2, 1, 3)
    tiled = jnp.broadcast_to(halves[:, :, None], (depth, 2, hg, SSM_GROUP_CH, HALF_STATE))
    return _keep_diagonal_blocks(tiled.reshape(depth, 2, hg * SSM_GROUP_CH, HALF_STATE), SSM_GROUP_CH, SSM_STATE)


def _contract_weights(c):
    depth = c.shape[0]
    hg = N_SSM_GROUPS // 2
    rows = c.reshape(depth, 2, hg, SSM_GROUP_CH, SSM_STATE).transpose(0, 1, 2, 4, 3).reshape(
        depth, 2, HALF_STATE, SSM_GROUP_CH)
    return _keep_diagonal_blocks(jnp.tile(rows, (1, 1, 1, hg)), SSM_STATE, SSM_GROUP_CH)


def _prepare_weights(w_in, w_out, attn_sinks, bb_re, bb_im, ssm_c_re, ssm_c_im, t_new):
    depth = w_in.shape[0]
    assert HEAD_PERM == tuple(kv * Q_GROUP + g for g in range(Q_GROUP) for kv in range(N_KV_HEADS))

    def permute_heads(w, axis):
        shape = w.shape
        w = w.reshape(shape[:axis] + (N_KV_HEADS, Q_GROUP, HEAD_DIM) + shape[axis + 1:])
        return jnp.swapaxes(w, axis, axis + 1).reshape(shape)

    base_za = ATTN_W + 2 * KV_W
    w_in_p = jnp.concatenate([permute_heads(w_in[:, :, :ATTN_W], 2),
                              permute_heads(w_in[:, :, base_za:base_za + ATTN_W], 2),
                              w_in[:, :, base_za + ATTN_W:], w_in[:, :, ATTN_W:base_za]], axis=2).astype(BF16)
    w_out_p = jnp.concatenate([permute_heads(w_out[:, :ATTN_W], 1), w_out[:, ATTN_W:]], axis=1).astype(BF16)
    sinks = jnp.swapaxes(attn_sinks.reshape(depth, N_KV_HEADS, Q_GROUP), 1, 2).reshape(depth, N_Q_HEADS)
    sink_lanes = jnp.broadcast_to(sinks[:, :, None], (depth, N_Q_HEADS, LANES))
    sink_rows = jnp.broadcast_to(jnp.repeat(sinks, t_new, axis=1)[:, :, None], (depth, N_Q_HEADS * t_new, LANES))
    w_b = jnp.concatenate([_expand_weights(bb_re), _expand_weights(bb_im)], axis=3).astype(BF16)
    w_c = jnp.concatenate([_contract_weights(ssm_c_re), -_contract_weights(ssm_c_im)], axis=2).astype(BF16)
    return w_in_p, w_out_p, sink_lanes, sink_rows, w_b, w_c


def kernel(x_prompt, x_sample, cache_k, cache_v, state_ssm_re, state_ssm_im, c_prompt, c_sample, norm_g, w_ada, b_ada, w_in, attn_sinks, ssm_a_re, ssm_a_im, ssm_log_dt, ssm_b_re, ssm_b_im, ssm_c_re, ssm_c_im, ssm_d, w_glu, b_glu, w_out, final_g):
    depth = w_in.shape[0]
    nb_p, seq, _ = x_prompt.shape
    nb_s, t_new, _ = x_sample.shape
    win = cache_k.shape[2]
    blk = WINDOW

    sample = Group(n_batch=nb_s, mod_block=0)
    prompt = Group(n_batch=nb_p, mod_block=nb_s // nb_p)
    mod = adaln_mod(jnp.concatenate([c_sample, c_prompt], axis=0), w_ada, b_ada)
    abar_re, abar_im, bb_re, bb_im = ssm_discretise(ssm_a_re, ssm_a_im, ssm_log_dt, ssm_b_re, ssm_b_im)
    w_in_p, w_out_p, sink_lanes, sink_rows, w_b, w_c = _prepare_weights(
        w_in, w_out, attn_sinks, bb_re, bb_im, ssm_c_re, ssm_c_im, t_new)
    norm_g3 = norm_g.reshape(depth, 1, D_MODEL)
    d3 = ssm_d.reshape(depth, 1, SSM_W)
    w_glu_b = w_glu.astype(BF16)
    b_glu3 = b_glu.reshape(depth, 1, SSM_W)
    final_g2 = final_g.reshape(1, D_MODEL)
    ssm_weights = (abar_re, abar_im, w_b, w_c, d3, w_glu_b, b_glu3)

    cos_p, sm_p = _rope_tables(jnp.arange(seq))
    cos_s, sm_s = _rope_tables(PAST_LEN + jnp.arange(t_new))

    xp = x_prompt
    xs = x_sample.transpose(1, 0, 2).reshape(t_new * nb_s, D_MODEL)
    zeros_state = jnp.zeros((1, nb_p, N_STATE), F32)
    new_k = cache_k.reshape(depth, nb_s, win, KV_W)
    new_v = cache_v.reshape(depth, nb_s, win, KV_W)
    state_re3 = state_ssm_re.reshape(depth, nb_s, N_STATE)
    state_im3 = state_ssm_im.reshape(depth, nb_s, N_STATE)

    outs = {k: [] for k in ("kp", "vp", "rp", "ip", "rs", "is")}
    rows_p, rows_s = BOUNDARY_ROWS, t_new * nb_s
    opening = lambda l, cos_t, sm_t: (norm_g3, w_in_p, cos_t, sm_t, l)
    xp, proj_p = layer_boundary(xp, mod, prompt, rows_p, open_=opening(0, cos_p, sm_p))
    xs, proj_s = layer_boundary(xs, mod, sample, rows_s, open_=opening(0, cos_s, sm_s))
    for l in range(depth):
        last = l == depth - 1

        q, k, v, za, u, zs = proj_p
        ya, ys, hre, him = prompt_branches(q, k, v, za, sink_lanes, u, zs, zeros_state, zeros_state, *ssm_weights,
                                           l, 0, nb_p, blk)
        xp, proj_p = layer_boundary(xp, mod, prompt, rows_p, close=(ya, ys, w_out_p, l),
                                    open_=None if last else opening(l + 1, cos_p, sm_p),
                                    final_g=final_g2, batch_major_out=last)
        outs["kp"].append(k[(seq - win) * nb_p:])
        outs["vp"].append(v[(seq - win) * nb_p:])
        outs["rp"].append(hre)
        outs["ip"].append(him)

        q, k, v, za, u, zs = proj_s
        ya, new_k, new_v = attention_sample(q, k, v, za, new_k, new_v, sink_rows, l, nb_s, t_new, SAMPLE_GROUP)
        ys, hre, him = ssm_branch(u, zs, state_re3, state_im3, *ssm_weights, l, l, nb_s, t_new)
        xs, proj_s = layer_boundary(xs, mod, sample, rows_s, close=(ya, ys, w_out_p, l),
                                    open_=None if last else opening(l + 1, cos_s, sm_s), final_g=final_g2)
        outs["rs"].append(hre)
        outs["is"].append(him)

    y_prompt = xp
    y_sample = xs.reshape(t_new, nb_s, D_MODEL).transpose(1, 0, 2)
    st = lambda key: jnp.stack(outs[key])
    window = lambda key: st(key).reshape(depth, win, nb_p, N_KV_HEADS, HEAD_DIM).transpose(0, 2, 1, 3, 4)
    state = lambda key, nb: st(key).reshape(depth, nb, N_SSM_GROUPS, SSM_STATE)
    return (y_prompt, y_sample, window("kp"), window("vp"), state("rp", nb_p), state("ip", nb_p),
            new_k.reshape(cache_k.shape), new_v.reshape(cache_v.shape), state("rs", nb_s), state("is", nb_s))
```

```python
import functools
from typing import NamedTuple

import jax
import jax.numpy as jnp
import numpy as np
from jax import lax
from jax.experimental import pallas as pl
from jax.experimental.pallas import tpu as pltpu

F32 = jnp.float32
BF16 = jnp.bfloat16

D_MODEL = 1024
HEAD_DIM = 64
ATTN_W = 512
N_Q_HEADS = 8
N_KV_HEADS = 2
Q_GROUP = N_Q_HEADS // N_KV_HEADS
KV_W = 128
WINDOW = 128
ROT_DIM = 16
ROPE_THETA = 500000.0
SSM_W = 512
SSM_GROUP_CH = 16
N_SSM_GROUPS = 32
SSM_STATE = 64
N_STATE = N_SSM_GROUPS * SSM_STATE
HALF_STATE = N_STATE // 2
EPS = 1e-6
NEG_INF = -1e30
LOG2_E = 1.4426950408889634
PAST_LEN = 8192

LANES = 128
N_CHUNKS = ATTN_W // LANES
BOUNDARY_ROWS = 1024
BOUNDARY_PARTS = 2
SSM_SLAB = 4 * LANES
SAMPLE_GROUP = 8
SSM_SUB_ROWS = 256
SSM_BUFFERS = 3
SUBLANES = 8
VMEM_LIMIT = 56 * 1024 * 1024

HEAD_PERM = (0, 4, 1, 5, 2, 6, 3, 7)


class Group(NamedTuple):
    n_batch: int
    mod_block: int


def _params(sem):
    return pltpu.CompilerParams(dimension_semantics=sem, vmem_limit_bytes=VMEM_LIMIT)


def _full(shape):
    return pl.BlockSpec(shape, lambda *_: (0,) * len(shape))


def _mod_kernel(c_ref, w_ref, b_ref, o_ref):
    c = c_ref[...]
    a = (c * jax.nn.sigmoid(c)).astype(BF16)
    o_ref[...] = jnp.dot(a, w_ref[...].astype(BF16), preferred_element_type=F32) + b_ref[...]


def adaln_mod(c_all, w_ada, b_ada):
    depth = w_ada.shape[0]
    n = c_all.shape[0]
    nj = 3 * D_MODEL // D_MODEL
    return pl.pallas_call(
        _mod_kernel,
        grid=(depth, nj),
        in_specs=[
            pl.BlockSpec((n, D_MODEL), lambda l, j: (0, 0)),
            pl.BlockSpec((None, D_MODEL, D_MODEL), lambda l, j: (l, 0, j)),
            pl.BlockSpec((None, 1, D_MODEL), lambda l, j: (l, 0, j)),
        ],
        out_specs=pl.BlockSpec((None, n, D_MODEL), lambda l, j: (l, 0, j)),
        out_shape=jax.ShapeDtypeStruct((depth, n, 3 * D_MODEL), F32),
        compiler_params=_params(("arbitrary", "arbitrary")),
        name="adaln_mod",
    )(c_all, w_ada, b_ada.reshape(depth, 1, 3 * D_MODEL))


def _disc_kernel(are_ref, aim_ref, ldt_ref, bre_ref, bim_ref, abre_ref, abim_ref, bbre_ref, bbim_ref):
    a_re = are_ref[...]
    a_im = aim_ref[...]
    dt = jnp.exp(ldt_ref[...])
    mag = jnp.exp(a_re * dt)
    abar_re = mag * jnp.cos(a_im * dt)
    abar_im = mag * jnp.sin(a_im * dt)
    den = a_re * a_re + a_im * a_im
    nr = abar_re - 1.0
    coef_re = (nr * a_re + abar_im * a_im) / den
    coef_im = (abar_im * a_re - nr * a_im) / den
    br = bre_ref[...]
    bi = bim_ref[...]
    abre_ref[...] = abar_re
    abim_ref[...] = abar_im
    bbre_ref[...] = coef_re * br - coef_im * bi
    bbim_ref[...] = coef_re * bi + coef_im * br


def ssm_discretise(a_re, a_im, log_dt, b_re, b_im):
    depth = a_re.shape[0]
    a_re = a_re.reshape(depth, 1, N_STATE)
    a_im = a_im.reshape(depth, 1, N_STATE)
    ldt = jnp.repeat(log_dt, SSM_STATE, axis=1).reshape(depth, 1, N_STATE)
    bt_re = b_re.reshape(depth, N_STATE, SSM_GROUP_CH).transpose(0, 2, 1)
    bt_im = b_im.reshape(depth, N_STATE, SSM_GROUP_CH).transpose(0, 2, 1)
    row = pl.BlockSpec((None, 1, N_STATE), lambda l: (l, 0, 0))
    mat = pl.BlockSpec((None, SSM_GROUP_CH, N_STATE), lambda l: (l, 0, 0))
    return pl.pallas_call(
        _disc_kernel,
        grid=(depth,),
        in_specs=[row, row, row, mat, mat],
        out_specs=[row, row, mat, mat],
        out_shape=[jax.ShapeDtypeStruct((depth, 1, N_STATE), F32)] * 2
        + [jax.ShapeDtypeStruct((depth, SSM_GROUP_CH, N_STATE), F32)] * 2,
        compiler_params=_params(("arbitrary",)),
        name="ssm_discretise",
    )(a_re, a_im, ldt, bt_re, bt_im)


def _rope(x, cos, sm):
    lane = lax.broadcasted_iota(jnp.int32, x.shape, 1) % HEAD_DIM
    partner = jnp.where(lane < ROT_DIM // 2, pltpu.roll(x, LANES - ROT_DIM // 2, 1), pltpu.roll(x, ROT_DIM // 2, 1))
    return x * cos + partner * sm


def _to_time_major(x_ref, scr, n_batch):
    steps = x_ref.shape[1]
    for b in range(n_batch):
        for c in range(D_MODEL // LANES):
            scr[c, pl.ds(b, steps, stride=n_batch), :] = x_ref[b, :, c * LANES:(c + 1) * LANES]
    return jnp.concatenate([scr[c] for c in range(D_MODEL // LANES)], axis=1)


def _from_time_major(x, o_ref, scr, n_batch):
    steps = o_ref.shape[1]
    for c in range(D_MODEL // LANES):
        scr[c] = x[:, c * LANES:(c + 1) * LANES]
    for b in range(n_batch):
        for c in range(D_MODEL // LANES):
            o_ref[b, :, c * LANES:(c + 1) * LANES] = scr[c, pl.ds(b, steps, stride=n_batch), :]


def _rms(x, g_ref):
    ms = jnp.mean(x * x, axis=-1, keepdims=True)
    return x * lax.rsqrt(ms + EPS) * g_ref[...]


def _per_sequence(y, n_batch, fn):
    rows = y.shape[0]
    return fn(y.reshape(rows // n_batch, n_batch, D_MODEL)).reshape(rows, D_MODEL)


def _residual_update(x, rows, ya_ref, ys_ref, gate_ref, w_ref, n_batch):
    mixed = jnp.concatenate([ya_ref[c, rows, :].astype(BF16) for c in range(N_CHUNKS)] + [ys_ref[rows, :]], axis=1)
    mix = jnp.dot(mixed, w_ref[...], preferred_element_type=F32)
    return x + _per_sequence(mix, n_batch, lambda m: m * gate_ref[...][None])


def _modulated_norm(x, g_ref, scale_ref, shift_ref, n_batch):
    y = _per_sequence(_rms(x, g_ref), n_batch, lambda y3: y3 * (1.0 + scale_ref[...])[None] + shift_ref[...][None])
    return y.astype(BF16)


def _project(h, rows, w_ref, rope_scr, q_ref, k_ref, v_ref, za_ref, u_ref, zs_ref):
    cos = rope_scr[0, rows, :]
    sm = rope_scr[1, rows, :]

    def proj(lo, width):
        return jnp.dot(h, w_ref[:, lo:lo + width], preferred_element_type=F32)

    q = proj(0, ATTN_W)
    za = proj(ATTN_W, ATTN_W)
    for c in range(N_CHUNKS):
        q_ref[c, rows, :] = _rope(q[:, c * LANES:(c + 1) * LANES], cos, sm)
        za_ref[c, rows, :] = za[:, c * LANES:(c + 1) * LANES]
    u_ref[rows, :] = proj(2 * ATTN_W, SSM_W)
    zs_ref[rows, :] = proj(2 * ATTN_W + SSM_W, SSM_W)
    kv = proj(2 * ATTN_W + 2 * SSM_W, 2 * KV_W)
    k_ref[rows, :] = _rope(kv[:, :KV_W], cos, sm)
    v_ref[rows, :] = kv[:, KV_W:]


def _boundary_kernel(*refs, n_batch, closes, opens, batch_major, block_rows, n_parts):
    refs = list(refs)
    x_ref = refs.pop(0)
    close_refs = [refs.pop(0) for _ in range(4)] if closes else None
    final_g_ref = refs.pop(0) if closes and not opens else None
    open_refs = [refs.pop(0) for _ in range(6)] if opens else None
    x_out_ref = refs.pop(0) if closes or batch_major else None
    proj_out_refs = [refs.pop(0) for _ in range(6)] if opens else None
    scr = refs.pop(0) if batch_major else None
    rope_scr = refs.pop(0) if opens else None

    if opens:
        g_ref, scale_ref, shift_ref, w_in_ref, cos_ref, sm_ref = open_refs
        steps = cos_ref.shape[0]
        for b in range(n_batch):
            rope_scr[0, pl.ds(b, steps, stride=n_batch), :] = cos_ref[...]
            rope_scr[1, pl.ds(b, steps, stride=n_batch), :] = sm_ref[...]

    part = block_rows // n_parts
    groups = [slice(i * part, (i + 1) * part) for i in range(n_parts)]
    x_full = _to_time_major(x_ref, scr, n_batch) if (batch_major and not closes) else None

    def updated(rows):
        x = x_full[rows, :] if x_full is not None else x_ref[rows, :]
        return _residual_update(x, rows, *close_refs, n_batch) if closes else x

    def normed(x, rows):
        if opens:
            if x_out_ref is not None:
                x_out_ref[rows, :] = x
            return _modulated_norm(x, g_ref, scale_ref, shift_ref, n_batch)
        return _rms(x, final_g_ref)

    xs = [updated(rows) for rows in groups]
    hs = [None] * n_parts
    hs[0] = normed(xs[0], groups[0])
    for i, rows in enumerate(groups):
        if opens:
            _project(hs[i], rows, w_in_ref, rope_scr, *proj_out_refs)
        if i + 1 < n_parts:
            hs[i + 1] = normed(xs[i + 1], groups[i + 1])
    if not opens:
        y = jnp.concatenate(hs, axis=0)
        if batch_major:
            _from_time_major(y, x_out_ref, scr, n_batch)
        else:
            x_out_ref[...] = y


def _layer_spec(arr, l):
    tail = arr.shape[1:]
    return pl.BlockSpec((None,) + tail, lambda *_: (l,) + (0,) * len(tail))


def _mod_spec(l, group, part):
    return pl.BlockSpec((None, group.n_batch, D_MODEL), lambda *_: (l, group.mod_block, part))


def layer_boundary(x, mod, group, block_rows, close=None, open_=None, final_g=None, batch_major_out=False):
    closes, opens = close is not None, open_ is not None
    batch_major_in = x.ndim == 3
    assert not (batch_major_in and closes) and not (batch_major_out and opens)
    batch_major = batch_major_in or batch_major_out
    t_rows = x.shape[0] * x.shape[1] if batch_major_in else x.shape[0]
    rows = lambda w: pl.BlockSpec((block_rows, w), lambda i: (i, 0))
    chunked = pl.BlockSpec((N_CHUNKS, block_rows, LANES), lambda i: (0, i, 0))
    blocked3 = pl.BlockSpec((group.n_batch, block_rows // group.n_batch, D_MODEL), lambda i: (0, i, 0))
    flat = lambda w: jax.ShapeDtypeStruct((t_rows, w), F32)
    chunked_shape = jax.ShapeDtypeStruct((N_CHUNKS, t_rows, LANES), F32)

    args, in_specs, out_specs, out_shape = [x], [blocked3 if batch_major_in else rows(D_MODEL)], [], []
    if closes:
        ya, ys, w_out, l = close
        args += [ya, ys, mod, w_out]
        in_specs += [chunked, rows(SSM_W), _mod_spec(l, group, 2), _layer_spec(w_out, l)]
        if not opens:
            args.append(final_g)
            in_specs.append(_full(final_g.shape))
    if opens:
        norm_g, w_in, cos_t, sm_t, l = open_
        args += [norm_g, mod, mod, w_in, cos_t, sm_t]
        per_step = pl.BlockSpec((block_rows // group.n_batch, LANES), lambda i: (i, 0))
        in_specs += [_layer_spec(norm_g, l), _mod_spec(l, group, 1), _mod_spec(l, group, 0), _layer_spec(w_in, l),
                     per_step, per_step]
    if closes or batch_major_in:
        out_specs.append(blocked3 if batch_major_out else rows(D_MODEL))
        out_shape.append(jax.ShapeDtypeStruct((group.n_batch, t_rows // group.n_batch, D_MODEL), F32)
                         if batch_major_out else flat(D_MODEL))
    if opens:
        out_specs += [chunked, rows(KV_W), rows(KV_W), chunked, rows(SSM_W), rows(SSM_W)]
        out_shape += [chunked_shape, flat(KV_W), flat(KV_W), chunked_shape, flat(SSM_W), flat(SSM_W)]
    scratch = [pltpu.VMEM((D_MODEL // LANES, block_rows, LANES), F32)] if batch_major else []
    if opens:
        scratch.append(pltpu.VMEM((2, block_rows, LANES), F32))
    outs = pl.pallas_call(
        functools.partial(_boundary_kernel, n_batch=group.n_batch, closes=closes, opens=opens,
                          batch_major=batch_major, block_rows=block_rows, n_parts=BOUNDARY_PARTS),
        grid=(t_rows // block_rows,),
        in_specs=in_specs,
        out_specs=out_specs,
        out_shape=out_shape,
        scratch_shapes=scratch,
        compiler_params=_params(("parallel",)),
        name="layer_boundary",
    )(*args)
    x_rows = outs[0] if (closes or batch_major_in) else x
    return x_rows, (tuple(outs[-6:]) if opens else None)


def _pad_queries(chunks):
    lane = lax.broadcasted_iota(jnp.int32, chunks[0].shape, 1)
    pieces = []
    for chunk in chunks:
        pieces.append(jnp.where(lane < HEAD_DIM, chunk, 0.0))
        pieces.append(jnp.where(lane >= HEAD_DIM, chunk, 0.0))
    return jnp.concatenate(pieces, axis=0).astype(BF16)


def _unpad_outputs(o, t):
    lane = lax.broadcasted_iota(jnp.int32, (t, LANES), 1)
    chunks = []
    for c in range(N_CHUNKS):
        lo = o[(2 * c) * t:(2 * c + 1) * t]
        hi = o[(2 * c + 1) * t:(2 * c + 2) * t]
        chunks.append(jnp.where(lane < HEAD_DIM, lo, hi))
    return chunks


def _gated_store(o_ref, za_ref, sel, ya_chunks):
    for c, ya in enumerate(ya_chunks):
        za = za_ref[c, sel, :]
        o_ref[c, sel, :] = ya * (za * jax.nn.sigmoid(za))


def _scores(qp, k, scale=HEAD_DIM ** -0.5):
    return lax.dot_general(qp, k.astype(BF16), (((1,), (1,)), ((), ())), preferred_element_type=F32) * scale


def _prompt_attention_stages(q_ref, kp_ref, kc_ref, vp_ref, vc_ref, za_ref, sink_ref, o_ref, n_batch, blk):
    n = pl.program_id(0)
    keys = 2 * blk
    j = lax.broadcasted_iota(jnp.int32, (keys, 2 * blk), 0)
    tq = lax.broadcasted_iota(jnp.int32, (keys, 2 * blk), 1) % blk
    first_key = jnp.where(n == 0, blk, 0)
    mask = (j >= jnp.maximum(tq, first_key)) & (j <= tq + WINDOW)
    low = lax.broadcasted_iota(jnp.int32, (blk, LANES), 1) < HEAD_DIM
    low_rows = lax.broadcasted_iota(jnp.int32, (KV_W, blk), 0) < HEAD_DIM
    ones = jnp.ones((SUBLANES, keys), F32)

    chunks = range(N_CHUNKS)
    sinks = [jnp.concatenate([sink_ref[2 * c:2 * c + 1, :], sink_ref[2 * c + 1:2 * c + 2, :]], axis=1) * LOG2_E
             for c in chunks]

    def rows_of(b):
        return pl.ds(b, blk, stride=n_batch)

    def score_stage(b):
        sel = rows_of(b)
        k = jnp.concatenate([kp_ref[sel, :], kc_ref[sel, :]], axis=0).astype(BF16)
        sts = []
        for c in chunks:
            qc = q_ref[c, sel, :]
            qp = jnp.concatenate([jnp.where(low, qc, 0.0), jnp.where(low, 0.0, qc)], axis=0).astype(BF16)
            sts.append(_scores(k, qp, HEAD_DIM ** -0.5 * LOG2_E))
        return sts

    def value_stage(b, sts):
        sel = rows_of(b)
        v = jnp.concatenate([vp_ref[sel, :], vc_ref[sel, :]], axis=0)
        vt_aug = jnp.concatenate([v.T, ones], axis=0).astype(BF16)
        def finish(c, ot, m):
            den = ot[KV_W:KV_W + 1, :] + jnp.exp2(sinks[c] - m)
            ot = ot[:KV_W, :] * (1.0 / den)
            ya = jnp.where(low_rows, ot[:, :blk], ot[:, blk:]).T
            za = za_ref[c, sel, :]
            o_ref[c, sel, :] = ya * (za * jax.nn.sigmoid(za))

        pending = None
        for c in chunks:
            st = jnp.where(mask, sts[c], NEG_INF)
            m = jnp.maximum(jnp.max(st, axis=0, keepdims=True), sinks[c])
            p = jnp.exp2(st - m).astype(BF16)
            ot = jnp.dot(vt_aug, p, preferred_element_type=F32)
            if pending is not None:
                finish(*pending)
            pending = (c, ot, m)
        finish(*pending)

    return score_stage, value_stage


def _attn_sample_kernel(q_ref, k_ref, v_ref, za_ref, ck_ref, cv_ref, sink_ref, o_ref, nk_ref, nv_ref,
                        *, n_batch, t_new, group):
    g = pl.program_id(0)
    win = ck_ref.shape[1]
    rows = N_Q_HEADS * t_new
    tq = lax.broadcasted_iota(jnp.int32, (rows, 2 * win), 0) % t_new
    j = lax.broadcasted_iota(jnp.int32, (rows, 2 * win), 1)
    dist_c = tq + win - j
    mask = ((j < win) & (dist_c >= 0) & (dist_c <= WINDOW)) | ((j >= win) & (j - win <= tq) & (j - win < t_new))
    sink = sink_ref[...]
    pad = jnp.zeros((win - t_new, KV_W), F32)
    ones = jnp.ones((2 * win, LANES), BF16)
    batch = range(group)
    sels = [pl.ds(g * group + i, t_new, stride=n_batch) for i in batch]

    scores = []
    for i in batch:
        qp = _pad_queries([q_ref[c, sels[i], :] for c in range(N_CHUNKS)])
        keys = jnp.concatenate([ck_ref[i], k_ref[sels[i], :], pad], axis=0)
        scores.append(jnp.where(mask, _scores(qp, keys), NEG_INF))
    outs = []
    for i in batch:
        values = jnp.concatenate([cv_ref[i], v_ref[sels[i], :], pad], axis=0).astype(BF16)
        s = scores[i]
        m = jnp.maximum(jnp.max(s, axis=-1, keepdims=True), sink[:, :1])
        p = jnp.exp(s - m).astype(BF16)
        o = jnp.dot(p, jnp.concatenate([values, ones], axis=1), preferred_element_type=F32)
        outs.append(o[:, :KV_W] / (o[:, KV_W:] + jnp.exp(sink - m)))
    for i in batch:
        _gated_store(o_ref, za_ref, sels[i], _unpad_outputs(outs[i], t_new))
        nk_ref[i, 0:win - t_new, :] = ck_ref[i, t_new:win, :]
        nk_ref[i, win - t_new:win, :] = k_ref[sels[i], :]
        nv_ref[i, 0:win - t_new, :] = cv_ref[i, t_new:win, :]
        nv_ref[i, win - t_new:win, :] = v_ref[sels[i], :]


def attention_sample(q, k, v, za, window_k, window_v, sink_rows, l, n_batch, t_new, group):
    t_rows = k.shape[0]
    win = window_k.shape[2]
    window = pl.BlockSpec((None, group, win, KV_W), lambda i: (l, i, 0, 0))
    chunked = _full((N_CHUNKS, t_rows, LANES))
    return pl.pallas_call(
        functools.partial(_attn_sample_kernel, n_batch=n_batch, t_new=t_new, group=group),
        grid=(n_batch // group,),
        in_specs=[chunked, _full((t_rows, KV_W)), _full((t_rows, KV_W)), chunked,
                  window, window, _layer_spec(sink_rows, l)],
        out_specs=[chunked, window, window],
        out_shape=[jax.ShapeDtypeStruct((N_CHUNKS, t_rows, LANES), F32),
                   jax.ShapeDtypeStruct(window_k.shape, F32), jax.ShapeDtypeStruct(window_v.shape, F32)],
        input_output_aliases={4: 1, 5: 2},
        compiler_params=_params(("arbitrary",)),
        name="attention_sample",
    )(q, k, v, za, window_k, window_v, sink_rows)


def _ssm_stages(u_ref, zs_ref, h0re_ref, h0im_ref, are_ref, aim_ref, wb_ref, wc_ref, d_ref, wg_ref, bg_ref,
                o_ref, hre_ref, him_ref, x_scr, n_batch, sub_rows, slab):
    step = pl.program_id(0)

    @pl.when(step == 0)
    def _():
        hre_ref[...] = h0re_ref[...]
        him_ref[...] = h0im_ref[...]

    rows = u_ref.shape[0]
    half_in = SSM_W // 2
    n_sub = rows // sub_rows
    n_tiles = n_batch // SUBLANES
    t_steps = sub_rows // n_batch
    n_buf = x_scr.shape[0]
    items = [(sc, s) for sc in range(n_sub) for s in range(2)]

    def expand(i):
        sc, s = items[i]
        ub = u_ref[sc * sub_rows:(sc + 1) * sub_rows, s * half_in:(s + 1) * half_in].astype(BF16)
        x_scr[i % n_buf] = jnp.dot(ub, wb_ref[s], preferred_element_type=F32)

    def recur(i):
        sc, s = items[i]
        buf = i % n_buf
        for jb in range(HALF_STATE // slab):
            n_lo = s * HALF_STATE + jb * slab
            ar = jnp.broadcast_to(are_ref[:, n_lo:n_lo + slab], (SUBLANES, slab))
            ai = jnp.broadcast_to(aim_ref[:, n_lo:n_lo + slab], (SUBLANES, slab))
            re = slice(jb * slab, (jb + 1) * slab)
            im = slice(HALF_STATE + jb * slab, HALF_STATE + (jb + 1) * slab)
            for tile in range(n_tiles):
                r0 = tile * SUBLANES
                hr = hre_ref[r0:r0 + SUBLANES, n_lo:n_lo + slab]
                hi = him_ref[r0:r0 + SUBLANES, n_lo:n_lo + slab]
                for t in range(t_steps):
                    row = slice(t * n_batch + r0, t * n_batch + r0 + SUBLANES)
                    hr, hi = (ar * hr - ai * hi + x_scr[buf, row, re], ar * hi + ai * hr + x_scr[buf, row, im])
                    x_scr[buf, row, re] = hr
                    x_scr[buf, row, im] = hi
                hre_ref[r0:r0 + SUBLANES, n_lo:n_lo + slab] = hr
                him_ref[r0:r0 + SUBLANES, n_lo:n_lo + slab] = hi

    y_halves = {}

    def contract(i):
        sc, s = items[i]
        y_halves[s] = jnp.dot(x_scr[i % n_buf].astype(BF16), wc_ref[s], preferred_element_type=F32)
        if s == 1:
            r = slice(sc * sub_rows, (sc + 1) * sub_rows)
            y = jnp.concatenate([y_halves[0], y_halves[1]], axis=1) + d_ref[...] * u_ref[r, :]
            y = jax.nn.gelu(y)
            gate = jax.nn.sigmoid(jnp.dot(y.astype(BF16), wg_ref[...], preferred_element_type=F32) + bg_ref[...])
            zs = zs_ref[r, :]
            o_ref[r, :] = (y * gate * (zs * jax.nn.sigmoid(zs))).astype(o_ref.dtype)

    return len(items), expand, recur, contract


def _ssm_kernel(*refs, n_batch, sub_rows, slab):
    n_items, expand, recur, contract = _ssm_stages(*refs, n_batch, sub_rows, slab)
    expand(0)
    for i in range(n_items + 1):
        if i + 1 < n_items:
            expand(i + 1)
        if i < n_items:
            recur(i)
        if i >= 1:
            contract(i - 1)


N_ATTN_INPUTS = 7
N_SSM_INPUTS = 11


def _prompt_branches_kernel(*refs, n_batch, blk, sub_rows, slab):
    n_in = N_ATTN_INPUTS + N_SSM_INPUTS
    attn_refs = refs[:N_ATTN_INPUTS] + refs[n_in:n_in + 1]
    ssm_refs = refs[N_ATTN_INPUTS:n_in] + refs[n_in + 1:]
    score_stage, value_stage = _prompt_attention_stages(*attn_refs, n_batch, blk)
    n_items, expand, recur, contract = _ssm_stages(*ssm_refs, n_batch, sub_rows, slab)
    assert n_items == n_batch
    expand(0)
    scores = score_stage(0)
    for i in range(n_items + 1):
        if i + 1 < n_items:
            expand(i + 1)
            next_scores = score_stage(i + 1)
        if i < n_items:
            recur(i)
            value_stage(i, scores)
            scores = next_scores
        if i >= 1:
            contract(i - 1)


def _ssm_call_parts(u, h0_re, h0_im, abar_re, abar_im, w_b, w_c, d, w_glu, b_glu, l, l_state, n_batch, r):
    t_rows = u.shape[0]
    sub_rows = max(SSM_SUB_ROWS, n_batch)
    rows = pl.BlockSpec((r, SSM_W), lambda i: (i, 0))
    state = _full((n_batch, N_STATE))
    layer = lambda a: _layer_spec(a, l)
    in_specs = [rows, rows, _layer_spec(h0_re, l_state), _layer_spec(h0_im, l_state), layer(abar_re), layer(abar_im),
                layer(w_b), layer(w_c), layer(d), layer(w_glu), layer(b_glu)]
    out_shape = [jax.ShapeDtypeStruct((t_rows, SSM_W), BF16),
                 jax.ShapeDtypeStruct((n_batch, N_STATE), F32), jax.ShapeDtypeStruct((n_batch, N_STATE), F32)]
    scratch = [pltpu.VMEM((SSM_BUFFERS, sub_rows, 2 * HALF_STATE), F32)]
    return in_specs, [rows, state, state], out_shape, scratch, sub_rows


def ssm_branch(u, zs, h0_re, h0_im, abar_re, abar_im, w_b, w_c, d, w_glu, b_glu, l, l_state, n_batch, t_chunk):
    r = t_chunk * n_batch
    in_specs, out_specs, out_shape, scratch, sub_rows = _ssm_call_parts(
        u, h0_re, h0_im, abar_re, abar_im, w_b, w_c, d, w_glu, b_glu, l, l_state, n_batch, r)
    return pl.pallas_call(
        functools.partial(_ssm_kernel, n_batch=n_batch, sub_rows=sub_rows, slab=SSM_SLAB),
        grid=(u.shape[0] // r,),
        in_specs=in_specs,
        out_specs=out_specs,
        out_shape=out_shape,
        scratch_shapes=scratch,
        compiler_params=_params(("arbitrary",)),
        name="ssm_branch",
    )(u, zs, h0_re, h0_im, abar_re, abar_im, w_b, w_c, d, w_glu, b_glu)


def prompt_branches(q, k, v, za, sinks, u, zs, h0_re, h0_im, abar_re, abar_im, w_b, w_c, d, w_glu, b_glu,
                    l, l_state, n_batch, blk):
    t_rows = k.shape[0]
    r = blk * n_batch
    cur = lambda w: pl.BlockSpec((r, w), lambda i: (i, 0))
    prev = lambda w: pl.BlockSpec((r, w), lambda i: (jnp.maximum(i - 1, 0), 0))
    chunked = pl.BlockSpec((N_CHUNKS, r, LANES), lambda i: (0, i, 0))
    ssm_in, ssm_out, ssm_shape, scratch, sub_rows = _ssm_call_parts(
        u, h0_re, h0_im, abar_re, abar_im, w_b, w_c, d, w_glu, b_glu, l, l_state, n_batch, r)
    attn_in = [chunked, prev(KV_W), cur(KV_W), prev(KV_W), cur(KV_W), chunked, _layer_spec(sinks, l)]
    assert len(attn_in) == N_ATTN_INPUTS and len(ssm_in) == N_SSM_INPUTS
    return pl.pallas_call(
        functools.partial(_prompt_branches_kernel, n_batch=n_batch, blk=blk, sub_rows=sub_rows, slab=SSM_SLAB),
        grid=(t_rows // r,),
        in_specs=attn_in + ssm_in,
        out_specs=[chunked] + ssm_out,
        out_shape=[jax.ShapeDtypeStruct((N_CHUNKS, t_rows, LANES), F32)] + ssm_shape,
        scratch_shapes=scratch,
        compiler_params=_params(("arbitrary",)),
        name="prompt_branches",
    )(q, k, k, v, v, za, sinks, u, zs, h0_re, h0_im, abar_re, abar_im, w_b, w_c, d, w_glu, b_glu)


def _rope_tables(pos):
    half = ROT_DIM // 2
    inv = ROPE_THETA ** (-jnp.arange(half, dtype=F32) / half)
    ang = pos.astype(F32)[:, None] * inv[None, :]
    d = np.arange(LANES) % HEAD_DIM
    idx = d % half
    cos = jnp.where(d < ROT_DIM, jnp.cos(ang)[:, idx], 1.0)
    sin = jnp.sin(ang)[:, idx]
    sm = jnp.where(d < half, -sin, jnp.where(d < ROT_DIM, sin, 0.0))
    return cos, sm


def _keep_diagonal_blocks(w, row_block, col_block):
    rows, cols = w.shape[-2:]
    keep = (np.arange(rows)[:, None] // row_block) == (np.arange(cols)[None, :] // col_block)
    return jnp.where(keep, w, 0.0)


def _expand_weights(bb):
    depth = bb.shape[0]
    hg = N_SSM_GROUPS // 2
    halves = bb.reshape(depth, SSM_GROUP_CH, 2, HALF_STATE).transpose(0, 2, 1, 3)
    tiled = jnp.broadcast_to(halves[:, :, None], (depth, 2, hg, SSM_GROUP_CH, HALF_STATE))
    return _keep_diagonal_blocks(tiled.reshape(depth, 2, hg * SSM_GROUP_CH, HALF_STATE), SSM_GROUP_CH, SSM_STATE)


def _contract_weights(c):
    depth = c.shape[0]
    hg = N_SSM_GROUPS // 2
    rows = c.reshape(depth, 2, hg, SSM_GROUP_CH, SSM_STATE).transpose(0, 1, 2, 4, 3).reshape(
        depth, 2, HALF_STATE, SSM_GROUP_CH)
    return _keep_diagonal_blocks(jnp.tile(rows, (1, 1, 1, hg)), SSM_STATE, SSM_GROUP_CH)


def _prepare_weights(w_in, w_out, attn_sinks, bb_re, bb_im, ssm_c_re, ssm_c_im, t_new):
    depth = w_in.shape[0]
    assert HEAD_PERM == tuple(kv * Q_GROUP + g for g in range(Q_GROUP) for kv in range(N_KV_HEADS))

    def permute_heads(w, axis):
        shape = w.shape
        w = w.reshape(shape[:axis] + (N_KV_HEADS, Q_GROUP, HEAD_DIM) + shape[axis + 1:])
        return jnp.swapaxes(w, axis, axis + 1).reshape(shape)

    base_za = ATTN_W + 2 * KV_W
    w_in_p = jnp.concatenate([permute_heads(w_in[:, :, :ATTN_W], 2),
                              permute_heads(w_in[:, :, base_za:base_za + ATTN_W], 2),
                              w_in[:, :, base_za + ATTN_W:], w_in[:, :, ATTN_W:base_za]], axis=2).astype(BF16)
    w_out_p = jnp.concatenate([permute_heads(w_out[:, :ATTN_W], 1), w_out[:, ATTN_W:]], axis=1).astype(BF16)
    sinks = jnp.swapaxes(attn_sinks.reshape(depth, N_KV_HEADS, Q_GROUP), 1, 2).reshape(depth, N_Q_HEADS)
    sink_lanes = jnp.broadcast_to(sinks[:, :, None], (depth, N_Q_HEADS, LANES))
    sink_rows = jnp.broadcast_to(jnp.repeat(sinks, t_new, axis=1)[:, :, None], (depth, N_Q_HEADS * t_new, LANES))
    w_b = jnp.concatenate([_expand_weights(bb_re), _expand_weights(bb_im)], axis=3).astype(BF16)
    w_c = jnp.concatenate([_contract_weights(ssm_c_re), -_contract_weights(ssm_c_im)], axis=2).astype(BF16)
    return w_in_p, w_out_p, sink_lanes, sink_rows, w_b, w_c


def kernel(x_prompt, x_sample, cache_k, cache_v, state_ssm_re, state_ssm_im, c_prompt, c_sample, norm_g, w_ada, b_ada, w_in, attn_sinks, ssm_a_re, ssm_a_im, ssm_log_dt, ssm_b_re, ssm_b_im, ssm_c_re, ssm_c_im, ssm_d, w_glu, b_glu, w_out, final_g):
    depth = w_in.shape[0]
    nb_p, seq, _ = x_prompt.shape
    nb_s, t_new, _ = x_sample.shape
    win = cache_k.shape[2]
    blk = WINDOW

    sample = Group(n_batch=nb_s, mod_block=0)
    prompt = Group(n_batch=nb_p, mod_block=nb_s // nb_p)
    mod = adaln_mod(jnp.concatenate([c_sample, c_prompt], axis=0), w_ada, b_ada)
    abar_re, abar_im, bb_re, bb_im = ssm_discretise(ssm_a_re, ssm_a_im, ssm_log_dt, ssm_b_re, ssm_b_im)
    w_in_p, w_out_p, sink_lanes, sink_rows, w_b, w_c = _prepare_weights(
        w_in, w_out, attn_sinks, bb_re, bb_im, ssm_c_re, ssm_c_im, t_new)
    norm_g3 = norm_g.reshape(depth, 1, D_MODEL)
    d3 = ssm_d.reshape(depth, 1, SSM_W)
    w_glu_b = w_glu.astype(BF16)
    b_glu3 = b_glu.reshape(depth, 1, SSM_W)
    final_g2 = final_g.reshape(1, D_MODEL)
    ssm_weights = (abar_re, abar_im, w_b, w_c, d3, w_glu_b, b_glu3)

    cos_p, sm_p = _rope_tables(jnp.arange(seq))
    cos_s, sm_s = _rope_tables(PAST_LEN + jnp.arange(t_new))

    xp = x_prompt
    xs = x_sample.transpose(1, 0, 2).reshape(t_new * nb_s, D_MODEL)
    zeros_state = jnp.zeros((1, nb_p, N_STATE), F32)
    new_k = cache_k.reshape(depth, nb_s, win, KV_W)
    new_v = cache_v.reshape(depth, nb_s, win, KV_W)
    state_re3 = state_ssm_re.reshape(depth, nb_s, N_STATE)
    state_im3 = state_ssm_im.reshape(depth, nb_s, N_STATE)

    outs = {k: [] for k in ("kp", "vp", "rp", "ip", "rs", "is")}
    rows_p, rows_s = BOUNDARY_ROWS, t_new * nb_s
    opening = lambda l, cos_t, sm_t: (norm_g3, w_in_p, cos_t, sm_t, l)
    xp, proj_p = layer_boundary(xp, mod, prompt, rows_p, open_=opening(0, cos_p, sm_p))
    xs, proj_s = layer_boundary(xs, mod, sample, rows_s, open_=opening(0, cos_s, sm_s))
    for l in range(depth):
        last = l == depth - 1

        q, k, v, za, u, zs = proj_p
        ya, ys, hre, him = prompt_branches(q, k, v, za, sink_lanes, u, zs, zeros_state, zeros_state, *ssm_weights,
                                           l, 0, nb_p, blk)
        xp, proj_p = layer_boundary(xp, mod, prompt, rows_p, close=(ya, ys, w_out_p, l),
                                    open_=None if last else opening(l + 1, cos_p, sm_p),
                                    final_g=final_g2, batch_major_out=last)
        outs["kp"].append(k[(seq - win) * nb_p:])
        outs["vp"].append(v[(seq - win) * nb_p:])
        outs["rp"].append(hre)
        outs["ip"].append(him)

        q, k, v, za, u, zs = proj_s
        ya, new_k, new_v = attention_sample(q, k, v, za, new_k, new_v, sink_rows, l, nb_s, t_new, SAMPLE_GROUP)
        ys, hre, him = ssm_branch(u, zs, state_re3, state_im3, *ssm_weights, l, l, nb_s, t_new)
        xs, proj_s = layer_boundary(xs, mod, sample, rows_s, close=(ya, ys, w_out_p, l),
                                    open_=None if last else opening(l + 1, cos_s, sm_s), final_g=final_g2)
        outs["rs"].append(hre)
        outs["is"].append(him)

    y_prompt = xp
    y_sample = xs.reshape(t_new, nb_s, D_MODEL).transpose(1, 0, 2)
    st = lambda key: jnp.stack(outs[key])
    window = lambda key: st(key).reshape(depth, win, nb_p, N_KV_HEADS, HEAD_DIM).transpose(0, 2, 1, 3, 4)
    state = lambda key, nb: st(key).reshape(depth, nb, N_SSM_GROUPS, SSM_STATE)
    return (y_prompt, y_sample, window("kp"), window("vp"), state("rp", nb_p), state("ip", nb_p),
            new_k.reshape(cache_k.shape), new_v.reshape(cache_v.shape), state("rs", nb_s), state("is", nb_s))
```

```python
import functools
from typing import NamedTuple

import jax
import jax.numpy as jnp
import numpy as np
from jax import lax
from jax.experimental import pallas as pl
from jax.experimental.pallas import tpu as pltpu

F32 = jnp.float32
BF16 = jnp.bfloat16

D_MODEL = 1024
HEAD_DIM = 64
ATTN_W = 512
N_Q_HEADS = 8
N_KV_HEADS = 2
Q_GROUP = N_Q_HEADS // N_KV_HEADS
KV_W = 128
WINDOW = 128
ROT_DIM = 16
ROPE_THETA = 500000.0
SSM_W = 512
SSM_GROUP_CH = 16
N_SSM_GROUPS = 32
SSM_STATE = 64
N_STATE = N_SSM_GROUPS * SSM_STATE
HALF_STATE = N_STATE // 2
EPS = 1e-6
NEG_INF = -1e30
LOG2_E = 1.4426950408889634
PAST_LEN = 8192

LANES = 128
N_CHUNKS = ATTN_W // LANES
BOUNDARY_ROWS = 1024
BOUNDARY_PARTS = 2
SSM_SLAB = 4 * LANES
SAMPLE_GROUP = 8
SSM_SUB_ROWS = 256
SSM_BUFFERS = 3
SUBLANES = 8
VMEM_LIMIT = 56 * 1024 * 1024

HEAD_PERM = (0, 4, 1, 5, 2, 6, 3, 7)


class Group(NamedTuple):
    n_batch: int
    mod_block: int


def _params(sem):
    return pltpu.CompilerParams(dimension_semantics=sem, vmem_limit_bytes=VMEM_LIMIT)


def _full(shape):
    return pl.BlockSpec(shape, lambda *_: (0,) * len(shape))


def _mod_kernel(c_ref, w_ref, b_ref, o_ref):
    c = c_ref[...]
    a = (c * jax.nn.sigmoid(c)).astype(BF16)
    o_ref[...] = jnp.dot(a, w_ref[...].astype(BF16), preferred_element_type=F32) + b_ref[...]


def adaln_mod(c_all, w_ada, b_ada):
    depth = w_ada.shape[0]
    n = c_all.shape[0]
    nj = 3 * D_MODEL // D_MODEL
    return pl.pallas_call(
        _mod_kernel,
        grid=(depth, nj),
        in_specs=[
            pl.BlockSpec((n, D_MODEL), lambda l, j: (0, 0)),
            pl.BlockSpec((None, D_MODEL, D_MODEL), lambda l, j: (l, 0, j)),
            pl.BlockSpec((None, 1, D_MODEL), lambda l, j: (l, 0, j)),
        ],
        out_specs=pl.BlockSpec((None, n, D_MODEL), lambda l, j: (l, 0, j)),
        out_shape=jax.ShapeDtypeStruct((depth, n, 3 * D_MODEL), F32),
        compiler_params=_params(("arbitrary", "arbitrary")),
        name="adaln_mod",
    )(c_all, w_ada, b_ada.reshape(depth, 1, 3 * D_MODEL))


def _disc_kernel(are_ref, aim_ref, ldt_ref, bre_ref, bim_ref, abre_ref, abim_ref, bbre_ref, bbim_ref):
    a_re = are_ref[...]
    a_im = aim_ref[...]
    dt = jnp.exp(ldt_ref[...])
    mag = jnp.exp(a_re * dt)
    abar_re = mag * jnp.cos(a_im * dt)
    abar_im = mag * jnp.sin(a_im * dt)
    den = a_re * a_re + a_im * a_im
    nr = abar_re - 1.0
    coef_re = (nr * a_re + abar_im * a_im) / den
    coef_im = (abar_im * a_re - nr * a_im) / den
    br = bre_ref[...]
    bi = bim_ref[...]
    abre_ref[...] = abar_re
    abim_ref[...] = abar_im
    bbre_ref[...] = coef_re * br - coef_im * bi
    bbim_ref[...] = coef_re * bi + coef_im * br


def ssm_discretise(a_re, a_im, log_dt, b_re, b_im):
    depth = a_re.shape[0]
    a_re = a_re.reshape(depth, 1, N_STATE)
    a_im = a_im.reshape(depth, 1, N_STATE)
    ldt = jnp.repeat(log_dt, SSM_STATE, axis=1).reshape(depth, 1, N_STATE)
    bt_re = b_re.reshape(depth, N_STATE, SSM_GROUP_CH).transpose(0, 2, 1)
    bt_im = b_im.reshape(depth, N_STATE, SSM_GROUP_CH).transpose(0, 2, 1)
    row = pl.BlockSpec((None, 1, N_STATE), lambda l: (l, 0, 0))
    mat = pl.BlockSpec((None, SSM_GROUP_CH, N_STATE), lambda l: (l, 0, 0))
    return pl.pallas_call(
        _disc_kernel,
        grid=(depth,),
        in_specs=[row, row, row, mat, mat],
        out_specs=[row, row, mat, mat],
        out_shape=[jax.ShapeDtypeStruct((depth, 1, N_STATE), F32)] * 2
        + [jax.ShapeDtypeStruct((depth, SSM_GROUP_CH, N_STATE), F32)] * 2,
        compiler_params=_params(("arbitrary",)),
        name="ssm_discretise",
    )(a_re, a_im, ldt, bt_re, bt_im)


def _rope(x, cos, sm):
    lane = lax.broadcasted_iota(jnp.int32, x.shape, 1) % HEAD_DIM
    partner = jnp.where(lane < ROT_DIM // 2, pltpu.roll(x, LANES - ROT_DIM // 2, 1), pltpu.roll(x, ROT_DIM // 2, 1))
    return x * cos + partner * sm


def _to_time_major(x_ref, scr, n_batch):
    steps = x_ref.shape[1]
    for b in range(n_batch):
        for c in range(D_MODEL // LANES):
            scr[c, pl.ds(b, steps, stride=n_batch), :] = x_ref[b, :, c * LANES:(c + 1) * LANES]
    return jnp.concatenate([scr[c] for c in range(D_MODEL // LANES)], axis=1)


def _from_time_major(x, o_ref, scr, n_batch):
    steps = o_ref.shape[1]
    for c in range(D_MODEL // LANES):
        scr[c] = x[:, c * LANES:(c + 1) * LANES]
    for b in range(n_batch):
        for c in range(D_MODEL // LANES):
            o_ref[b, :, c * LANES:(c + 1) * LANES] = scr[c, pl.ds(b, steps, stride=n_batch), :]


def _rms(x, g_ref):
    ms = jnp.mean(x * x, axis=-1, keepdims=True)
    return x * lax.rsqrt(ms + EPS) * g_ref[...]


def _per_sequence(y, n_batch, fn):
    rows = y.shape[0]
    return fn(y.reshape(rows // n_batch, n_batch, D_MODEL)).reshape(rows, D_MODEL)


def _residual_update(x, rows, ya_ref, ys_ref, gate_ref, w_ref, n_batch):
    mixed = jnp.concatenate([ya_ref[c, rows, :].astype(BF16) for c in range(N_CHUNKS)] + [ys_ref[rows, :]], axis=1)
    mix = jnp.dot(mixed, w_ref[...], preferred_element_type=F32)
    return x + _per_sequence(mix, n_batch, lambda m: m * gate_ref[...][None])


def _modulated_norm(x, g_ref, scale_ref, shift_ref, n_batch):
    y = _per_sequence(_rms(x, g_ref), n_batch, lambda y3: y3 * (1.0 + scale_ref[...])[None] + shift_ref[...][None])
    return y.astype(BF16)


def _silu(z):
    return z * jax.nn.sigmoid(z)


def _project(h, rows, w_ref, rope_scr, q_ref, k_ref, v_ref, ga_ref, u_ref, gs_ref):
    cos = rope_scr[0, rows, :]
    sm = rope_scr[1, rows, :]

    def proj(lo, width):
        return jnp.dot(h, w_ref[:, lo:lo + width], preferred_element_type=F32)

    q = proj(0, ATTN_W)
    ga = _silu(proj(ATTN_W, ATTN_W))
    for c in range(N_CHUNKS):
        q_ref[c, rows, :] = _rope(q[:, c * LANES:(c + 1) * LANES], cos, sm)
        ga_ref[c, rows, :] = ga[:, c * LANES:(c + 1) * LANES]
    u_ref[rows, :] = proj(2 * ATTN_W, SSM_W)
    gs_ref[rows, :] = _silu(proj(2 * ATTN_W + SSM_W, SSM_W))
    kv = proj(2 * ATTN_W + 2 * SSM_W, 2 * KV_W)
    k_ref[rows, :] = _rope(kv[:, :KV_W], cos, sm)
    v_ref[rows, :] = kv[:, KV_W:]


def _boundary_kernel(*refs, n_batch, closes, opens, batch_major, block_rows, n_parts):
    refs = list(refs)
    x_ref = refs.pop(0)
    close_refs = [refs.pop(0) for _ in range(4)] if closes else None
    final_g_ref = refs.pop(0) if closes and not opens else None
    open_refs = [refs.pop(0) for _ in range(6)] if opens else None
    x_out_ref = refs.pop(0) if closes or batch_major else None
    proj_out_refs = [refs.pop(0) for _ in range(6)] if opens else None
    scr = refs.pop(0) if batch_major else None
    rope_scr = refs.pop(0) if opens else None

    if opens:
        g_ref, scale_ref, shift_ref, w_in_ref, cos_ref, sm_ref = open_refs
        steps = cos_ref.shape[0]
        for b in range(n_batch):
            rope_scr[0, pl.ds(b, steps, stride=n_batch), :] = cos_ref[...]
            rope_scr[1, pl.ds(b, steps, stride=n_batch), :] = sm_ref[...]

    part = block_rows // n_parts
    groups = [slice(i * part, (i + 1) * part) for i in range(n_parts)]
    x_full = _to_time_major(x_ref, scr, n_batch) if (batch_major and not closes) else None

    def updated(rows):
        x = x_full[rows, :] if x_full is not None else x_ref[rows, :]
        return _residual_update(x, rows, *close_refs, n_batch) if closes else x

    def normed(x, rows):
        if opens:
            if x_out_ref is not None:
                x_out_ref[rows, :] = x
            return _modulated_norm(x, g_ref, scale_ref, shift_ref, n_batch)
        return _rms(x, final_g_ref)

    xs = [updated(rows) for rows in groups]
    hs = [None] * n_parts
    hs[0] = normed(xs[0], groups[0])
    for i, rows in enumerate(groups):
        if opens:
            _project(hs[i], rows, w_in_ref, rope_scr, *proj_out_refs)
        if i + 1 < n_parts:
            hs[i + 1] = normed(xs[i + 1], groups[i + 1])
    if not opens:
        y = jnp.concatenate(hs, axis=0)
        if batch_major:
            _from_time_major(y, x_out_ref, scr, n_batch)
        else:
            x_out_ref[...] = y


def _layer_spec(arr, l):
    tail = arr.shape[1:]
    return pl.BlockSpec((None,) + tail, lambda *_: (l,) + (0,) * len(tail))


def _mod_spec(l, group, part):
    return pl.BlockSpec((None, group.n_batch, D_MODEL), lambda *_: (l, group.mod_block, part))


def layer_boundary(x, mod, group, block_rows, close=None, open_=None, final_g=None, batch_major_out=False):
    closes, opens = close is not None, open_ is not None
    batch_major_in = x.ndim == 3
    assert not (batch_major_in and closes) and not (batch_major_out and opens)
    batch_major = batch_major_in or batch_major_out
    t_rows = x.shape[0] * x.shape[1] if batch_major_in else x.shape[0]
    rows = lambda w: pl.BlockSpec((block_rows, w), lambda i: (i, 0))
    chunked = pl.BlockSpec((N_CHUNKS, block_rows, LANES), lambda i: (0, i, 0))
    blocked3 = pl.BlockSpec((group.n_batch, block_rows // group.n_batch, D_MODEL), lambda i: (0, i, 0))
    flat = lambda w: jax.ShapeDtypeStruct((t_rows, w), F32)
    chunked_shape = jax.ShapeDtypeStruct((N_CHUNKS, t_rows, LANES), F32)

    args, in_specs, out_specs, out_shape = [x], [blocked3 if batch_major_in else rows(D_MODEL)], [], []
    if closes:
        ya, ys, w_out, l = close
        args += [ya, ys, mod, w_out]
        in_specs += [chunked, rows(SSM_W), _mod_spec(l, group, 2), _layer_spec(w_out, l)]
        if not opens:
            args.append(final_g)
            in_specs.append(_full(final_g.shape))
    if opens:
        norm_g, w_in, cos_t, sm_t, l = open_
        args += [norm_g, mod, mod, w_in, cos_t, sm_t]
        per_step = pl.BlockSpec((block_rows // group.n_batch, LANES), lambda i: (i, 0))
        in_specs += [_layer_spec(norm_g, l), _mod_spec(l, group, 1), _mod_spec(l, group, 0), _layer_spec(w_in, l),
                     per_step, per_step]
    if closes or batch_major_in:
        out_specs.append(blocked3 if batch_major_out else rows(D_MODEL))
        out_shape.append(jax.ShapeDtypeStruct((group.n_batch, t_rows // group.n_batch, D_MODEL), F32)
                         if batch_major_out else flat(D_MODEL))
    if opens:
        out_specs += [chunked, rows(KV_W), rows(KV_W), chunked, rows(SSM_W), rows(SSM_W)]
        out_shape += [chunked_shape, flat(KV_W), flat(KV_W), chunked_shape, flat(SSM_W), flat(SSM_W)]
    scratch = [pltpu.VMEM((D_MODEL // LANES, block_rows, LANES), F32)] if batch_major else []
    if opens:
        scratch.append(pltpu.VMEM((2, block_rows, LANES), F32))
    outs = pl.pallas_call(
        functools.partial(_boundary_kernel, n_batch=group.n_batch, closes=closes, opens=opens,
                          batch_major=batch_major, block_rows=block_rows, n_parts=BOUNDARY_PARTS),
        grid=(t_rows // block_rows,),
        in_specs=in_specs,
        out_specs=out_specs,
        out_shape=out_shape,
        scratch_shapes=scratch,
        compiler_params=_params(("parallel",)),
        name="layer_boundary",
    )(*args)
    x_rows = outs[0] if (closes or batch_major_in) else x
    return x_rows, (tuple(outs[-6:]) if opens else None)


def _pad_queries(chunks):
    lane = lax.broadcasted_iota(jnp.int32, chunks[0].shape, 1)
    pieces = []
    for chunk in chunks:
        pieces.append(jnp.where(lane < HEAD_DIM, chunk, 0.0))
        pieces.append(jnp.where(lane >= HEAD_DIM, chunk, 0.0))
    return jnp.concatenate(pieces, axis=0).astype(BF16)


def _unpad_outputs(o, t):
    lane = lax.broadcasted_iota(jnp.int32, (t, LANES), 1)
    chunks = []
    for c in range(N_CHUNKS):
        lo = o[(2 * c) * t:(2 * c + 1) * t]
        hi = o[(2 * c + 1) * t:(2 * c + 2) * t]
        chunks.append(jnp.where(lane < HEAD_DIM, lo, hi))
    return chunks


def _gated_store(o_ref, ga_ref, sel, ya_chunks):
    for c, ya in enumerate(ya_chunks):
        o_ref[c, sel, :] = ya * ga_ref[c, sel, :]


def _scores(qp, k, scale=HEAD_DIM ** -0.5):
    return lax.dot_general(qp, k.astype(BF16), (((1,), (1,)), ((), ())), preferred_element_type=F32) * scale


def _prompt_attention_stages(q_ref, kp_ref, kc_ref, vp_ref, vc_ref, za_ref, sink_ref, o_ref, n_batch, blk):
    assert blk == WINDOW
    n = pl.program_id(0)
    qb = blk // 2
    keys = blk + qb
    cols = 2 * LANES
    j = lax.broadcasted_iota(jnp.int32, (keys, cols), 0)
    tq = lax.broadcasted_iota(jnp.int32, (keys, cols), 1) % qb
    masks = [(j >= jnp.maximum(tq, jnp.where(n == 0, blk - h * qb, 0))) & (j <= tq + WINDOW) for h in range(2)]
    low = lax.broadcasted_iota(jnp.int32, (qb, LANES), 1) < HEAD_DIM
    ones = jnp.ones((SUBLANES, keys), F32)
    units = [(h, pr) for h in range(2) for pr in range(N_CHUNKS // 2)]
    sinks = [jnp.concatenate([sink_ref[slot:slot + 1, :qb] for slot in range(4 * pr, 4 * pr + 4)], axis=1) * LOG2_E
             for pr in range(N_CHUNKS // 2)]

    def query_rows(b, h):
        return pl.ds(b + h * qb * n_batch, qb, stride=n_batch)

    def window(prev_ref, cur_ref, b, h):
        if h == 0:
            return jnp.concatenate([prev_ref[pl.ds(b, blk, stride=n_batch), :], cur_ref[query_rows(b, 0), :]], axis=0)
        return jnp.concatenate([prev_ref[query_rows(b, 1), :], cur_ref[pl.ds(b, blk, stride=n_batch), :]], axis=0)

    def score_stage(b):
        sts = []
        for h in range(2):
            k = window(kp_ref, kc_ref, b, h).astype(BF16)
            for pr in range(N_CHUNKS // 2):
                pieces = []
                for c in (2 * pr, 2 * pr + 1):
                    qc = q_ref[c, query_rows(b, h), :]
                    pieces += [jnp.where(low, qc, 0.0), jnp.where(low, 0.0, qc)]
                qp = jnp.concatenate(pieces, axis=0).astype(BF16)
                sts.append(_scores(k, qp, HEAD_DIM ** -0.5 * LOG2_E))
        return sts

    def value_stage(b, sts):
        vt_aug = [jnp.concatenate([window(vp_ref, vc_ref, b, h).T, ones], axis=0).astype(BF16) for h in range(2)]

        def finish(h, pr, ot, m):
            den = ot[KV_W:KV_W + 1, :] + jnp.exp2(sinks[pr] - m)
            o = (ot[:KV_W, :] * (1.0 / den)).T
            sel = query_rows(b, h)
            for i, c in enumerate((2 * pr, 2 * pr + 1)):
                lo = 2 * i * qb
                ya = jnp.where(low, o[lo:lo + qb], o[lo + qb:lo + 2 * qb])
                o_ref[c, sel, :] = ya * za_ref[c, sel, :]

        pending = None
        for (h, pr), st in zip(units, sts):
            st = jnp.where(masks[h], st, NEG_INF)
            m = jnp.maximum(jnp.max(st, axis=0, keepdims=True), sinks[pr])
            p = jnp.exp2(st - m).astype(BF16)
            ot = jnp.dot(vt_aug[h], p, preferred_element_type=F32)
            if pending is not None:
                finish(*pending)
            pending = (h, pr, ot, m)
        finish(*pending)

    return score_stage, value_stage


def _attn_sample_kernel(q_ref, k_ref, v_ref, za_ref, ck_ref, cv_ref, sink_ref, o_ref, nk_ref, nv_ref,
                        *, n_batch, t_new, group):
    g = pl.program_id(0)
    win = ck_ref.shape[1]
    rows = N_Q_HEADS * t_new
    tq = lax.broadcasted_iota(jnp.int32, (rows, 2 * win), 0) % t_new
    j = lax.broadcasted_iota(jnp.int32, (rows, 2 * win), 1)
    dist_c = tq + win - j
    mask = ((j < win) & (dist_c >= 0) & (dist_c <= WINDOW)) | ((j >= win) & (j - win <= tq) & (j - win < t_new))
    sink = sink_ref[...]
    pad = jnp.zeros((win - t_new, KV_W), F32)
    ones = jnp.ones((2 * win, LANES), BF16)
    batch = range(group)
    sels = [pl.ds(g * group + i, t_new, stride=n_batch) for i in batch]

    scores = []
    for i in batch:
        qp = _pad_queries([q_ref[c, sels[i], :] for c in range(N_CHUNKS)])
        keys = jnp.concatenate([ck_ref[i], k_ref[sels[i], :], pad], axis=0)
        scores.append(jnp.where(mask, _scores(qp, keys), NEG_INF))
    outs = []
    for i in batch:
        values = jnp.concatenate([cv_ref[i], v_ref[sels[i], :], pad], axis=0).astype(BF16)
        s = scores[i]
        m = jnp.maximum(jnp.max(s, axis=-1, keepdims=True), sink[:, :1])
        p = jnp.exp(s - m).astype(BF16)
        o = jnp.dot(p, jnp.concatenate([values, ones], axis=1), preferred_element_type=F32)
        outs.append(o[:, :KV_W] / (o[:, KV_W:] + jnp.exp(sink - m)))
    for i in batch:
        _gated_store(o_ref, za_ref, sels[i], _unpad_outputs(outs[i], t_new))
        nk_ref[i, 0:win - t_new, :] = ck_ref[i, t_new:win, :]
        nk_ref[i, win - t_new:win, :] = k_ref[sels[i], :]
        nv_ref[i, 0:win - t_new, :] = cv_ref[i, t_new:win, :]
        nv_ref[i, win - t_new:win, :] = v_ref[sels[i], :]


def attention_sample(q, k, v, za, window_k, window_v, sink_rows, l, n_batch, t_new, group):
    t_rows = k.shape[0]
    win = window_k.shape[2]
    window = pl.BlockSpec((None, group, win, KV_W), lambda i: (l, i, 0, 0))
    chunked = _full((N_CHUNKS, t_rows, LANES))
    return pl.pallas_call(
        functools.partial(_attn_sample_kernel, n_batch=n_batch, t_new=t_new, group=group),
        grid=(n_batch // group,),
        in_specs=[chunked, _full((t_rows, KV_W)), _full((t_rows, KV_W)), chunked,
                  window, window, _layer_spec(sink_rows, l)],
        out_specs=[chunked, window, window],
        out_shape=[jax.ShapeDtypeStruct((N_CHUNKS, t_rows, LANES), F32),
                   jax.ShapeDtypeStruct(window_k.shape, F32), jax.ShapeDtypeStruct(window_v.shape, F32)],
        input_output_aliases={4: 1, 5: 2},
        compiler_params=_params(("arbitrary",)),
        name="attention_sample",
    )(q, k, v, za, window_k, window_v, sink_rows)


def _ssm_stages(u_ref, zs_ref, h0re_ref, h0im_ref, are_ref, aim_ref, wb_ref, wc_ref, d_ref, wg_ref, bg_ref,
                o_ref, hre_ref, him_ref, x_scr, n_batch, sub_rows, slab):
    step = pl.program_id(0)

    @pl.when(step == 0)
    def _():
        hre_ref[...] = h0re_ref[...]
        him_ref[...] = h0im_ref[...]

    rows = u_ref.shape[0]
    half_in = SSM_W // 2
    n_sub = rows // sub_rows
    n_tiles = n_batch // SUBLANES
    t_steps = sub_rows // n_batch
    n_buf = x_scr.shape[0]
    items = [(sc, s) for sc in range(n_sub) for s in range(2)]

    def expand(i):
        sc, s = items[i]
        ub = u_ref[sc * sub_rows:(sc + 1) * sub_rows, s * half_in:(s + 1) * half_in].astype(BF16)
        x_scr[i % n_buf] = jnp.dot(ub, wb_ref[s], preferred_element_type=F32)

    def recur(i):
        sc, s = items[i]
        buf = i % n_buf
        for jb in range(HALF_STATE // slab):
            n_lo = s * HALF_STATE + jb * slab
            ar = jnp.broadcast_to(are_ref[:, n_lo:n_lo + slab], (SUBLANES, slab))
            ai = jnp.broadcast_to(aim_ref[:, n_lo:n_lo + slab], (SUBLANES, slab))
            re = slice(jb * slab, (jb + 1) * slab)
            im = slice(HALF_STATE + jb * slab, HALF_STATE + (jb + 1) * slab)
            for tile in range(n_tiles):
                r0 = tile * SUBLANES
                hr = hre_ref[r0:r0 + SUBLANES, n_lo:n_lo + slab]
                hi = him_ref[r0:r0 + SUBLANES, n_lo:n_lo + slab]
                for t in range(t_steps):
                    row = slice(t * n_batch + r0, t * n_batch + r0 + SUBLANES)
                    hr, hi = (ar * hr - ai * hi + x_scr[buf, row, re], ar * hi + ai * hr + x_scr[buf, row, im])
                    x_scr[buf, row, re] = hr
                    x_scr[buf, row, im] = hi
                hre_ref[r0:r0 + SUBLANES, n_lo:n_lo + slab] = hr
                him_ref[r0:r0 + SUBLANES, n_lo:n_lo + slab] = hi

    y_halves = {}

    def contract(i):
        sc, s = items[i]
        y_halves[s] = jnp.dot(x_scr[i % n_buf].astype(BF16), wc_ref[s], preferred_element_type=F32)
        if s == 1:
            r = slice(sc * sub_rows, (sc + 1) * sub_rows)
            y = jnp.concatenate([y_halves[0], y_halves[1]], axis=1) + d_ref[...] * u_ref[r, :]
            y = jax.nn.gelu(y)
            gate = jax.nn.sigmoid(jnp.dot(y.astype(BF16), wg_ref[...], preferred_element_type=F32) + bg_ref[...])
            o_ref[r, :] = (y * gate * zs_ref[r, :]).astype(o_ref.dtype)

    return len(items), expand, recur, contract


def _ssm_kernel(*refs, n_batch, sub_rows, slab):
    n_items, expand, recur, contract = _ssm_stages(*refs, n_batch, sub_rows, slab)
    expand(0)
    for i in range(n_items + 1):
        if i + 1 < n_items:
            expand(i + 1)
        if i < n_items:
            recur(i)
        if i >= 1:
            contract(i - 1)


N_ATTN_INPUTS = 7
N_SSM_INPUTS = 11


def _prompt_branches_kernel(*refs, n_batch, blk, sub_rows, slab):
    n_in = N_ATTN_INPUTS + N_SSM_INPUTS
    attn_refs = refs[:N_ATTN_INPUTS] + refs[n_in:n_in + 1]
    ssm_refs = refs[N_ATTN_INPUTS:n_in] + refs[n_in + 1:]
    score_stage, value_stage = _prompt_attention_stages(*attn_refs, n_batch, blk)
    n_items, expand, recur, contract = _ssm_stages(*ssm_refs, n_batch, sub_rows, slab)
    assert n_items == n_batch
    expand(0)
    scores = score_stage(0)
    for i in range(n_items + 1):
        if i + 1 < n_items:
            expand(i + 1)
            next_scores = score_stage(i + 1)
        if i < n_items:
            recur(i)
            value_stage(i, scores)
            scores = next_scores
        if i >= 1:
            contract(i - 1)


def _ssm_call_parts(u, h0_re, h0_im, abar_re, abar_im, w_b, w_c, d, w_glu, b_glu, l, l_state, n_batch, r):
    t_rows = u.shape[0]
    sub_rows = max(SSM_SUB_ROWS, n_batch)
    rows = pl.BlockSpec((r, SSM_W), lambda i: (i, 0))
    state = _full((n_batch, N_STATE))
    layer = lambda a: _layer_spec(a, l)
    in_specs = [rows, rows, _layer_spec(h0_re, l_state), _layer_spec(h0_im, l_state), layer(abar_re), layer(abar_im),
                layer(w_b), layer(w_c), layer(d), layer(w_glu), layer(b_glu)]
    out_shape = [jax.ShapeDtypeStruct((t_rows, SSM_W), BF16),
                 jax.ShapeDtypeStruct((n_batch, N_STATE), F32), jax.ShapeDtypeStruct((n_batch, N_STATE), F32)]
    scratch = [pltpu.VMEM((SSM_BUFFERS, sub_rows, 2 * HALF_STATE), F32)]
    return in_specs, [rows, state, state], out_shape, scratch, sub_rows


def ssm_branch(u, zs, h0_re, h0_im, abar_re, abar_im, w_b, w_c, d, w_glu, b_glu, l, l_state, n_batch, t_chunk):
    r = t_chunk * n_batch
    in_specs, out_specs, out_shape, scratch, sub_rows = _ssm_call_parts(
        u, h0_re, h0_im, abar_re, abar_im, w_b, w_c, d, w_glu, b_glu, l, l_state, n_batch, r)
    return pl.pallas_call(
        functools.partial(_ssm_kernel, n_batch=n_batch, sub_rows=sub_rows, slab=SSM_SLAB),
        grid=(u.shape[0] // r,),
        in_specs=in_specs,
        out_specs=out_specs,
        out_shape=out_shape,
        scratch_shapes=scratch,
        compiler_params=_params(("arbitrary",)),
        name="ssm_branch",
    )(u, zs, h0_re, h0_im, abar_re, abar_im, w_b, w_c, d, w_glu, b_glu)


def prompt_branches(q, k, v, za, sinks, u, zs, h0_re, h0_im, abar_re, abar_im, w_b, w_c, d, w_glu, b_glu,
                    l, l_state, n_batch, blk):
    t_rows = k.shape[0]
    r = blk * n_batch
    cur = lambda w: pl.BlockSpec((r, w), lambda i: (i, 0))
    prev = lambda w: pl.BlockSpec((r, w), lambda i: (jnp.maximum(i - 1, 0), 0))
    chunked = pl.BlockSpec((N_CHUNKS, r, LANES), lambda i: (0, i, 0))
    ssm_in, ssm_out, ssm_shape, scratch, sub_rows = _ssm_call_parts(
        u, h0_re, h0_im, abar_re, abar_im, w_b, w_c, d, w_glu, b_glu, l, l_state, n_batch, r)
    attn_in = [chunked, prev(KV_W), cur(KV_W), prev(KV_W), cur(KV_W), chunked, _layer_spec(sinks, l)]
    assert len(attn_in) == N_ATTN_INPUTS and len(ssm_in) == N_SSM_INPUTS
    return pl.pallas_call(
        functools.partial(_prompt_branches_kernel, n_batch=n_batch, blk=blk, sub_rows=sub_rows, slab=SSM_SLAB),
        grid=(t_rows // r,),
        in_specs=attn_in + ssm_in,
        out_specs=[chunked] + ssm_out,
        out_shape=[jax.ShapeDtypeStruct((N_CHUNKS, t_rows, LANES), F32)] + ssm_shape,
        scratch_shapes=scratch,
        compiler_params=_params(("arbitrary",)),
        name="prompt_branches",
    )(q, k, k, v, v, za, sinks, u, zs, h0_re, h0_im, abar_re, abar_im, w_b, w_c, d, w_glu, b_glu)


def _rope_tables(pos):
    half = ROT_DIM // 2
    inv = ROPE_THETA ** (-jnp.arange(half, dtype=F32) / half)
    ang = pos.astype(F32)[:, None] * inv[None, :]
    d = np.arange(LANES) % HEAD_DIM
    idx = d % half
    cos = jnp.where(d < ROT_DIM, jnp.cos(ang)[:, idx], 1.0)
    sin = jnp.sin(ang)[:, idx]
    sm = jnp.where(d < half, -sin, jnp.where(d < ROT_DIM, sin, 0.0))
    return cos, sm


def _keep_diagonal_blocks(w, row_block, col_block):
    rows, cols = w.shape[-2:]
    keep = (np.arange(rows)[:, None] // row_block) == (np.arange(cols)[None, :] // col_block)
    return jnp.where(keep, w, 0.0)


def _expand_weights(bb):
    depth = bb.shape[0]
    hg = N_SSM_GROUPS // 2
    halves = bb.reshape(depth, SSM_GROUP_CH, 2, HALF_STATE).transpose(0, 2, 1, 3)
    tiled = jnp.broadcast_to(halves[:, :, None], (depth, 2, hg, SSM_GROUP_CH, HALF_STATE))
    return _keep_diagonal_blocks(tiled.reshape(depth, 2, hg * SSM_GROUP_CH, HALF_STATE), SSM_GROUP_CH, SSM_STATE)


def _contract_weights(c):
    depth = c.shape[0]
    hg = N_SSM_GROUPS // 2
    rows = c.reshape(depth, 2, hg, SSM_GROUP_CH, SSM_STATE).transpose(0, 1, 2, 4, 3).reshape(
        depth, 2, HALF_STATE, SSM_GROUP_CH)
    return _keep_diagonal_blocks(jnp.tile(rows, (1, 1, 1, hg)), SSM_STATE, SSM_GROUP_CH)


def _prepare_weights(w_in, w_out, attn_sinks, bb_re, bb_im, ssm_c_re, ssm_c_im, t_new):
    depth = w_in.shape[0]
    assert HEAD_PERM == tuple(kv * Q_GROUP + g for g in range(Q_GROUP) for kv in range(N_KV_HEADS))

    def permute_heads(w, axis):
        shape = w.shape
        w = w.reshape(shape[:axis] + (N_KV_HEADS, Q_GROUP, HEAD_DIM) + shape[axis + 1:])
        return jnp.swapaxes(w, axis, axis + 1).reshape(shape)

    base_za = ATTN_W + 2 * KV_W
    w_in_p = jnp.concatenate([permute_heads(w_in[:, :, :ATTN_W], 2),
                              permute_heads(w_in[:, :, base_za:base_za + ATTN_W], 2),
                              w_in[:, :, base_za + ATTN_W:], w_in[:, :, ATTN_W:base_za]], axis=2).astype(BF16)
    w_out_p = jnp.concatenate([permute_heads(w_out[:, :ATTN_W], 1), w_out[:, ATTN_W:]], axis=1).astype(BF16)
    sinks = jnp.swapaxes(attn_sinks.reshape(depth, N_KV_HEADS, Q_GROUP), 1, 2).reshape(depth, N_Q_HEADS)
    sink_lanes = jnp.broadcast_to(sinks[:, :, None], (depth, N_Q_HEADS, LANES))
    sink_rows = jnp.broadcast_to(jnp.repeat(sinks, t_new, axis=1)[:, :, None], (depth, N_Q_HEADS * t_new, LANES))
    w_b = jnp.concatenate([_expand_weights(bb_re), _expand_weights(bb_im)], axis=3).astype(BF16)
    w_c = jnp.concatenate([_contract_weights(ssm_c_re), -_contract_weights(ssm_c_im)], axis=2).astype(BF16)
    return w_in_p, w_out_p, sink_lanes, sink_rows, w_b, w_c


def kernel(x_prompt, x_sample, cache_k, cache_v, state_ssm_re, state_ssm_im, c_prompt, c_sample, norm_g, w_ada, b_ada, w_in, attn_sinks, ssm_a_re, ssm_a_im, ssm_log_dt, ssm_b_re, ssm_b_im, ssm_c_re, ssm_c_im, ssm_d, w_glu, b_glu, w_out, final_g):
    depth = w_in.shape[0]
    nb_p, seq, _ = x_prompt.shape
    nb_s, t_new, _ = x_sample.shape
    win = cache_k.shape[2]
    blk = WINDOW

    sample = Group(n_batch=nb_s, mod_block=0)
    prompt = Group(n_batch=nb_p, mod_block=nb_s // nb_p)
    mod = adaln_mod(jnp.concatenate([c_sample, c_prompt], axis=0), w_ada, b_ada)
    abar_re, abar_im, bb_re, bb_im = ssm_discretise(ssm_a_re, ssm_a_im, ssm_log_dt, ssm_b_re, ssm_b_im)
    w_in_p, w_out_p, sink_lanes, sink_rows, w_b, w_c = _prepare_weights(
        w_in, w_out, attn_sinks, bb_re, bb_im, ssm_c_re, ssm_c_im, t_new)
    norm_g3 = norm_g.reshape(depth, 1, D_MODEL)
    d3 = ssm_d.reshape(depth, 1, SSM_W)
    w_glu_b = w_glu.astype(BF16)
    b_glu3 = b_glu.reshape(depth, 1, SSM_W)
    final_g2 = final_g.reshape(1, D_MODEL)
    ssm_weights = (abar_re, abar_im, w_b, w_c, d3, w_glu_b, b_glu3)

    cos_p, sm_p = _rope_tables(jnp.arange(seq))
    cos_s, sm_s = _rope_tables(PAST_LEN + jnp.arange(t_new))

    xp = x_prompt
    xs = x_sample.transpose(1, 0, 2).reshape(t_new * nb_s, D_MODEL)
    zeros_state = jnp.zeros((1, nb_p, N_STATE), F32)
    new_k = cache_k.reshape(depth, nb_s, win, KV_W)
    new_v = cache_v.reshape(depth, nb_s, win, KV_W)
    state_re3 = state_ssm_re.reshape(depth, nb_s, N_STATE)
    state_im3 = state_ssm_im.reshape(depth, nb_s, N_STATE)

    outs = {k: [] for k in ("kp", "vp", "rp", "ip", "rs", "is")}
    rows_p, rows_s = BOUNDARY_ROWS, t_new * nb_s
    opening = lambda l, cos_t, sm_t: (norm_g3, w_in_p, cos_t, sm_t, l)
    xp, proj_p = layer_boundary(xp, mod, prompt, rows_p, open_=opening(0, cos_p, sm_p))
    xs, proj_s = layer_boundary(xs, mod, sample, rows_s, open_=opening(0, cos_s, sm_s))
    for l in range(depth):
        last = l == depth - 1

        q, k, v, za, u, zs = proj_p
        ya, ys, hre, him = prompt_branches(q, k, v, za, sink_lanes, u, zs, zeros_state, zeros_state, *ssm_weights,
                                           l, 0, nb_p, blk)
        xp, proj_p = layer_boundary(xp, mod, prompt, rows_p, close=(ya, ys, w_out_p, l),
                                    open_=None if last else opening(l + 1, cos_p, sm_p),
                                    final_g=final_g2, batch_major_out=last)
        outs["kp"].append(k[(seq - win) * nb_p:])
        outs["vp"].append(v[(seq - win) * nb_p:])
        outs["rp"].append(hre)
        outs["ip"].append(him)

        q, k, v, za, u, zs = proj_s
        ya, new_k, new_v = attention_sample(q, k, v, za, new_k, new_v, sink_rows, l, nb_s, t_new, SAMPLE_GROUP)
        ys, hre, him = ssm_branch(u, zs, state_re3, state_im3, *ssm_weights, l, l, nb_s, t_new)
        xs, proj_s = layer_boundary(xs, mod, sample, rows_s, close=(ya, ys, w_out_p, l),
                                    open_=None if last else opening(l + 1, cos_s, sm_s), final_g=final_g2)
        outs["rs"].append(hre)
        outs["is"].append(him)

    y_prompt = xp
    y_sample = xs.reshape(t_new, nb_s, D_MODEL).transpose(1, 0, 2)
    st = lambda key: jnp.stack(outs[key])
    window = lambda key: st(key).reshape(depth, win, nb_p, N_KV_HEADS, HEAD_DIM).transpose(0, 2, 1, 3, 4)
    state = lambda key, nb: st(key).reshape(depth, nb, N_SSM_GROUPS, SSM_STATE)
    return (y_prompt, y_sample, window("kp"), window("vp"), state("rp", nb_p), state("ip", nb_p),
            new_k.reshape(cache_k.shape), new_v.reshape(cache_v.shape), state("rs", nb_s), state("is", nb_s))
```

```python
import functools
from typing import NamedTuple

import jax
import jax.numpy as jnp
import numpy as np
from jax import lax
from jax.experimental import pallas as pl
from jax.experimental.pallas import tpu as pltpu

F32 = jnp.float32
BF16 = jnp.bfloat16

D_MODEL = 1024
HEAD_DIM = 64
ATTN_W = 512
N_Q_HEADS = 8
N_KV_HEADS = 2
Q_GROUP = N_Q_HEADS // N_KV_HEADS
KV_W = 128
WINDOW = 128
ROT_DIM = 16
ROPE_THETA = 500000.0
SSM_W = 512
SSM_GROUP_CH = 16
N_SSM_GROUPS = 32
SSM_STATE = 64
N_STATE = N_SSM_GROUPS * SSM_STATE
COL_Q, COL_KV, COL_ZA = 0, ATTN_W, ATTN_W + 2 * KV_W
COL_U, COL_ZS = COL_ZA + ATTN_W, COL_ZA + ATTN_W + SSM_W
HALF_STATE = N_STATE // 2
EPS = 1e-6
NEG_INF = -1e30
LOG2_E = 1.4426950408889634
PAST_LEN = 8192

LANES = 128
N_CHUNKS = ATTN_W // LANES
BOUNDARY_ROWS = 1024
BOUNDARY_PARTS = 2
SSM_SLAB = 4 * LANES
SAMPLE_GROUP = 8
SSM_SUB_ROWS = 256
SSM_BUFFERS = 3
SUBLANES = 8
VMEM_LIMIT = 56 * 1024 * 1024

HEAD_PERM = (0, 4, 1, 5, 2, 6, 3, 7)


class Group(NamedTuple):
    n_batch: int
    mod_block: int


def _params(sem):
    return pltpu.CompilerParams(dimension_semantics=sem, vmem_limit_bytes=VMEM_LIMIT)


def _full(shape):
    return pl.BlockSpec(shape, lambda *_: (0,) * len(shape))


def _mod_kernel(c_ref, w_ref, b_ref, o_ref):
    c = c_ref[...]
    a = (c * jax.nn.sigmoid(c)).astype(BF16)
    o_ref[...] = jnp.dot(a, w_ref[...].astype(BF16), preferred_element_type=F32) + b_ref[...]


def adaln_mod(c_all, w_ada, b_ada):
    depth = w_ada.shape[0]
    n = c_all.shape[0]
    nj = 3 * D_MODEL // D_MODEL
    return pl.pallas_call(
        _mod_kernel,
        grid=(depth, nj),
        in_specs=[
            pl.BlockSpec((n, D_MODEL), lambda l, j: (0, 0)),
            pl.BlockSpec((None, D_MODEL, D_MODEL), lambda l, j: (l, 0, j)),
            pl.BlockSpec((None, 1, D_MODEL), lambda l, j: (l, 0, j)),
        ],
        out_specs=pl.BlockSpec((None, n, D_MODEL), lambda l, j: (l, 0, j)),
        out_shape=jax.ShapeDtypeStruct((depth, n, 3 * D_MODEL), F32),
        compiler_params=_params(("arbitrary", "arbitrary")),
        name="adaln_mod",
    )(c_all, w_ada, b_ada.reshape(depth, 1, 3 * D_MODEL))


def _disc_kernel(are_ref, aim_ref, ldt_ref, bre_ref, bim_ref, abre_ref, abim_ref, bbre_ref, bbim_ref):
    a_re = are_ref[...]
    a_im = aim_ref[...]
    dt = jnp.exp(ldt_ref[...])
    mag = jnp.exp(a_re * dt)
    abar_re = mag * jnp.cos(a_im * dt)
    abar_im = mag * jnp.sin(a_im * dt)
    den = a_re * a_re + a_im * a_im
    nr = abar_re - 1.0
    coef_re = (nr * a_re + abar_im * a_im) / den
    coef_im = (abar_im * a_re - nr * a_im) / den
    br = bre_ref[...]
    bi = bim_ref[...]
    abre_ref[...] = abar_re
    abim_ref[...] = abar_im
    bbre_ref[...] = coef_re * br - coef_im * bi
    bbim_ref[...] = coef_re * bi + coef_im * br


def ssm_discretise(a_re, a_im, log_dt, b_re, b_im):
    depth = a_re.shape[0]
    a_re = a_re.reshape(depth, 1, N_STATE)
    a_im = a_im.reshape(depth, 1, N_STATE)
    ldt = jnp.repeat(log_dt, SSM_STATE, axis=1).reshape(depth, 1, N_STATE)
    bt_re = b_re.reshape(depth, N_STATE, SSM_GROUP_CH).transpose(0, 2, 1)
    bt_im = b_im.reshape(depth, N_STATE, SSM_GROUP_CH).transpose(0, 2, 1)
    row = pl.BlockSpec((None, 1, N_STATE), lambda l: (l, 0, 0))
    mat = pl.BlockSpec((None, SSM_GROUP_CH, N_STATE), lambda l: (l, 0, 0))
    return pl.pallas_call(
        _disc_kernel,
        grid=(depth,),
        in_specs=[row, row, row, mat, mat],
        out_specs=[row, row, mat, mat],
        out_shape=[jax.ShapeDtypeStruct((depth, 1, N_STATE), F32)] * 2
        + [jax.ShapeDtypeStruct((depth, SSM_GROUP_CH, N_STATE), F32)] * 2,
        compiler_params=_params(("arbitrary",)),
        name="ssm_discretise",
    )(a_re, a_im, ldt, bt_re, bt_im)


def _rope(x, cos, sm):
    lane = lax.broadcasted_iota(jnp.int32, x.shape, 1) % HEAD_DIM
    partner = jnp.where(lane < ROT_DIM // 2, pltpu.roll(x, LANES - ROT_DIM // 2, 1), pltpu.roll(x, ROT_DIM // 2, 1))
    return x * cos + partner * sm


def _to_time_major(x_ref, scr, n_batch):
    steps = x_ref.shape[1]
    for b in range(n_batch):
        for c in range(D_MODEL // LANES):
            scr[c, pl.ds(b, steps, stride=n_batch), :] = x_ref[b, :, c * LANES:(c + 1) * LANES]
    return jnp.concatenate([scr[c] for c in range(D_MODEL // LANES)], axis=1)


def _from_time_major(x, o_ref, scr, n_batch):
    steps = o_ref.shape[1]
    for c in range(D_MODEL // LANES):
        scr[c] = x[:, c * LANES:(c + 1) * LANES]
    for b in range(n_batch):
        for c in range(D_MODEL // LANES):
            o_ref[b, :, c * LANES:(c + 1) * LANES] = scr[c, pl.ds(b, steps, stride=n_batch), :]


def _rms(x, g_ref):
    ms = jnp.mean(x * x, axis=-1, keepdims=True)
    return x * lax.rsqrt(ms + EPS) * g_ref[...]


def _per_sequence(y, n_batch, fn):
    rows = y.shape[0]
    return fn(y.reshape(rows // n_batch, n_batch, D_MODEL)).reshape(rows, D_MODEL)


def _pair_heads(chunks):
    low = lax.broadcasted_iota(jnp.int32, chunks[0].shape, 1) < HEAD_DIM
    swap = lambda x: pltpu.roll(x, HEAD_DIM, 1)
    n0, n1, n2, n3 = chunks
    return [jnp.where(low, n0, swap(n2)), jnp.where(low, swap(n0), n2),
            jnp.where(low, n1, swap(n3)), jnp.where(low, swap(n1), n3)]


def _unpair_heads(chunks):
    low = lax.broadcasted_iota(jnp.int32, chunks[0].shape, 1) < HEAD_DIM
    swap = lambda x: pltpu.roll(x, HEAD_DIM, 1)
    p0, p1, p2, p3 = chunks
    return [jnp.where(low, p0, swap(p1)), jnp.where(low, p2, swap(p3)),
            jnp.where(low, swap(p0), p1), jnp.where(low, swap(p2), p3)]


def _residual_update(x, rows, ya_ref, ys_ref, gate_ref, w_ref, n_batch):
    ya = _unpair_heads([ya_ref[c, rows, :] for c in range(N_CHUNKS)])
    mixed = jnp.concatenate([y.astype(BF16) for y in ya] + [ys_ref[rows, :]], axis=1)
    mix = jnp.dot(mixed, w_ref[...], preferred_element_type=F32)
    return x + _per_sequence(mix, n_batch, lambda m: m * gate_ref[...][None])


def _modulated_norm(x, g_ref, scale_ref, shift_ref, n_batch):
    y = _per_sequence(_rms(x, g_ref), n_batch, lambda y3: y3 * (1.0 + scale_ref[...])[None] + shift_ref[...][None])
    return y.astype(BF16)


def _silu(z):
    return z * jax.nn.sigmoid(z)


def _project(h, rows, w_ref, rope_scr, q_ref, k_ref, v_ref, ga_ref, u_ref, gs_ref):
    cos = rope_scr[0, rows, :]
    sm = rope_scr[1, rows, :]

    def proj(lo, width):
        return jnp.dot(h, w_ref[:, lo:lo + width], preferred_element_type=F32)

    chunks_of = lambda a: [a[:, c * LANES:(c + 1) * LANES] for c in range(N_CHUNKS)]
    q = _pair_heads(chunks_of(proj(COL_Q, ATTN_W)))
    ga = _pair_heads(chunks_of(_silu(proj(COL_ZA, ATTN_W))))
    for c in range(N_CHUNKS):
        q_ref[c, rows, :] = _rope(q[c], cos, sm)
        ga_ref[c, rows, :] = ga[c]
    u_ref[rows, :] = proj(COL_U, SSM_W)
    gs_ref[rows, :] = _silu(proj(COL_ZS, SSM_W))
    kv = proj(COL_KV, 2 * KV_W)
    k_ref[rows, :] = _rope(kv[:, :KV_W], cos, sm)
    v_ref[rows, :] = kv[:, KV_W:]


def _boundary_kernel(*refs, n_batch, closes, opens, batch_major, block_rows, n_parts):
    refs = list(refs)
    x_ref = refs.pop(0)
    close_refs = [refs.pop(0) for _ in range(4)] if closes else None
    final_g_ref = refs.pop(0) if closes and not opens else None
    open_refs = [refs.pop(0) for _ in range(6)] if opens else None
    x_out_ref = refs.pop(0) if closes or batch_major else None
    proj_out_refs = [refs.pop(0) for _ in range(6)] if opens else None
    scr = refs.pop(0) if batch_major else None
    rope_scr = refs.pop(0) if opens else None

    if opens:
        g_ref, scale_ref, shift_ref, w_in_ref, cos_ref, sm_ref = open_refs
        steps = cos_ref.shape[0]
        for b in range(n_batch):
            rope_scr[0, pl.ds(b, steps, stride=n_batch), :] = cos_ref[...]
            rope_scr[1, pl.ds(b, steps, stride=n_batch), :] = sm_ref[...]

    part = block_rows // n_parts
    groups = [slice(i * part, (i + 1) * part) for i in range(n_parts)]
    x_full = _to_time_major(x_ref, scr, n_batch) if (batch_major and not closes) else None

    def updated(rows):
        x = x_full[rows, :] if x_full is not None else x_ref[rows, :]
        return _residual_update(x, rows, *close_refs, n_batch) if closes else x

    def normed(x, rows):
        if opens:
            if x_out_ref is not None:
                x_out_ref[rows, :] = x
            return _modulated_norm(x, g_ref, scale_ref, shift_ref, n_batch)
        return _rms(x, final_g_ref)

    xs = [updated(rows) for rows in groups]
    hs = [None] * n_parts
    hs[0] = normed(xs[0], groups[0])
    for i, rows in enumerate(groups):
        if opens:
            _project(hs[i], rows, w_in_ref, rope_scr, *proj_out_refs)
        if i + 1 < n_parts:
            hs[i + 1] = normed(xs[i + 1], groups[i + 1])
    if not opens:
        y = jnp.concatenate(hs, axis=0)
        if batch_major:
            _from_time_major(y, x_out_ref, scr, n_batch)
        else:
            x_out_ref[...] = y


def _layer_spec(arr, l):
    tail = arr.shape[1:]
    return pl.BlockSpec((None,) + tail, lambda *_: (l,) + (0,) * len(tail))


def _mod_spec(l, group, part):
    return pl.BlockSpec((None, group.n_batch, D_MODEL), lambda *_: (l, group.mod_block, part))


def layer_boundary(x, mod, group, block_rows, close=None, open_=None, final_g=None, batch_major_out=False):
    closes, opens = close is not None, open_ is not None
    batch_major_in = x.ndim == 3
    assert not (batch_major_in and closes) and not (batch_major_out and opens)
    batch_major = batch_major_in or batch_major_out
    t_rows = x.shape[0] * x.shape[1] if batch_major_in else x.shape[0]
    rows = lambda w: pl.BlockSpec((block_rows, w), lambda i: (i, 0))
    chunked = pl.BlockSpec((N_CHUNKS, block_rows, LANES), lambda i: (0, i, 0))
    blocked3 = pl.BlockSpec((group.n_batch, block_rows // group.n_batch, D_MODEL), lambda i: (0, i, 0))
    flat = lambda w: jax.ShapeDtypeStruct((t_rows, w), F32)
    chunked_shape = jax.ShapeDtypeStruct((N_CHUNKS, t_rows, LANES), F32)

    args, in_specs, out_specs, out_shape = [x], [blocked3 if batch_major_in else rows(D_MODEL)], [], []
    if closes:
        ya, ys, w_out, l = close
        args += [ya, ys, mod, w_out]
        in_specs += [chunked, rows(SSM_W), _mod_spec(l, group, 2), _layer_spec(w_out, l)]
        if not opens:
            args.append(final_g)
            in_specs.append(_full(final_g.shape))
    if opens:
        norm_g, w_in, cos_t, sm_t, l = open_
        args += [norm_g, mod, mod, w_in, cos_t, sm_t]
        per_step = pl.BlockSpec((block_rows // group.n_batch, LANES), lambda i: (i, 0))
        in_specs += [_layer_spec(norm_g, l), _mod_spec(l, group, 1), _mod_spec(l, group, 0), _layer_spec(w_in, l),
                     per_step, per_step]
    if closes or batch_major_in:
        out_specs.append(blocked3 if batch_major_out else rows(D_MODEL))
        out_shape.append(jax.ShapeDtypeStruct((group.n_batch, t_rows // group.n_batch, D_MODEL), F32)
                         if batch_major_out else flat(D_MODEL))
    if opens:
        out_specs += [chunked, rows(KV_W), rows(KV_W), chunked, rows(SSM_W), rows(SSM_W)]
        out_shape += [chunked_shape, flat(KV_W), flat(KV_W), chunked_shape, flat(SSM_W), flat(SSM_W)]
    scratch = [pltpu.VMEM((D_MODEL // LANES, block_rows, LANES), F32)] if batch_major else []
    if opens:
        scratch.append(pltpu.VMEM((2, block_rows, LANES), F32))
    outs = pl.pallas_call(
        functools.partial(_boundary_kernel, n_batch=group.n_batch, closes=closes, opens=opens,
                          batch_major=batch_major, block_rows=block_rows, n_parts=BOUNDARY_PARTS),
        grid=(t_rows // block_rows,),
        in_specs=in_specs,
        out_specs=out_specs,
        out_shape=out_shape,
        scratch_shapes=scratch,
        compiler_params=_params(("parallel",)),
        name="layer_boundary",
    )(*args)
    x_rows = outs[0] if (closes or batch_major_in) else x
    return x_rows, (tuple(outs[-6:]) if opens else None)


def _pad_queries(chunks):
    lane = lax.broadcasted_iota(jnp.int32, chunks[0].shape, 1)
    pieces = []
    for chunk in chunks:
        pieces.append(jnp.where(lane < HEAD_DIM, chunk, 0.0))
        pieces.append(jnp.where(lane >= HEAD_DIM, chunk, 0.0))
    return jnp.concatenate(pieces, axis=0).astype(BF16)


def _unpad_outputs(o, t):
    lane = lax.broadcasted_iota(jnp.int32, (t, LANES), 1)
    chunks = []
    for c in range(N_CHUNKS):
        lo = o[(2 * c) * t:(2 * c + 1) * t]
        hi = o[(2 * c + 1) * t:(2 * c + 2) * t]
        chunks.append(jnp.where(lane < HEAD_DIM, lo, hi))
    return chunks


def _gated_store(o_ref, ga_ref, sel, ya_chunks):
    for c, ya in enumerate(ya_chunks):
        o_ref[c, sel, :] = ya * ga_ref[c, sel, :]


def _scores(qp, k, scale=HEAD_DIM ** -0.5):
    return lax.dot_general(qp, k.astype(BF16), (((1,), (1,)), ((), ())), preferred_element_type=F32) * scale


def _prompt_attention_stages(q_ref, kp_ref, kc_ref, vp_ref, vc_ref, za_ref, sink_ref, o_ref, n_batch, blk):
    assert blk == WINDOW
    n = pl.program_id(0)
    qb = blk // 2
    keys = blk + qb
    cols = 2 * LANES
    j = lax.broadcasted_iota(jnp.int32, (keys, cols), 0)
    tq = lax.broadcasted_iota(jnp.int32, (keys, cols), 1) % qb
    masks = [(j >= jnp.maximum(tq, jnp.where(n == 0, blk - h * qb, 0))) & (j <= tq + WINDOW) for h in range(2)]
    low = lax.broadcasted_iota(jnp.int32, (qb, LANES), 1) < HEAD_DIM
    ones = jnp.ones((SUBLANES, keys), F32)
    units = [(h, pr) for h in range(2) for pr in range(N_CHUNKS // 2)]
    sinks = [jnp.concatenate([sink_ref[slot:slot + 1, :qb] for slot in range(4 * pr, 4 * pr + 4)], axis=1) * LOG2_E
             for pr in range(N_CHUNKS // 2)]

    def query_rows(b, h):
        return pl.ds(b + h * qb * n_batch, qb, stride=n_batch)

    def window(prev_ref, cur_ref, b, h):
        if h == 0:
            return jnp.concatenate([prev_ref[pl.ds(b, blk, stride=n_batch), :], cur_ref[query_rows(b, 0), :]], axis=0)
        return jnp.concatenate([prev_ref[query_rows(b, 1), :], cur_ref[pl.ds(b, blk, stride=n_batch), :]], axis=0)

    def score_stage(b):
        sts = []
        for h in range(2):
            k = window(kp_ref, kc_ref, b, h).astype(BF16)
            for pr in range(N_CHUNKS // 2):
                pieces = []
                for c in (2 * pr, 2 * pr + 1):
                    qc = q_ref[c, query_rows(b, h), :]
                    pieces += [jnp.where(low, qc, 0.0), jnp.where(low, 0.0, qc)]
                qp = jnp.concatenate(pieces, axis=0).astype(BF16)
                sts.append(_scores(k, qp, HEAD_DIM ** -0.5 * LOG2_E))
        return sts

    def value_stage(b, sts):
        vt_aug = [jnp.concatenate([window(vp_ref, vc_ref, b, h).T, ones], axis=0).astype(BF16) for h in range(2)]

        def finish(h, pr, ot, m):
            den = ot[KV_W:KV_W + 1, :] + jnp.exp2(sinks[pr] - m)
            o = (ot[:KV_W, :] * (1.0 / den)).T
            sel = query_rows(b, h)
            for i, c in enumerate((2 * pr, 2 * pr + 1)):
                lo = 2 * i * qb
                ya = jnp.where(low, o[lo:lo + qb], o[lo + qb:lo + 2 * qb])
                o_ref[c, sel, :] = ya * za_ref[c, sel, :]

        pending = None
        for (h, pr), st in zip(units, sts):
            st = jnp.where(masks[h], st, NEG_INF)
            m = jnp.maximum(jnp.max(st, axis=0, keepdims=True), sinks[pr])
            p = jnp.exp2(st - m).astype(BF16)
            ot = jnp.dot(vt_aug[h], p, preferred_element_type=F32)
            if pending is not None:
                finish(*pending)
            pending = (h, pr, ot, m)
        finish(*pending)

    return score_stage, value_stage


def _attn_sample_kernel(q_ref, k_ref, v_ref, za_ref, ck_ref, cv_ref, sink_ref, o_ref, nk_ref, nv_ref,
                        *, n_batch, t_new, group):
    g = pl.program_id(0)
    win = ck_ref.shape[1]
    rows = N_Q_HEADS * t_new
    tq = lax.broadcasted_iota(jnp.int32, (rows, 2 * win), 0) % t_new
    j = lax.broadcasted_iota(jnp.int32, (rows, 2 * win), 1)
    dist_c = tq + win - j
    mask = ((j < win) & (dist_c >= 0) & (dist_c <= WINDOW)) | ((j >= win) & (j - win <= tq) & (j - win < t_new))
    sink = sink_ref[...]
    pad = jnp.zeros((win - t_new, KV_W), F32)
    ones = jnp.ones((2 * win, LANES), BF16)
    batch = range(group)
    sels = [pl.ds(g * group + i, t_new, stride=n_batch) for i in batch]

    scores = []
    for i in batch:
        qp = _pad_queries([q_ref[c, sels[i], :] for c in range(N_CHUNKS)])
        keys = jnp.concatenate([ck_ref[i], k_ref[sels[i], :], pad], axis=0)
        scores.append(jnp.where(mask, _scores(qp, keys), NEG_INF))
    outs = []
    for i in batch:
        values = jnp.concatenate([cv_ref[i], v_ref[sels[i], :], pad], axis=0).astype(BF16)
        s = scores[i]
        m = jnp.maximum(jnp.max(s, axis=-1, keepdims=True), sink[:, :1])
        p = jnp.exp(s - m).astype(BF16)
        o = jnp.dot(p, jnp.concatenate([values, ones], axis=1), preferred_element_type=F32)
        outs.append(o[:, :KV_W] / (o[:, KV_W:] + jnp.exp(sink - m)))
    for i in batch:
        _gated_store(o_ref, za_ref, sels[i], _unpad_outputs(outs[i], t_new))
        nk_ref[i, 0:win - t_new, :] = ck_ref[i, t_new:win, :]
        nk_ref[i, win - t_new:win, :] = k_ref[sels[i], :]
        nv_ref[i, 0:win - t_new, :] = cv_ref[i, t_new:win, :]
        nv_ref[i, win - t_new:win, :] = v_ref[sels[i], :]


def attention_sample(q, k, v, za, window_k, window_v, sink_rows, l, n_batch, t_new, group):
    t_rows = k.shape[0]
    win = window_k.shape[2]
    window = pl.BlockSpec((None, group, win, KV_W), lambda i: (l, i, 0, 0))
    chunked = _full((N_CHUNKS, t_rows, LANES))
    return pl.pallas_call(
        functools.partial(_attn_sample_kernel, n_batch=n_batch, t_new=t_new, group=group),
        grid=(n_batch // group,),
        in_specs=[chunked, _full((t_rows, KV_W)), _full((t_rows, KV_W)), chunked,
                  window, window, _layer_spec(sink_rows, l)],
        out_specs=[chunked, window, window],
        out_shape=[jax.ShapeDtypeStruct((N_CHUNKS, t_rows, LANES), F32),
                   jax.ShapeDtypeStruct(window_k.shape, F32), jax.ShapeDtypeStruct(window_v.shape, F32)],
        input_output_aliases={4: 1, 5: 2},
        compiler_params=_params(("arbitrary",)),
        name="attention_sample",
    )(q, k, v, za, window_k, window_v, sink_rows)


def _ssm_stages(u_ref, zs_ref, h0re_ref, h0im_ref, are_ref, aim_ref, wb_ref, wc_ref, d_ref, wg_ref, bg_ref,
                o_ref, hre_ref, him_ref, x_scr, n_batch, sub_rows, slab):
    step = pl.program_id(0)

    @pl.when(step == 0)
    def _():
        hre_ref[...] = h0re_ref[...]
        him_ref[...] = h0im_ref[...]

    rows = u_ref.shape[0]
    half_in = SSM_W // 2
    n_sub = rows // sub_rows
    n_tiles = n_batch // SUBLANES
    t_steps = sub_rows // n_batch
    n_buf = x_scr.shape[0]
    items = [(sc, s) for sc in range(n_sub) for s in range(2)]

    def expand(i):
        sc, s = items[i]
        ub = u_ref[sc * sub_rows:(sc + 1) * sub_rows, s * half_in:(s + 1) * half_in].astype(BF16)
        x_scr[i % n_buf] = jnp.dot(ub, wb_ref[s], preferred_element_type=F32)

    def recur(i):
        sc, s = items[i]
        buf = i % n_buf
        for jb in range(HALF_STATE // slab):
            n_lo = s * HALF_STATE + jb * slab
            ar = jnp.broadcast_to(are_ref[:, n_lo:n_lo + slab], (SUBLANES, slab))
            ai = jnp.broadcast_to(aim_ref[:, n_lo:n_lo + slab], (SUBLANES, slab))
            re = slice(jb * slab, (jb + 1) * slab)
            im = slice(HALF_STATE + jb * slab, HALF_STATE + (jb + 1) * slab)
            for tile in range(n_tiles):
                r0 = tile * SUBLANES
                hr = hre_ref[r0:r0 + SUBLANES, n_lo:n_lo + slab]
                hi = him_ref[r0:r0 + SUBLANES, n_lo:n_lo + slab]
                for t in range(t_steps):
                    row = slice(t * n_batch + r0, t * n_batch + r0 + SUBLANES)
                    hr, hi = (ar * hr - ai * hi + x_scr[buf, row, re], ar * hi + ai * hr + x_scr[buf, row, im])
                    x_scr[buf, row, re] = hr
                    x_scr[buf, row, im] = hi
                hre_ref[r0:r0 + SUBLANES, n_lo:n_lo + slab] = hr
                him_ref[r0:r0 + SUBLANES, n_lo:n_lo + slab] = hi

    y_halves = {}

    def contract(i):
        sc, s = items[i]
        y_halves[s] = jnp.dot(x_scr[i % n_buf].astype(BF16), wc_ref[s], preferred_element_type=F32)
        if s == 1:
            r = slice(sc * sub_rows, (sc + 1) * sub_rows)
            y = jnp.concatenate([y_halves[0], y_halves[1]], axis=1) + d_ref[...] * u_ref[r, :]
            y = jax.nn.gelu(y)
            gate = jax.nn.sigmoid(jnp.dot(y.astype(BF16), wg_ref[...], preferred_element_type=F32) + bg_ref[...])
            o_ref[r, :] = (y * gate * zs_ref[r, :]).astype(o_ref.dtype)

    return len(items), expand, recur, contract


def _ssm_kernel(*refs, n_batch, sub_rows, slab):
    n_items, expand, recur, contract = _ssm_stages(*refs, n_batch, sub_rows, slab)
    expand(0)
    for i in range(n_items + 1):
        if i + 1 < n_items:
            expand(i + 1)
        if i < n_items:
            recur(i)
        if i >= 1:
            contract(i - 1)


N_ATTN_INPUTS = 7
N_SSM_INPUTS = 11


def _prompt_branches_kernel(*refs, n_batch, blk, sub_rows, slab):
    n_in = N_ATTN_INPUTS + N_SSM_INPUTS
    attn_refs = refs[:N_ATTN_INPUTS] + refs[n_in:n_in + 1]
    ssm_refs = refs[N_ATTN_INPUTS:n_in] + refs[n_in + 1:]
    score_stage, value_stage = _prompt_attention_stages(*attn_refs, n_batch, blk)
    n_items, expand, recur, contract = _ssm_stages(*ssm_refs, n_batch, sub_rows, slab)
    assert n_items == n_batch
    expand(0)
    scores = score_stage(0)
    for i in range(n_items + 1):
        if i + 1 < n_items:
            expand(i + 1)
            next_scores = score_stage(i + 1)
        if i < n_items:
            recur(i)
            value_stage(i, scores)
            scores = next_scores
        if i >= 1:
            contract(i - 1)


def _ssm_call_parts(u, h0_re, h0_im, abar_re, abar_im, w_b, w_c, d, w_glu, b_glu, l, l_state, n_batch, r):
    t_rows = u.shape[0]
    sub_rows = max(SSM_SUB_ROWS, n_batch)
    rows = pl.BlockSpec((r, SSM_W), lambda i: (i, 0))
    state = _full((n_batch, N_STATE))
    layer = lambda a: _layer_spec(a, l)
    in_specs = [rows, rows, _layer_spec(h0_re, l_state), _layer_spec(h0_im, l_state), layer(abar_re), layer(abar_im),
                layer(w_b), layer(w_c), layer(d), layer(w_glu), layer(b_glu)]
    out_shape = [jax.ShapeDtypeStruct((t_rows, SSM_W), BF16),
                 jax.ShapeDtypeStruct((n_batch, N_STATE), F32), jax.ShapeDtypeStruct((n_batch, N_STATE), F32)]
    scratch = [pltpu.VMEM((SSM_BUFFERS, sub_rows, 2 * HALF_STATE), F32)]
    return in_specs, [rows, state, state], out_shape, scratch, sub_rows


def ssm_branch(u, zs, h0_re, h0_im, abar_re, abar_im, w_b, w_c, d, w_glu, b_glu, l, l_state, n_batch, t_chunk):
    r = t_chunk * n_batch
    in_specs, out_specs, out_shape, scratch, sub_rows = _ssm_call_parts(
        u, h0_re, h0_im, abar_re, abar_im, w_b, w_c, d, w_glu, b_glu, l, l_state, n_batch, r)
    return pl.pallas_call(
        functools.partial(_ssm_kernel, n_batch=n_batch, sub_rows=sub_rows, slab=SSM_SLAB),
        grid=(u.shape[0] // r,),
        in_specs=in_specs,
        out_specs=out_specs,
        out_shape=out_shape,
        scratch_shapes=scratch,
        compiler_params=_params(("arbitrary",)),
        name="ssm_branch",
    )(u, zs, h0_re, h0_im, abar_re, abar_im, w_b, w_c, d, w_glu, b_glu)


def prompt_branches(q, k, v, za, sinks, u, zs, h0_re, h0_im, abar_re, abar_im, w_b, w_c, d, w_glu, b_glu,
                    l, l_state, n_batch, blk):
    t_rows = k.shape[0]
    r = blk * n_batch
    cur = lambda w: pl.BlockSpec((r, w), lambda i: (i, 0))
    prev = lambda w: pl.BlockSpec((r, w), lambda i: (jnp.maximum(i - 1, 0), 0))
    chunked = pl.BlockSpec((N_CHUNKS, r, LANES), lambda i: (0, i, 0))
    ssm_in, ssm_out, ssm_shape, scratch, sub_rows = _ssm_call_parts(
        u, h0_re, h0_im, abar_re, abar_im, w_b, w_c, d, w_glu, b_glu, l, l_state, n_batch, r)
    attn_in = [chunked, prev(KV_W), cur(KV_W), prev(KV_W), cur(KV_W), chunked, _layer_spec(sinks, l)]
    assert len(attn_in) == N_ATTN_INPUTS and len(ssm_in) == N_SSM_INPUTS
    return pl.pallas_call(
        functools.partial(_prompt_branches_kernel, n_batch=n_batch, blk=blk, sub_rows=sub_rows, slab=SSM_SLAB),
        grid=(t_rows // r,),
        in_specs=attn_in + ssm_in,
        out_specs=[chunked] + ssm_out,
        out_shape=[jax.ShapeDtypeStruct((N_CHUNKS, t_rows, LANES), F32)] + ssm_shape,
        scratch_shapes=scratch,
        compiler_params=_params(("arbitrary",)),
        name="prompt_branches",
    )(q, k, k, v, v, za, sinks, u, zs, h0_re, h0_im, abar_re, abar_im, w_b, w_c, d, w_glu, b_glu)


def _rope_tables(pos):
    half = ROT_DIM // 2
    inv = ROPE_THETA ** (-jnp.arange(half, dtype=F32) / half)
    ang = pos.astype(F32)[:, None] * inv[None, :]
    d = np.arange(LANES) % HEAD_DIM
    idx = d % half
    cos = jnp.where(d < ROT_DIM, jnp.cos(ang)[:, idx], 1.0)
    sin = jnp.sin(ang)[:, idx]
    sm = jnp.where(d < half, -sin, jnp.where(d < ROT_DIM, sin, 0.0))
    return cos, sm


def _keep_diagonal_blocks(w, row_block, col_block):
    rows, cols = w.shape[-2:]
    keep = (np.arange(rows)[:, None] // row_block) == (np.arange(cols)[None, :] // col_block)
    return jnp.where(keep, w, 0.0)


def _expand_weights(bb):
    depth = bb.shape[0]
    hg = N_SSM_GROUPS // 2
    halves = bb.reshape(depth, SSM_GROUP_CH, 2, HALF_STATE).transpose(0, 2, 1, 3)
    tiled = jnp.broadcast_to(halves[:, :, None], (depth, 2, hg, SSM_GROUP_CH, HALF_STATE))
    return _keep_diagonal_blocks(tiled.reshape(depth, 2, hg * SSM_GROUP_CH, HALF_STATE), SSM_GROUP_CH, SSM_STATE)


def _contract_weights(c):
    depth = c.shape[0]
    hg = N_SSM_GROUPS // 2
    rows = c.reshape(depth, 2, hg, SSM_GROUP_CH, SSM_STATE).transpose(0, 1, 2, 4, 3).reshape(
        depth, 2, HALF_STATE, SSM_GROUP_CH)
    return _keep_diagonal_blocks(jnp.tile(rows, (1, 1, 1, hg)), SSM_STATE, SSM_GROUP_CH)


def _prepare_weights(w_in, w_out, attn_sinks, bb_re, bb_im, ssm_c_re, ssm_c_im, t_new):
    depth = w_in.shape[0]
    assert HEAD_PERM == tuple(kv * Q_GROUP + g for g in range(Q_GROUP) for kv in range(N_KV_HEADS))
    w_in_p = w_in.astype(BF16)
    w_out_p = w_out.astype(BF16)
    sinks = jnp.swapaxes(attn_sinks.reshape(depth, N_KV_HEADS, Q_GROUP), 1, 2).reshape(depth, N_Q_HEADS)
    sink_lanes = jnp.broadcast_to(sinks[:, :, None], (depth, N_Q_HEADS, LANES))
    sink_rows = jnp.broadcast_to(jnp.repeat(sinks, t_new, axis=1)[:, :, None], (depth, N_Q_HEADS * t_new, LANES))
    w_b = jnp.concatenate([_expand_weights(bb_re), _expand_weights(bb_im)], axis=3).astype(BF16)
    w_c = jnp.concatenate([_contract_weights(ssm_c_re), -_contract_weights(ssm_c_im)], axis=2).astype(BF16)
    return w_in_p, w_out_p, sink_lanes, sink_rows, w_b, w_c


def kernel(x_prompt, x_sample, cache_k, cache_v, state_ssm_re, state_ssm_im, c_prompt, c_sample, norm_g, w_ada, b_ada, w_in, attn_sinks, ssm_a_re, ssm_a_im, ssm_log_dt, ssm_b_re, ssm_b_im, ssm_c_re, ssm_c_im, ssm_d, w_glu, b_glu, w_out, final_g):
    depth = w_in.shape[0]
    nb_p, seq, _ = x_prompt.shape
    nb_s, t_new, _ = x_sample.shape
    win = cache_k.shape[2]
    blk = WINDOW

    sample = Group(n_batch=nb_s, mod_block=0)
    prompt = Group(n_batch=nb_p, mod_block=nb_s // nb_p)
    mod = adaln_mod(jnp.concatenate([c_sample, c_prompt], axis=0), w_ada, b_ada)
    abar_re, abar_im, bb_re, bb_im = ssm_discretise(ssm_a_re, ssm_a_im, ssm_log_dt, ssm_b_re, ssm_b_im)
    w_in_p, w_out_p, sink_lanes, sink_rows, w_b, w_c = _prepare_weights(
        w_in, w_out, attn_sinks, bb_re, bb_im, ssm_c_re, ssm_c_im, t_new)
    norm_g3 = norm_g.reshape(depth, 1, D_MODEL)
    d3 = ssm_d.reshape(depth, 1, SSM_W)
    w_glu_b = w_glu.astype(BF16)
    b_glu3 = b_glu.reshape(depth, 1, SSM_W)
    final_g2 = final_g.reshape(1, D_MODEL)
    ssm_weights = (abar_re, abar_im, w_b, w_c, d3, w_glu_b, b_glu3)

    cos_p, sm_p = _rope_tables(jnp.arange(seq))
    cos_s, sm_s = _rope_tables(PAST_LEN + jnp.arange(t_new))

    xp = x_prompt
    xs = x_sample.transpose(1, 0, 2).reshape(t_new * nb_s, D_MODEL)
    zeros_state = jnp.zeros((1, nb_p, N_STATE), F32)
    new_k = cache_k.reshape(depth, nb_s, win, KV_W)
    new_v = cache_v.reshape(depth, nb_s, win, KV_W)
    state_re3 = state_ssm_re.reshape(depth, nb_s, N_STATE)
    state_im3 = state_ssm_im.reshape(depth, nb_s, N_STATE)

    outs = {k: [] for k in ("kp", "vp", "rp", "ip", "rs", "is")}
    rows_p, rows_s = BOUNDARY_ROWS, t_new * nb_s
    opening = lambda l, cos_t, sm_t: (norm_g3, w_in_p, cos_t, sm_t, l)
    xp, proj_p = layer_boundary(xp, mod, prompt, rows_p, open_=opening(0, cos_p, sm_p))
    xs, proj_s = layer_boundary(xs, mod, sample, rows_s, open_=opening(0, cos_s, sm_s))
    for l in range(depth):
        last = l == depth - 1

        q, k, v, za, u, zs = proj_p
        ya, ys, hre, him = prompt_branches(q, k, v, za, sink_lanes, u, zs, zeros_state, zeros_state, *ssm_weights,
                                           l, 0, nb_p, blk)
        xp, proj_p = layer_boundary(xp, mod, prompt, rows_p, close=(ya, ys, w_out_p, l),
                                    open_=None if last else opening(l + 1, cos_p, sm_p),
                                    final_g=final_g2, batch_major_out=last)
        outs["kp"].append(k[(seq - win) * nb_p:])
        outs["vp"].append(v[(seq - win) * nb_p:])
        outs["rp"].append(hre)
        outs["ip"].append(him)

        q, k, v, za, u, zs = proj_s
        ya, new_k, new_v = attention_sample(q, k, v, za, new_k, new_v, sink_rows, l, nb_s, t_new, SAMPLE_GROUP)
        ys, hre, him = ssm_branch(u, zs, state_re3, state_im3, *ssm_weights, l, l, nb_s, t_new)
        xs, proj_s = layer_boundary(xs, mod, sample, rows_s, close=(ya, ys, w_out_p, l),
                                    open_=None if last else opening(l + 1, cos_s, sm_s), final_g=final_g2)
        outs["rs"].append(hre)
        outs["is"].append(him)

    y_prompt = xp
    y_sample = xs.reshape(t_new, nb_s, D_MODEL).transpose(1, 0, 2)
    st = lambda key: jnp.stack(outs[key])
    window = lambda key: st(key).reshape(depth, win, nb_p, N_KV_HEADS, HEAD_DIM).transpose(0, 2, 1, 3, 4)
    state = lambda key, nb: st(key).reshape(depth, nb, N_SSM_GROUPS, SSM_STATE)
    return (y_prompt, y_sample, window("kp"), window("vp"), state("rp", nb_p), state("ip", nb_p),
            new_k.reshape(cache_k.shape), new_v.reshape(cache_v.shape), state("rs", nb_s), state("is", nb_s))
```

```python
import functools
from typing import NamedTuple

import jax
import jax.numpy as jnp
import numpy as np
from jax import lax
from jax.experimental import pallas as pl
from jax.experimental.pallas import tpu as pltpu

F32 = jnp.float32
BF16 = jnp.bfloat16

D_MODEL = 1024
HEAD_DIM = 64
ATTN_W = 512
N_Q_HEADS = 8
N_KV_HEADS = 2
Q_GROUP = N_Q_HEADS // N_KV_HEADS
KV_W = 128
WINDOW = 128
ROT_DIM = 16
ROPE_THETA = 500000.0
SSM_W = 512
SSM_GROUP_CH = 16
N_SSM_GROUPS = 32
SSM_STATE = 64
N_STATE = N_SSM_GROUPS * SSM_STATE
COL_Q, COL_KV, COL_ZA = 0, ATTN_W, ATTN_W + 2 * KV_W
COL_U, COL_ZS = COL_ZA + ATTN_W, COL_ZA + ATTN_W + SSM_W
HALF_STATE = N_STATE // 2
EPS = 1e-6
NEG_INF = -1e30
LOG2_E = 1.4426950408889634
PAST_LEN = 8192

LANES = 128
N_CHUNKS = ATTN_W // LANES
BOUNDARY_ROWS = 1024
BOUNDARY_PARTS = 2
SSM_SLAB = 4 * LANES
SAMPLE_GROUP = 16
SSM_SUB_ROWS = 256
PROMPT_SSM_SUB_ROWS = 256
SSM_BUFFERS = 3
SUBLANES = 8
VMEM_LIMIT = 56 * 1024 * 1024

HEAD_PERM = (0, 4, 1, 5, 2, 6, 3, 7)


class Group(NamedTuple):
    n_batch: int
    mod_block: int


def _params(sem):
    return pltpu.CompilerParams(dimension_semantics=sem, vmem_limit_bytes=VMEM_LIMIT)


def _full(shape):
    return pl.BlockSpec(shape, lambda *_: (0,) * len(shape))


def _mod_kernel(c_ref, w_ref, b_ref, o_ref):
    c = c_ref[...]
    a = (c * jax.nn.sigmoid(c)).astype(BF16)
    o_ref[...] = jnp.dot(a, w_ref[...].astype(BF16), preferred_element_type=F32) + b_ref[...]


def adaln_mod(c_all, w_ada, b_ada):
    depth = w_ada.shape[0]
    n = c_all.shape[0]
    nj = 3 * D_MODEL // D_MODEL
    return pl.pallas_call(
        _mod_kernel,
        grid=(depth, nj),
        in_specs=[
            pl.BlockSpec((n, D_MODEL), lambda l, j: (0, 0)),
            pl.BlockSpec((None, D_MODEL, D_MODEL), lambda l, j: (l, 0, j)),
            pl.BlockSpec((None, 1, D_MODEL), lambda l, j: (l, 0, j)),
        ],
        out_specs=pl.BlockSpec((None, n, D_MODEL), lambda l, j: (l, 0, j)),
        out_shape=jax.ShapeDtypeStruct((depth, n, 3 * D_MODEL), F32),
        compiler_params=_params(("arbitrary", "arbitrary")),
        name="adaln_mod",
    )(c_all, w_ada, b_ada.reshape(depth, 1, 3 * D_MODEL))


def _disc_kernel(are_ref, aim_ref, ldt_ref, bre_ref, bim_ref, abre_ref, abim_ref, bbre_ref, bbim_ref):
    a_re = are_ref[...]
    a_im = aim_ref[...]
    dt = jnp.exp(ldt_ref[...])
    mag = jnp.exp(a_re * dt)
    abar_re = mag * jnp.cos(a_im * dt)
    abar_im = mag * jnp.sin(a_im * dt)
    den = a_re * a_re + a_im * a_im
    nr = abar_re - 1.0
    coef_re = (nr * a_re + abar_im * a_im) / den
    coef_im = (abar_im * a_re - nr * a_im) / den
    br = bre_ref[...]
    bi = bim_ref[...]
    abre_ref[...] = abar_re
    abim_ref[...] = abar_im
    bbre_ref[...] = coef_re * br - coef_im * bi
    bbim_ref[...] = coef_re * bi + coef_im * br


def ssm_discretise(a_re, a_im, log_dt, b_re, b_im):
    depth = a_re.shape[0]
    a_re = a_re.reshape(depth, 1, N_STATE)
    a_im = a_im.reshape(depth, 1, N_STATE)
    ldt = jnp.repeat(log_dt, SSM_STATE, axis=1).reshape(depth, 1, N_STATE)
    bt_re = b_re.reshape(depth, N_STATE, SSM_GROUP_CH).transpose(0, 2, 1)
    bt_im = b_im.reshape(depth, N_STATE, SSM_GROUP_CH).transpose(0, 2, 1)
    row = pl.BlockSpec((None, 1, N_STATE), lambda l: (l, 0, 0))
    mat = pl.BlockSpec((None, SSM_GROUP_CH, N_STATE), lambda l: (l, 0, 0))
    return pl.pallas_call(
        _disc_kernel,
        grid=(depth,),
        in_specs=[row, row, row, mat, mat],
        out_specs=[row, row, mat, mat],
        out_shape=[jax.ShapeDtypeStruct((depth, 1, N_STATE), F32)] * 2
        + [jax.ShapeDtypeStruct((depth, SSM_GROUP_CH, N_STATE), F32)] * 2,
        compiler_params=_params(("arbitrary",)),
        name="ssm_discretise",
    )(a_re, a_im, ldt, bt_re, bt_im)


def _rope(x, cos, sm):
    lane = lax.broadcasted_iota(jnp.int32, x.shape, 1) % HEAD_DIM
    partner = jnp.where(lane < ROT_DIM // 2, pltpu.roll(x, LANES - ROT_DIM // 2, 1), pltpu.roll(x, ROT_DIM // 2, 1))
    return x * cos + partner * sm


def _to_time_major(x_ref, scr, n_batch):
    steps = x_ref.shape[1]
    for b in range(n_batch):
        for c in range(D_MODEL // LANES):
            scr[c, pl.ds(b, steps, stride=n_batch), :] = x_ref[b, :, c * LANES:(c + 1) * LANES]
    return jnp.concatenate([scr[c] for c in range(D_MODEL // LANES)], axis=1)


def _from_time_major(x, o_ref, scr, n_batch):
    steps = o_ref.shape[1]
    for c in range(D_MODEL // LANES):
        scr[c] = x[:, c * LANES:(c + 1) * LANES]
    for b in range(n_batch):
        for c in range(D_MODEL // LANES):
            o_ref[b, :, c * LANES:(c + 1) * LANES] = scr[c, pl.ds(b, steps, stride=n_batch), :]


def _rms(x, g_ref):
    ms = jnp.mean(x * x, axis=-1, keepdims=True)
    return x * lax.rsqrt(ms + EPS) * g_ref[...]


def _per_sequence(y, n_batch, fn):
    rows = y.shape[0]
    return fn(y.reshape(rows // n_batch, n_batch, D_MODEL)).reshape(rows, D_MODEL)


def _pair_heads(chunks):
    low = lax.broadcasted_iota(jnp.int32, chunks[0].shape, 1) < HEAD_DIM
    swap = lambda x: pltpu.roll(x, HEAD_DIM, 1)
    n0, n1, n2, n3 = chunks
    return [jnp.where(low, n0, swap(n2)), jnp.where(low, swap(n0), n2),
            jnp.where(low, n1, swap(n3)), jnp.where(low, swap(n1), n3)]


def _unpair_heads(chunks):
    low = lax.broadcasted_iota(jnp.int32, chunks[0].shape, 1) < HEAD_DIM
    swap = lambda x: pltpu.roll(x, HEAD_DIM, 1)
    p0, p1, p2, p3 = chunks
    return [jnp.where(low, p0, swap(p1)), jnp.where(low, p2, swap(p3)),
            jnp.where(low, swap(p0), p1), jnp.where(low, swap(p2), p3)]


def _residual_update(x, rows, ya_ref, ys_ref, gate_ref, w_ref, n_batch):
    ya = _unpair_heads([ya_ref[c, rows, :] for c in range(N_CHUNKS)])
    mixed = jnp.concatenate([y.astype(BF16) for y in ya] + [ys_ref[rows, :]], axis=1)
    mix = jnp.dot(mixed, w_ref[...], preferred_element_type=F32)
    return x + _per_sequence(mix, n_batch, lambda m: m * gate_ref[...][None])


def _modulated_norm(x, g_ref, scale_ref, shift_ref, n_batch):
    y = _per_sequence(_rms(x, g_ref), n_batch, lambda y3: y3 * (1.0 + scale_ref[...])[None] + shift_ref[...][None])
    return y.astype(BF16)


def _silu(z):
    return z * jax.nn.sigmoid(z)


def _project(h, rows, w_ref, rope_scr, q_ref, k_ref, v_ref, ga_ref, u_ref, gs_ref):
    cos = rope_scr[0, rows, :]
    sm = rope_scr[1, rows, :]

    def proj(lo, width):
        return jnp.dot(h, w_ref[:, lo:lo + width], preferred_element_type=F32)

    chunks_of = lambda a: [a[:, c * LANES:(c + 1) * LANES] for c in range(N_CHUNKS)]
    q = _pair_heads(chunks_of(proj(COL_Q, ATTN_W)))
    ga = _pair_heads(chunks_of(_silu(proj(COL_ZA, ATTN_W))))
    for c in range(N_CHUNKS):
        q_ref[c, rows, :] = _rope(q[c], cos, sm)
        ga_ref[c, rows, :] = ga[c]
    u_ref[rows, :] = proj(COL_U, SSM_W)
    gs_ref[rows, :] = _silu(proj(COL_ZS, SSM_W))
    kv = proj(COL_KV, 2 * KV_W)
    k_ref[rows, :] = _rope(kv[:, :KV_W], cos, sm)
    v_ref[rows, :] = kv[:, KV_W:]


def _boundary_kernel(*refs, n_batch, closes, opens, batch_major, block_rows, n_parts):
    refs = list(refs)
    x_ref = refs.pop(0)
    close_refs = [refs.pop(0) for _ in range(4)] if closes else None
    final_g_ref = refs.pop(0) if closes and not opens else None
    open_refs = [refs.pop(0) for _ in range(6)] if opens else None
    x_out_ref = refs.pop(0) if closes or batch_major else None
    proj_out_refs = [refs.pop(0) for _ in range(6)] if opens else None
    scr = refs.pop(0) if batch_major else None
    rope_scr = refs.pop(0) if opens else None

    if opens:
        g_ref, scale_ref, shift_ref, w_in_ref, cos_ref, sm_ref = open_refs
        steps = cos_ref.shape[0]
        for b in range(n_batch):
            rope_scr[0, pl.ds(b, steps, stride=n_batch), :] = cos_ref[...]
            rope_scr[1, pl.ds(b, steps, stride=n_batch), :] = sm_ref[...]

    part = block_rows // n_parts
    groups = [slice(i * part, (i + 1) * part) for i in range(n_parts)]
    x_full = _to_time_major(x_ref, scr, n_batch) if (batch_major and not closes) else None

    def updated(rows):
        x = x_full[rows, :] if x_full is not None else x_ref[rows, :]
        return _residual_update(x, rows, *close_refs, n_batch) if closes else x

    def normed(x, rows):
        if opens:
            if x_out_ref is not None:
                x_out_ref[rows, :] = x
            return _modulated_norm(x, g_ref, scale_ref, shift_ref, n_batch)
        return _rms(x, final_g_ref)

    xs = [updated(rows) for rows in groups]
    hs = [None] * n_parts
    hs[0] = normed(xs[0], groups[0])
    for i, rows in enumerate(groups):
        if opens:
            _project(hs[i], rows, w_in_ref, rope_scr, *proj_out_refs)
        if i + 1 < n_parts:
            hs[i + 1] = normed(xs[i + 1], groups[i + 1])
    if not opens:
        y = jnp.concatenate(hs, axis=0)
        if batch_major:
            _from_time_major(y, x_out_ref, scr, n_batch)
        else:
            x_out_ref[...] = y


def _layer_spec(arr, l):
    tail = arr.shape[1:]
    return pl.BlockSpec((None,) + tail, lambda *_: (l,) + (0,) * len(tail))


def _mod_spec(l, group, part):
    return pl.BlockSpec((None, group.n_batch, D_MODEL), lambda *_: (l, group.mod_block, part))


def layer_boundary(x, mod, group, block_rows, close=None, open_=None, final_g=None, batch_major_out=False):
    closes, opens = close is not None, open_ is not None
    batch_major_in = x.ndim == 3
    assert not (batch_major_in and closes) and not (batch_major_out and opens)
    batch_major = batch_major_in or batch_major_out
    t_rows = x.shape[0] * x.shape[1] if batch_major_in else x.shape[0]
    rows = lambda w: pl.BlockSpec((block_rows, w), lambda i: (i, 0))
    chunked = pl.BlockSpec((N_CHUNKS, block_rows, LANES), lambda i: (0, i, 0))
    blocked3 = pl.BlockSpec((group.n_batch, block_rows // group.n_batch, D_MODEL), lambda i: (0, i, 0))
    flat = lambda w: jax.ShapeDtypeStruct((t_rows, w), F32)
    chunked_shape = jax.ShapeDtypeStruct((N_CHUNKS, t_rows, LANES), F32)

    args, in_specs, out_specs, out_shape = [x], [blocked3 if batch_major_in else rows(D_MODEL)], [], []
    if closes:
        ya, ys, w_out, l = close
        args += [ya, ys, mod, w_out]
        in_specs += [chunked, rows(SSM_W), _mod_spec(l, group, 2), _layer_spec(w_out, l)]
        if not opens:
            args.append(final_g)
            in_specs.append(_full(final_g.shape))
    if opens:
        norm_g, w_in, cos_t, sm_t, l = open_
        args += [norm_g, mod, mod, w_in, cos_t, sm_t]
        per_step = pl.BlockSpec((block_rows // group.n_batch, LANES), lambda i: (i, 0))
        in_specs += [_layer_spec(norm_g, l), _mod_spec(l, group, 1), _mod_spec(l, group, 0), _layer_spec(w_in, l),
                     per_step, per_step]
    if closes or batch_major_in:
        out_specs.append(blocked3 if batch_major_out else rows(D_MODEL))
        out_shape.append(jax.ShapeDtypeStruct((group.n_batch, t_rows // group.n_batch, D_MODEL), F32)
                         if batch_major_out else flat(D_MODEL))
    if opens:
        out_specs += [chunked, rows(KV_W), rows(KV_W), chunked, rows(SSM_W), rows(SSM_W)]
        out_shape += [chunked_shape, flat(KV_W), flat(KV_W), chunked_shape, flat(SSM_W), flat(SSM_W)]
    scratch = [pltpu.VMEM((D_MODEL // LANES, block_rows, LANES), F32)] if batch_major else []
    if opens:
        scratch.append(pltpu.VMEM((2, block_rows, LANES), F32))
    outs = pl.pallas_call(
        functools.partial(_boundary_kernel, n_batch=group.n_batch, closes=closes, opens=opens,
                          batch_major=batch_major, block_rows=block_rows, n_parts=BOUNDARY_PARTS),
        grid=(t_rows // block_rows,),
        in_specs=in_specs,
        out_specs=out_specs,
        out_shape=out_shape,
        scratch_shapes=scratch,
        compiler_params=_params(("parallel",)),
        name="layer_boundary",
    )(*args)
    x_rows = outs[0] if (closes or batch_major_in) else x
    return x_rows, (tuple(outs[-6:]) if opens else None)


def _pad_queries(chunks):
    lane = lax.broadcasted_iota(jnp.int32, chunks[0].shape, 1)
    pieces = []
    for chunk in chunks:
        pieces.append(jnp.where(lane < HEAD_DIM, chunk, 0.0))
        pieces.append(jnp.where(lane >= HEAD_DIM, chunk, 0.0))
    return jnp.concatenate(pieces, axis=0).astype(BF16)


def _unpad_outputs(o, t):
    lane = lax.broadcasted_iota(jnp.int32, (t, LANES), 1)
    chunks = []
    for c in range(N_CHUNKS):
        lo = o[(2 * c) * t:(2 * c + 1) * t]
        hi = o[(2 * c + 1) * t:(2 * c + 2) * t]
        chunks.append(jnp.where(lane < HEAD_DIM, lo, hi))
    return chunks


def _gated_store(o_ref, ga_ref, sel, ya_chunks):
    for c, ya in enumerate(ya_chunks):
        o_ref[c, sel, :] = ya * ga_ref[c, sel, :]


def _scores(qp, k, scale=HEAD_DIM ** -0.5):
    return lax.dot_general(qp, k.astype(BF16), (((1,), (1,)), ((), ())), preferred_element_type=F32) * scale


def _prompt_attention_stages(q_ref, kp_ref, kc_ref, vp_ref, vc_ref, za_ref, sink_ref, o_ref, n_batch, blk):
    assert blk == WINDOW
    n = pl.program_id(0)
    qb = blk // 2
    keys = blk + qb
    cols = 2 * LANES
    j = lax.broadcasted_iota(jnp.int32, (keys, cols), 0)
    tq = lax.broadcasted_iota(jnp.int32, (keys, cols), 1) % qb
    masks = [(j >= jnp.maximum(tq, jnp.where(n == 0, blk - h * qb, 0))) & (j <= tq + WINDOW) for h in range(2)]
    low = lax.broadcasted_iota(jnp.int32, (qb, LANES), 1) < HEAD_DIM
    ones = jnp.ones((SUBLANES, keys), F32)
    units = [(h, pr) for h in range(2) for pr in range(N_CHUNKS // 2)]
    sinks = [jnp.concatenate([sink_ref[slot:slot + 1, :qb] for slot in range(4 * pr, 4 * pr + 4)], axis=1) * LOG2_E
             for pr in range(N_CHUNKS // 2)]

    def query_rows(b, h):
        return pl.ds(b + h * qb * n_batch, qb, stride=n_batch)

    def window(prev_ref, cur_ref, b, h):
        if h == 0:
            return jnp.concatenate([prev_ref[pl.ds(b, blk, stride=n_batch), :], cur_ref[query_rows(b, 0), :]], axis=0)
        return jnp.concatenate([prev_ref[query_rows(b, 1), :], cur_ref[pl.ds(b, blk, stride=n_batch), :]], axis=0)

    def score_stage(b):
        sts = []
        for h in range(2):
            k = window(kp_ref, kc_ref, b, h).astype(BF16)
            for pr in range(N_CHUNKS // 2):
                pieces = []
                for c in (2 * pr, 2 * pr + 1):
                    qc = q_ref[c, query_rows(b, h), :]
                    pieces += [jnp.where(low, qc, 0.0), jnp.where(low, 0.0, qc)]
                qp = jnp.concatenate(pieces, axis=0).astype(BF16)
                sts.append(_scores(k, qp, HEAD_DIM ** -0.5 * LOG2_E))
        return sts

    def value_stage(b, sts):
        vt_aug = [jnp.concatenate([window(vp_ref, vc_ref, b, h).T, ones], axis=0).astype(BF16) for h in range(2)]

        def finish(h, pr, ot, m):
            den = ot[KV_W:KV_W + 1, :] + jnp.exp2(sinks[pr] - m)
            o = (ot[:KV_W, :] * (1.0 / den)).T
            sel = query_rows(b, h)
            for i, c in enumerate((2 * pr, 2 * pr + 1)):
                lo = 2 * i * qb
                ya = jnp.where(low, o[lo:lo + qb], o[lo + qb:lo + 2 * qb])
                o_ref[c, sel, :] = ya * za_ref[c, sel, :]

        pending = None
        for (h, pr), st in zip(units, sts):
            st = jnp.where(masks[h], st, NEG_INF)
            m = jnp.maximum(jnp.max(st, axis=0, keepdims=True), sinks[pr])
            p = jnp.exp2(st - m).astype(BF16)
            ot = jnp.dot(vt_aug[h], p, preferred_element_type=F32)
            if pending is not None:
                finish(*pending)
            pending = (h, pr, ot, m)
        finish(*pending)

    return score_stage, value_stage


def _attn_sample_kernel(q_ref, k_ref, v_ref, za_ref, ck_ref, cv_ref, sink_ref, o_ref, nk_ref, nv_ref,
                        *, n_batch, t_new, group):
    g = pl.program_id(0)
    win = ck_ref.shape[1]
    rows = N_Q_HEADS * t_new
    tq = lax.broadcasted_iota(jnp.int32, (rows, 2 * win), 0) % t_new
    j = lax.broadcasted_iota(jnp.int32, (rows, 2 * win), 1)
    dist_c = tq + win - j
    mask = ((j < win) & (dist_c >= 0) & (dist_c <= WINDOW)) | ((j >= win) & (j - win <= tq) & (j - win < t_new))
    sink = sink_ref[...]
    pad = jnp.zeros((win - t_new, KV_W), F32)
    ones = jnp.ones((2 * win, LANES), BF16)
    batch = range(group)
    sels = [pl.ds(g * group + i, t_new, stride=n_batch) for i in batch]

    scores = []
    for i in batch:
        qp = _pad_queries([q_ref[c, sels[i], :] for c in range(N_CHUNKS)])
        keys = jnp.concatenate([ck_ref[i], k_ref[sels[i], :], pad], axis=0)
        scores.append(jnp.where(mask, _scores(qp, keys), NEG_INF))
    outs = []
    for i in batch:
        values = jnp.concatenate([cv_ref[i], v_ref[sels[i], :], pad], axis=0).astype(BF16)
        s = scores[i]
        m = jnp.maximum(jnp.max(s, axis=-1, keepdims=True), sink[:, :1])
        p = jnp.exp(s - m).astype(BF16)
        o = jnp.dot(p, jnp.concatenate([values, ones], axis=1), preferred_element_type=F32)
        outs.append(o[:, :KV_W] / (o[:, KV_W:] + jnp.exp(sink - m)))
    for i in batch:
        _gated_store(o_ref, za_ref, sels[i], _unpad_outputs(outs[i], t_new))
        nk_ref[i, 0:win - t_new, :] = ck_ref[i, t_new:win, :]
        nk_ref[i, win - t_new:win, :] = k_ref[sels[i], :]
        nv_ref[i, 0:win - t_new, :] = cv_ref[i, t_new:win, :]
        nv_ref[i, win - t_new:win, :] = v_ref[sels[i], :]


def attention_sample(q, k, v, za, window_k, window_v, sink_rows, l, n_batch, t_new, group):
    t_rows = k.shape[0]
    win = window_k.shape[2]
    window = pl.BlockSpec((None, group, win, KV_W), lambda i: (l, i, 0, 0))
    chunked = _full((N_CHUNKS, t_rows, LANES))
    return pl.pallas_call(
        functools.partial(_attn_sample_kernel, n_batch=n_batch, t_new=t_new, group=group),
        grid=(n_batch // group,),
        in_specs=[chunked, _full((t_rows, KV_W)), _full((t_rows, KV_W)), chunked,
                  window, window, _layer_spec(sink_rows, l)],
        out_specs=[chunked, window, window],
        out_shape=[jax.ShapeDtypeStruct((N_CHUNKS, t_rows, LANES), F32),
                   jax.ShapeDtypeStruct(window_k.shape, F32), jax.ShapeDtypeStruct(window_v.shape, F32)],
        input_output_aliases={4: 1, 5: 2},
        compiler_params=_params(("arbitrary",)),
        name="attention_sample",
    )(q, k, v, za, window_k, window_v, sink_rows)


def _ssm_stages(u_ref, zs_ref, h0re_ref, h0im_ref, are_ref, aim_ref, wb_ref, wc_ref, d_ref, wg_ref, bg_ref,
                o_ref, hre_ref, him_ref, x_scr, n_batch, sub_rows, slab):
    step = pl.program_id(0)

    @pl.when(step == 0)
    def _():
        hre_ref[...] = h0re_ref[...]
        him_ref[...] = h0im_ref[...]

    rows = u_ref.shape[0]
    half_in = SSM_W // 2
    n_sub = rows // sub_rows
    n_tiles = n_batch // SUBLANES
    t_steps = sub_rows // n_batch
    n_buf = x_scr.shape[0]
    items = [(sc, s) for sc in range(n_sub) for s in range(2)]

    def expand(i):
        sc, s = items[i]
        ub = u_ref[sc * sub_rows:(sc + 1) * sub_rows, s * half_in:(s + 1) * half_in].astype(BF16)
        x_scr[i % n_buf] = jnp.dot(ub, wb_ref[s], preferred_element_type=F32)

    def recur(i):
        sc, s = items[i]
        buf = i % n_buf
        for jb in range(HALF_STATE // slab):
            n_lo = s * HALF_STATE + jb * slab
            ar = jnp.broadcast_to(are_ref[:, n_lo:n_lo + slab], (SUBLANES, slab))
            ai = jnp.broadcast_to(aim_ref[:, n_lo:n_lo + slab], (SUBLANES, slab))
            re = slice(jb * slab, (jb + 1) * slab)
            im = slice(HALF_STATE + jb * slab, HALF_STATE + (jb + 1) * slab)
            for tile in range(n_tiles):
                r0 = tile * SUBLANES
                hr = hre_ref[r0:r0 + SUBLANES, n_lo:n_lo + slab]
                hi = him_ref[r0:r0 + SUBLANES, n_lo:n_lo + slab]
                for t in range(t_steps):
                    row = slice(t * n_batch + r0, t * n_batch + r0 + SUBLANES)
                    hr, hi = (ar * hr - ai * hi + x_scr[buf, row, re], ar * hi + ai * hr + x_scr[buf, row, im])
                    x_scr[buf, row, re] = hr
                    x_scr[buf, row, im] = hi
                hre_ref[r0:r0 + SUBLANES, n_lo:n_lo + slab] = hr
                him_ref[r0:r0 + SUBLANES, n_lo:n_lo + slab] = hi

    y_halves = {}

    def contract(i):
        sc, s = items[i]
        y_halves[s] = jnp.dot(x_scr[i % n_buf].astype(BF16), wc_ref[s], preferred_element_type=F32)
        if s == 1:
            r = slice(sc * sub_rows, (sc + 1) * sub_rows)
            y = jnp.concatenate([y_halves[0], y_halves[1]], axis=1) + d_ref[...] * u_ref[r, :]
            y = jax.nn.gelu(y)
            gate = jax.nn.sigmoid(jnp.dot(y.astype(BF16), wg_ref[...], preferred_element_type=F32) + bg_ref[...])
            o_ref[r, :] = (y * gate * zs_ref[r, :]).astype(o_ref.dtype)

    return len(items), expand, recur, contract


def _ssm_kernel(*refs, n_batch, sub_rows, slab):
    n_items, expand, recur, contract = _ssm_stages(*refs, n_batch, sub_rows, slab)
    expand(0)
    for i in range(n_items + 1):
        if i + 1 < n_items:
            expand(i + 1)
        if i < n_items:
            recur(i)
        if i >= 1:
            contract(i - 1)


N_ATTN_INPUTS = 7
N_SSM_INPUTS = 11


def _prompt_branches_kernel(*refs, n_batch, blk, sub_rows, slab):
    n_in = N_ATTN_INPUTS + N_SSM_INPUTS
    attn_refs = refs[:N_ATTN_INPUTS] + refs[n_in:n_in + 1]
    ssm_refs = refs[N_ATTN_INPUTS:n_in] + refs[n_in + 1:]
    score_stage, value_stage = _prompt_attention_stages(*attn_refs, n_batch, blk)
    n_items, expand, recur, contract = _ssm_stages(*ssm_refs, n_batch, sub_rows, slab)
    per_item = n_batch // n_items
    assert per_item * n_items == n_batch
    seqs_of = lambda i: range(i * per_item, (i + 1) * per_item)
    expand(0)
    scores = [score_stage(b) for b in seqs_of(0)]
    for i in range(n_items + 1):
        if i < n_items:
            recur(i)
        if i + 1 < n_items:
            expand(i + 1)
        if i < n_items:
            for b, sts in zip(seqs_of(i), scores):
                value_stage(b, sts)
        if i + 1 < n_items:
            scores = [score_stage(b) for b in seqs_of(i + 1)]
        if i >= 1:
            contract(i - 1)


def _ssm_call_parts(u, h0_re, h0_im, abar_re, abar_im, w_b, w_c, d, w_glu, b_glu, l, l_state, n_batch, r,
                    sub_rows=SSM_SUB_ROWS):
    t_rows = u.shape[0]
    sub_rows = max(sub_rows, n_batch)
    rows = pl.BlockSpec((r, SSM_W), lambda i: (i, 0))
    state = _full((n_batch, N_STATE))
    layer = lambda a: _layer_spec(a, l)
    in_specs = [rows, rows, _layer_spec(h0_re, l_state), _layer_spec(h0_im, l_state), layer(abar_re), layer(abar_im),
                layer(w_b), layer(w_c), layer(d), layer(w_glu), layer(b_glu)]
    out_shape = [jax.ShapeDtypeStruct((t_rows, SSM_W), BF16),
                 jax.ShapeDtypeStruct((n_batch, N_STATE), F32), jax.ShapeDtypeStruct((n_batch, N_STATE), F32)]
    scratch = [pltpu.VMEM((SSM_BUFFERS, sub_rows, 2 * HALF_STATE), F32)]
    return in_specs, [rows, state, state], out_shape, scratch, sub_rows


def ssm_branch(u, zs, h0_re, h0_im, abar_re, abar_im, w_b, w_c, d, w_glu, b_glu, l, l_state, n_batch, t_chunk):
    r = t_chunk * n_batch
    in_specs, out_specs, out_shape, scratch, sub_rows = _ssm_call_parts(
        u, h0_re, h0_im, abar_re, abar_im, w_b, w_c, d, w_glu, b_glu, l, l_state, n_batch, r)
    return pl.pallas_call(
        functools.partial(_ssm_kernel, n_batch=n_batch, sub_rows=sub_rows, slab=SSM_SLAB),
        grid=(u.shape[0] // r,),
        in_specs=in_specs,
        out_specs=out_specs,
        out_shape=out_shape,
        scratch_shapes=scratch,
        compiler_params=_params(("arbitrary",)),
        name="ssm_branch",
    )(u, zs, h0_re, h0_im, abar_re, abar_im, w_b, w_c, d, w_glu, b_glu)


def prompt_branches(q, k, v, za, sinks, u, zs, h0_re, h0_im, abar_re, abar_im, w_b, w_c, d, w_glu, b_glu,
                    l, l_state, n_batch, blk):
    t_rows = k.shape[0]
    r = blk * n_batch
    cur = lambda w: pl.BlockSpec((r, w), lambda i: (i, 0))
    prev = lambda w: pl.BlockSpec((r, w), lambda i: (jnp.maximum(i - 1, 0), 0))
    chunked = pl.BlockSpec((N_CHUNKS, r, LANES), lambda i: (0, i, 0))
    ssm_in, ssm_out, ssm_shape, scratch, sub_rows = _ssm_call_parts(
        u, h0_re, h0_im, abar_re, abar_im, w_b, w_c, d, w_glu, b_glu, l, l_state, n_batch, r, PROMPT_SSM_SUB_ROWS)
    attn_in = [chunked, prev(KV_W), cur(KV_W), prev(KV_W), cur(KV_W), chunked, _layer_spec(sinks, l)]
    assert len(attn_in) == N_ATTN_INPUTS and len(ssm_in) == N_SSM_INPUTS
    return pl.pallas_call(
        functools.partial(_prompt_branches_kernel, n_batch=n_batch, blk=blk, sub_rows=sub_rows, slab=SSM_SLAB),
        grid=(t_rows // r,),
        in_specs=attn_in + ssm_in,
        out_specs=[chunked] + ssm_out,
        out_shape=[jax.ShapeDtypeStruct((N_CHUNKS, t_rows, LANES), F32)] + ssm_shape,
        scratch_shapes=scratch,
        compiler_params=_params(("arbitrary",)),
        name="prompt_branches",
    )(q, k, k, v, v, za, sinks, u, zs, h0_re, h0_im, abar_re, abar_im, w_b, w_c, d, w_glu, b_glu)


def _rope_tables(pos):
    half = ROT_DIM // 2
    inv = ROPE_THETA ** (-jnp.arange(half, dtype=F32) / half)
    ang = pos.astype(F32)[:, None] * inv[None, :]
    d = np.arange(LANES) % HEAD_DIM
    idx = d % half
    cos = jnp.where(d < ROT_DIM, jnp.cos(ang)[:, idx], 1.0)
    sin = jnp.sin(ang)[:, idx]
    sm = jnp.where(d < half, -sin, jnp.where(d < ROT_DIM, sin, 0.0))
    return cos, sm


def _keep_diagonal_blocks(w, row_block, col_block):
    rows, cols = w.shape[-2:]
    keep = (np.arange(rows)[:, None] // row_block) == (np.arange(cols)[None, :] // col_block)
    return jnp.where(keep, w, 0.0)


def _expand_weights(bb):
    depth = bb.shape[0]
    hg = N_SSM_GROUPS // 2
    halves = bb.reshape(depth, SSM_GROUP_CH, 2, HALF_STATE).transpose(0, 2, 1, 3)
    tiled = jnp.broadcast_to(halves[:, :, None], (depth, 2, hg, SSM_GROUP_CH, HALF_STATE))
    return _keep_diagonal_blocks(tiled.reshape(depth, 2, hg * SSM_GROUP_CH, HALF_STATE), SSM_GROUP_CH, SSM_STATE)


def _contract_weights(c):
    depth = c.shape[0]
    hg = N_SSM_GROUPS // 2
    rows = c.reshape(depth, 2, hg, SSM_GROUP_CH, SSM_STATE).transpose(0, 1, 2, 4, 3).reshape(
        depth, 2, HALF_STATE, SSM_GROUP_CH)
    return _keep_diagonal_blocks(jnp.tile(rows, (1, 1, 1, hg)), SSM_STATE, SSM_GROUP_CH)


def _prepare_weights(w_in, w_out, attn_sinks, bb_re, bb_im, ssm_c_re, ssm_c_im, t_new):
    depth = w_in.shape[0]
    assert HEAD_PERM == tuple(kv * Q_GROUP + g for g in range(Q_GROUP) for kv in range(N_KV_HEADS))
    w_in_p = w_in.astype(BF16)
    w_out_p = w_out.astype(BF16)
    sinks = jnp.swapaxes(attn_sinks.reshape(depth, N_KV_HEADS, Q_GROUP), 1, 2).reshape(depth, N_Q_HEADS)
    sink_lanes = jnp.broadcast_to(sinks[:, :, None], (depth, N_Q_HEADS, LANES))
    sink_rows = jnp.broadcast_to(jnp.repeat(sinks, t_new, axis=1)[:, :, None], (depth, N_Q_HEADS * t_new, LANES))
    w_b = jnp.concatenate([_expand_weights(bb_re), _expand_weights(bb_im)], axis=3).astype(BF16)
    w_c = jnp.concatenate([_contract_weights(ssm_c_re), -_contract_weights(ssm_c_im)], axis=2).astype(BF16)
    return w_in_p, w_out_p, sink_lanes, sink_rows, w_b, w_c


def kernel(x_prompt, x_sample, cache_k, cache_v, state_ssm_re, state_ssm_im, c_prompt, c_sample, norm_g, w_ada, b_ada, w_in, attn_sinks, ssm_a_re, ssm_a_im, ssm_log_dt, ssm_b_re, ssm_b_im, ssm_c_re, ssm_c_im, ssm_d, w_glu, b_glu, w_out, final_g):
    depth = w_in.shape[0]
    nb_p, seq, _ = x_prompt.shape
    nb_s, t_new, _ = x_sample.shape
    win = cache_k.shape[2]
    blk = WINDOW

    sample = Group(n_batch=nb_s, mod_block=0)
    prompt = Group(n_batch=nb_p, mod_block=nb_s // nb_p)
    mod = adaln_mod(jnp.concatenate([c_sample, c_prompt], axis=0), w_ada, b_ada)
    abar_re, abar_im, bb_re, bb_im = ssm_discretise(ssm_a_re, ssm_a_im, ssm_log_dt, ssm_b_re, ssm_b_im)
    w_in_p, w_out_p, sink_lanes, sink_rows, w_b, w_c = _prepare_weights(
        w_in, w_out, attn_sinks, bb_re, bb_im, ssm_c_re, ssm_c_im, t_new)
    norm_g3 = norm_g.reshape(depth, 1, D_MODEL)
    d3 = ssm_d.reshape(depth, 1, SSM_W)
    w_glu_b = w_glu.astype(BF16)
    b_glu3 = b_glu.reshape(depth, 1, SSM_W)
    final_g2 = final_g.reshape(1, D_MODEL)
    ssm_weights = (abar_re, abar_im, w_b, w_c, d3, w_glu_b, b_glu3)

    cos_p, sm_p = _rope_tables(jnp.arange(seq))
    cos_s, sm_s = _rope_tables(PAST_LEN + jnp.arange(t_new))

    xp = x_prompt
    xs = x_sample.transpose(1, 0, 2).reshape(t_new * nb_s, D_MODEL)
    zeros_state = jnp.zeros((1, nb_p, N_STATE), F32)
    new_k = cache_k.reshape(depth, nb_s, win, KV_W)
    new_v = cache_v.reshape(depth, nb_s, win, KV_W)
    state_re3 = state_ssm_re.reshape(depth, nb_s, N_STATE)
    state_im3 = state_ssm_im.reshape(depth, nb_s, N_STATE)

    outs = {k: [] for k in ("kp", "vp", "rp", "ip", "rs", "is")}
    rows_p, rows_s = BOUNDARY_ROWS, t_new * nb_s
    opening = lambda l, cos_t, sm_t: (norm_g3, w_in_p, cos_t, sm_t, l)
    xp, proj_p = layer_boundary(xp, mod, prompt, rows_p, open_=opening(0, cos_p, sm_p))
    xs, proj_s = layer_boundary(xs, mod, sample, rows_s, open_=opening(0, cos_s, sm_s))
    for l in range(depth):
        last = l == depth - 1

        q, k, v, za, u, zs = proj_p
        ya, ys, hre, him = prompt_branches(q, k, v, za, sink_lanes, u, zs, zeros_state, zeros_state, *ssm_weights,
                                           l, 0, nb_p, blk)
        xp, proj_p = layer_boundary(xp, mod, prompt, rows_p, close=(ya, ys, w_out_p, l),
                                    open_=None if last else opening(l + 1, cos_p, sm_p),
                                    final_g=final_g2, batch_major_out=last)
        outs["kp"].append(k[(seq - win) * nb_p:])
        outs["vp"].append(v[(seq - win) * nb_p:])
        outs["rp"].append(hre)
        outs["ip"].append(him)

        q, k, v, za, u, zs = proj_s
        ya, new_k, new_v = attention_sample(q, k, v, za, new_k, new_v, sink_rows, l, nb_s, t_new, SAMPLE_GROUP)
        ys, hre, him = ssm_branch(u, zs, state_re3, state_im3, *ssm_weights, l, l, nb_s, t_new)
        xs, proj_s = layer_boundary(xs, mod, sample, rows_s, close=(ya, ys, w_out_p, l),
                                    open_=None if last else opening(l + 1, cos_s, sm_s), final_g=final_g2)
        outs["rs"].append(hre)
        outs["is"].append(him)

    y_prompt = xp
    y_sample = xs.reshape(t_new, nb_s, D_MODEL).transpose(1, 0, 2)
    st = lambda key: jnp.stack(outs[key])
    window = lambda key: st(key).reshape(depth, win, nb_p, N_KV_HEADS, HEAD_DIM).transpose(0, 2, 1, 3, 4)
    state = lambda key, nb: st(key).reshape(depth, nb, N_SSM_GROUPS, SSM_STATE)
    return (y_prompt, y_sample, window("kp"), window("vp"), state("rp", nb_p), state("ip", nb_p),
            new_k.reshape(cache_k.shape), new_v.reshape(cache_v.shape), state("rs", nb_s), state("is", nb_s))
```

```python
import functools
from typing import NamedTuple

import jax
import jax.numpy as jnp
import numpy as np
from jax import lax
from jax.experimental import pallas as pl
from jax.experimental.pallas import tpu as pltpu

F32 = jnp.float32
BF16 = jnp.bfloat16

D_MODEL = 1024
HEAD_DIM = 64
ATTN_W = 512
N_Q_HEADS = 8
N_KV_HEADS = 2
Q_GROUP = N_Q_HEADS // N_KV_HEADS
KV_W = 128
WINDOW = 128
ROT_DIM = 16
ROPE_THETA = 500000.0
SSM_W = 512
SSM_GROUP_CH = 16
N_SSM_GROUPS = 32
SSM_STATE = 64
N_STATE = N_SSM_GROUPS * SSM_STATE
COL_Q, COL_KV, COL_ZA = 0, ATTN_W, ATTN_W + 2 * KV_W
COL_U, COL_ZS = COL_ZA + ATTN_W, COL_ZA + ATTN_W + SSM_W
HALF_STATE = N_STATE // 2
EPS = 1e-6
NEG_INF = -1e30
LOG2_E = 1.4426950408889634
PAST_LEN = 8192

MOD_PARTS = 3
LANES = 128
N_CHUNKS = ATTN_W // LANES
BOUNDARY_ROWS = 1024
BOUNDARY_PARTS = 2
SSM_SLAB = 4 * LANES
SAMPLE_GROUP = 16
SSM_SUB_ROWS = 256
SSM_BUFFERS = 3
SUBLANES = 8
VMEM_LIMIT = 56 * 1024 * 1024

HEAD_PERM = (0, 4, 1, 5, 2, 6, 3, 7)


class Group(NamedTuple):
    n_batch: int
    mod_block: int


def _params(sem):
    return pltpu.CompilerParams(dimension_semantics=sem, vmem_limit_bytes=VMEM_LIMIT)


def _full(shape):
    return pl.BlockSpec(shape, lambda *_: (0,) * len(shape))


def _mod_kernel(c_ref, w_ref, b_ref, o_ref):
    c = c_ref[...]
    a = (c * jax.nn.sigmoid(c)).astype(BF16)
    o_ref[...] = jnp.dot(a, w_ref[...].astype(BF16), preferred_element_type=F32) + b_ref[...]


def adaln_mod(c_all, w_ada, b_ada):
    depth = w_ada.shape[0]
    n = c_all.shape[0]
    return pl.pallas_call(
        _mod_kernel,
        grid=(depth, MOD_PARTS),
        in_specs=[
            pl.BlockSpec((n, D_MODEL), lambda l, j: (0, 0)),
            pl.BlockSpec((None, D_MODEL, D_MODEL), lambda l, j: (l, 0, j)),
            pl.BlockSpec((None, 1, D_MODEL), lambda l, j: (l, 0, j)),
        ],
        out_specs=pl.BlockSpec((None, n, D_MODEL), lambda l, j: (l, 0, j)),
        out_shape=jax.ShapeDtypeStruct((depth, n, 3 * D_MODEL), F32),
        compiler_params=_params(("arbitrary", "arbitrary")),
        name="adaln_mod",
    )(c_all, w_ada, b_ada.reshape(depth, 1, 3 * D_MODEL))


def _disc_kernel(are_ref, aim_ref, ldt_ref, bre_ref, bim_ref, abre_ref, abim_ref, bbre_ref, bbim_ref):
    a_re = are_ref[...]
    a_im = aim_ref[...]
    dt = jnp.exp(ldt_ref[...])
    mag = jnp.exp(a_re * dt)
    abar_re = mag * jnp.cos(a_im * dt)
    abar_im = mag * jnp.sin(a_im * dt)
    den = a_re * a_re + a_im * a_im
    nr = abar_re - 1.0
    coef_re = (nr * a_re + abar_im * a_im) / den
    coef_im = (abar_im * a_re - nr * a_im) / den
    br = bre_ref[...]
    bi = bim_ref[...]
    abre_ref[...] = abar_re
    abim_ref[...] = abar_im
    bbre_ref[...] = coef_re * br - coef_im * bi
    bbim_ref[...] = coef_re * bi + coef_im * br


def ssm_discretise(a_re, a_im, log_dt, b_re, b_im):
    depth = a_re.shape[0]
    a_re = a_re.reshape(depth, 1, N_STATE)
    a_im = a_im.reshape(depth, 1, N_STATE)
    ldt = jnp.repeat(log_dt, SSM_STATE, axis=1).reshape(depth, 1, N_STATE)
    bt_re = b_re.reshape(depth, N_STATE, SSM_GROUP_CH).transpose(0, 2, 1)
    bt_im = b_im.reshape(depth, N_STATE, SSM_GROUP_CH).transpose(0, 2, 1)
    row = pl.BlockSpec((None, 1, N_STATE), lambda l: (l, 0, 0))
    mat = pl.BlockSpec((None, SSM_GROUP_CH, N_STATE), lambda l: (l, 0, 0))
    return pl.pallas_call(
        _disc_kernel,
        grid=(depth,),
        in_specs=[row, row, row, mat, mat],
        out_specs=[row, row, mat, mat],
        out_shape=[jax.ShapeDtypeStruct((depth, 1, N_STATE), F32)] * 2
        + [jax.ShapeDtypeStruct((depth, SSM_GROUP_CH, N_STATE), F32)] * 2,
        compiler_params=_params(("arbitrary",)),
        name="ssm_discretise",
    )(a_re, a_im, ldt, bt_re, bt_im)


def _rope(x, cos, sm):
    lane = lax.broadcasted_iota(jnp.int32, x.shape, 1) % HEAD_DIM
    partner = jnp.where(lane < ROT_DIM // 2, pltpu.roll(x, LANES - ROT_DIM // 2, 1), pltpu.roll(x, ROT_DIM // 2, 1))
    return x * cos + partner * sm


def _to_time_major(x_ref, scr, n_batch):
    steps = x_ref.shape[1]
    for b in range(n_batch):
        for c in range(D_MODEL // LANES):
            scr[c, pl.ds(b, steps, stride=n_batch), :] = x_ref[b, :, c * LANES:(c + 1) * LANES]
    return jnp.concatenate([scr[c] for c in range(D_MODEL // LANES)], axis=1)


def _from_time_major(x, o_ref, scr, n_batch):
    steps = o_ref.shape[1]
    for c in range(D_MODEL // LANES):
        scr[c] = x[:, c * LANES:(c + 1) * LANES]
    for b in range(n_batch):
        for c in range(D_MODEL // LANES):
            o_ref[b, :, c * LANES:(c + 1) * LANES] = scr[c, pl.ds(b, steps, stride=n_batch), :]


def _rms(x, g_ref):
    ms = jnp.mean(x * x, axis=-1, keepdims=True)
    return x * lax.rsqrt(ms + EPS) * g_ref[...]


def _per_sequence(y, n_batch, fn):
    rows = y.shape[0]
    return fn(y.reshape(rows // n_batch, n_batch, D_MODEL)).reshape(rows, D_MODEL)


def _pair_heads(chunks):
    low = lax.broadcasted_iota(jnp.int32, chunks[0].shape, 1) < HEAD_DIM
    swap = lambda x: pltpu.roll(x, HEAD_DIM, 1)
    n0, n1, n2, n3 = chunks
    return [jnp.where(low, n0, swap(n2)), jnp.where(low, swap(n0), n2),
            jnp.where(low, n1, swap(n3)), jnp.where(low, swap(n1), n3)]


def _unpair_heads(chunks):
    low = lax.broadcasted_iota(jnp.int32, chunks[0].shape, 1) < HEAD_DIM
    swap = lambda x: pltpu.roll(x, HEAD_DIM, 1)
    p0, p1, p2, p3 = chunks
    return [jnp.where(low, p0, swap(p1)), jnp.where(low, p2, swap(p3)),
            jnp.where(low, swap(p0), p1), jnp.where(low, swap(p2), p3)]


def _residual_update(x, rows, ya_ref, ys_ref, gate_ref, w_ref, n_batch):
    ya = _unpair_heads([ya_ref[c, rows, :] for c in range(N_CHUNKS)])
    mixed = jnp.concatenate([y.astype(BF16) for y in ya] + [ys_ref[rows, :]], axis=1)
    mix = jnp.dot(mixed, w_ref[...], preferred_element_type=F32)
    return x + _per_sequence(mix, n_batch, lambda m: m * gate_ref[...][None])


def _modulated_norm(x, g_ref, scale_ref, shift_ref, n_batch):
    y = _per_sequence(_rms(x, g_ref), n_batch, lambda y3: y3 * (1.0 + scale_ref[...])[None] + shift_ref[...][None])
    return y.astype(BF16)


def _silu(z):
    return z * jax.nn.sigmoid(z)


def _project(h, rows, w_ref, rope_scr, q_ref, k_ref, v_ref, ga_ref, u_ref, gs_ref):
    cos = rope_scr[0, rows, :]
    sm = rope_scr[1, rows, :]

    def proj(lo, width):
        return jnp.dot(h, w_ref[:, lo:lo + width], preferred_element_type=F32)

    chunks_of = lambda a: [a[:, c * LANES:(c + 1) * LANES] for c in range(N_CHUNKS)]
    q = _pair_heads(chunks_of(proj(COL_Q, ATTN_W)))
    ga = _pair_heads(chunks_of(_silu(proj(COL_ZA, ATTN_W))))
    for c in range(N_CHUNKS):
        q_ref[c, rows, :] = _rope(q[c], cos, sm)
        ga_ref[c, rows, :] = ga[c]
    u_ref[rows, :] = proj(COL_U, SSM_W)
    gs_ref[rows, :] = _silu(proj(COL_ZS, SSM_W))
    kv = proj(COL_KV, 2 * KV_W)
    k_ref[rows, :] = _rope(kv[:, :KV_W], cos, sm)
    v_ref[rows, :] = kv[:, KV_W:]


def _boundary_kernel(*refs, n_batch, closes, opens, batch_major, block_rows, n_parts):
    refs = list(refs)
    x_ref = refs.pop(0)
    close_refs = [refs.pop(0) for _ in range(4)] if closes else None
    final_g_ref = refs.pop(0) if closes and not opens else None
    open_refs = [refs.pop(0) for _ in range(6)] if opens else None
    x_out_ref = refs.pop(0) if closes or batch_major else None
    proj_out_refs = [refs.pop(0) for _ in range(6)] if opens else None
    scr = refs.pop(0) if batch_major else None
    rope_scr = refs.pop(0) if opens else None

    if opens:
        g_ref, scale_ref, shift_ref, w_in_ref, cos_ref, sm_ref = open_refs
        steps = cos_ref.shape[0]
        for b in range(n_batch):
            rope_scr[0, pl.ds(b, steps, stride=n_batch), :] = cos_ref[...]
            rope_scr[1, pl.ds(b, steps, stride=n_batch), :] = sm_ref[...]

    part = block_rows // n_parts
    groups = [slice(i * part, (i + 1) * part) for i in range(n_parts)]
    x_full = _to_time_major(x_ref, scr, n_batch) if (batch_major and not closes) else None

    def updated(rows):
        x = x_full[rows, :] if x_full is not None else x_ref[rows, :]
        return _residual_update(x, rows, *close_refs, n_batch) if closes else x

    def normed(x, rows):
        if opens:
            if x_out_ref is not None:
                x_out_ref[rows, :] = x
            return _modulated_norm(x, g_ref, scale_ref, shift_ref, n_batch)
        return _rms(x, final_g_ref)

    xs = [updated(rows) for rows in groups]
    hs = [None] * n_parts
    hs[0] = normed(xs[0], groups[0])
    for i, rows in enumerate(groups):
        if opens:
            _project(hs[i], rows, w_in_ref, rope_scr, *proj_out_refs)
        if i + 1 < n_parts:
            hs[i + 1] = normed(xs[i + 1], groups[i + 1])
    if not opens:
        y = jnp.concatenate(hs, axis=0)
        if batch_major:
            _from_time_major(y, x_out_ref, scr, n_batch)
        else:
            x_out_ref[...] = y


def _layer_spec(arr, l):
    tail = arr.shape[1:]
    return pl.BlockSpec((None,) + tail, lambda *_: (l,) + (0,) * len(tail))


def _mod_spec(l, group, part):
    return pl.BlockSpec((None, group.n_batch, D_MODEL), lambda *_: (l, group.mod_block, part))


def layer_boundary(x, mod, group, block_rows, close=None, open_=None, final_g=None, batch_major_out=False):
    closes, opens = close is not None, open_ is not None
    batch_major_in = x.ndim == 3
    assert not (batch_major_in and closes) and not (batch_major_out and opens)
    batch_major = batch_major_in or batch_major_out
    t_rows = x.shape[0] * x.shape[1] if batch_major_in else x.shape[0]
    rows = lambda w: pl.BlockSpec((block_rows, w), lambda i: (i, 0))
    chunked = pl.BlockSpec((N_CHUNKS, block_rows, LANES), lambda i: (0, i, 0))
    blocked3 = pl.BlockSpec((group.n_batch, block_rows // group.n_batch, D_MODEL), lambda i: (0, i, 0))
    flat = lambda w: jax.ShapeDtypeStruct((t_rows, w), F32)
    chunked_shape = jax.ShapeDtypeStruct((N_CHUNKS, t_rows, LANES), F32)

    args, in_specs, out_specs, out_shape = [x], [blocked3 if batch_major_in else rows(D_MODEL)], [], []
    if closes:
        ya, ys, w_out, l = close
        args += [ya, ys, mod, w_out]
        in_specs += [chunked, rows(SSM_W), _mod_spec(l, group, 2), _layer_spec(w_out, l)]
        if not opens:
            args.append(final_g)
            in_specs.append(_full(final_g.shape))
    if opens:
        norm_g, w_in, cos_t, sm_t, l = open_
        args += [norm_g, mod, mod, w_in, cos_t, sm_t]
        per_step = pl.BlockSpec((block_rows // group.n_batch, LANES), lambda i: (i, 0))
        in_specs += [_layer_spec(norm_g, l), _mod_spec(l, group, 1), _mod_spec(l, group, 0), _layer_spec(w_in, l),
                     per_step, per_step]
    if closes or batch_major_in:
        out_specs.append(blocked3 if batch_major_out else rows(D_MODEL))
        out_shape.append(jax.ShapeDtypeStruct((group.n_batch, t_rows // group.n_batch, D_MODEL), F32)
                         if batch_major_out else flat(D_MODEL))
    if opens:
        out_specs += [chunked, rows(KV_W), rows(KV_W), chunked, rows(SSM_W), rows(SSM_W)]
        out_shape += [chunked_shape, flat(KV_W), flat(KV_W), chunked_shape, flat(SSM_W), flat(SSM_W)]
    scratch = [pltpu.VMEM((D_MODEL // LANES, block_rows, LANES), F32)] if batch_major else []
    if opens:
        scratch.append(pltpu.VMEM((2, block_rows, LANES), F32))
    outs = pl.pallas_call(
        functools.partial(_boundary_kernel, n_batch=group.n_batch, closes=closes, opens=opens,
                          batch_major=batch_major, block_rows=block_rows, n_parts=BOUNDARY_PARTS),
        grid=(t_rows // block_rows,),
        in_specs=in_specs,
        out_specs=out_specs,
        out_shape=out_shape,
        scratch_shapes=scratch,
        compiler_params=_params(("parallel",)),
        name="layer_boundary",
    )(*args)
    x_rows = outs[0] if (closes or batch_major_in) else x
    return x_rows, (tuple(outs[-6:]) if opens else None)


def _pad_queries(chunks):
    lane = lax.broadcasted_iota(jnp.int32, chunks[0].shape, 1)
    pieces = []
    for chunk in chunks:
        pieces.append(jnp.where(lane < HEAD_DIM, chunk, 0.0))
        pieces.append(jnp.where(lane >= HEAD_DIM, chunk, 0.0))
    return jnp.concatenate(pieces, axis=0).astype(BF16)


def _unpad_outputs(o, t):
    lane = lax.broadcasted_iota(jnp.int32, (t, LANES), 1)
    chunks = []
    for c in range(N_CHUNKS):
        lo = o[(2 * c) * t:(2 * c + 1) * t]
        hi = o[(2 * c + 1) * t:(2 * c + 2) * t]
        chunks.append(jnp.where(lane < HEAD_DIM, lo, hi))
    return chunks


def _gated_store(o_ref, ga_ref, sel, ya_chunks):
    for c, ya in enumerate(ya_chunks):
        o_ref[c, sel, :] = ya * ga_ref[c, sel, :]


def _scores(qp, k, scale=HEAD_DIM ** -0.5):
    return lax.dot_general(qp, k.astype(BF16), (((1,), (1,)), ((), ())), preferred_element_type=F32) * scale


def _prompt_attention_stages(q_ref, kp_ref, kc_ref, vp_ref, vc_ref, za_ref, sink_ref, o_ref, n_batch, blk):
    assert blk == WINDOW
    n = pl.program_id(0)
    qb = blk // 2
    keys = blk + qb
    cols = 2 * LANES
    j = lax.broadcasted_iota(jnp.int32, (keys, cols), 0)
    tq = lax.broadcasted_iota(jnp.int32, (keys, cols), 1) % qb
    masks = [(j >= jnp.maximum(tq, jnp.where(n == 0, blk - h * qb, 0))) & (j <= tq + WINDOW) for h in range(2)]
    low = lax.broadcasted_iota(jnp.int32, (qb, LANES), 1) < HEAD_DIM
    ones = jnp.ones((SUBLANES, keys), F32)
    units = [(h, pr) for h in range(2) for pr in range(N_CHUNKS // 2)]
    sinks = [jnp.concatenate([sink_ref[slot:slot + 1, :qb] for slot in range(4 * pr, 4 * pr + 4)], axis=1) * LOG2_E
             for pr in range(N_CHUNKS // 2)]

    def query_rows(b, h):
        return pl.ds(b + h * qb * n_batch, qb, stride=n_batch)

    def window(prev_ref, cur_ref, b, h):
        if h == 0:
            return jnp.concatenate([prev_ref[pl.ds(b, blk, stride=n_batch), :], cur_ref[query_rows(b, 0), :]], axis=0)
        return jnp.concatenate([prev_ref[query_rows(b, 1), :], cur_ref[pl.ds(b, blk, stride=n_batch), :]], axis=0)

    def score_stage(b):
        sts = []
        for h in range(2):
            k = window(kp_ref, kc_ref, b, h).astype(BF16)
            for pr in range(N_CHUNKS // 2):
                pieces = []
                for c in (2 * pr, 2 * pr + 1):
                    qc = q_ref[c, query_rows(b, h), :]
                    pieces += [jnp.where(low, qc, 0.0), jnp.where(low, 0.0, qc)]
                qp = jnp.concatenate(pieces, axis=0).astype(BF16)
                sts.append(_scores(k, qp, HEAD_DIM ** -0.5 * LOG2_E))
        return sts

    def value_stage(b, sts):
        vt_aug = [jnp.concatenate([window(vp_ref, vc_ref, b, h).T, ones], axis=0).astype(BF16) for h in range(2)]

        def finish(h, pr, ot, m):
            den = ot[KV_W:KV_W + 1, :] + jnp.exp2(sinks[pr] - m)
            o = (ot[:KV_W, :] * (1.0 / den)).T
            sel = query_rows(b, h)
            for i, c in enumerate((2 * pr, 2 * pr + 1)):
                lo = 2 * i * qb
                ya = jnp.where(low, o[lo:lo + qb], o[lo + qb:lo + 2 * qb])
                o_ref[c, sel, :] = ya * za_ref[c, sel, :]

        pending = None
        for (h, pr), st in zip(units, sts):
            st = jnp.where(masks[h], st, NEG_INF)
            m = jnp.maximum(jnp.max(st, axis=0, keepdims=True), sinks[pr])
            p = jnp.exp2(st - m).astype(BF16)
            ot = jnp.dot(vt_aug[h], p, preferred_element_type=F32)
            if pending is not None:
                finish(*pending)
            pending = (h, pr, ot, m)
        finish(*pending)

    return score_stage, value_stage


def _attn_sample_kernel(q_ref, k_ref, v_ref, za_ref, ck_ref, cv_ref, sink_ref, nk_all_ref, nv_all_ref,
                        o_ref, nk_ref, nv_ref, *, n_batch, t_new, group):
    del nk_all_ref, nv_all_ref
    g = pl.program_id(0)
    win = ck_ref.shape[2]
    rows = N_Q_HEADS * t_new
    tq = lax.broadcasted_iota(jnp.int32, (rows, 2 * win), 0) % t_new
    j = lax.broadcasted_iota(jnp.int32, (rows, 2 * win), 1)
    dist_c = tq + win - j
    mask = ((j < win) & (dist_c >= 0) & (dist_c <= WINDOW)) | ((j >= win) & (j - win <= tq) & (j - win < t_new))
    sink = sink_ref[...]
    pad = jnp.zeros((win - t_new, KV_W), F32)
    ones_rows = jnp.ones((LANES, win), F32)
    ones_cols = jnp.ones((win, LANES), F32)
    kept = lax.broadcasted_iota(jnp.int32, (KV_W, win), 1) < win - t_new
    batch = range(group)
    sels = [pl.ds(g * group + i, t_new, stride=n_batch) for i in batch]

    scores = []
    for i in batch:
        qp = _pad_queries([q_ref[c, sels[i], :] for c in range(N_CHUNKS)])
        s_cache = jnp.dot(qp, ck_ref[i].astype(BF16), preferred_element_type=F32) * (HEAD_DIM ** -0.5)
        s_new = _scores(qp, jnp.concatenate([k_ref[sels[i], :], pad], axis=0))
        scores.append(jnp.where(mask, jnp.concatenate([s_cache, s_new], axis=1), NEG_INF))
    outs = []
    for i in batch:
        s = scores[i]
        m = jnp.maximum(jnp.max(s, axis=-1, keepdims=True), sink[:, :1])
        p = jnp.exp(s - m).astype(BF16)
        vt_aug = jnp.concatenate([cv_ref[i], ones_rows], axis=0).astype(BF16)
        v_new = jnp.concatenate([jnp.concatenate([v_ref[sels[i], :], pad], axis=0), ones_cols], axis=1).astype(BF16)
        o = (lax.dot_general(p[:, :win], vt_aug, (((1,), (1,)), ((), ())), preferred_element_type=F32)
             + jnp.dot(p[:, win:], v_new, preferred_element_type=F32))
        outs.append(o[:, :KV_W] / (o[:, KV_W:] + jnp.exp(sink - m)))
    for i in batch:
        _gated_store(o_ref, za_ref, sels[i], _unpad_outputs(outs[i], t_new))
        for new_ref, old_ref, rows_ref in ((nk_ref, ck_ref, k_ref), (nv_ref, cv_ref, v_ref)):
            appended = jnp.concatenate([pad, rows_ref[sels[i], :]], axis=0).T
            new_ref[i] = jnp.where(kept, pltpu.roll(old_ref[i], win - t_new, 1), appended)


def attention_sample(q, k, v, za, cache_kt, cache_vt, sink_rows, new_kt, new_vt, l, n_batch, t_new, group):
    t_rows = k.shape[0]
    win = cache_kt.shape[3]
    window = pl.BlockSpec((None, group, KV_W, win), lambda i: (l, i, 0, 0))
    chunked = _full((N_CHUNKS, t_rows, LANES))
    in_place = pl.BlockSpec(memory_space=pl.ANY)
    return pl.pallas_call(
        functools.partial(_attn_sample_kernel, n_batch=n_batch, t_new=t_new, group=group),
        grid=(n_batch // group,),
        in_specs=[chunked, _full((t_rows, KV_W)), _full((t_rows, KV_W)), chunked,
                  window, window, _layer_spec(sink_rows, l), in_place, in_place],
        out_specs=[chunked, window, window],
        out_shape=[jax.ShapeDtypeStruct((N_CHUNKS, t_rows, LANES), F32),
                   jax.ShapeDtypeStruct(new_kt.shape, F32), jax.ShapeDtypeStruct(new_vt.shape, F32)],
        input_output_aliases={7: 1, 8: 2},
        compiler_params=_params(("arbitrary",)),
        name="attention_sample",
    )(q, k, v, za, cache_kt, cache_vt, sink_rows, new_kt, new_vt)


def _ssm_stages(u_ref, zs_ref, h0re_ref, h0im_ref, are_ref, aim_ref, wb_ref, wc_ref, d_ref, wg_ref, bg_ref,
                o_ref, hre_ref, him_ref, x_scr, n_batch, sub_rows, slab):
    step = pl.program_id(0)

    @pl.when(step == 0)
    def _():
        hre_ref[...] = h0re_ref[...]
        him_ref[...] = h0im_ref[...]

    rows = u_ref.shape[0]
    half_in = SSM_W // 2
    n_sub = rows // sub_rows
    n_tiles = n_batch // SUBLANES
    t_steps = sub_rows // n_batch
    n_buf = x_scr.shape[0]
    items = [(sc, s) for sc in range(n_sub) for s in range(2)]

    def expand(i):
        sc, s = items[i]
        ub = u_ref[sc * sub_rows:(sc + 1) * sub_rows, s * half_in:(s + 1) * half_in].astype(BF16)
        x_scr[i % n_buf] = jnp.dot(ub, wb_ref[s], preferred_element_type=F32)

    def recur(i):
        sc, s = items[i]
        buf = i % n_buf
        for jb in range(HALF_STATE // slab):
            n_lo = s * HALF_STATE + jb * slab
            ar = jnp.broadcast_to(are_ref[:, n_lo:n_lo + slab], (SUBLANES, slab))
            ai = jnp.broadcast_to(aim_ref[:, n_lo:n_lo + slab], (SUBLANES, slab))
            re = slice(jb * slab, (jb + 1) * slab)
            im = slice(HALF_STATE + jb * slab, HALF_STATE + (jb + 1) * slab)
            for tile in range(n_tiles):
                r0 = tile * SUBLANES
                hr = hre_ref[r0:r0 + SUBLANES, n_lo:n_lo + slab]
                hi = him_ref[r0:r0 + SUBLANES, n_lo:n_lo + slab]
                for t in range(t_steps):
                    row = slice(t * n_batch + r0, t * n_batch + r0 + SUBLANES)
                    hr, hi = (ar * hr - ai * hi + x_scr[buf, row, re], ar * hi + ai * hr + x_scr[buf, row, im])
                    x_scr[buf, row, re] = hr
                    x_scr[buf, row, im] = hi
                hre_ref[r0:r0 + SUBLANES, n_lo:n_lo + slab] = hr
                him_ref[r0:r0 + SUBLANES, n_lo:n_lo + slab] = hi

    y_halves = {}

    def contract(i):
        sc, s = items[i]
        y_halves[s] = jnp.dot(x_scr[i % n_buf].astype(BF16), wc_ref[s], preferred_element_type=F32)
        if s == 1:
            r = slice(sc * sub_rows, (sc + 1) * sub_rows)
            y = jnp.concatenate([y_halves[0], y_halves[1]], axis=1) + d_ref[...] * u_ref[r, :]
            y = jax.nn.gelu(y)
            gate = jax.nn.sigmoid(jnp.dot(y.astype(BF16), wg_ref[...], preferred_element_type=F32) + bg_ref[...])
            o_ref[r, :] = (y * gate * zs_ref[r, :]).astype(o_ref.dtype)

    return len(items), expand, recur, contract


def _ssm_kernel(*refs, n_batch, sub_rows, slab):
    n_items, expand, recur, contract = _ssm_stages(*refs, n_batch, sub_rows, slab)
    expand(0)
    for i in range(n_items + 1):
        if i + 1 < n_items:
            expand(i + 1)
        if i < n_items:
            recur(i)
        if i >= 1:
            contract(i - 1)


N_ATTN_INPUTS = 7
N_SSM_INPUTS = 11


def _prompt_branches_kernel(*refs, n_batch, blk, sub_rows, slab):
    n_in = N_ATTN_INPUTS + N_SSM_INPUTS
    attn_refs = refs[:N_ATTN_INPUTS] + refs[n_in:n_in + 1]
    ssm_refs = refs[N_ATTN_INPUTS:n_in] + refs[n_in + 1:]
    score_stage, value_stage = _prompt_attention_stages(*attn_refs, n_batch, blk)
    n_items, expand, recur, contract = _ssm_stages(*ssm_refs, n_batch, sub_rows, slab)
    per_item = n_batch // n_items
    assert per_item * n_items == n_batch
    seqs_of = lambda i: range(i * per_item, (i + 1) * per_item)
    expand(0)
    scores = [score_stage(b) for b in seqs_of(0)]
    for i in range(n_items + 1):
        if i < n_items:
            recur(i)
        if i + 1 < n_items:
            expand(i + 1)
        if i < n_items:
            for b, sts in zip(seqs_of(i), scores):
                value_stage(b, sts)
        if i + 1 < n_items:
            scores = [score_stage(b) for b in seqs_of(i + 1)]
        if i >= 1:
            contract(i - 1)


def _ssm_call_parts(u, h0_re, h0_im, abar_re, abar_im, w_b, w_c, d, w_glu, b_glu, l, l_state, n_batch, r):
    t_rows = u.shape[0]
    sub_rows = max(SSM_SUB_ROWS, n_batch)
    rows = pl.BlockSpec((r, SSM_W), lambda i: (i, 0))
    state = _full((n_batch, N_STATE))
    layer = lambda a: _layer_spec(a, l)
    in_specs = [rows, rows, _layer_spec(h0_re, l_state), _layer_spec(h0_im, l_state), layer(abar_re), layer(abar_im),
                layer(w_b), layer(w_c), layer(d), layer(w_glu), layer(b_glu)]
    out_shape = [jax.ShapeDtypeStruct((t_rows, SSM_W), BF16),
                 jax.ShapeDtypeStruct((n_batch, N_STATE), F32), jax.ShapeDtypeStruct((n_batch, N_STATE), F32)]
    scratch = [pltpu.VMEM((SSM_BUFFERS, sub_rows, 2 * HALF_STATE), F32)]
    return in_specs, [rows, state, state], out_shape, scratch, sub_rows


def ssm_branch(u, zs, h0_re, h0_im, abar_re, abar_im, w_b, w_c, d, w_glu, b_glu, l, l_state, n_batch, t_chunk):
    r = t_chunk * n_batch
    in_specs, out_specs, out_shape, scratch, sub_rows = _ssm_call_parts(
        u, h0_re, h0_im, abar_re, abar_im, w_b, w_c, d, w_glu, b_glu, l, l_state, n_batch, r)
    return pl.pallas_call(
        functools.partial(_ssm_kernel, n_batch=n_batch, sub_rows=sub_rows, slab=SSM_SLAB),
        grid=(u.shape[0] // r,),
        in_specs=in_specs,
        out_specs=out_specs,
        out_shape=out_shape,
        scratch_shapes=scratch,
        compiler_params=_params(("arbitrary",)),
        name="ssm_branch",
    )(u, zs, h0_re, h0_im, abar_re, abar_im, w_b, w_c, d, w_glu, b_glu)


def prompt_branches(q, k, v, za, sinks, u, zs, h0_re, h0_im, abar_re, abar_im, w_b, w_c, d, w_glu, b_glu,
                    l, l_state, n_batch, blk):
    t_rows = k.shape[0]
    r = blk * n_batch
    cur = lambda w: pl.BlockSpec((r, w), lambda i: (i, 0))
    prev = lambda w: pl.BlockSpec((r, w), lambda i: (jnp.maximum(i - 1, 0), 0))
    chunked = pl.BlockSpec((N_CHUNKS, r, LANES), lambda i: (0, i, 0))
    ssm_in, ssm_out, ssm_shape, scratch, sub_rows = _ssm_call_parts(
        u, h0_re, h0_im, abar_re, abar_im, w_b, w_c, d, w_glu, b_glu, l, l_state, n_batch, r)
    attn_in = [chunked, prev(KV_W), cur(KV_W), prev(KV_W), cur(KV_W), chunked, _layer_spec(sinks, l)]
    assert len(attn_in) == N_ATTN_INPUTS and len(ssm_in) == N_SSM_INPUTS
    return pl.pallas_call(
        functools.partial(_prompt_branches_kernel, n_batch=n_batch, blk=blk, sub_rows=sub_rows, slab=SSM_SLAB),
        grid=(t_rows // r,),
        in_specs=attn_in + ssm_in,
        out_specs=[chunked] + ssm_out,
        out_shape=[jax.ShapeDtypeStruct((N_CHUNKS, t_rows, LANES), F32)] + ssm_shape,
        scratch_shapes=scratch,
        compiler_params=_params(("arbitrary",)),
        name="prompt_branches",
    )(q, k, k, v, v, za, sinks, u, zs, h0_re, h0_im, abar_re, abar_im, w_b, w_c, d, w_glu, b_glu)


def _rope_tables(pos):
    half = ROT_DIM // 2
    inv = ROPE_THETA ** (-jnp.arange(half, dtype=F32) / half)
    ang = pos.astype(F32)[:, None] * inv[None, :]
    d = np.arange(LANES) % HEAD_DIM
    idx = d % half
    cos = jnp.where(d < ROT_DIM, jnp.cos(ang)[:, idx], 1.0)
    sin = jnp.sin(ang)[:, idx]
    sm = jnp.where(d < half, -sin, jnp.where(d < ROT_DIM, sin, 0.0))
    return cos, sm


def _keep_diagonal_blocks(w, row_block, col_block):
    rows, cols = w.shape[-2:]
    keep = (np.arange(rows)[:, None] // row_block) == (np.arange(cols)[None, :] // col_block)
    return jnp.where(keep, w, 0.0)


def _expand_weights(bb):
    depth = bb.shape[0]
    hg = N_SSM_GROUPS // 2
    halves = bb.reshape(depth, SSM_GROUP_CH, 2, HALF_STATE).transpose(0, 2, 1, 3)
    tiled = jnp.broadcast_to(halves[:, :, None], (depth, 2, hg, SSM_GROUP_CH, HALF_STATE))
    return _keep_diagonal_blocks(tiled.reshape(depth, 2, hg * SSM_GROUP_CH, HALF_STATE), SSM_GROUP_CH, SSM_STATE)


def _contract_weights(c):
    depth = c.shape[0]
    hg = N_SSM_GROUPS // 2
    rows = c.reshape(depth, 2, hg, SSM_GROUP_CH, SSM_STATE).transpose(0, 1, 2, 4, 3).reshape(
        depth, 2, HALF_STATE, SSM_GROUP_CH)
    return _keep_diagonal_blocks(jnp.tile(rows, (1, 1, 1, hg)), SSM_STATE, SSM_GROUP_CH)


def _prepare_weights(w_in, w_out, attn_sinks, bb_re, bb_im, ssm_c_re, ssm_c_im, t_new):
    depth = w_in.shape[0]
    assert HEAD_PERM == tuple(kv * Q_GROUP + g for g in range(Q_GROUP) for kv in range(N_KV_HEADS))
    w_in_p = w_in.astype(BF16)
    w_out_p = w_out.astype(BF16)
    sinks = jnp.swapaxes(attn_sinks.reshape(depth, N_KV_HEADS, Q_GROUP), 1, 2).reshape(depth, N_Q_HEADS)
    sink_lanes = jnp.broadcast_to(sinks[:, :, None], (depth, N_Q_HEADS, LANES))
    sink_rows = jnp.broadcast_to(jnp.repeat(sinks, t_new, axis=1)[:, :, None], (depth, N_Q_HEADS * t_new, LANES))
    w_b = jnp.concatenate([_expand_weights(bb_re), _expand_weights(bb_im)], axis=3).astype(BF16)
    w_c = jnp.concatenate([_contract_weights(ssm_c_re), -_contract_weights(ssm_c_im)], axis=2).astype(BF16)
    return w_in_p, w_out_p, sink_lanes, sink_rows, w_b, w_c


def kernel(x_prompt, x_sample, cache_k, cache_v, state_ssm_re, state_ssm_im, c_prompt, c_sample, norm_g, w_ada, b_ada, w_in, attn_sinks, ssm_a_re, ssm_a_im, ssm_log_dt, ssm_b_re, ssm_b_im, ssm_c_re, ssm_c_im, ssm_d, w_glu, b_glu, w_out, final_g):
    depth = w_in.shape[0]
    nb_p, seq, _ = x_prompt.shape
    nb_s, t_new, _ = x_sample.shape
    win = cache_k.shape[2]
    blk = WINDOW

    sample = Group(n_batch=nb_s, mod_block=0)
    prompt = Group(n_batch=nb_p, mod_block=nb_s // nb_p)
    mod = adaln_mod(jnp.concatenate([c_sample, c_prompt], axis=0), w_ada, b_ada)
    abar_re, abar_im, bb_re, bb_im = ssm_discretise(ssm_a_re, ssm_a_im, ssm_log_dt, ssm_b_re, ssm_b_im)
    w_in_p, w_out_p, sink_lanes, sink_rows, w_b, w_c = _prepare_weights(
        w_in, w_out, attn_sinks, bb_re, bb_im, ssm_c_re, ssm_c_im, t_new)
    norm_g3 = norm_g.reshape(depth, 1, D_MODEL)
    d3 = ssm_d.reshape(depth, 1, SSM_W)
    w_glu_b = w_glu.astype(BF16)
    b_glu3 = b_glu.reshape(depth, 1, SSM_W)
    final_g2 = final_g.reshape(1, D_MODEL)
    ssm_weights = (abar_re, abar_im, w_b, w_c, d3, w_glu_b, b_glu3)

    cos_p, sm_p = _rope_tables(jnp.arange(seq))
    cos_s, sm_s = _rope_tables(PAST_LEN + jnp.arange(t_new))

    xp = x_prompt
    xs = x_sample.transpose(1, 0, 2).reshape(t_new * nb_s, D_MODEL)
    zeros_state = jnp.zeros((1, nb_p, N_STATE), F32)
    transposed = lambda c: c.transpose(0, 1, 3, 4, 2).reshape(depth, nb_s, KV_W, win)
    cache_kt, cache_vt = transposed(cache_k), transposed(cache_v)
    new_kt, new_vt = jnp.zeros_like(cache_kt), jnp.zeros_like(cache_vt)
    state_re3 = state_ssm_re.reshape(depth, nb_s, N_STATE)
    state_im3 = state_ssm_im.reshape(depth, nb_s, N_STATE)

    outs = {k: [] for k in ("kp", "vp", "rp", "ip", "rs", "is")}
    rows_p, rows_s = BOUNDARY_ROWS, t_new * nb_s
    opening = lambda l, cos_t, sm_t: (norm_g3, w_in_p, cos_t, sm_t, l)
    xp, proj_p = layer_boundary(xp, mod, prompt, rows_p, open_=opening(0, cos_p, sm_p))
    xs, proj_s = layer_boundary(xs, mod, sample, rows_s, open_=opening(0, cos_s, sm_s))
    for l in range(depth):
        last = l == depth - 1

        q, k, v, za, u, zs = proj_p
        ya, ys, hre, him = prompt_branches(q, k, v, za, sink_lanes, u, zs, zeros_state, zeros_state, *ssm_weights,
                                           l, 0, nb_p, blk)
        xp, proj_p = layer_boundary(xp, mod, prompt, rows_p, close=(ya, ys, w_out_p, l),
                                    open_=None if last else opening(l + 1, cos_p, sm_p),
                                    final_g=final_g2, batch_major_out=last)
        outs["kp"].append(k[(seq - win) * nb_p:])
        outs["vp"].append(v[(seq - win) * nb_p:])
        outs["rp"].append(hre)
        outs["ip"].append(him)

        q, k, v, za, u, zs = proj_s
        ya, new_kt, new_vt = attention_sample(q, k, v, za, cache_kt, cache_vt, sink_rows, new_kt, new_vt, l,
                                              nb_s, t_new, SAMPLE_GROUP)
        ys, hre, him = ssm_branch(u, zs, state_re3, state_im3, *ssm_weights, l, l, nb_s, t_new)
        xs, proj_s = layer_boundary(xs, mod, sample, rows_s, close=(ya, ys, w_out_p, l),
                                    open_=None if last else opening(l + 1, cos_s, sm_s), final_g=final_g2)
        outs["rs"].append(hre)
        outs["is"].append(him)

    y_prompt = xp
    y_sample = xs.reshape(t_new, nb_s, D_MODEL).transpose(1, 0, 2)
    st = lambda key: jnp.stack(outs[key])
    window = lambda key: st(key).reshape(depth, win, nb_p, N_KV_HEADS, HEAD_DIM).transpose(0, 2, 1, 3, 4)
    state = lambda key, nb: st(key).reshape(depth, nb, N_SSM_GROUPS, SSM_STATE)
    untransposed = lambda w: w.reshape(depth, nb_s, N_KV_HEADS, HEAD_DIM, win).transpose(0, 1, 4, 2, 3)
    return (y_prompt, y_sample, window("kp"), window("vp"), state("rp", nb_p), state("ip", nb_p),
            untransposed(new_kt), untransposed(new_vt), state("rs", nb_s), state("is", nb_s))
```

```python
import functools
from typing import NamedTuple

import jax
import jax.numpy as jnp
import numpy as np
from jax import lax
from jax.experimental import pallas as pl
from jax.experimental.pallas import tpu as pltpu

F32 = jnp.float32
BF16 = jnp.bfloat16

D_MODEL = 1024
HEAD_DIM = 64
ATTN_W = 512
N_Q_HEADS = 8
N_KV_HEADS = 2
Q_GROUP = N_Q_HEADS // N_KV_HEADS
KV_W = 128
WINDOW = 128
ROT_DIM = 16
ROPE_THETA = 500000.0
SSM_W = 512
SSM_GROUP_CH = 16
N_SSM_GROUPS = 32
SSM_STATE = 64
N_STATE = N_SSM_GROUPS * SSM_STATE
COL_Q, COL_KV, COL_ZA = 0, ATTN_W, ATTN_W + 2 * KV_W
COL_U, COL_ZS = COL_ZA + ATTN_W, COL_ZA + ATTN_W + SSM_W
HALF_STATE = N_STATE // 2
EPS = 1e-6
NEG_INF = -1e30
LOG2_E = 1.4426950408889634
PAST_LEN = 8192

MOD_PARTS = 3
LANES = 128
N_CHUNKS = ATTN_W // LANES
BOUNDARY_ROWS = 1024
BOUNDARY_PARTS = 2
SSM_SLAB = 4 * LANES
SAMPLE_GROUP = 16
SSM_SUB_ROWS = 256
SSM_BUFFERS = 3
SUBLANES = 8
VMEM_LIMIT = 56 * 1024 * 1024

HEAD_PERM = (0, 4, 1, 5, 2, 6, 3, 7)


class Group(NamedTuple):
    n_batch: int
    mod_block: int


def _params(sem):
    return pltpu.CompilerParams(dimension_semantics=sem, vmem_limit_bytes=VMEM_LIMIT)


def _full(shape):
    return pl.BlockSpec(shape, lambda *_: (0,) * len(shape))


def _mod_kernel(c_ref, w_ref, b_ref, o_ref):
    c = c_ref[...]
    a = (c * jax.nn.sigmoid(c)).astype(BF16)
    o_ref[...] = jnp.dot(a, w_ref[...].astype(BF16), preferred_element_type=F32) + b_ref[...]


def adaln_mod(c_all, w_ada, b_ada):
    depth = w_ada.shape[0]
    n = c_all.shape[0]
    return pl.pallas_call(
        _mod_kernel,
        grid=(depth, MOD_PARTS),
        in_specs=[
            pl.BlockSpec((n, D_MODEL), lambda l, j: (0, 0)),
            pl.BlockSpec((None, D_MODEL, D_MODEL), lambda l, j: (l, 0, j)),
            pl.BlockSpec((None, 1, D_MODEL), lambda l, j: (l, 0, j)),
        ],
        out_specs=pl.BlockSpec((None, n, D_MODEL), lambda l, j: (l, 0, j)),
        out_shape=jax.ShapeDtypeStruct((depth, n, 3 * D_MODEL), F32),
        compiler_params=_params(("arbitrary", "arbitrary")),
        name="adaln_mod",
    )(c_all, w_ada, b_ada.reshape(depth, 1, 3 * D_MODEL))


def _disc_kernel(are_ref, aim_ref, ldt_ref, bre_ref, bim_ref, abre_ref, abim_ref, bbre_ref, bbim_ref):
    a_re = are_ref[...]
    a_im = aim_ref[...]
    dt = jnp.exp(ldt_ref[...])
    mag = jnp.exp(a_re * dt)
    abar_re = mag * jnp.cos(a_im * dt)
    abar_im = mag * jnp.sin(a_im * dt)
    den = a_re * a_re + a_im * a_im
    nr = abar_re - 1.0
    coef_re = (nr * a_re + abar_im * a_im) / den
    coef_im = (abar_im * a_re - nr * a_im) / den
    br = bre_ref[...]
    bi = bim_ref[...]
    abre_ref[...] = abar_re
    abim_ref[...] = abar_im
    bbre_ref[...] = coef_re * br - coef_im * bi
    bbim_ref[...] = coef_re * bi + coef_im * br


def ssm_discretise(a_re, a_im, log_dt, b_re, b_im):
    depth = a_re.shape[0]
    a_re = a_re.reshape(depth, 1, N_STATE)
    a_im = a_im.reshape(depth, 1, N_STATE)
    ldt = jnp.repeat(log_dt, SSM_STATE, axis=1).reshape(depth, 1, N_STATE)
    bt_re = b_re.reshape(depth, N_STATE, SSM_GROUP_CH).transpose(0, 2, 1)
    bt_im = b_im.reshape(depth, N_STATE, SSM_GROUP_CH).transpose(0, 2, 1)
    row = pl.BlockSpec((None, 1, N_STATE), lambda l: (l, 0, 0))
    mat = pl.BlockSpec((None, SSM_GROUP_CH, N_STATE), lambda l: (l, 0, 0))
    return pl.pallas_call(
        _disc_kernel,
        grid=(depth,),
        in_specs=[row, row, row, mat, mat],
        out_specs=[row, row, mat, mat],
        out_shape=[jax.ShapeDtypeStruct((depth, 1, N_STATE), F32)] * 2
        + [jax.ShapeDtypeStruct((depth, SSM_GROUP_CH, N_STATE), F32)] * 2,
        compiler_params=_params(("arbitrary",)),
        name="ssm_discretise",
    )(a_re, a_im, ldt, bt_re, bt_im)


def _rope(x, cos, sm):
    lane = lax.broadcasted_iota(jnp.int32, x.shape, 1) % HEAD_DIM
    partner = jnp.where(lane < ROT_DIM // 2, pltpu.roll(x, LANES - ROT_DIM // 2, 1), pltpu.roll(x, ROT_DIM // 2, 1))
    return x * cos + partner * sm


def _to_time_major(x_ref, scr, n_batch):
    steps = x_ref.shape[1]
    for b in range(n_batch):
        for c in range(D_MODEL // LANES):
            scr[c, pl.ds(b, steps, stride=n_batch), :] = x_ref[b, :, c * LANES:(c + 1) * LANES]
    return jnp.concatenate([scr[c] for c in range(D_MODEL // LANES)], axis=1)


def _from_time_major(x, o_ref, scr, n_batch):
    steps = o_ref.shape[1]
    for c in range(D_MODEL // LANES):
        scr[c] = x[:, c * LANES:(c + 1) * LANES]
    for b in range(n_batch):
        for c in range(D_MODEL // LANES):
            o_ref[b, :, c * LANES:(c + 1) * LANES] = scr[c, pl.ds(b, steps, stride=n_batch), :]


def _rms(x, g_ref):
    ms = jnp.mean(x * x, axis=-1, keepdims=True)
    return x * lax.rsqrt(ms + EPS) * g_ref[...]


def _per_sequence(y, n_batch, fn):
    rows = y.shape[0]
    return fn(y.reshape(rows // n_batch, n_batch, D_MODEL)).reshape(rows, D_MODEL)


def _pair_heads(chunks):
    low = lax.broadcasted_iota(jnp.int32, chunks[0].shape, 1) < HEAD_DIM
    swap = lambda x: pltpu.roll(x, HEAD_DIM, 1)
    n0, n1, n2, n3 = chunks
    return [jnp.where(low, n0, swap(n2)), jnp.where(low, swap(n0), n2),
            jnp.where(low, n1, swap(n3)), jnp.where(low, swap(n1), n3)]


def _unpair_heads(chunks):
    low = lax.broadcasted_iota(jnp.int32, chunks[0].shape, 1) < HEAD_DIM
    swap = lambda x: pltpu.roll(x, HEAD_DIM, 1)
    p0, p1, p2, p3 = chunks
    return [jnp.where(low, p0, swap(p1)), jnp.where(low, p2, swap(p3)),
            jnp.where(low, swap(p0), p1), jnp.where(low, swap(p2), p3)]


def _residual_update(x, rows, ya_ref, ys_ref, gate_ref, w_ref, n_batch):
    ya = _unpair_heads([ya_ref[c, rows, :] for c in range(N_CHUNKS)])
    mixed = jnp.concatenate([y.astype(BF16) for y in ya] + [ys_ref[rows, :]], axis=1)
    mix = jnp.dot(mixed, w_ref[...], preferred_element_type=F32)
    return x + _per_sequence(mix, n_batch, lambda m: m * gate_ref[...][None])


def _modulated_norm(x, g_ref, scale_ref, shift_ref, n_batch):
    y = _per_sequence(_rms(x, g_ref), n_batch, lambda y3: y3 * (1.0 + scale_ref[...])[None] + shift_ref[...][None])
    return y.astype(BF16)


def _silu(z):
    return z * jax.nn.sigmoid(z)


def _project(h, rows, w_ref, rope_scr, q_ref, k_ref, v_ref, ga_ref, u_ref, gs_ref):
    cos = rope_scr[0, rows, :]
    sm = rope_scr[1, rows, :]

    def proj(lo, width):
        return jnp.dot(h, w_ref[:, lo:lo + width], preferred_element_type=F32)

    chunks_of = lambda a: [a[:, c * LANES:(c + 1) * LANES] for c in range(N_CHUNKS)]
    q = _pair_heads(chunks_of(proj(COL_Q, ATTN_W)))
    ga = _pair_heads(chunks_of(_silu(proj(COL_ZA, ATTN_W))))
    for c in range(N_CHUNKS):
        q_ref[c, rows, :] = _rope(q[c], cos, sm)
        ga_ref[c, rows, :] = ga[c]
    u_ref[rows, :] = proj(COL_U, SSM_W)
    gs_ref[rows, :] = _silu(proj(COL_ZS, SSM_W))
    kv = proj(COL_KV, 2 * KV_W)
    k_ref[rows, :] = _rope(kv[:, :KV_W], cos, sm)
    v_ref[rows, :] = kv[:, KV_W:]


def _boundary_kernel(*refs, n_batch, closes, opens, batch_major, block_rows, n_parts):
    refs = list(refs)
    x_ref = refs.pop(0)
    close_refs = [refs.pop(0) for _ in range(4)] if closes else None
    final_g_ref = refs.pop(0) if closes and not opens else None
    open_refs = [refs.pop(0) for _ in range(6)] if opens else None
    x_out_ref = refs.pop(0) if closes or batch_major else None
    proj_out_refs = [refs.pop(0) for _ in range(6)] if opens else None
    scr = refs.pop(0) if batch_major else None
    rope_scr = refs.pop(0) if opens else None

    if opens:
        g_ref, scale_ref, shift_ref, w_in_ref, cos_ref, sm_ref = open_refs
        steps = cos_ref.shape[0]
        for b in range(n_batch):
            rope_scr[0, pl.ds(b, steps, stride=n_batch), :] = cos_ref[...]
            rope_scr[1, pl.ds(b, steps, stride=n_batch), :] = sm_ref[...]

    part = block_rows // n_parts
    groups = [slice(i * part, (i + 1) * part) for i in range(n_parts)]
    x_full = _to_time_major(x_ref, scr, n_batch) if (batch_major and not closes) else None

    def updated(rows):
        x = x_full[rows, :] if x_full is not None else x_ref[rows, :]
        return _residual_update(x, rows, *close_refs, n_batch) if closes else x

    def normed(x, rows):
        if opens:
            if x_out_ref is not None:
                x_out_ref[rows, :] = x
            return _modulated_norm(x, g_ref, scale_ref, shift_ref, n_batch)
        return _rms(x, final_g_ref)

    xs = [updated(rows) for rows in groups]
    hs = [None] * n_parts
    hs[0] = normed(xs[0], groups[0])
    for i, rows in enumerate(groups):
        if opens:
            _project(hs[i], rows, w_in_ref, rope_scr, *proj_out_refs)
        if i + 1 < n_parts:
            hs[i + 1] = normed(xs[i + 1], groups[i + 1])
    if not opens:
        y = jnp.concatenate(hs, axis=0)
        if batch_major:
            _from_time_major(y, x_out_ref, scr, n_batch)
        else:
            x_out_ref[...] = y


def _layer_spec(arr, l):
    tail = arr.shape[1:]
    return pl.BlockSpec((None,) + tail, lambda *_: (l,) + (0,) * len(tail))


def _mod_spec(l, group, part):
    return pl.BlockSpec((None, group.n_batch, D_MODEL), lambda *_: (l, group.mod_block, part))


def layer_boundary(x, mod, group, block_rows, close=None, open_=None, final_g=None, batch_major_out=False):
    closes, opens = close is not None, open_ is not None
    batch_major_in = x.ndim == 3
    assert not (batch_major_in and closes) and not (batch_major_out and opens)
    batch_major = batch_major_in or batch_major_out
    t_rows = x.shape[0] * x.shape[1] if batch_major_in else x.shape[0]
    rows = lambda w: pl.BlockSpec((block_rows, w), lambda i: (i, 0))
    chunked = pl.BlockSpec((N_CHUNKS, block_rows, LANES), lambda i: (0, i, 0))
    blocked3 = pl.BlockSpec((group.n_batch, block_rows // group.n_batch, D_MODEL), lambda i: (0, i, 0))
    flat = lambda w: jax.ShapeDtypeStruct((t_rows, w), F32)
    chunked_shape = jax.ShapeDtypeStruct((N_CHUNKS, t_rows, LANES), F32)

    args, in_specs, out_specs, out_shape = [x], [blocked3 if batch_major_in else rows(D_MODEL)], [], []
    if closes:
        ya, ys, w_out, l = close
        args += [ya, ys, mod, w_out]
        in_specs += [chunked, rows(SSM_W), _mod_spec(l, group, 2), _layer_spec(w_out, l)]
        if not opens:
            args.append(final_g)
            in_specs.append(_full(final_g.shape))
    if opens:
        norm_g, w_in, cos_t, sm_t, l = open_
        args += [norm_g, mod, mod, w_in, cos_t, sm_t]
        per_step = pl.BlockSpec((block_rows // group.n_batch, LANES), lambda i: (i, 0))
        in_specs += [_layer_spec(norm_g, l), _mod_spec(l, group, 1), _mod_spec(l, group, 0), _layer_spec(w_in, l),
                     per_step, per_step]
    if closes or batch_major_in:
        out_specs.append(blocked3 if batch_major_out else rows(D_MODEL))
        out_shape.append(jax.ShapeDtypeStruct((group.n_batch, t_rows // group.n_batch, D_MODEL), F32)
                         if batch_major_out else flat(D_MODEL))
    if opens:
        out_specs += [chunked, rows(KV_W), rows(KV_W), chunked, rows(SSM_W), rows(SSM_W)]
        out_shape += [chunked_shape, flat(KV_W), flat(KV_W), chunked_shape, flat(SSM_W), flat(SSM_W)]
    scratch = [pltpu.VMEM((D_MODEL // LANES, block_rows, LANES), F32)] if batch_major else []
    if opens:
        scratch.append(pltpu.VMEM((2, block_rows, LANES), F32))
    outs = pl.pallas_call(
        functools.partial(_boundary_kernel, n_batch=group.n_batch, closes=closes, opens=opens,
                          batch_major=batch_major, block_rows=block_rows, n_parts=BOUNDARY_PARTS),
        grid=(t_rows // block_rows,),
        in_specs=in_specs,
        out_specs=out_specs,
        out_shape=out_shape,
        scratch_shapes=scratch,
        compiler_params=_params(("parallel",)),
        name="layer_boundary",
    )(*args)
    x_rows = outs[0] if (closes or batch_major_in) else x
    return x_rows, (tuple(outs[-6:]) if opens else None)


def _pad_queries(chunks):
    lane = lax.broadcasted_iota(jnp.int32, chunks[0].shape, 1)
    pieces = []
    for chunk in chunks:
        pieces.append(jnp.where(lane < HEAD_DIM, chunk, 0.0))
        pieces.append(jnp.where(lane >= HEAD_DIM, chunk, 0.0))
    return jnp.concatenate(pieces, axis=0).astype(BF16)


def _unpad_outputs(o, t):
    lane = lax.broadcasted_iota(jnp.int32, (t, LANES), 1)
    chunks = []
    for c in range(N_CHUNKS):
        lo = o[(2 * c) * t:(2 * c + 1) * t]
        hi = o[(2 * c + 1) * t:(2 * c + 2) * t]
        chunks.append(jnp.where(lane < HEAD_DIM, lo, hi))
    return chunks


def _gated_store(o_ref, ga_ref, sel, ya_chunks):
    for c, ya in enumerate(ya_chunks):
        o_ref[c, sel, :] = ya * ga_ref[c, sel, :]


def _scores(qp, k, scale=HEAD_DIM ** -0.5):
    return lax.dot_general(qp, k.astype(BF16), (((1,), (1,)), ((), ())), preferred_element_type=F32) * scale


def _prompt_attention_stages(q_ref, kp_ref, kc_ref, vp_ref, vc_ref, za_ref, sink_ref, o_ref, n_batch, blk):
    assert blk == WINDOW
    n = pl.program_id(0)
    qb = blk // 2
    keys = blk + qb
    cols = 2 * LANES
    j = lax.broadcasted_iota(jnp.int32, (keys, cols), 0)
    tq = lax.broadcasted_iota(jnp.int32, (keys, cols), 1) % qb
    masks = [(j >= jnp.maximum(tq, jnp.where(n == 0, blk - h * qb, 0))) & (j <= tq + WINDOW) for h in range(2)]
    low = lax.broadcasted_iota(jnp.int32, (qb, LANES), 1) < HEAD_DIM
    ones = jnp.ones((SUBLANES, keys), F32)
    units = [(h, pr) for h in range(2) for pr in range(N_CHUNKS // 2)]
    sinks = [jnp.concatenate([sink_ref[slot:slot + 1, :qb] for slot in range(4 * pr, 4 * pr + 4)], axis=1) * LOG2_E
             for pr in range(N_CHUNKS // 2)]

    def query_rows(b, h):
        return pl.ds(b + h * qb * n_batch, qb, stride=n_batch)

    def window(prev_ref, cur_ref, b, h):
        if h == 0:
            return jnp.concatenate([prev_ref[pl.ds(b, blk, stride=n_batch), :], cur_ref[query_rows(b, 0), :]], axis=0)
        return jnp.concatenate([prev_ref[query_rows(b, 1), :], cur_ref[pl.ds(b, blk, stride=n_batch), :]], axis=0)

    def score_stage(b):
        sts = []
        for h in range(2):
            k = window(kp_ref, kc_ref, b, h).astype(BF16)
            for pr in range(N_CHUNKS // 2):
                pieces = []
                for c in (2 * pr, 2 * pr + 1):
                    qc = q_ref[c, query_rows(b, h), :]
                    pieces += [jnp.where(low, qc, 0.0), jnp.where(low, 0.0, qc)]
                qp = jnp.concatenate(pieces, axis=0).astype(BF16)
                sts.append(_scores(k, qp, HEAD_DIM ** -0.5 * LOG2_E))
        return sts

    def value_stage(b, sts):
        vt_aug = [jnp.concatenate([window(vp_ref, vc_ref, b, h).T, ones], axis=0).astype(BF16) for h in range(2)]

        def finish(h, pr, ot, m):
            den = ot[KV_W:KV_W + 1, :] + jnp.exp2(sinks[pr] - m)
            o = (ot[:KV_W, :] * (1.0 / den)).T
            sel = query_rows(b, h)
            for i, c in enumerate((2 * pr, 2 * pr + 1)):
                lo = 2 * i * qb
                ya = jnp.where(low, o[lo:lo + qb], o[lo + qb:lo + 2 * qb])
                o_ref[c, sel, :] = ya * za_ref[c, sel, :]

        pending = None
        for (h, pr), st in zip(units, sts):
            st = jnp.where(masks[h], st, NEG_INF)
            m = jnp.maximum(jnp.max(st, axis=0, keepdims=True), sinks[pr])
            p = jnp.exp2(st - m).astype(BF16)
            ot = jnp.dot(vt_aug[h], p, preferred_element_type=F32)
            if pending is not None:
                finish(*pending)
            pending = (h, pr, ot, m)
        finish(*pending)

    return score_stage, value_stage


def _attn_sample_kernel(q_ref, k_ref, v_ref, za_ref, ck_ref, cv_ref, sink_ref, nk_all_ref, nv_all_ref,
                        o_ref, nk_ref, nv_ref, *, n_batch, t_new, group):
    del nk_all_ref, nv_all_ref
    g = pl.program_id(0)
    win = ck_ref.shape[2]
    rows = N_Q_HEADS * t_new
    tq = lax.broadcasted_iota(jnp.int32, (rows, 2 * win), 0) % t_new
    j = lax.broadcasted_iota(jnp.int32, (rows, 2 * win), 1)
    dist_c = tq + win - j
    mask = ((j < win) & (dist_c >= 0) & (dist_c <= WINDOW)) | ((j >= win) & (j - win <= tq) & (j - win < t_new))
    sink = sink_ref[...]
    pad = jnp.zeros((win - t_new, KV_W), F32)
    ones_rows = jnp.ones((LANES, win), F32)
    ones_cols = jnp.ones((win, LANES), F32)
    kept = lax.broadcasted_iota(jnp.int32, (KV_W, win), 1) < win - t_new
    batch = range(group)
    sels = [pl.ds(g * group + i, t_new, stride=n_batch) for i in batch]

    scores = []
    for i in batch:
        qp = _pad_queries([q_ref[c, sels[i], :] for c in range(N_CHUNKS)])
        s_cache = jnp.dot(qp, ck_ref[i].astype(BF16), preferred_element_type=F32) * (HEAD_DIM ** -0.5)
        s_new = _scores(qp, jnp.concatenate([k_ref[sels[i], :], pad], axis=0))
        scores.append(jnp.where(mask, jnp.concatenate([s_cache, s_new], axis=1), NEG_INF))
    outs = []
    for i in batch:
        s = scores[i]
        m = jnp.maximum(jnp.max(s, axis=-1, keepdims=True), sink[:, :1])
        p = jnp.exp(s - m).astype(BF16)
        vt_aug = jnp.concatenate([cv_ref[i], ones_rows], axis=0).astype(BF16)
        v_new = jnp.concatenate([jnp.concatenate([v_ref[sels[i], :], pad], axis=0), ones_cols], axis=1).astype(BF16)
        o = (lax.dot_general(p[:, :win], vt_aug, (((1,), (1,)), ((), ())), preferred_element_type=F32)
             + jnp.dot(p[:, win:], v_new, preferred_element_type=F32))
        outs.append(o[:, :KV_W] / (o[:, KV_W:] + jnp.exp(sink - m)))
    for i in batch:
        _gated_store(o_ref, za_ref, sels[i], _unpad_outputs(outs[i], t_new))
        for new_ref, old_ref, rows_ref in ((nk_ref, ck_ref, k_ref), (nv_ref, cv_ref, v_ref)):
            appended = jnp.concatenate([pad, rows_ref[sels[i], :]], axis=0).T
            new_ref[i] = jnp.where(kept, pltpu.roll(old_ref[i], win - t_new, 1), appended)


def attention_sample(q, k, v, za, cache_kt, cache_vt, sink_rows, new_kt, new_vt, l, n_batch, t_new, group):
    t_rows = k.shape[0]
    win = cache_kt.shape[3]
    window = pl.BlockSpec((None, group, KV_W, win), lambda i: (l, i, 0, 0))
    chunked = _full((N_CHUNKS, t_rows, LANES))
    in_place = pl.BlockSpec(memory_space=pl.ANY)
    return pl.pallas_call(
        functools.partial(_attn_sample_kernel, n_batch=n_batch, t_new=t_new, group=group),
        grid=(n_batch // group,),
        in_specs=[chunked, _full((t_rows, KV_W)), _full((t_rows, KV_W)), chunked,
                  window, window, _layer_spec(sink_rows, l), in_place, in_place],
        out_specs=[chunked, window, window],
        out_shape=[jax.ShapeDtypeStruct((N_CHUNKS, t_rows, LANES), F32),
                   jax.ShapeDtypeStruct(new_kt.shape, F32), jax.ShapeDtypeStruct(new_vt.shape, F32)],
        input_output_aliases={7: 1, 8: 2},
        compiler_params=_params(("arbitrary",)),
        name="attention_sample",
    )(q, k, v, za, cache_kt, cache_vt, sink_rows, new_kt, new_vt)


def _ssm_stages(u_ref, zs_ref, h0re_ref, h0im_ref, are_ref, aim_ref, wb_ref, wc_ref, d_ref, wg_ref, bg_ref,
                o_ref, hre_ref, him_ref, x_scr, n_batch, sub_rows, slab, h0_transposed=False):
    step = pl.program_id(0)

    @pl.when(step == 0)
    def _():
        hre_ref[...] = h0re_ref[...].T if h0_transposed else h0re_ref[...]
        him_ref[...] = h0im_ref[...].T if h0_transposed else h0im_ref[...]

    rows = u_ref.shape[0]
    half_in = SSM_W // 2
    n_sub = rows // sub_rows
    n_tiles = n_batch // SUBLANES
    t_steps = sub_rows // n_batch
    n_buf = x_scr.shape[0]
    items = [(sc, s) for sc in range(n_sub) for s in range(2)]

    def expand(i):
        sc, s = items[i]
        ub = u_ref[sc * sub_rows:(sc + 1) * sub_rows, s * half_in:(s + 1) * half_in].astype(BF16)
        x_scr[i % n_buf] = jnp.dot(ub, wb_ref[s], preferred_element_type=F32)

    def recur(i):
        sc, s = items[i]
        buf = i % n_buf
        for jb in range(HALF_STATE // slab):
            n_lo = s * HALF_STATE + jb * slab
            ar = jnp.broadcast_to(are_ref[:, n_lo:n_lo + slab], (SUBLANES, slab))
            ai = jnp.broadcast_to(aim_ref[:, n_lo:n_lo + slab], (SUBLANES, slab))
            re = slice(jb * slab, (jb + 1) * slab)
            im = slice(HALF_STATE + jb * slab, HALF_STATE + (jb + 1) * slab)
            for tile in range(n_tiles):
                r0 = tile * SUBLANES
                hr = hre_ref[r0:r0 + SUBLANES, n_lo:n_lo + slab]
                hi = him_ref[r0:r0 + SUBLANES, n_lo:n_lo + slab]
                for t in range(t_steps):
                    row = slice(t * n_batch + r0, t * n_batch + r0 + SUBLANES)
                    hr, hi = (ar * hr - ai * hi + x_scr[buf, row, re], ar * hi + ai * hr + x_scr[buf, row, im])
                    x_scr[buf, row, re] = hr
                    x_scr[buf, row, im] = hi
                hre_ref[r0:r0 + SUBLANES, n_lo:n_lo + slab] = hr
                him_ref[r0:r0 + SUBLANES, n_lo:n_lo + slab] = hi

    y_halves = {}

    def contract(i):
        sc, s = items[i]
        y_halves[s] = jnp.dot(x_scr[i % n_buf].astype(BF16), wc_ref[s], preferred_element_type=F32)
        if s == 1:
            r = slice(sc * sub_rows, (sc + 1) * sub_rows)
            y = jnp.concatenate([y_halves[0], y_halves[1]], axis=1) + d_ref[...] * u_ref[r, :]
            y = jax.nn.gelu(y)
            gate = jax.nn.sigmoid(jnp.dot(y.astype(BF16), wg_ref[...], preferred_element_type=F32) + bg_ref[...])
            o_ref[r, :] = (y * gate * zs_ref[r, :]).astype(o_ref.dtype)

    return len(items), expand, recur, contract


def _ssm_kernel(*refs, n_batch, sub_rows, slab):
    inputs, (o_ref, hre_t_ref, him_t_ref, x_scr, hre_ref, him_ref) = refs[:N_SSM_INPUTS], refs[N_SSM_INPUTS:]
    n_items, expand, recur, contract = _ssm_stages(*inputs, o_ref, hre_ref, him_ref, x_scr, n_batch, sub_rows, slab,
                                                   h0_transposed=True)
    expand(0)
    for i in range(n_items + 1):
        if i + 1 < n_items:
            expand(i + 1)
        if i < n_items:
            recur(i)
        if i >= 1:
            contract(i - 1)

    @pl.when(pl.program_id(0) == pl.num_programs(0) - 1)
    def _():
        hre_t_ref[...] = hre_ref[...].T
        him_t_ref[...] = him_ref[...].T


N_ATTN_INPUTS = 7
N_SSM_INPUTS = 11


def _prompt_branches_kernel(*refs, n_batch, blk, sub_rows, slab):
    n_in = N_ATTN_INPUTS + N_SSM_INPUTS
    attn_refs = refs[:N_ATTN_INPUTS] + refs[n_in:n_in + 1]
    ssm_refs = refs[N_ATTN_INPUTS:n_in] + refs[n_in + 1:]
    score_stage, value_stage = _prompt_attention_stages(*attn_refs, n_batch, blk)
    n_items, expand, recur, contract = _ssm_stages(*ssm_refs, n_batch, sub_rows, slab)
    per_item = n_batch // n_items
    assert per_item * n_items == n_batch
    seqs_of = lambda i: range(i * per_item, (i + 1) * per_item)
    expand(0)
    scores = [score_stage(b) for b in seqs_of(0)]
    for i in range(n_items + 1):
        if i < n_items:
            recur(i)
        if i + 1 < n_items:
            expand(i + 1)
        if i < n_items:
            for b, sts in zip(seqs_of(i), scores):
                value_stage(b, sts)
        if i + 1 < n_items:
            scores = [score_stage(b) for b in seqs_of(i + 1)]
        if i >= 1:
            contract(i - 1)


def _ssm_call_parts(u, h0_re, h0_im, abar_re, abar_im, w_b, w_c, d, w_glu, b_glu, l, l_state, n_batch, r):
    t_rows = u.shape[0]
    sub_rows = max(SSM_SUB_ROWS, n_batch)
    rows = pl.BlockSpec((r, SSM_W), lambda i: (i, 0))
    state = _full((n_batch, N_STATE))
    layer = lambda a: _layer_spec(a, l)
    in_specs = [rows, rows, _layer_spec(h0_re, l_state), _layer_spec(h0_im, l_state), layer(abar_re), layer(abar_im),
                layer(w_b), layer(w_c), layer(d), layer(w_glu), layer(b_glu)]
    out_shape = [jax.ShapeDtypeStruct((t_rows, SSM_W), BF16),
                 jax.ShapeDtypeStruct((n_batch, N_STATE), F32), jax.ShapeDtypeStruct((n_batch, N_STATE), F32)]
    scratch = [pltpu.VMEM((SSM_BUFFERS, sub_rows, 2 * HALF_STATE), F32)]
    return in_specs, [rows, state, state], out_shape, scratch, sub_rows


def ssm_branch(u, zs, h0_re, h0_im, abar_re, abar_im, w_b, w_c, d, w_glu, b_glu, l, l_state, n_batch, t_chunk):
    r = t_chunk * n_batch
    in_specs, out_specs, out_shape, scratch, sub_rows = _ssm_call_parts(
        u, h0_re, h0_im, abar_re, abar_im, w_b, w_c, d, w_glu, b_glu, l, l_state, n_batch, r)
    out_specs = out_specs[:1] + [_full((N_STATE, n_batch))] * 2
    out_shape = out_shape[:1] + [jax.ShapeDtypeStruct((N_STATE, n_batch), F32)] * 2
    scratch = scratch + [pltpu.VMEM((n_batch, N_STATE), F32)] * 2
    return pl.pallas_call(
        functools.partial(_ssm_kernel, n_batch=n_batch, sub_rows=sub_rows, slab=SSM_SLAB),
        grid=(u.shape[0] // r,),
        in_specs=in_specs,
        out_specs=out_specs,
        out_shape=out_shape,
        scratch_shapes=scratch,
        compiler_params=_params(("arbitrary",)),
        name="ssm_branch",
    )(u, zs, h0_re, h0_im, abar_re, abar_im, w_b, w_c, d, w_glu, b_glu)


def prompt_branches(q, k, v, za, sinks, u, zs, h0_re, h0_im, abar_re, abar_im, w_b, w_c, d, w_glu, b_glu,
                    l, l_state, n_batch, blk):
    t_rows = k.shape[0]
    r = blk * n_batch
    cur = lambda w: pl.BlockSpec((r, w), lambda i: (i, 0))
    prev = lambda w: pl.BlockSpec((r, w), lambda i: (jnp.maximum(i - 1, 0), 0))
    chunked = pl.BlockSpec((N_CHUNKS, r, LANES), lambda i: (0, i, 0))
    ssm_in, ssm_out, ssm_shape, scratch, sub_rows = _ssm_call_parts(
        u, h0_re, h0_im, abar_re, abar_im, w_b, w_c, d, w_glu, b_glu, l, l_state, n_batch, r)
    attn_in = [chunked, prev(KV_W), cur(KV_W), prev(KV_W), cur(KV_W), chunked, _layer_spec(sinks, l)]
    assert len(attn_in) == N_ATTN_INPUTS and len(ssm_in) == N_SSM_INPUTS
    return pl.pallas_call(
        functools.partial(_prompt_branches_kernel, n_batch=n_batch, blk=blk, sub_rows=sub_rows, slab=SSM_SLAB),
        grid=(t_rows // r,),
        in_specs=attn_in + ssm_in,
        out_specs=[chunked] + ssm_out,
        out_shape=[jax.ShapeDtypeStruct((N_CHUNKS, t_rows, LANES), F32)] + ssm_shape,
        scratch_shapes=scratch,
        compiler_params=_params(("arbitrary",)),
        name="prompt_branches",
    )(q, k, k, v, v, za, sinks, u, zs, h0_re, h0_im, abar_re, abar_im, w_b, w_c, d, w_glu, b_glu)


def _rope_tables(pos):
    half = ROT_DIM // 2
    inv = ROPE_THETA ** (-jnp.arange(half, dtype=F32) / half)
    ang = pos.astype(F32)[:, None] * inv[None, :]
    d = np.arange(LANES) % HEAD_DIM
    idx = d % half
    cos = jnp.where(d < ROT_DIM, jnp.cos(ang)[:, idx], 1.0)
    sin = jnp.sin(ang)[:, idx]
    sm = jnp.where(d < half, -sin, jnp.where(d < ROT_DIM, sin, 0.0))
    return cos, sm


def _keep_diagonal_blocks(w, row_block, col_block):
    rows, cols = w.shape[-2:]
    keep = (np.arange(rows)[:, None] // row_block) == (np.arange(cols)[None, :] // col_block)
    return jnp.where(keep, w, 0.0)


def _expand_weights(bb):
    depth = bb.shape[0]
    hg = N_SSM_GROUPS // 2
    halves = bb.reshape(depth, SSM_GROUP_CH, 2, HALF_STATE).transpose(0, 2, 1, 3)
    tiled = jnp.broadcast_to(halves[:, :, None], (depth, 2, hg, SSM_GROUP_CH, HALF_STATE))
    return _keep_diagonal_blocks(tiled.reshape(depth, 2, hg * SSM_GROUP_CH, HALF_STATE), SSM_GROUP_CH, SSM_STATE)


def _contract_weights(c):
    depth = c.shape[0]
    hg = N_SSM_GROUPS // 2
    rows = c.reshape(depth, 2, hg, SSM_GROUP_CH, SSM_STATE).transpose(0, 1, 2, 4, 3).reshape(
        depth, 2, HALF_STATE, SSM_GROUP_CH)
    return _keep_diagonal_blocks(jnp.tile(rows, (1, 1, 1, hg)), SSM_STATE, SSM_GROUP_CH)


def _prepare_weights(w_in, w_out, attn_sinks, bb_re, bb_im, ssm_c_re, ssm_c_im, t_new):
    depth = w_in.shape[0]
    assert HEAD_PERM == tuple(kv * Q_GROUP + g for g in range(Q_GROUP) for kv in range(N_KV_HEADS))
    w_in_p = w_in.astype(BF16)
    w_out_p = w_out.astype(BF16)
    sinks = jnp.swapaxes(attn_sinks.reshape(depth, N_KV_HEADS, Q_GROUP), 1, 2).reshape(depth, N_Q_HEADS)
    sink_lanes = jnp.broadcast_to(sinks[:, :, None], (depth, N_Q_HEADS, LANES))
    sink_rows = jnp.broadcast_to(jnp.repeat(sinks, t_new, axis=1)[:, :, None], (depth, N_Q_HEADS * t_new, LANES))
    w_b = jnp.concatenate([_expand_weights(bb_re), _expand_weights(bb_im)], axis=3).astype(BF16)
    w_c = jnp.concatenate([_contract_weights(ssm_c_re), -_contract_weights(ssm_c_im)], axis=2).astype(BF16)
    return w_in_p, w_out_p, sink_lanes, sink_rows, w_b, w_c


def kernel(x_prompt, x_sample, cache_k, cache_v, state_ssm_re, state_ssm_im, c_prompt, c_sample, norm_g, w_ada, b_ada, w_in, attn_sinks, ssm_a_re, ssm_a_im, ssm_log_dt, ssm_b_re, ssm_b_im, ssm_c_re, ssm_c_im, ssm_d, w_glu, b_glu, w_out, final_g):
    depth = w_in.shape[0]
    nb_p, seq, _ = x_prompt.shape
    nb_s, t_new, _ = x_sample.shape
    win = cache_k.shape[2]
    blk = WINDOW

    sample = Group(n_batch=nb_s, mod_block=0)
    prompt = Group(n_batch=nb_p, mod_block=nb_s // nb_p)
    mod = adaln_mod(jnp.concatenate([c_sample, c_prompt], axis=0), w_ada, b_ada)
    abar_re, abar_im, bb_re, bb_im = ssm_discretise(ssm_a_re, ssm_a_im, ssm_log_dt, ssm_b_re, ssm_b_im)
    w_in_p, w_out_p, sink_lanes, sink_rows, w_b, w_c = _prepare_weights(
        w_in, w_out, attn_sinks, bb_re, bb_im, ssm_c_re, ssm_c_im, t_new)
    norm_g3 = norm_g.reshape(depth, 1, D_MODEL)
    d3 = ssm_d.reshape(depth, 1, SSM_W)
    w_glu_b = w_glu.astype(BF16)
    b_glu3 = b_glu.reshape(depth, 1, SSM_W)
    final_g2 = final_g.reshape(1, D_MODEL)
    ssm_weights = (abar_re, abar_im, w_b, w_c, d3, w_glu_b, b_glu3)

    cos_p, sm_p = _rope_tables(jnp.arange(seq))
    cos_s, sm_s = _rope_tables(PAST_LEN + jnp.arange(t_new))

    xp = x_prompt
    xs = x_sample.transpose(1, 0, 2).reshape(t_new * nb_s, D_MODEL)
    zeros_state = jnp.zeros((1, nb_p, N_STATE), F32)
    transposed = lambda c: c.transpose(0, 1, 3, 4, 2).reshape(depth, nb_s, KV_W, win)
    cache_kt, cache_vt = transposed(cache_k), transposed(cache_v)
    new_kt, new_vt = jnp.zeros_like(cache_kt), jnp.zeros_like(cache_vt)
    state_re3 = state_ssm_re.transpose(0, 2, 3, 1).reshape(depth, N_STATE, nb_s)
    state_im3 = state_ssm_im.transpose(0, 2, 3, 1).reshape(depth, N_STATE, nb_s)

    outs = {k: [] for k in ("kp", "vp", "rp", "ip", "rs", "is")}
    rows_p, rows_s = BOUNDARY_ROWS, t_new * nb_s
    opening = lambda l, cos_t, sm_t: (norm_g3, w_in_p, cos_t, sm_t, l)
    xp, proj_p = layer_boundary(xp, mod, prompt, rows_p, open_=opening(0, cos_p, sm_p))
    xs, proj_s = layer_boundary(xs, mod, sample, rows_s, open_=opening(0, cos_s, sm_s))
    for l in range(depth):
        last = l == depth - 1

        q, k, v, za, u, zs = proj_p
        ya, ys, hre, him = prompt_branches(q, k, v, za, sink_lanes, u, zs, zeros_state, zeros_state, *ssm_weights,
                                           l, 0, nb_p, blk)
        xp, proj_p = layer_boundary(xp, mod, prompt, rows_p, close=(ya, ys, w_out_p, l),
                                    open_=None if last else opening(l + 1, cos_p, sm_p),
                                    final_g=final_g2, batch_major_out=last)
        outs["kp"].append(k[(seq - win) * nb_p:])
        outs["vp"].append(v[(seq - win) * nb_p:])
        outs["rp"].append(hre)
        outs["ip"].append(him)

        q, k, v, za, u, zs = proj_s
        ya, new_kt, new_vt = attention_sample(q, k, v, za, cache_kt, cache_vt, sink_rows, new_kt, new_vt, l,
                                              nb_s, t_new, SAMPLE_GROUP)
        ys, hre, him = ssm_branch(u, zs, state_re3, state_im3, *ssm_weights, l, l, nb_s, t_new)
        xs, proj_s = layer_boundary(xs, mod, sample, rows_s, close=(ya, ys, w_out_p, l),
                                    open_=None if last else opening(l + 1, cos_s, sm_s), final_g=final_g2)
        outs["rs"].append(hre)
        outs["is"].append(him)

    y_prompt = xp
    y_sample = xs.reshape(t_new, nb_s, D_MODEL).transpose(1, 0, 2)
    st = lambda key: jnp.stack(outs[key])
    window = lambda key: st(key).reshape(depth, win, nb_p, N_KV_HEADS, HEAD_DIM).transpose(0, 2, 1, 3, 4)
    state = lambda key, nb: st(key).reshape(depth, nb, N_SSM_GROUPS, SSM_STATE)
    state_t = lambda key: st(key).reshape(depth, N_SSM_GROUPS, SSM_STATE, nb_s).transpose(0, 3, 1, 2)
    untransposed = lambda w: w.reshape(depth, nb_s, N_KV_HEADS, HEAD_DIM, win).transpose(0, 1, 4, 2, 3)
    return (y_prompt, y_sample, window("kp"), window("vp"), state("rp", nb_p), state("ip", nb_p),
            untransposed(new_kt), untransposed(new_vt), state_t("rs"), state_t("is"))
```

```python
import functools
from typing import NamedTuple

import jax
import jax.numpy as jnp
import numpy as np
from jax import lax
from jax.experimental import pallas as pl
from jax.experimental.pallas import tpu as pltpu

F32 = jnp.float32
BF16 = jnp.bfloat16

D_MODEL = 1024
HEAD_DIM = 64
ATTN_W = 512
N_Q_HEADS = 8
N_KV_HEADS = 2
Q_GROUP = N_Q_HEADS // N_KV_HEADS
KV_W = 128
WINDOW = 128
ROT_DIM = 16
ROPE_THETA = 500000.0
SSM_W = 512
SSM_GROUP_CH = 16
N_SSM_GROUPS = 32
SSM_STATE = 64
N_STATE = N_SSM_GROUPS * SSM_STATE
COL_Q, COL_KV, COL_ZA = 0, ATTN_W, ATTN_W + 2 * KV_W
COL_U, COL_ZS = COL_ZA + ATTN_W, COL_ZA + ATTN_W + SSM_W
HALF_STATE = N_STATE // 2
EPS = 1e-6
NEG_INF = -1e30
LOG2_E = 1.4426950408889634
PAST_LEN = 8192

MOD_PARTS = 3
LANES = 128
N_CHUNKS = ATTN_W // LANES
BOUNDARY_ROWS = 1024
BOUNDARY_PARTS = 2
SSM_SLAB = 4 * LANES
SAMPLE_GROUP = 16
SSM_SUB_ROWS = 256
SSM_BUFFERS = 3
SUBLANES = 8
VMEM_LIMIT = 56 * 1024 * 1024

HEAD_PERM = (0, 4, 1, 5, 2, 6, 3, 7)


class Group(NamedTuple):
    n_batch: int
    mod_block: int


def _params(sem):
    return pltpu.CompilerParams(dimension_semantics=sem, vmem_limit_bytes=VMEM_LIMIT)


def _full(shape):
    return pl.BlockSpec(shape, lambda *_: (0,) * len(shape))


def _mod_kernel(c_ref, w_ref, b_ref, o_ref):
    c = c_ref[...]
    a = (c * jax.nn.sigmoid(c)).astype(BF16)
    o_ref[...] = jnp.dot(a, w_ref[...].astype(BF16), preferred_element_type=F32) + b_ref[...]


def adaln_mod(c_all, w_ada, b_ada):
    depth = w_ada.shape[0]
    n = c_all.shape[0]
    return pl.pallas_call(
        _mod_kernel,
        grid=(depth, MOD_PARTS),
        in_specs=[
            pl.BlockSpec((n, D_MODEL), lambda l, j: (0, 0)),
            pl.BlockSpec((None, D_MODEL, D_MODEL), lambda l, j: (l, 0, j)),
            pl.BlockSpec((None, 1, D_MODEL), lambda l, j: (l, 0, j)),
        ],
        out_specs=pl.BlockSpec((None, n, D_MODEL), lambda l, j: (l, 0, j)),
        out_shape=jax.ShapeDtypeStruct((depth, n, 3 * D_MODEL), F32),
        compiler_params=_params(("arbitrary", "arbitrary")),
        name="adaln_mod",
    )(c_all, w_ada, b_ada.reshape(depth, 1, 3 * D_MODEL))


def _disc_kernel(are_ref, aim_ref, ldt_ref, bre_ref, bim_ref, abre_ref, abim_ref, bbre_ref, bbim_ref):
    a_re = are_ref[...]
    a_im = aim_ref[...]
    dt = jnp.exp(ldt_ref[...])
    mag = jnp.exp(a_re * dt)
    abar_re = mag * jnp.cos(a_im * dt)
    abar_im = mag * jnp.sin(a_im * dt)
    den = a_re * a_re + a_im * a_im
    nr = abar_re - 1.0
    coef_re = (nr * a_re + abar_im * a_im) / den
    coef_im = (abar_im * a_re - nr * a_im) / den
    br = bre_ref[...]
    bi = bim_ref[...]
    abre_ref[...] = abar_re
    abim_ref[...] = abar_im
    bbre_ref[...] = coef_re * br - coef_im * bi
    bbim_ref[...] = coef_re * bi + coef_im * br


def ssm_discretise(a_re, a_im, log_dt, b_re, b_im):
    depth = a_re.shape[0]
    a_re = a_re.reshape(depth, 1, N_STATE)
    a_im = a_im.reshape(depth, 1, N_STATE)
    ldt = jnp.repeat(log_dt, SSM_STATE, axis=1).reshape(depth, 1, N_STATE)
    bt_re = b_re.reshape(depth, N_STATE, SSM_GROUP_CH).transpose(0, 2, 1)
    bt_im = b_im.reshape(depth, N_STATE, SSM_GROUP_CH).transpose(0, 2, 1)
    row = pl.BlockSpec((None, 1, N_STATE), lambda l: (l, 0, 0))
    mat = pl.BlockSpec((None, SSM_GROUP_CH, N_STATE), lambda l: (l, 0, 0))
    return pl.pallas_call(
        _disc_kernel,
        grid=(depth,),
        in_specs=[row, row, row, mat, mat],
        out_specs=[row, row, mat, mat],
        out_shape=[jax.ShapeDtypeStruct((depth, 1, N_STATE), F32)] * 2
        + [jax.ShapeDtypeStruct((depth, SSM_GROUP_CH, N_STATE), F32)] * 2,
        compiler_params=_params(("arbitrary",)),
        name="ssm_discretise",
    )(a_re, a_im, ldt, bt_re, bt_im)


def _rope(x, cos, sm):
    lane = lax.broadcasted_iota(jnp.int32, x.shape, 1) % HEAD_DIM
    partner = jnp.where(lane < ROT_DIM // 2, pltpu.roll(x, LANES - ROT_DIM // 2, 1), pltpu.roll(x, ROT_DIM // 2, 1))
    return x * cos + partner * sm


def _to_time_major(x_ref):
    n_batch, steps, width = x_ref.shape
    return jnp.swapaxes(x_ref[...], 0, 1).reshape(steps * n_batch, width)


def _from_time_major(x, o_ref, n_batch):
    rows, width = x.shape
    o_ref[...] = jnp.swapaxes(x.reshape(rows // n_batch, n_batch, width), 0, 1)


def _rms(x, g_ref):
    ms = jnp.mean(x * x, axis=-1, keepdims=True)
    return x * lax.rsqrt(ms + EPS) * g_ref[...]


def _per_sequence(y, n_batch, fn):
    rows = y.shape[0]
    return fn(y.reshape(rows // n_batch, n_batch, D_MODEL)).reshape(rows, D_MODEL)


def _pair_heads(chunks):
    low = lax.broadcasted_iota(jnp.int32, chunks[0].shape, 1) < HEAD_DIM
    swap = lambda x: pltpu.roll(x, HEAD_DIM, 1)
    n0, n1, n2, n3 = chunks
    return [jnp.where(low, n0, swap(n2)), jnp.where(low, swap(n0), n2),
            jnp.where(low, n1, swap(n3)), jnp.where(low, swap(n1), n3)]


def _unpair_heads(chunks):
    low = lax.broadcasted_iota(jnp.int32, chunks[0].shape, 1) < HEAD_DIM
    swap = lambda x: pltpu.roll(x, HEAD_DIM, 1)
    p0, p1, p2, p3 = chunks
    return [jnp.where(low, p0, swap(p1)), jnp.where(low, p2, swap(p3)),
            jnp.where(low, swap(p0), p1), jnp.where(low, swap(p2), p3)]


def _residual_update(x, rows, ya_ref, ys_ref, gate_ref, w_ref, n_batch):
    ya = _unpair_heads([ya_ref[c, rows, :] for c in range(N_CHUNKS)])
    mixed = jnp.concatenate([y.astype(BF16) for y in ya] + [ys_ref[rows, :]], axis=1)
    mix = jnp.dot(mixed, w_ref[...], preferred_element_type=F32)
    return x + _per_sequence(mix, n_batch, lambda m: m * gate_ref[...][None])


def _modulated_norm(x, g_ref, scale_ref, shift_ref, n_batch):
    y = _per_sequence(_rms(x, g_ref), n_batch, lambda y3: y3 * (1.0 + scale_ref[...])[None] + shift_ref[...][None])
    return y.astype(BF16)


def _silu(z):
    return z * jax.nn.sigmoid(z)


def _project(h, rows, w_ref, rope_scr, q_ref, k_ref, v_ref, ga_ref, u_ref, gs_ref):
    cos = rope_scr[0, rows, :]
    sm = rope_scr[1, rows, :]

    def proj(lo, width):
        return jnp.dot(h, w_ref[:, lo:lo + width], preferred_element_type=F32)

    chunks_of = lambda a: [a[:, c * LANES:(c + 1) * LANES] for c in range(N_CHUNKS)]
    q = _pair_heads(chunks_of(proj(COL_Q, ATTN_W)))
    ga = _pair_heads(chunks_of(_silu(proj(COL_ZA, ATTN_W))))
    for c in range(N_CHUNKS):
        q_ref[c, rows, :] = _rope(q[c], cos, sm)
        ga_ref[c, rows, :] = ga[c]
    u_ref[rows, :] = proj(COL_U, SSM_W)
    gs_ref[rows, :] = _silu(proj(COL_ZS, SSM_W))
    kv = proj(COL_KV, 2 * KV_W)
    k_ref[rows, :] = _rope(kv[:, :KV_W], cos, sm)
    v_ref[rows, :] = kv[:, KV_W:]


def _boundary_kernel(*refs, n_batch, closes, opens, batch_major, block_rows, n_parts):
    refs = list(refs)
    x_ref = refs.pop(0)
    close_refs = [refs.pop(0) for _ in range(4)] if closes else None
    final_g_ref = refs.pop(0) if closes and not opens else None
    open_refs = [refs.pop(0) for _ in range(6)] if opens else None
    x_out_ref = refs.pop(0) if closes or batch_major else None
    proj_out_refs = [refs.pop(0) for _ in range(6)] if opens else None
    rope_scr = refs.pop(0) if opens else None

    if opens:
        g_ref, scale_ref, shift_ref, w_in_ref, cos_ref, sm_ref = open_refs
        steps = cos_ref.shape[0]
        for b in range(n_batch):
            rope_scr[0, pl.ds(b, steps, stride=n_batch), :] = cos_ref[...]
            rope_scr[1, pl.ds(b, steps, stride=n_batch), :] = sm_ref[...]

    part = block_rows // n_parts
    groups = [slice(i * part, (i + 1) * part) for i in range(n_parts)]
    x_full = _to_time_major(x_ref) if (batch_major and not closes) else None

    def updated(rows):
        x = x_full[rows, :] if x_full is not None else x_ref[rows, :]
        return _residual_update(x, rows, *close_refs, n_batch) if closes else x

    def normed(x, rows):
        if opens:
            if x_out_ref is not None:
                x_out_ref[rows, :] = x
            return _modulated_norm(x, g_ref, scale_ref, shift_ref, n_batch)
        return _rms(x, final_g_ref)

    xs = [updated(rows) for rows in groups]
    hs = [None] * n_parts
    hs[0] = normed(xs[0], groups[0])
    for i, rows in enumerate(groups):
        if opens:
            _project(hs[i], rows, w_in_ref, rope_scr, *proj_out_refs)
        if i + 1 < n_parts:
            hs[i + 1] = normed(xs[i + 1], groups[i + 1])
    if not opens:
        y = jnp.concatenate(hs, axis=0)
        if batch_major:
            _from_time_major(y, x_out_ref, n_batch)
        else:
            x_out_ref[...] = y


def _layer_spec(arr, l):
    tail = arr.shape[1:]
    return pl.BlockSpec((None,) + tail, lambda *_: (l,) + (0,) * len(tail))


def _mod_spec(l, group, part):
    return pl.BlockSpec((None, group.n_batch, D_MODEL), lambda *_: (l, group.mod_block, part))


def layer_boundary(x, mod, group, block_rows, close=None, open_=None, final_g=None, batch_major_out=False):
    closes, opens = close is not None, open_ is not None
    batch_major_in = x.ndim == 3
    assert not (batch_major_in and closes) and not (batch_major_out and opens)
    batch_major = batch_major_in or batch_major_out
    t_rows = x.shape[0] * x.shape[1] if batch_major_in else x.shape[0]
    rows = lambda w: pl.BlockSpec((block_rows, w), lambda i: (i, 0))
    chunked = pl.BlockSpec((N_CHUNKS, block_rows, LANES), lambda i: (0, i, 0))
    blocked3 = pl.BlockSpec((group.n_batch, block_rows // group.n_batch, D_MODEL), lambda i: (0, i, 0))
    flat = lambda w: jax.ShapeDtypeStruct((t_rows, w), F32)
    chunked_shape = jax.ShapeDtypeStruct((N_CHUNKS, t_rows, LANES), F32)

    args, in_specs, out_specs, out_shape = [x], [blocked3 if batch_major_in else rows(D_MODEL)], [], []
    if closes:
        ya, ys, w_out, l = close
        args += [ya, ys, mod, w_out]
        in_specs += [chunked, rows(SSM_W), _mod_spec(l, group, 2), _layer_spec(w_out, l)]
        if not opens:
            args.append(final_g)
            in_specs.append(_full(final_g.shape))
    if opens:
        norm_g, w_in, cos_t, sm_t, l = open_
        args += [norm_g, mod, mod, w_in, cos_t, sm_t]
        per_step = pl.BlockSpec((block_rows // group.n_batch, LANES), lambda i: (i, 0))
        in_specs += [_layer_spec(norm_g, l), _mod_spec(l, group, 1), _mod_spec(l, group, 0), _layer_spec(w_in, l),
                     per_step, per_step]
    if closes or batch_major_in:
        out_specs.append(blocked3 if batch_major_out else rows(D_MODEL))
        out_shape.append(jax.ShapeDtypeStruct((group.n_batch, t_rows // group.n_batch, D_MODEL), F32)
                         if batch_major_out else flat(D_MODEL))
    if opens:
        out_specs += [chunked, rows(KV_W), rows(KV_W), chunked, rows(SSM_W), rows(SSM_W)]
        out_shape += [chunked_shape, flat(KV_W), flat(KV_W), chunked_shape, flat(SSM_W), flat(SSM_W)]
    scratch = [pltpu.VMEM((2, block_rows, LANES), F32)] if opens else []
    outs = pl.pallas_call(
        functools.partial(_boundary_kernel, n_batch=group.n_batch, closes=closes, opens=opens,
                          batch_major=batch_major, block_rows=block_rows, n_parts=BOUNDARY_PARTS),
        grid=(t_rows // block_rows,),
        in_specs=in_specs,
        out_specs=out_specs,
        out_shape=out_shape,
        scratch_shapes=scratch,
        compiler_params=_params(("parallel",)),
        name="layer_boundary",
    )(*args)
    x_rows = outs[0] if (closes or batch_major_in) else x
    return x_rows, (tuple(outs[-6:]) if opens else None)


def _pad_queries(chunks):
    lane = lax.broadcasted_iota(jnp.int32, chunks[0].shape, 1)
    pieces = []
    for chunk in chunks:
        pieces.append(jnp.where(lane < HEAD_DIM, chunk, 0.0))
        pieces.append(jnp.where(lane >= HEAD_DIM, chunk, 0.0))
    return jnp.concatenate(pieces, axis=0).astype(BF16)


def _unpad_outputs(o, t):
    lane = lax.broadcasted_iota(jnp.int32, (t, LANES), 1)
    chunks = []
    for c in range(N_CHUNKS):
        lo = o[(2 * c) * t:(2 * c + 1) * t]
        hi = o[(2 * c + 1) * t:(2 * c + 2) * t]
        chunks.append(jnp.where(lane < HEAD_DIM, lo, hi))
    return chunks


def _gated_store(o_ref, ga_ref, sel, ya_chunks):
    for c, ya in enumerate(ya_chunks):
        o_ref[c, sel, :] = ya * ga_ref[c, sel, :]


def _scores(qp, k, scale=HEAD_DIM ** -0.5):
    return lax.dot_general(qp, k.astype(BF16), (((1,), (1,)), ((), ())), preferred_element_type=F32) * scale


def _prompt_attention_stages(q_ref, kp_ref, kc_ref, vp_ref, vc_ref, za_ref, sink_ref, o_ref, n_batch, blk):
    assert blk == WINDOW
    n = pl.program_id(0)
    qb = blk // 2
    keys = blk + qb
    cols = 2 * LANES
    j = lax.broadcasted_iota(jnp.int32, (keys, cols), 0)
    tq = lax.broadcasted_iota(jnp.int32, (keys, cols), 1) % qb
    masks = [(j >= jnp.maximum(tq, jnp.where(n == 0, blk - h * qb, 0))) & (j <= tq + WINDOW) for h in range(2)]
    low = lax.broadcasted_iota(jnp.int32, (qb, LANES), 1) < HEAD_DIM
    ones = jnp.ones((SUBLANES, keys), F32)
    units = [(h, pr) for h in range(2) for pr in range(N_CHUNKS // 2)]
    sinks = [jnp.concatenate([sink_ref[slot:slot + 1, :qb] for slot in range(4 * pr, 4 * pr + 4)], axis=1) * LOG2_E
             for pr in range(N_CHUNKS // 2)]

    def query_rows(b, h):
        return pl.ds(b + h * qb * n_batch, qb, stride=n_batch)

    def window(prev_ref, cur_ref, b, h):
        if h == 0:
            return jnp.concatenate([prev_ref[pl.ds(b, blk, stride=n_batch), :], cur_ref[query_rows(b, 0), :]], axis=0)
        return jnp.concatenate([prev_ref[query_rows(b, 1), :], cur_ref[pl.ds(b, blk, stride=n_batch), :]], axis=0)

    def score_stage(b):
        sts = []
        for h in range(2):
            k = window(kp_ref, kc_ref, b, h).astype(BF16)
            for pr in range(N_CHUNKS // 2):
                pieces = []
                for c in (2 * pr, 2 * pr + 1):
                    qc = q_ref[c, query_rows(b, h), :]
                    pieces += [jnp.where(low, qc, 0.0), jnp.where(low, 0.0, qc)]
                qp = jnp.concatenate(pieces, axis=0).astype(BF16)
                sts.append(_scores(k, qp, HEAD_DIM ** -0.5 * LOG2_E))
        return sts

    def value_stage(b, sts):
        vt_aug = [jnp.concatenate([window(vp_ref, vc_ref, b, h).T, ones], axis=0).astype(BF16) for h in range(2)]

        def finish(h, pr, ot, m):
            den = ot[KV_W:KV_W + 1, :] + jnp.exp2(sinks[pr] - m)
            o = (ot[:KV_W, :] * (1.0 / den)).T
            sel = query_rows(b, h)
            for i, c in enumerate((2 * pr, 2 * pr + 1)):
                lo = 2 * i * qb
                ya = jnp.where(low, o[lo:lo + qb], o[lo + qb:lo + 2 * qb])
                o_ref[c, sel, :] = ya * za_ref[c, sel, :]

        pending = None
        for (h, pr), st in zip(units, sts):
            st = jnp.where(masks[h], st, NEG_INF)
            m = jnp.maximum(jnp.max(st, axis=0, keepdims=True), sinks[pr])
            p = jnp.exp2(st - m).astype(BF16)
            ot = jnp.dot(vt_aug[h], p, preferred_element_type=F32)
            if pending is not None:
                finish(*pending)
            pending = (h, pr, ot, m)
        finish(*pending)

    return score_stage, value_stage


def _attn_sample_kernel(q_ref, k_ref, v_ref, za_ref, ck_ref, cv_ref, sink_ref, nk_all_ref, nv_all_ref,
                        o_ref, nk_ref, nv_ref, *, n_batch, t_new, group):
    del nk_all_ref, nv_all_ref
    g = pl.program_id(0)
    win = ck_ref.shape[2]
    rows = N_Q_HEADS * t_new
    tq = lax.broadcasted_iota(jnp.int32, (rows, 2 * win), 0) % t_new
    j = lax.broadcasted_iota(jnp.int32, (rows, 2 * win), 1)
    dist_c = tq + win - j
    mask = ((j < win) & (dist_c >= 0) & (dist_c <= WINDOW)) | ((j >= win) & (j - win <= tq) & (j - win < t_new))
    sink = sink_ref[...]
    pad = jnp.zeros((win - t_new, KV_W), F32)
    ones_rows = jnp.ones((LANES, win), F32)
    ones_cols = jnp.ones((win, LANES), F32)
    kept = lax.broadcasted_iota(jnp.int32, (KV_W, win), 1) < win - t_new
    batch = range(group)
    sels = [pl.ds(g * group + i, t_new, stride=n_batch) for i in batch]

    scores = []
    for i in batch:
        qp = _pad_queries([q_ref[c, sels[i], :] for c in range(N_CHUNKS)])
        s_cache = jnp.dot(qp, ck_ref[i].astype(BF16), preferred_element_type=F32) * (HEAD_DIM ** -0.5)
        s_new = _scores(qp, jnp.concatenate([k_ref[sels[i], :], pad], axis=0))
        scores.append(jnp.where(mask, jnp.concatenate([s_cache, s_new], axis=1), NEG_INF))
    outs = []
    for i in batch:
        s = scores[i]
        m = jnp.maximum(jnp.max(s, axis=-1, keepdims=True), sink[:, :1])
        p = jnp.exp(s - m).astype(BF16)
        vt_aug = jnp.concatenate([cv_ref[i], ones_rows], axis=0).astype(BF16)
        v_new = jnp.concatenate([jnp.concatenate([v_ref[sels[i], :], pad], axis=0), ones_cols], axis=1).astype(BF16)
        o = (lax.dot_general(p[:, :win], vt_aug, (((1,), (1,)), ((), ())), preferred_element_type=F32)
             + jnp.dot(p[:, win:], v_new, preferred_element_type=F32))
        outs.append(o[:, :KV_W] / (o[:, KV_W:] + jnp.exp(sink - m)))
    for i in batch:
        _gated_store(o_ref, za_ref, sels[i], _unpad_outputs(outs[i], t_new))
        for new_ref, old_ref, rows_ref in ((nk_ref, ck_ref, k_ref), (nv_ref, cv_ref, v_ref)):
            appended = jnp.concatenate([pad, rows_ref[sels[i], :]], axis=0).T
            new_ref[i] = jnp.where(kept, pltpu.roll(old_ref[i], win - t_new, 1), appended)


def attention_sample(q, k, v, za, cache_kt, cache_vt, sink_rows, new_kt, new_vt, l, n_batch, t_new, group):
    t_rows = k.shape[0]
    win = cache_kt.shape[3]
    window = pl.BlockSpec((None, group, KV_W, win), lambda i: (l, i, 0, 0))
    chunked = _full((N_CHUNKS, t_rows, LANES))
    in_place = pl.BlockSpec(memory_space=pl.ANY)
    return pl.pallas_call(
        functools.partial(_attn_sample_kernel, n_batch=n_batch, t_new=t_new, group=group),
        grid=(n_batch // group,),
        in_specs=[chunked, _full((t_rows, KV_W)), _full((t_rows, KV_W)), chunked,
                  window, window, _layer_spec(sink_rows, l), in_place, in_place],
        out_specs=[chunked, window, window],
        out_shape=[jax.ShapeDtypeStruct((N_CHUNKS, t_rows, LANES), F32),
                   jax.ShapeDtypeStruct(new_kt.shape, F32), jax.ShapeDtypeStruct(new_vt.shape, F32)],
        input_output_aliases={7: 1, 8: 2},
        compiler_params=_params(("arbitrary",)),
        name="attention_sample",
    )(q, k, v, za, cache_kt, cache_vt, sink_rows, new_kt, new_vt)


def _ssm_stages(u_ref, zs_ref, h0re_ref, h0im_ref, are_ref, aim_ref, wb_ref, wc_ref, d_ref, wg_ref, bg_ref,
                o_ref, hre_ref, him_ref, x_scr, n_batch, sub_rows, slab, h0_transposed=False):
    step = pl.program_id(0)

    @pl.when(step == 0)
    def _():
        hre_ref[...] = h0re_ref[...].T if h0_transposed else h0re_ref[...]
        him_ref[...] = h0im_ref[...].T if h0_transposed else h0im_ref[...]

    rows = u_ref.shape[0]
    half_in = SSM_W // 2
    n_sub = rows // sub_rows
    n_tiles = n_batch // SUBLANES
    t_steps = sub_rows // n_batch
    n_buf = x_scr.shape[0]
    items = [(sc, s) for sc in range(n_sub) for s in range(2)]

    def expand(i):
        sc, s = items[i]
        ub = u_ref[sc * sub_rows:(sc + 1) * sub_rows, s * half_in:(s + 1) * half_in].astype(BF16)
        x_scr[i % n_buf] = jnp.dot(ub, wb_ref[s], preferred_element_type=F32)

    def recur(i):
        sc, s = items[i]
        buf = i % n_buf
        for jb in range(HALF_STATE // slab):
            n_lo = s * HALF_STATE + jb * slab
            ar = jnp.broadcast_to(are_ref[:, n_lo:n_lo + slab], (SUBLANES, slab))
            ai = jnp.broadcast_to(aim_ref[:, n_lo:n_lo + slab], (SUBLANES, slab))
            re = slice(jb * slab, (jb + 1) * slab)
            im = slice(HALF_STATE + jb * slab, HALF_STATE + (jb + 1) * slab)
            for tile in range(n_tiles):
                r0 = tile * SUBLANES
                hr = hre_ref[r0:r0 + SUBLANES, n_lo:n_lo + slab]
                hi = him_ref[r0:r0 + SUBLANES, n_lo:n_lo + slab]
                for t in range(t_steps):
                    row = slice(t * n_batch + r0, t * n_batch + r0 + SUBLANES)
                    hr, hi = (ar * hr - ai * hi + x_scr[buf, row, re], ar * hi + ai * hr + x_scr[buf, row, im])
                    x_scr[buf, row, re] = hr
                    x_scr[buf, row, im] = hi
                hre_ref[r0:r0 + SUBLANES, n_lo:n_lo + slab] = hr
                him_ref[r0:r0 + SUBLANES, n_lo:n_lo + slab] = hi

    y_halves = {}

    def contract(i):
        sc, s = items[i]
        y_halves[s] = jnp.dot(x_scr[i % n_buf].astype(BF16), wc_ref[s], preferred_element_type=F32)
        if s == 1:
            r = slice(sc * sub_rows, (sc + 1) * sub_rows)
            y = jnp.concatenate([y_halves[0], y_halves[1]], axis=1) + d_ref[...] * u_ref[r, :]
            y = jax.nn.gelu(y)
            gate = jax.nn.sigmoid(jnp.dot(y.astype(BF16), wg_ref[...], preferred_element_type=F32) + bg_ref[...])
            o_ref[r, :] = (y * gate * zs_ref[r, :]).astype(o_ref.dtype)

    return len(items), expand, recur, contract


def _ssm_kernel(*refs, n_batch, sub_rows, slab):
    inputs, (o_ref, hre_t_ref, him_t_ref, x_scr, hre_ref, him_ref) = refs[:N_SSM_INPUTS], refs[N_SSM_INPUTS:]
    n_items, expand, recur, contract = _ssm_stages(*inputs, o_ref, hre_ref, him_ref, x_scr, n_batch, sub_rows, slab,
                                                   h0_transposed=True)
    expand(0)
    for i in range(n_items + 1):
        if i + 1 < n_items:
            expand(i + 1)
        if i < n_items:
            recur(i)
        if i >= 1:
            contract(i - 1)

    @pl.when(pl.program_id(0) == pl.num_programs(0) - 1)
    def _():
        hre_t_ref[...] = hre_ref[...].T
        him_t_ref[...] = him_ref[...].T


N_ATTN_INPUTS = 7
N_SSM_INPUTS = 11


def _prompt_branches_kernel(*refs, n_batch, blk, sub_rows, slab):
    n_in = N_ATTN_INPUTS + N_SSM_INPUTS
    attn_refs = refs[:N_ATTN_INPUTS] + refs[n_in:n_in + 1]
    ssm_refs = refs[N_ATTN_INPUTS:n_in] + refs[n_in + 1:]
    score_stage, value_stage = _prompt_attention_stages(*attn_refs, n_batch, blk)
    n_items, expand, recur, contract = _ssm_stages(*ssm_refs, n_batch, sub_rows, slab)
    per_item = n_batch // n_items
    assert per_item * n_items == n_batch
    seqs_of = lambda i: range(i * per_item, (i + 1) * per_item)
    expand(0)
    scores = [score_stage(b) for b in seqs_of(0)]
    for i in range(n_items + 1):
        if i < n_items:
            recur(i)
        if i + 1 < n_items:
            expand(i + 1)
        if i < n_items:
            for b, sts in zip(seqs_of(i), scores):
                value_stage(b, sts)
        if i + 1 < n_items:
            scores = [score_stage(b) for b in seqs_of(i + 1)]
        if i >= 1:
            contract(i - 1)


def _ssm_call_parts(u, h0_re, h0_im, abar_re, abar_im, w_b, w_c, d, w_glu, b_glu, l, l_state, n_batch, r):
    t_rows = u.shape[0]
    sub_rows = max(SSM_SUB_ROWS, n_batch)
    rows = pl.BlockSpec((r, SSM_W), lambda i: (i, 0))
    state = _full((n_batch, N_STATE))
    layer = lambda a: _layer_spec(a, l)
    in_specs = [rows, rows, _layer_spec(h0_re, l_state), _layer_spec(h0_im, l_state), layer(abar_re), layer(abar_im),
                layer(w_b), layer(w_c), layer(d), layer(w_glu), layer(b_glu)]
    out_shape = [jax.ShapeDtypeStruct((t_rows, SSM_W), BF16),
                 jax.ShapeDtypeStruct((n_batch, N_STATE), F32), jax.ShapeDtypeStruct((n_batch, N_STATE), F32)]
    scratch = [pltpu.VMEM((SSM_BUFFERS, sub_rows, 2 * HALF_STATE), F32)]
    return in_specs, [rows, state, state], out_shape, scratch, sub_rows


def ssm_branch(u, zs, h0_re, h0_im, abar_re, abar_im, w_b, w_c, d, w_glu, b_glu, l, l_state, n_batch, t_chunk):
    r = t_chunk * n_batch
    in_specs, out_specs, out_shape, scratch, sub_rows = _ssm_call_parts(
        u, h0_re, h0_im, abar_re, abar_im, w_b, w_c, d, w_glu, b_glu, l, l_state, n_batch, r)
    out_specs = out_specs[:1] + [_full((N_STATE, n_batch))] * 2
    out_shape = out_shape[:1] + [jax.ShapeDtypeStruct((N_STATE, n_batch), F32)] * 2
    scratch = scratch + [pltpu.VMEM((n_batch, N_STATE), F32)] * 2
    return pl.pallas_call(
        functools.partial(_ssm_kernel, n_batch=n_batch, sub_rows=sub_rows, slab=SSM_SLAB),
        grid=(u.shape[0] // r,),
        in_specs=in_specs,
        out_specs=out_specs,
        out_shape=out_shape,
        scratch_shapes=scratch,
        compiler_params=_params(("arbitrary",)),
        name="ssm_branch",
    )(u, zs, h0_re, h0_im, abar_re, abar_im, w_b, w_c, d, w_glu, b_glu)


def prompt_branches(q, k, v, za, sinks, u, zs, h0_re, h0_im, abar_re, abar_im, w_b, w_c, d, w_glu, b_glu,
                    l, l_state, n_batch, blk):
    t_rows = k.shape[0]
    r = blk * n_batch
    cur = lambda w: pl.BlockSpec((r, w), lambda i: (i, 0))
    prev = lambda w: pl.BlockSpec((r, w), lambda i: (jnp.maximum(i - 1, 0), 0))
    chunked = pl.BlockSpec((N_CHUNKS, r, LANES), lambda i: (0, i, 0))
    ssm_in, ssm_out, ssm_shape, scratch, sub_rows = _ssm_call_parts(
        u, h0_re, h0_im, abar_re, abar_im, w_b, w_c, d, w_glu, b_glu, l, l_state, n_batch, r)
    attn_in = [chunked, prev(KV_W), cur(KV_W), prev(KV_W), cur(KV_W), chunked, _layer_spec(sinks, l)]
    assert len(attn_in) == N_ATTN_INPUTS and len(ssm_in) == N_SSM_INPUTS
    return pl.pallas_call(
        functools.partial(_prompt_branches_kernel, n_batch=n_batch, blk=blk, sub_rows=sub_rows, slab=SSM_SLAB),
        grid=(t_rows // r,),
        in_specs=attn_in + ssm_in,
        out_specs=[chunked] + ssm_out,
        out_shape=[jax.ShapeDtypeStruct((N_CHUNKS, t_rows, LANES), F32)] + ssm_shape,
        scratch_shapes=scratch,
        compiler_params=_params(("arbitrary",)),
        name="prompt_branches",
    )(q, k, k, v, v, za, sinks, u, zs, h0_re, h0_im, abar_re, abar_im, w_b, w_c, d, w_glu, b_glu)


def _rope_tables(pos):
    half = ROT_DIM // 2
    inv = ROPE_THETA ** (-jnp.arange(half, dtype=F32) / half)
    ang = pos.astype(F32)[:, None] * inv[None, :]
    d = np.arange(LANES) % HEAD_DIM
    idx = d % half
    cos = jnp.where(d < ROT_DIM, jnp.cos(ang)[:, idx], 1.0)
    sin = jnp.sin(ang)[:, idx]
    sm = jnp.where(d < half, -sin, jnp.where(d < ROT_DIM, sin, 0.0))
    return cos, sm


def _keep_diagonal_blocks(w, row_block, col_block):
    rows, cols = w.shape[-2:]
    keep = (np.arange(rows)[:, None] // row_block) == (np.arange(cols)[None, :] // col_block)
    return jnp.where(keep, w, 0.0)


def _expand_weights(bb):
    depth = bb.shape[0]
    hg = N_SSM_GROUPS // 2
    halves = bb.reshape(depth, SSM_GROUP_CH, 2, HALF_STATE).transpose(0, 2, 1, 3)
    tiled = jnp.broadcast_to(halves[:, :, None], (depth, 2, hg, SSM_GROUP_CH, HALF_STATE))
    return _keep_diagonal_blocks(tiled.reshape(depth, 2, hg * SSM_GROUP_CH, HALF_STATE), SSM_GROUP_CH, SSM_STATE)


def _contract_weights(c):
    depth = c.shape[0]
    hg = N_SSM_GROUPS // 2
    rows = c.reshape(depth, 2, hg, SSM_GROUP_CH, SSM_STATE).transpose(0, 1, 2, 4, 3).reshape(
        depth, 2, HALF_STATE, SSM_GROUP_CH)
    return _keep_diagonal_blocks(jnp.tile(rows, (1, 1, 1, hg)), SSM_STATE, SSM_GROUP_CH)


def _prepare_weights(w_in, w_out, attn_sinks, bb_re, bb_im, ssm_c_re, ssm_c_im, t_new):
    depth = w_in.shape[0]
    assert HEAD_PERM == tuple(kv * Q_GROUP + g for g in range(Q_GROUP) for kv in range(N_KV_HEADS))
    w_in_p = w_in.astype(BF16)
    w_out_p = w_out.astype(BF16)
    sinks = jnp.swapaxes(attn_sinks.reshape(depth, N_KV_HEADS, Q_GROUP), 1, 2).reshape(depth, N_Q_HEADS)
    sink_lanes = jnp.broadcast_to(sinks[:, :, None], (depth, N_Q_HEADS, LANES))
    sink_rows = jnp.broadcast_to(jnp.repeat(sinks, t_new, axis=1)[:, :, None], (depth, N_Q_HEADS * t_new, LANES))
    w_b = jnp.concatenate([_expand_weights(bb_re), _expand_weights(bb_im)], axis=3).astype(BF16)
    w_c = jnp.concatenate([_contract_weights(ssm_c_re), -_contract_weights(ssm_c_im)], axis=2).astype(BF16)
    return w_in_p, w_out_p, sink_lanes, sink_rows, w_b, w_c


def kernel(x_prompt, x_sample, cache_k, cache_v, state_ssm_re, state_ssm_im, c_prompt, c_sample, norm_g, w_ada, b_ada, w_in, attn_sinks, ssm_a_re, ssm_a_im, ssm_log_dt, ssm_b_re, ssm_b_im, ssm_c_re, ssm_c_im, ssm_d, w_glu, b_glu, w_out, final_g):
    depth = w_in.shape[0]
    nb_p, seq, _ = x_prompt.shape
    nb_s, t_new, _ = x_sample.shape
    win = cache_k.shape[2]
    blk = WINDOW

    sample = Group(n_batch=nb_s, mod_block=0)
    prompt = Group(n_batch=nb_p, mod_block=nb_s // nb_p)
    mod = adaln_mod(jnp.concatenate([c_sample, c_prompt], axis=0), w_ada, b_ada)
    abar_re, abar_im, bb_re, bb_im = ssm_discretise(ssm_a_re, ssm_a_im, ssm_log_dt, ssm_b_re, ssm_b_im)
    w_in_p, w_out_p, sink_lanes, sink_rows, w_b, w_c = _prepare_weights(
        w_in, w_out, attn_sinks, bb_re, bb_im, ssm_c_re, ssm_c_im, t_new)
    norm_g3 = norm_g.reshape(depth, 1, D_MODEL)
    d3 = ssm_d.reshape(depth, 1, SSM_W)
    w_glu_b = w_glu.astype(BF16)
    b_glu3 = b_glu.reshape(depth, 1, SSM_W)
    final_g2 = final_g.reshape(1, D_MODEL)
    ssm_weights = (abar_re, abar_im, w_b, w_c, d3, w_glu_b, b_glu3)

    cos_p, sm_p = _rope_tables(jnp.arange(seq))
    cos_s, sm_s = _rope_tables(PAST_LEN + jnp.arange(t_new))

    xp = x_prompt
    xs = x_sample.transpose(1, 0, 2).reshape(t_new * nb_s, D_MODEL)
    zeros_state = jnp.zeros((1, nb_p, N_STATE), F32)
    transposed = lambda c: c.transpose(0, 1, 3, 4, 2).reshape(depth, nb_s, KV_W, win)
    cache_kt, cache_vt = transposed(cache_k), transposed(cache_v)
    new_kt, new_vt = jnp.zeros_like(cache_kt), jnp.zeros_like(cache_vt)
    state_re3 = state_ssm_re.transpose(0, 2, 3, 1).reshape(depth, N_STATE, nb_s)
    state_im3 = state_ssm_im.transpose(0, 2, 3, 1).reshape(depth, N_STATE, nb_s)

    outs = {k: [] for k in ("kp", "vp", "rp", "ip", "rs", "is")}
    rows_p, rows_s = BOUNDARY_ROWS, t_new * nb_s
    opening = lambda l, cos_t, sm_t: (norm_g3, w_in_p, cos_t, sm_t, l)
    xp, proj_p = layer_boundary(xp, mod, prompt, rows_p, open_=opening(0, cos_p, sm_p))
    xs, proj_s = layer_boundary(xs, mod, sample, rows_s, open_=opening(0, cos_s, sm_s))
    for l in range(depth):
        last = l == depth - 1

        q, k, v, za, u, zs = proj_p
        ya, ys, hre, him = prompt_branches(q, k, v, za, sink_lanes, u, zs, zeros_state, zeros_state, *ssm_weights,
                                           l, 0, nb_p, blk)
        xp, proj_p = layer_boundary(xp, mod, prompt, rows_p, close=(ya, ys, w_out_p, l),
                                    open_=None if last else opening(l + 1, cos_p, sm_p),
                                    final_g=final_g2, batch_major_out=last)
        outs["kp"].append(k[(seq - win) * nb_p:])
        outs["vp"].append(v[(seq - win) * nb_p:])
        outs["rp"].append(hre)
        outs["ip"].append(him)

        q, k, v, za, u, zs = proj_s
        ya, new_kt, new_vt = attention_sample(q, k, v, za, cache_kt, cache_vt, sink_rows, new_kt, new_vt, l,
                                              nb_s, t_new, SAMPLE_GROUP)
        ys, hre, him = ssm_branch(u, zs, state_re3, state_im3, *ssm_weights, l, l, nb_s, t_new)
        xs, proj_s = layer_boundary(xs, mod, sample, rows_s, close=(ya, ys, w_out_p, l),
                                    open_=None if last else opening(l + 1, cos_s, sm_s), final_g=final_g2)
        outs["rs"].append(hre)
        outs["is"].append(him)

    y_prompt = xp
    y_sample = xs.reshape(t_new, nb_s, D_MODEL).transpose(1, 0, 2)
    st = lambda key: jnp.stack(outs[key])
    window = lambda key: st(key).reshape(depth, win, nb_p, N_KV_HEADS, HEAD_DIM).transpose(0, 2, 1, 3, 4)
    state = lambda key, nb: st(key).reshape(depth, nb, N_SSM_GROUPS, SSM_STATE)
    state_t = lambda key: st(key).reshape(depth, N_SSM_GROUPS, SSM_STATE, nb_s).transpose(0, 3, 1, 2)
    untransposed = lambda w: w.reshape(depth, nb_s, N_KV_HEADS, HEAD_DIM, win).transpose(0, 1, 4, 2, 3)
    return (y_prompt, y_sample, window("kp"), window("vp"), state("rp", nb_p), state("ip", nb_p),
            untransposed(new_kt), untransposed(new_vt), state_t("rs"), state_t("is"))
```

```python
import functools
from typing import NamedTuple

import jax
import jax.numpy as jnp
import numpy as np
from jax import lax
from jax.experimental import pallas as pl
from jax.experimental.pallas import tpu as pltpu

F32 = jnp.float32
BF16 = jnp.bfloat16

D_MODEL = 1024
HEAD_DIM = 64
ATTN_W = 512
N_Q_HEADS = 8
N_KV_HEADS = 2
Q_GROUP = N_Q_HEADS // N_KV_HEADS
KV_W = 128
WINDOW = 128
ROT_DIM = 16
ROPE_THETA = 500000.0
SSM_W = 512
SSM_GROUP_CH = 16
N_SSM_GROUPS = 32
SSM_STATE = 64
N_STATE = N_SSM_GROUPS * SSM_STATE
COL_Q, COL_KV, COL_ZA = 0, ATTN_W, ATTN_W + 2 * KV_W
COL_U, COL_ZS = COL_ZA + ATTN_W, COL_ZA + ATTN_W + SSM_W
HALF_STATE = N_STATE // 2
EPS = 1e-6
NEG_INF = -1e30
LOG2_E = 1.4426950408889634
PAST_LEN = 8192

MOD_PARTS = 3
LANES = 128
N_CHUNKS = ATTN_W // LANES
BOUNDARY_ROWS = 1024
BOUNDARY_PARTS = 2
SSM_SLAB = 4 * LANES
SAMPLE_GROUP = 16
SSM_SUB_ROWS = 256
SSM_BUFFERS = 3
SUBLANES = 8
VMEM_LIMIT = 56 * 1024 * 1024

HEAD_PERM = (0, 4, 1, 5, 2, 6, 3, 7)


class Group(NamedTuple):
    n_batch: int
    mod_block: int


def _params(sem):
    return pltpu.CompilerParams(dimension_semantics=sem, vmem_limit_bytes=VMEM_LIMIT)


def _full(shape):
    return pl.BlockSpec(shape, lambda *_: (0,) * len(shape))


def _mod_kernel(c_ref, w_ref, b_ref, o_ref):
    c = c_ref[...]
    a = (c * jax.nn.sigmoid(c)).astype(BF16)
    o_ref[...] = jnp.dot(a, w_ref[...].astype(BF16), preferred_element_type=F32) + b_ref[...]


def adaln_mod(c_all, w_ada, b_ada):
    depth = w_ada.shape[0]
    n = c_all.shape[0]
    return pl.pallas_call(
        _mod_kernel,
        grid=(depth, MOD_PARTS),
        in_specs=[
            pl.BlockSpec((n, D_MODEL), lambda l, j: (0, 0)),
            pl.BlockSpec((None, D_MODEL, D_MODEL), lambda l, j: (l, 0, j)),
            pl.BlockSpec((None, 1, D_MODEL), lambda l, j: (l, 0, j)),
        ],
        out_specs=pl.BlockSpec((None, n, D_MODEL), lambda l, j: (l, 0, j)),
        out_shape=jax.ShapeDtypeStruct((depth, n, 3 * D_MODEL), F32),
        compiler_params=_params(("arbitrary", "arbitrary")),
        name="adaln_mod",
    )(c_all, w_ada, b_ada.reshape(depth, 1, 3 * D_MODEL))


def _disc_kernel(are_ref, aim_ref, ldt_ref, bre_ref, bim_ref, abre_ref, abim_ref, bbre_ref, bbim_ref):
    a_re = are_ref[...]
    a_im = aim_ref[...]
    dt = jnp.exp(ldt_ref[...])
    mag = jnp.exp(a_re * dt)
    abar_re = mag * jnp.cos(a_im * dt)
    abar_im = mag * jnp.sin(a_im * dt)
    den = a_re * a_re + a_im * a_im
    nr = abar_re - 1.0
    coef_re = (nr * a_re + abar_im * a_im) / den
    coef_im = (abar_im * a_re - nr * a_im) / den
    br = bre_ref[...]
    bi = bim_ref[...]
    abre_ref[...] = abar_re
    abim_ref[...] = abar_im
    bbre_ref[...] = coef_re * br - coef_im * bi
    bbim_ref[...] = coef_re * bi + coef_im * br


def ssm_discretise(a_re, a_im, log_dt, b_re, b_im):
    depth = a_re.shape[0]
    a_re = a_re.reshape(depth, 1, N_STATE)
    a_im = a_im.reshape(depth, 1, N_STATE)
    ldt = jnp.repeat(log_dt, SSM_STATE, axis=1).reshape(depth, 1, N_STATE)
    bt_re = b_re.reshape(depth, N_STATE, SSM_GROUP_CH).transpose(0, 2, 1)
    bt_im = b_im.reshape(depth, N_STATE, SSM_GROUP_CH).transpose(0, 2, 1)
    row = pl.BlockSpec((None, 1, N_STATE), lambda l: (l, 0, 0))
    mat = pl.BlockSpec((None, SSM_GROUP_CH, N_STATE), lambda l: (l, 0, 0))
    return pl.pallas_call(
        _disc_kernel,
        grid=(depth,),
        in_specs=[row, row, row, mat, mat],
        out_specs=[row, row, mat, mat],
        out_shape=[jax.ShapeDtypeStruct((depth, 1, N_STATE), F32)] * 2
        + [jax.ShapeDtypeStruct((depth, SSM_GROUP_CH, N_STATE), F32)] * 2,
        compiler_params=_params(("arbitrary",)),
        name="ssm_discretise",
    )(a_re, a_im, ldt, bt_re, bt_im)


def _rope(x, cos, sm):
    lane = lax.broadcasted_iota(jnp.int32, x.shape, 1) % HEAD_DIM
    partner = jnp.where(lane < ROT_DIM // 2, pltpu.roll(x, LANES - ROT_DIM // 2, 1), pltpu.roll(x, ROT_DIM // 2, 1))
    return x * cos + partner * sm


def _to_time_major(x_ref):
    n_batch, steps, width = x_ref.shape
    return jnp.swapaxes(x_ref[...], 0, 1).reshape(steps * n_batch, width)


def _from_time_major(x, o_ref, n_batch):
    rows, width = x.shape
    o_ref[...] = jnp.swapaxes(x.reshape(rows // n_batch, n_batch, width), 0, 1)


def _rms(x, g_ref):
    ms = jnp.mean(x * x, axis=-1, keepdims=True)
    return x * lax.rsqrt(ms + EPS) * g_ref[...]


def _per_sequence(y, n_batch, fn):
    rows = y.shape[0]
    return fn(y.reshape(rows // n_batch, n_batch, D_MODEL)).reshape(rows, D_MODEL)


def _pair_heads(chunks):
    low = lax.broadcasted_iota(jnp.int32, chunks[0].shape, 1) < HEAD_DIM
    swap = lambda x: pltpu.roll(x, HEAD_DIM, 1)
    n0, n1, n2, n3 = chunks
    return [jnp.where(low, n0, swap(n2)), jnp.where(low, swap(n0), n2),
            jnp.where(low, n1, swap(n3)), jnp.where(low, swap(n1), n3)]


def _unpair_heads(chunks):
    low = lax.broadcasted_iota(jnp.int32, chunks[0].shape, 1) < HEAD_DIM
    swap = lambda x: pltpu.roll(x, HEAD_DIM, 1)
    p0, p1, p2, p3 = chunks
    return [jnp.where(low, p0, swap(p1)), jnp.where(low, p2, swap(p3)),
            jnp.where(low, swap(p0), p1), jnp.where(low, swap(p2), p3)]


def _residual_update(x, rows, ya_ref, ys_ref, gate_ref, w_ref, n_batch):
    ya = _unpair_heads([ya_ref[c, rows, :] for c in range(N_CHUNKS)])
    mixed = jnp.concatenate([y.astype(BF16) for y in ya] + [ys_ref[rows, :]], axis=1)
    mix = jnp.dot(mixed, w_ref[...], preferred_element_type=F32)
    return x + _per_sequence(mix, n_batch, lambda m: m * gate_ref[...][None])


def _modulated_norm(x, g_ref, scale_ref, shift_ref, n_batch):
    y = _per_sequence(_rms(x, g_ref), n_batch, lambda y3: y3 * (1.0 + scale_ref[...])[None] + shift_ref[...][None])
    return y.astype(BF16)


def _silu(z):
    return z * jax.nn.sigmoid(z)


def _project(h, rows, w_ref, rope_scr, q_ref, k_ref, v_ref, ga_ref, u_ref, gs_ref):
    cos = rope_scr[0, rows, :]
    sm = rope_scr[1, rows, :]

    def proj(lo, width):
        return jnp.dot(h, w_ref[:, lo:lo + width], preferred_element_type=F32)

    chunks_of = lambda a: [a[:, c * LANES:(c + 1) * LANES] for c in range(N_CHUNKS)]
    q = _pair_heads(chunks_of(proj(COL_Q, ATTN_W)))
    ga = _pair_heads(chunks_of(_silu(proj(COL_ZA, ATTN_W))))
    for c in range(N_CHUNKS):
        q_ref[c, rows, :] = _rope(q[c], cos, sm)
        ga_ref[c, rows, :] = ga[c]
    u_ref[rows, :] = proj(COL_U, SSM_W)
    gs_ref[rows, :] = _silu(proj(COL_ZS, SSM_W))
    kv = proj(COL_KV, 2 * KV_W)
    k_ref[rows, :] = _rope(kv[:, :KV_W], cos, sm)
    v_ref[rows, :] = kv[:, KV_W:]


def _boundary_kernel(*refs, n_batch, closes, opens, batch_major, block_rows, n_parts):
    refs = list(refs)
    x_ref = refs.pop(0)
    close_refs = [refs.pop(0) for _ in range(4)] if closes else None
    final_g_ref = refs.pop(0) if closes and not opens else None
    open_refs = [refs.pop(0) for _ in range(6)] if opens else None
    x_out_ref = refs.pop(0) if closes or batch_major else None
    proj_out_refs = [refs.pop(0) for _ in range(6)] if opens else None
    rope_scr = refs.pop(0) if opens else None

    if opens:
        g_ref, scale_ref, shift_ref, w_in_ref, cos_ref, sm_ref = open_refs
        steps = cos_ref.shape[0]
        for b in range(n_batch):
            rope_scr[0, pl.ds(b, steps, stride=n_batch), :] = cos_ref[...]
            rope_scr[1, pl.ds(b, steps, stride=n_batch), :] = sm_ref[...]

    part = block_rows // n_parts
    groups = [slice(i * part, (i + 1) * part) for i in range(n_parts)]
    x_full = _to_time_major(x_ref) if (batch_major and not closes) else None

    def updated(rows):
        x = x_full[rows, :] if x_full is not None else x_ref[rows, :]
        return _residual_update(x, rows, *close_refs, n_batch) if closes else x

    def normed(x, rows):
        if opens:
            if x_out_ref is not None:
                x_out_ref[rows, :] = x
            return _modulated_norm(x, g_ref, scale_ref, shift_ref, n_batch)
        return _rms(x, final_g_ref)

    xs = [updated(rows) for rows in groups]
    hs = [None] * n_parts
    hs[0] = normed(xs[0], groups[0])
    for i, rows in enumerate(groups):
        if opens:
            _project(hs[i], rows, w_in_ref, rope_scr, *proj_out_refs)
        if i + 1 < n_parts:
            hs[i + 1] = normed(xs[i + 1], groups[i + 1])
    if not opens:
        y = jnp.concatenate(hs, axis=0)
        if batch_major:
            _from_time_major(y, x_out_ref, n_batch)
        else:
            x_out_ref[...] = y


def _layer_spec(arr, l):
    tail = arr.shape[1:]
    return pl.BlockSpec((None,) + tail, lambda *_: (l,) + (0,) * len(tail))


def _mod_spec(l, group, part):
    return pl.BlockSpec((None, group.n_batch, D_MODEL), lambda *_: (l, group.mod_block, part))


def layer_boundary(x, mod, group, block_rows, close=None, open_=None, final_g=None, batch_major_out=False):
    closes, opens = close is not None, open_ is not None
    batch_major_in = x.ndim == 3
    assert not (batch_major_in and closes) and not (batch_major_out and opens)
    batch_major = batch_major_in or batch_major_out
    t_rows = x.shape[0] * x.shape[1] if batch_major_in else x.shape[0]
    rows = lambda w: pl.BlockSpec((block_rows, w), lambda i: (i, 0))
    chunked = pl.BlockSpec((N_CHUNKS, block_rows, LANES), lambda i: (0, i, 0))
    blocked3 = pl.BlockSpec((group.n_batch, block_rows // group.n_batch, D_MODEL), lambda i: (0, i, 0))
    flat = lambda w: jax.ShapeDtypeStruct((t_rows, w), F32)
    chunked_shape = jax.ShapeDtypeStruct((N_CHUNKS, t_rows, LANES), F32)

    args, in_specs, out_specs, out_shape = [x], [blocked3 if batch_major_in else rows(D_MODEL)], [], []
    if closes:
        ya, ys, w_out, l = close
        args += [ya, ys, mod, w_out]
        in_specs += [chunked, rows(SSM_W), _mod_spec(l, group, 2), _layer_spec(w_out, l)]
        if not opens:
            args.append(final_g)
            in_specs.append(_full(final_g.shape))
    if opens:
        norm_g, w_in, cos_t, sm_t, l = open_
        args += [norm_g, mod, mod, w_in, cos_t, sm_t]
        per_step = pl.BlockSpec((block_rows // group.n_batch, LANES), lambda i: (i, 0))
        in_specs += [_layer_spec(norm_g, l), _mod_spec(l, group, 1), _mod_spec(l, group, 0), _layer_spec(w_in, l),
                     per_step, per_step]
    if closes or batch_major_in:
        out_specs.append(blocked3 if batch_major_out else rows(D_MODEL))
        out_shape.append(jax.ShapeDtypeStruct((group.n_batch, t_rows // group.n_batch, D_MODEL), F32)
                         if batch_major_out else flat(D_MODEL))
    if opens:
        out_specs += [chunked, rows(KV_W), rows(KV_W), chunked, rows(SSM_W), rows(SSM_W)]
        out_shape += [chunked_shape, flat(KV_W), flat(KV_W), chunked_shape, flat(SSM_W), flat(SSM_W)]
    scratch = [pltpu.VMEM((2, block_rows, LANES), F32)] if opens else []
    outs = pl.pallas_call(
        functools.partial(_boundary_kernel, n_batch=group.n_batch, closes=closes, opens=opens,
                          batch_major=batch_major, block_rows=block_rows, n_parts=BOUNDARY_PARTS),
        grid=(t_rows // block_rows,),
        in_specs=in_specs,
        out_specs=out_specs,
        out_shape=out_shape,
        scratch_shapes=scratch,
        compiler_params=_params(("parallel",)),
        name="layer_boundary",
    )(*args)
    x_rows = outs[0] if (closes or batch_major_in) else x
    return x_rows, (tuple(outs[-6:]) if opens else None)


def _pad_queries(chunks):
    lane = lax.broadcasted_iota(jnp.int32, chunks[0].shape, 1)
    pieces = []
    for chunk in chunks:
        pieces.append(jnp.where(lane < HEAD_DIM, chunk, 0.0))
        pieces.append(jnp.where(lane >= HEAD_DIM, chunk, 0.0))
    return jnp.concatenate(pieces, axis=0).astype(BF16)


def _unpad_outputs(o, t):
    lane = lax.broadcasted_iota(jnp.int32, (t, LANES), 1)
    chunks = []
    for c in range(N_CHUNKS):
        lo = o[(2 * c) * t:(2 * c + 1) * t]
        hi = o[(2 * c + 1) * t:(2 * c + 2) * t]
        chunks.append(jnp.where(lane < HEAD_DIM, lo, hi))
    return chunks


def _gated_store(o_ref, ga_ref, sel, ya_chunks):
    for c, ya in enumerate(ya_chunks):
        o_ref[c, sel, :] = ya * ga_ref[c, sel, :]


def _scores(qp, k, scale=HEAD_DIM ** -0.5):
    return lax.dot_general(qp, k.astype(BF16), (((1,), (1,)), ((), ())), preferred_element_type=F32) * scale


def _prompt_attention_stages(q_ref, kp_ref, kc_ref, vp_ref, vc_ref, za_ref, sink_ref, o_ref, n_batch, blk):
    assert blk == WINDOW
    n = pl.program_id(0)
    qb = blk // 2
    keys = blk + qb
    cols = 2 * LANES
    j = lax.broadcasted_iota(jnp.int32, (keys, cols), 0)
    tq = lax.broadcasted_iota(jnp.int32, (keys, cols), 1) % qb
    masks = [(j >= jnp.maximum(tq, jnp.where(n == 0, blk - h * qb, 0))) & (j <= tq + WINDOW) for h in range(2)]
    low = lax.broadcasted_iota(jnp.int32, (qb, LANES), 1) < HEAD_DIM
    ones = jnp.ones((SUBLANES, keys), F32)
    units = [(h, pr) for h in range(2) for pr in range(N_CHUNKS // 2)]
    sinks = [jnp.concatenate([sink_ref[slot:slot + 1, :qb] for slot in range(4 * pr, 4 * pr + 4)], axis=1) * LOG2_E
             for pr in range(N_CHUNKS // 2)]

    def query_rows(b, h):
        return pl.ds(b + h * qb * n_batch, qb, stride=n_batch)

    def window(prev_ref, cur_ref, b, h):
        if h == 0:
            return jnp.concatenate([prev_ref[pl.ds(b, blk, stride=n_batch), :], cur_ref[query_rows(b, 0), :]], axis=0)
        return jnp.concatenate([prev_ref[query_rows(b, 1), :], cur_ref[pl.ds(b, blk, stride=n_batch), :]], axis=0)

    def score_stage(b):
        sts = []
        for h in range(2):
            k = window(kp_ref, kc_ref, b, h).astype(BF16)
            for pr in range(N_CHUNKS // 2):
                pieces = []
                for c in (2 * pr, 2 * pr + 1):
                    qc = q_ref[c, query_rows(b, h), :]
                    pieces += [jnp.where(low, qc, 0.0), jnp.where(low, 0.0, qc)]
                qp = jnp.concatenate(pieces, axis=0).astype(BF16)
                sts.append(_scores(k, qp, HEAD_DIM ** -0.5 * LOG2_E))
        return sts

    def value_stage(b, sts):
        vt_aug = [jnp.concatenate([window(vp_ref, vc_ref, b, h).T, ones], axis=0).astype(BF16) for h in range(2)]

        def finish(h, pr, ot, m):
            den = ot[KV_W:KV_W + 1, :] + jnp.exp2(sinks[pr] - m)
            o = (ot[:KV_W, :] * (1.0 / den)).T
            sel = query_rows(b, h)
            for i, c in enumerate((2 * pr, 2 * pr + 1)):
                lo = 2 * i * qb
                ya = jnp.where(low, o[lo:lo + qb], o[lo + qb:lo + 2 * qb])
                o_ref[c, sel, :] = ya * za_ref[c, sel, :]

        pending = None
        for (h, pr), st in zip(units, sts):
            st = jnp.where(masks[h], st, NEG_INF)
            m = jnp.maximum(jnp.max(st, axis=0, keepdims=True), sinks[pr])
            p = jnp.exp2(st - m).astype(BF16)
            ot = jnp.dot(vt_aug[h], p, preferred_element_type=F32)
            if pending is not None:
                finish(*pending)
            pending = (h, pr, ot, m)
        finish(*pending)

    return score_stage, value_stage


def _attn_sample_kernel(q_ref, k_ref, v_ref, za_ref, ck_ref, cv_ref, sink_ref, nk_all_ref, nv_all_ref,
                        o_ref, nk_ref, nv_ref, *, n_batch, t_new, group):
    del nk_all_ref, nv_all_ref
    g = pl.program_id(0)
    win = ck_ref.shape[2]
    rows = N_Q_HEADS * t_new
    tq = lax.broadcasted_iota(jnp.int32, (rows, 2 * win), 0) % t_new
    j = lax.broadcasted_iota(jnp.int32, (rows, 2 * win), 1)
    dist_c = tq + win - j
    mask = ((j < win) & (dist_c >= 0) & (dist_c <= WINDOW)) | ((j >= win) & (j - win <= tq) & (j - win < t_new))
    sink = sink_ref[...]
    pad = jnp.zeros((win - t_new, KV_W), F32)
    ones_rows = jnp.ones((LANES, win), F32)
    ones_cols = jnp.ones((win, LANES), F32)
    kept = lax.broadcasted_iota(jnp.int32, (KV_W, win), 1) < win - t_new
    batch = range(group)
    sels = [pl.ds(g * group + i, t_new, stride=n_batch) for i in batch]

    scores = []
    for i in batch:
        qp = _pad_queries([q_ref[c, sels[i], :] for c in range(N_CHUNKS)])
        s_cache = jnp.dot(qp, ck_ref[i].astype(BF16), preferred_element_type=F32) * (HEAD_DIM ** -0.5)
        s_new = _scores(qp, jnp.concatenate([k_ref[sels[i], :], pad], axis=0))
        scores.append(jnp.where(mask, jnp.concatenate([s_cache, s_new], axis=1), NEG_INF))
    outs = []
    for i in batch:
        s = scores[i]
        m = jnp.maximum(jnp.max(s, axis=-1, keepdims=True), sink[:, :1])
        p = jnp.exp(s - m).astype(BF16)
        vt_aug = jnp.concatenate([cv_ref[i], ones_rows], axis=0).astype(BF16)
        v_new = jnp.concatenate([jnp.concatenate([v_ref[sels[i], :], pad], axis=0), ones_cols], axis=1).astype(BF16)
        o = (lax.dot_general(p[:, :win], vt_aug, (((1,), (1,)), ((), ())), preferred_element_type=F32)
             + jnp.dot(p[:, win:], v_new, preferred_element_type=F32))
        outs.append(o[:, :KV_W] / (o[:, KV_W:] + jnp.exp(sink - m)))
    for i in batch:
        _gated_store(o_ref, za_ref, sels[i], _unpad_outputs(outs[i], t_new))
    assert group * t_new <= win
    unused = [jnp.zeros((win - group * t_new, KV_W), F32)] if group * t_new < win else []
    for new_ref, old_ref, rows_ref in ((nk_ref, ck_ref, k_ref), (nv_ref, cv_ref, v_ref)):
        new_t = jnp.concatenate([rows_ref[sels[i], :] for i in batch] + unused, axis=0).T
        for i in batch:
            shift = (win - t_new - i * t_new) % win
            appended = pltpu.roll(new_t, shift, 1) if shift else new_t
            new_ref[i] = jnp.where(kept, pltpu.roll(old_ref[i], win - t_new, 1), appended)


def attention_sample(q, k, v, za, cache_kt, cache_vt, sink_rows, new_kt, new_vt, l, n_batch, t_new, group):
    t_rows = k.shape[0]
    win = cache_kt.shape[3]
    window = pl.BlockSpec((None, group, KV_W, win), lambda i: (l, i, 0, 0))
    chunked = _full((N_CHUNKS, t_rows, LANES))
    in_place = pl.BlockSpec(memory_space=pl.ANY)
    return pl.pallas_call(
        functools.partial(_attn_sample_kernel, n_batch=n_batch, t_new=t_new, group=group),
        grid=(n_batch // group,),
        in_specs=[chunked, _full((t_rows, KV_W)), _full((t_rows, KV_W)), chunked,
                  window, window, _layer_spec(sink_rows, l), in_place, in_place],
        out_specs=[chunked, window, window],
        out_shape=[jax.ShapeDtypeStruct((N_CHUNKS, t_rows, LANES), F32),
                   jax.ShapeDtypeStruct(new_kt.shape, F32), jax.ShapeDtypeStruct(new_vt.shape, F32)],
        input_output_aliases={7: 1, 8: 2},
        compiler_params=_params(("arbitrary",)),
        name="attention_sample",
    )(q, k, v, za, cache_kt, cache_vt, sink_rows, new_kt, new_vt)


def _ssm_stages(u_ref, zs_ref, h0re_ref, h0im_ref, are_ref, aim_ref, wb_ref, wc_ref, d_ref, wg_ref, bg_ref,
                o_ref, hre_ref, him_ref, x_scr, n_batch, sub_rows, slab, h0_transposed=False):
    step = pl.program_id(0)

    @pl.when(step == 0)
    def _():
        hre_ref[...] = h0re_ref[...].T if h0_transposed else h0re_ref[...]
        him_ref[...] = h0im_ref[...].T if h0_transposed else h0im_ref[...]

    rows = u_ref.shape[0]
    half_in = SSM_W // 2
    n_sub = rows // sub_rows
    n_tiles = n_batch // SUBLANES
    t_steps = sub_rows // n_batch
    n_buf = x_scr.shape[0]
    items = [(sc, s) for sc in range(n_sub) for s in range(2)]

    def expand(i):
        sc, s = items[i]
        ub = u_ref[sc * sub_rows:(sc + 1) * sub_rows, s * half_in:(s + 1) * half_in].astype(BF16)
        x_scr[i % n_buf] = jnp.dot(ub, wb_ref[s], preferred_element_type=F32)

    def recur(i):
        sc, s = items[i]
        buf = i % n_buf
        for jb in range(HALF_STATE // slab):
            n_lo = s * HALF_STATE + jb * slab
            ar = jnp.broadcast_to(are_ref[:, n_lo:n_lo + slab], (SUBLANES, slab))
            ai = jnp.broadcast_to(aim_ref[:, n_lo:n_lo + slab], (SUBLANES, slab))
            re = slice(jb * slab, (jb + 1) * slab)
            im = slice(HALF_STATE + jb * slab, HALF_STATE + (jb + 1) * slab)
            for tile in range(n_tiles):
                r0 = tile * SUBLANES
                hr = hre_ref[r0:r0 + SUBLANES, n_lo:n_lo + slab]
                hi = him_ref[r0:r0 + SUBLANES, n_lo:n_lo + slab]
                for t in range(t_steps):
                    row = slice(t * n_batch + r0, t * n_batch + r0 + SUBLANES)
                    hr, hi = (ar * hr - ai * hi + x_scr[buf, row, re], ar * hi + ai * hr + x_scr[buf, row, im])
                    x_scr[buf, row, re] = hr
                    x_scr[buf, row, im] = hi
                hre_ref[r0:r0 + SUBLANES, n_lo:n_lo + slab] = hr
                him_ref[r0:r0 + SUBLANES, n_lo:n_lo + slab] = hi

    y_halves = {}

    def contract(i):
        sc, s = items[i]
        y_halves[s] = jnp.dot(x_scr[i % n_buf].astype(BF16), wc_ref[s], preferred_element_type=F32)
        if s == 1:
            r = slice(sc * sub_rows, (sc + 1) * sub_rows)
            y = jnp.concatenate([y_halves[0], y_halves[1]], axis=1) + d_ref[...] * u_ref[r, :]
            y = jax.nn.gelu(y)
            gate = jax.nn.sigmoid(jnp.dot(y.astype(BF16), wg_ref[...], preferred_element_type=F32) + bg_ref[...])
            o_ref[r, :] = (y * gate * zs_ref[r, :]).astype(o_ref.dtype)

    return len(items), expand, recur, contract


def _ssm_kernel(*refs, n_batch, sub_rows, slab):
    inputs, (o_ref, hre_t_ref, him_t_ref, x_scr, hre_ref, him_ref) = refs[:N_SSM_INPUTS], refs[N_SSM_INPUTS:]
    n_items, expand, recur, contract = _ssm_stages(*inputs, o_ref, hre_ref, him_ref, x_scr, n_batch, sub_rows, slab,
                                                   h0_transposed=True)
    expand(0)
    for i in range(n_items + 1):
        if i + 1 < n_items:
            expand(i + 1)
        if i < n_items:
            recur(i)
        if i >= 1:
            contract(i - 1)

    @pl.when(pl.program_id(0) == pl.num_programs(0) - 1)
    def _():
        hre_t_ref[...] = hre_ref[...].T
        him_t_ref[...] = him_ref[...].T


N_ATTN_INPUTS = 7
N_SSM_INPUTS = 11


def _prompt_branches_kernel(*refs, n_batch, blk, sub_rows, slab):
    n_in = N_ATTN_INPUTS + N_SSM_INPUTS
    attn_refs = refs[:N_ATTN_INPUTS] + refs[n_in:n_in + 1]
    ssm_refs = refs[N_ATTN_INPUTS:n_in] + refs[n_in + 1:]
    score_stage, value_stage = _prompt_attention_stages(*attn_refs, n_batch, blk)
    n_items, expand, recur, contract = _ssm_stages(*ssm_refs, n_batch, sub_rows, slab)
    per_item = n_batch // n_items
    assert per_item * n_items == n_batch
    seqs_of = lambda i: range(i * per_item, (i + 1) * per_item)
    expand(0)
    scores = [score_stage(b) for b in seqs_of(0)]
    for i in range(n_items + 1):
        if i < n_items:
            recur(i)
        if i + 1 < n_items:
            expand(i + 1)
        if i < n_items:
            for b, sts in zip(seqs_of(i), scores):
                value_stage(b, sts)
        if i + 1 < n_items:
            scores = [score_stage(b) for b in seqs_of(i + 1)]
        if i >= 1:
            contract(i - 1)


def _ssm_call_parts(u, h0_re, h0_im, abar_re, abar_im, w_b, w_c, d, w_glu, b_glu, l, l_state, n_batch, r):
    t_rows = u.shape[0]
    sub_rows = max(SSM_SUB_ROWS, n_batch)
    rows = pl.BlockSpec((r, SSM_W), lambda i: (i, 0))
    state = _full((n_batch, N_STATE))
    layer = lambda a: _layer_spec(a, l)
    in_specs = [rows, rows, _layer_spec(h0_re, l_state), _layer_spec(h0_im, l_state), layer(abar_re), layer(abar_im),
                layer(w_b), layer(w_c), layer(d), layer(w_glu), layer(b_glu)]
    out_shape = [jax.ShapeDtypeStruct((t_rows, SSM_W), BF16),
                 jax.ShapeDtypeStruct((n_batch, N_STATE), F32), jax.ShapeDtypeStruct((n_batch, N_STATE), F32)]
    scratch = [pltpu.VMEM((SSM_BUFFERS, sub_rows, 2 * HALF_STATE), F32)]
    return in_specs, [rows, state, state], out_shape, scratch, sub_rows


def ssm_branch(u, zs, h0_re, h0_im, abar_re, abar_im, w_b, w_c, d, w_glu, b_glu, l, l_state, n_batch, t_chunk):
    r = t_chunk * n_batch
    in_specs, out_specs, out_shape, scratch, sub_rows = _ssm_call_parts(
        u, h0_re, h0_im, abar_re, abar_im, w_b, w_c, d, w_glu, b_glu, l, l_state, n_batch, r)
    out_specs = out_specs[:1] + [_full((N_STATE, n_batch))] * 2
    out_shape = out_shape[:1] + [jax.ShapeDtypeStruct((N_STATE, n_batch), F32)] * 2
    scratch = scratch + [pltpu.VMEM((n_batch, N_STATE), F32)] * 2
    return pl.pallas_call(
        functools.partial(_ssm_kernel, n_batch=n_batch, sub_rows=sub_rows, slab=SSM_SLAB),
        grid=(u.shape[0] // r,),
        in_specs=in_specs,
        out_specs=out_specs,
        out_shape=out_shape,
        scratch_shapes=scratch,
        compiler_params=_params(("arbitrary",)),
        name="ssm_branch",
    )(u, zs, h0_re, h0_im, abar_re, abar_im, w_b, w_c, d, w_glu, b_glu)


def prompt_branches(q, k, v, za, sinks, u, zs, h0_re, h0_im, abar_re, abar_im, w_b, w_c, d, w_glu, b_glu,
                    l, l_state, n_batch, blk):
    t_rows = k.shape[0]
    r = blk * n_batch
    cur = lambda w: pl.BlockSpec((r, w), lambda i: (i, 0))
    prev = lambda w: pl.BlockSpec((r, w), lambda i: (jnp.maximum(i - 1, 0), 0))
    chunked = pl.BlockSpec((N_CHUNKS, r, LANES), lambda i: (0, i, 0))
    ssm_in, ssm_out, ssm_shape, scratch, sub_rows = _ssm_call_parts(
        u, h0_re, h0_im, abar_re, abar_im, w_b, w_c, d, w_glu, b_glu, l, l_state, n_batch, r)
    attn_in = [chunked, prev(KV_W), cur(KV_W), prev(KV_W), cur(KV_W), chunked, _layer_spec(sinks, l)]
    assert len(attn_in) == N_ATTN_INPUTS and len(ssm_in) == N_SSM_INPUTS
    return pl.pallas_call(
        functools.partial(_prompt_branches_kernel, n_batch=n_batch, blk=blk, sub_rows=sub_rows, slab=SSM_SLAB),
        grid=(t_rows // r,),
        in_specs=attn_in + ssm_in,
        out_specs=[chunked] + ssm_out,
        out_shape=[jax.ShapeDtypeStruct((N_CHUNKS, t_rows, LANES), F32)] + ssm_shape,
        scratch_shapes=scratch,
        compiler_params=_params(("arbitrary",)),
        name="prompt_branches",
    )(q, k, k, v, v, za, sinks, u, zs, h0_re, h0_im, abar_re, abar_im, w_b, w_c, d, w_glu, b_glu)


def _rope_tables(pos):
    half = ROT_DIM // 2
    inv = ROPE_THETA ** (-jnp.arange(half, dtype=F32) / half)
    ang = pos.astype(F32)[:, None] * inv[None, :]
    d = np.arange(LANES) % HEAD_DIM
    idx = d % half
    cos = jnp.where(d < ROT_DIM, jnp.cos(ang)[:, idx], 1.0)
    sin = jnp.sin(ang)[:, idx]
    sm = jnp.where(d < half, -sin, jnp.where(d < ROT_DIM, sin, 0.0))
    return cos, sm


def _keep_diagonal_blocks(w, row_block, col_block):
    rows, cols = w.shape[-2:]
    keep = (np.arange(rows)[:, None] // row_block) == (np.arange(cols)[None, :] // col_block)
    return jnp.where(keep, w, 0.0)


def _expand_weights(bb):
    depth = bb.shape[0]
    hg = N_SSM_GROUPS // 2
    halves = bb.reshape(depth, SSM_GROUP_CH, 2, HALF_STATE).transpose(0, 2, 1, 3)
    tiled = jnp.broadcast_to(halves[:, :, None], (depth, 2, hg, SSM_GROUP_CH, HALF_STATE))
    return _keep_diagonal_blocks(tiled.reshape(depth, 2, hg * SSM_GROUP_CH, HALF_STATE), SSM_GROUP_CH, SSM_STATE)


def _contract_weights(c):
    depth = c.shape[0]
    hg = N_SSM_GROUPS // 2
    rows = c.reshape(depth, 2, hg, SSM_GROUP_CH, SSM_STATE).transpose(0, 1, 2, 4, 3).reshape(
        depth, 2, HALF_STATE, SSM_GROUP_CH)
    return _keep_diagonal_blocks(jnp.tile(rows, (1, 1, 1, hg)), SSM_STATE, SSM_GROUP_CH)


def _prepare_weights(w_in, w_out, attn_sinks, bb_re, bb_im, ssm_c_re, ssm_c_im, t_new):
    depth = w_in.shape[0]
    assert HEAD_PERM == tuple(kv * Q_GROUP + g for g in range(Q_GROUP) for kv in range(N_KV_HEADS))
    w_in_p = w_in.astype(BF16)
    w_out_p = w_out.astype(BF16)
    sinks = jnp.swapaxes(attn_sinks.reshape(depth, N_KV_HEADS, Q_GROUP), 1, 2).reshape(depth, N_Q_HEADS)
    sink_lanes = jnp.broadcast_to(sinks[:, :, None], (depth, N_Q_HEADS, LANES))
    sink_rows = jnp.broadcast_to(jnp.repeat(sinks, t_new, axis=1)[:, :, None], (depth, N_Q_HEADS * t_new, LANES))
    w_b = jnp.concatenate([_expand_weights(bb_re), _expand_weights(bb_im)], axis=3).astype(BF16)
    w_c = jnp.concatenate([_contract_weights(ssm_c_re), -_contract_weights(ssm_c_im)], axis=2).astype(BF16)
    return w_in_p, w_out_p, sink_lanes, sink_rows, w_b, w_c


def kernel(x_prompt, x_sample, cache_k, cache_v, state_ssm_re, state_ssm_im, c_prompt, c_sample, norm_g, w_ada, b_ada, w_in, attn_sinks, ssm_a_re, ssm_a_im, ssm_log_dt, ssm_b_re, ssm_b_im, ssm_c_re, ssm_c_im, ssm_d, w_glu, b_glu, w_out, final_g):
    depth = w_in.shape[0]
    nb_p, seq, _ = x_prompt.shape
    nb_s, t_new, _ = x_sample.shape
    win = cache_k.shape[2]
    blk = WINDOW

    sample = Group(n_batch=nb_s, mod_block=0)
    prompt = Group(n_batch=nb_p, mod_block=nb_s // nb_p)
    mod = adaln_mod(jnp.concatenate([c_sample, c_prompt], axis=0), w_ada, b_ada)
    abar_re, abar_im, bb_re, bb_im = ssm_discretise(ssm_a_re, ssm_a_im, ssm_log_dt, ssm_b_re, ssm_b_im)
    w_in_p, w_out_p, sink_lanes, sink_rows, w_b, w_c = _prepare_weights(
        w_in, w_out, attn_sinks, bb_re, bb_im, ssm_c_re, ssm_c_im, t_new)
    norm_g3 = norm_g.reshape(depth, 1, D_MODEL)
    d3 = ssm_d.reshape(depth, 1, SSM_W)
    w_glu_b = w_glu.astype(BF16)
    b_glu3 = b_glu.reshape(depth, 1, SSM_W)
    final_g2 = final_g.reshape(1, D_MODEL)
    ssm_weights = (abar_re, abar_im, w_b, w_c, d3, w_glu_b, b_glu3)

    cos_p, sm_p = _rope_tables(jnp.arange(seq))
    cos_s, sm_s = _rope_tables(PAST_LEN + jnp.arange(t_new))

    xp = x_prompt
    xs = x_sample.transpose(1, 0, 2).reshape(t_new * nb_s, D_MODEL)
    zeros_state = jnp.zeros((1, nb_p, N_STATE), F32)
    transposed = lambda c: c.transpose(0, 1, 3, 4, 2).reshape(depth, nb_s, KV_W, win)
    cache_kt, cache_vt = transposed(cache_k), transposed(cache_v)
    new_kt, new_vt = jnp.zeros_like(cache_kt), jnp.zeros_like(cache_vt)
    state_re3 = state_ssm_re.transpose(0, 2, 3, 1).reshape(depth, N_STATE, nb_s)
    state_im3 = state_ssm_im.transpose(0, 2, 3, 1).reshape(depth, N_STATE, nb_s)

    outs = {k: [] for k in ("kp", "vp", "rp", "ip", "rs", "is")}
    rows_p, rows_s = BOUNDARY_ROWS, t_new * nb_s
    opening = lambda l, cos_t, sm_t: (norm_g3, w_in_p, cos_t, sm_t, l)
    xp, proj_p = layer_boundary(xp, mod, prompt, rows_p, open_=opening(0, cos_p, sm_p))
    xs, proj_s = layer_boundary(xs, mod, sample, rows_s, open_=opening(0, cos_s, sm_s))
    for l in range(depth):
        last = l == depth - 1

        q, k, v, za, u, zs = proj_p
        ya, ys, hre, him = prompt_branches(q, k, v, za, sink_lanes, u, zs, zeros_state, zeros_state, *ssm_weights,
                                           l, 0, nb_p, blk)
        xp, proj_p = layer_boundary(xp, mod, prompt, rows_p, close=(ya, ys, w_out_p, l),
                                    open_=None if last else opening(l + 1, cos_p, sm_p),
                                    final_g=final_g2, batch_major_out=last)
        outs["kp"].append(k[(seq - win) * nb_p:])
        outs["vp"].append(v[(seq - win) * nb_p:])
        outs["rp"].append(hre)
        outs["ip"].append(him)

        q, k, v, za, u, zs = proj_s
        ya, new_kt, new_vt = attention_sample(q, k, v, za, cache_kt, cache_vt, sink_rows, new_kt, new_vt, l,
                                              nb_s, t_new, SAMPLE_GROUP)
        ys, hre, him = ssm_branch(u, zs, state_re3, state_im3, *ssm_weights, l, l, nb_s, t_new)
        xs, proj_s = layer_boundary(xs, mod, sample, rows_s, close=(ya, ys, w_out_p, l),
                                    open_=None if last else opening(l + 1, cos_s, sm_s), final_g=final_g2)
        outs["rs"].append(hre)
        outs["is"].append(him)

    y_prompt = xp
    y_sample = xs.reshape(t_new, nb_s, D_MODEL).transpose(1, 0, 2)
    st = lambda key: jnp.stack(outs[key])
    window = lambda key: st(key).reshape(depth, win, nb_p, N_KV_HEADS, HEAD_DIM).transpose(0, 2, 1, 3, 4)
    state = lambda key, nb: st(key).reshape(depth, nb, N_SSM_GROUPS, SSM_STATE)
    state_t = lambda key: st(key).reshape(depth, N_SSM_GROUPS, SSM_STATE, nb_s).transpose(0, 3, 1, 2)
    untransposed = lambda w: w.reshape(depth, nb_s, N_KV_HEADS, HEAD_DIM, win).transpose(0, 1, 4, 2, 3)
    return (y_prompt, y_sample, window("kp"), window("vp"), state("rp", nb_p), state("ip", nb_p),
            untransposed(new_kt), untransposed(new_vt), state_t("rs"), state_t("is"))
```

```python
import functools
from typing import NamedTuple

import jax
import jax.numpy as jnp
import numpy as np
from jax import lax
from jax.experimental import pallas as pl
from jax.experimental.pallas import tpu as pltpu

F32 = jnp.float32
BF16 = jnp.bfloat16

D_MODEL = 1024
HEAD_DIM = 64
ATTN_W = 512
N_Q_HEADS = 8
N_KV_HEADS = 2
Q_GROUP = N_Q_HEADS // N_KV_HEADS
KV_W = 128
WINDOW = 128
ROT_DIM = 16
ROPE_THETA = 500000.0
SSM_W = 512
SSM_GROUP_CH = 16
N_SSM_GROUPS = 32
SSM_STATE = 64
N_STATE = N_SSM_GROUPS * SSM_STATE
COL_Q, COL_KV, COL_ZA = 0, ATTN_W, ATTN_W + 2 * KV_W
COL_U, COL_ZS = COL_ZA + ATTN_W, COL_ZA + ATTN_W + SSM_W
HALF_STATE = N_STATE // 2
EPS = 1e-6
NEG_INF = -1e30
LOG2_E = 1.4426950408889634
PAST_LEN = 8192

MOD_PARTS = 3
LANES = 128
N_CHUNKS = ATTN_W // LANES
BOUNDARY_ROWS = 1024
BOUNDARY_PARTS = 2
SSM_SLAB = 4 * LANES
SAMPLE_GROUP = 32
SSM_SUB_ROWS = 256
SSM_BUFFERS = 3
SUBLANES = 8
VMEM_LIMIT = 56 * 1024 * 1024

HEAD_PERM = (0, 4, 1, 5, 2, 6, 3, 7)


class Group(NamedTuple):
    n_batch: int
    mod_block: int


def _params(sem):
    return pltpu.CompilerParams(dimension_semantics=sem, vmem_limit_bytes=VMEM_LIMIT)


def _full(shape):
    return pl.BlockSpec(shape, lambda *_: (0,) * len(shape))


def _mod_kernel(c_ref, w_ref, b_ref, o_ref):
    c = c_ref[...]
    a = (c * jax.nn.sigmoid(c)).astype(BF16)
    o_ref[...] = jnp.dot(a, w_ref[...].astype(BF16), preferred_element_type=F32) + b_ref[...]


def adaln_mod(c_all, w_ada, b_ada):
    depth = w_ada.shape[0]
    n = c_all.shape[0]
    return pl.pallas_call(
        _mod_kernel,
        grid=(depth, MOD_PARTS),
        in_specs=[
            pl.BlockSpec((n, D_MODEL), lambda l, j: (0, 0)),
            pl.BlockSpec((None, D_MODEL, D_MODEL), lambda l, j: (l, 0, j)),
            pl.BlockSpec((None, 1, D_MODEL), lambda l, j: (l, 0, j)),
        ],
        out_specs=pl.BlockSpec((None, n, D_MODEL), lambda l, j: (l, 0, j)),
        out_shape=jax.ShapeDtypeStruct((depth, n, 3 * D_MODEL), F32),
        compiler_params=_params(("arbitrary", "arbitrary")),
        name="adaln_mod",
    )(c_all, w_ada, b_ada.reshape(depth, 1, 3 * D_MODEL))


def _disc_kernel(are_ref, aim_ref, ldt_ref, bre_ref, bim_ref, abre_ref, abim_ref, bbre_ref, bbim_ref):
    a_re = are_ref[...]
    a_im = aim_ref[...]
    dt = jnp.exp(ldt_ref[...])
    mag = jnp.exp(a_re * dt)
    abar_re = mag * jnp.cos(a_im * dt)
    abar_im = mag * jnp.sin(a_im * dt)
    den = a_re * a_re + a_im * a_im
    nr = abar_re - 1.0
    coef_re = (nr * a_re + abar_im * a_im) / den
    coef_im = (abar_im * a_re - nr * a_im) / den
    br = bre_ref[...]
    bi = bim_ref[...]
    abre_ref[...] = abar_re
    abim_ref[...] = abar_im
    bbre_ref[...] = coef_re * br - coef_im * bi
    bbim_ref[...] = coef_re * bi + coef_im * br


def ssm_discretise(a_re, a_im, log_dt, b_re, b_im):
    depth = a_re.shape[0]
    a_re = a_re.reshape(depth, 1, N_STATE)
    a_im = a_im.reshape(depth, 1, N_STATE)
    ldt = jnp.repeat(log_dt, SSM_STATE, axis=1).reshape(depth, 1, N_STATE)
    bt_re = b_re.reshape(depth, N_STATE, SSM_GROUP_CH).transpose(0, 2, 1)
    bt_im = b_im.reshape(depth, N_STATE, SSM_GROUP_CH).transpose(0, 2, 1)
    row = pl.BlockSpec((None, 1, N_STATE), lambda l: (l, 0, 0))
    mat = pl.BlockSpec((None, SSM_GROUP_CH, N_STATE), lambda l: (l, 0, 0))
    return pl.pallas_call(
        _disc_kernel,
        grid=(depth,),
        in_specs=[row, row, row, mat, mat],
        out_specs=[row, row, mat, mat],
        out_shape=[jax.ShapeDtypeStruct((depth, 1, N_STATE), F32)] * 2
        + [jax.ShapeDtypeStruct((depth, SSM_GROUP_CH, N_STATE), F32)] * 2,
        compiler_params=_params(("arbitrary",)),
        name="ssm_discretise",
    )(a_re, a_im, ldt, bt_re, bt_im)


def _rope(x, cos, sm):
    lane = lax.broadcasted_iota(jnp.int32, x.shape, 1) % HEAD_DIM
    partner = jnp.where(lane < ROT_DIM // 2, pltpu.roll(x, LANES - ROT_DIM // 2, 1), pltpu.roll(x, ROT_DIM // 2, 1))
    return x * cos + partner * sm


def _to_time_major(x_ref):
    n_batch, steps, width = x_ref.shape
    return jnp.swapaxes(x_ref[...], 0, 1).reshape(steps * n_batch, width)


def _from_time_major(x, o_ref, n_batch):
    rows, width = x.shape
    o_ref[...] = jnp.swapaxes(x.reshape(rows // n_batch, n_batch, width), 0, 1)


def _rms(x, g_ref):
    ms = jnp.mean(x * x, axis=-1, keepdims=True)
    return x * lax.rsqrt(ms + EPS) * g_ref[...]


def _per_sequence(y, n_batch, fn):
    rows = y.shape[0]
    return fn(y.reshape(rows // n_batch, n_batch, D_MODEL)).reshape(rows, D_MODEL)


def _pair_heads(chunks):
    low = lax.broadcasted_iota(jnp.int32, chunks[0].shape, 1) < HEAD_DIM
    swap = lambda x: pltpu.roll(x, HEAD_DIM, 1)
    n0, n1, n2, n3 = chunks
    return [jnp.where(low, n0, swap(n2)), jnp.where(low, swap(n0), n2),
            jnp.where(low, n1, swap(n3)), jnp.where(low, swap(n1), n3)]


def _unpair_heads(chunks):
    low = lax.broadcasted_iota(jnp.int32, chunks[0].shape, 1) < HEAD_DIM
    swap = lambda x: pltpu.roll(x, HEAD_DIM, 1)
    p0, p1, p2, p3 = chunks
    return [jnp.where(low, p0, swap(p1)), jnp.where(low, p2, swap(p3)),
            jnp.where(low, swap(p0), p1), jnp.where(low, swap(p2), p3)]


def _residual_update(x, rows, ya_ref, ys_ref, gate_ref, w_ref, n_batch):
    ya = _unpair_heads([ya_ref[c, rows, :] for c in range(N_CHUNKS)])
    mixed = jnp.concatenate([y.astype(BF16) for y in ya] + [ys_ref[rows, :]], axis=1)
    mix = jnp.dot(mixed, w_ref[...], preferred_element_type=F32)
    return x + _per_sequence(mix, n_batch, lambda m: m * gate_ref[...][None])


def _modulated_norm(x, g_ref, scale_ref, shift_ref, n_batch):
    y = _per_sequence(_rms(x, g_ref), n_batch, lambda y3: y3 * (1.0 + scale_ref[...])[None] + shift_ref[...][None])
    return y.astype(BF16)


def _silu(z):
    return z * jax.nn.sigmoid(z)


def _project(h, rows, w_ref, rope_scr, q_ref, k_ref, v_ref, ga_ref, u_ref, gs_ref):
    cos = rope_scr[0, rows, :]
    sm = rope_scr[1, rows, :]

    def proj(lo, width):
        return jnp.dot(h, w_ref[:, lo:lo + width], preferred_element_type=F32)

    chunks_of = lambda a: [a[:, c * LANES:(c + 1) * LANES] for c in range(N_CHUNKS)]
    q = _pair_heads(chunks_of(proj(COL_Q, ATTN_W)))
    ga = _pair_heads(chunks_of(_silu(proj(COL_ZA, ATTN_W))))
    for c in range(N_CHUNKS):
        q_ref[c, rows, :] = _rope(q[c], cos, sm)
        ga_ref[c, rows, :] = ga[c]
    u_ref[rows, :] = proj(COL_U, SSM_W)
    gs_ref[rows, :] = _silu(proj(COL_ZS, SSM_W))
    kv = proj(COL_KV, 2 * KV_W)
    k_ref[rows, :] = _rope(kv[:, :KV_W], cos, sm)
    v_ref[rows, :] = kv[:, KV_W:]


def _boundary_kernel(*refs, n_batch, closes, opens, batch_major, block_rows, n_parts):
    refs = list(refs)
    x_ref = refs.pop(0)
    close_refs = [refs.pop(0) for _ in range(4)] if closes else None
    final_g_ref = refs.pop(0) if closes and not opens else None
    open_refs = [refs.pop(0) for _ in range(6)] if opens else None
    x_out_ref = refs.pop(0) if closes or batch_major else None
    proj_out_refs = [refs.pop(0) for _ in range(6)] if opens else None
    rope_scr = refs.pop(0) if opens else None

    if opens:
        g_ref, scale_ref, shift_ref, w_in_ref, cos_ref, sm_ref = open_refs
        steps = cos_ref.shape[0]
        for b in range(n_batch):
            rope_scr[0, pl.ds(b, steps, stride=n_batch), :] = cos_ref[...]
            rope_scr[1, pl.ds(b, steps, stride=n_batch), :] = sm_ref[...]

    part = block_rows // n_parts
    groups = [slice(i * part, (i + 1) * part) for i in range(n_parts)]
    x_full = _to_time_major(x_ref) if (batch_major and not closes) else None

    def updated(rows):
        x = x_full[rows, :] if x_full is not None else x_ref[rows, :]
        return _residual_update(x, rows, *close_refs, n_batch) if closes else x

    def normed(x, rows):
        if opens:
            if x_out_ref is not None:
                x_out_ref[rows, :] = x
            return _modulated_norm(x, g_ref, scale_ref, shift_ref, n_batch)
        return _rms(x, final_g_ref)

    xs = [updated(rows) for rows in groups]
    hs = [None] * n_parts
    hs[0] = normed(xs[0], groups[0])
    for i, rows in enumerate(groups):
        if opens:
            _project(hs[i], rows, w_in_ref, rope_scr, *proj_out_refs)
        if i + 1 < n_parts:
            hs[i + 1] = normed(xs[i + 1], groups[i + 1])
    if not opens:
        y = jnp.concatenate(hs, axis=0)
        if batch_major:
            _from_time_major(y, x_out_ref, n_batch)
        else:
            x_out_ref[...] = y


def _layer_spec(arr, l):
    tail = arr.shape[1:]
    return pl.BlockSpec((None,) + tail, lambda *_: (l,) + (0,) * len(tail))


def _mod_spec(l, group, part):
    return pl.BlockSpec((None, group.n_batch, D_MODEL), lambda *_: (l, group.mod_block, part))


def layer_boundary(x, mod, group, block_rows, close=None, open_=None, final_g=None, batch_major_out=False):
    closes, opens = close is not None, open_ is not None
    batch_major_in = x.ndim == 3
    assert not (batch_major_in and closes) and not (batch_major_out and opens)
    batch_major = batch_major_in or batch_major_out
    t_rows = x.shape[0] * x.shape[1] if batch_major_in else x.shape[0]
    rows = lambda w: pl.BlockSpec((block_rows, w), lambda i: (i, 0))
    chunked = pl.BlockSpec((N_CHUNKS, block_rows, LANES), lambda i: (0, i, 0))
    blocked3 = pl.BlockSpec((group.n_batch, block_rows // group.n_batch, D_MODEL), lambda i: (0, i, 0))
    flat = lambda w: jax.ShapeDtypeStruct((t_rows, w), F32)
    chunked_shape = jax.ShapeDtypeStruct((N_CHUNKS, t_rows, LANES), F32)

    args, in_specs, out_specs, out_shape = [x], [blocked3 if batch_major_in else rows(D_MODEL)], [], []
    if closes:
        ya, ys, w_out, l = close
        args += [ya, ys, mod, w_out]
        in_specs += [chunked, rows(SSM_W), _mod_spec(l, group, 2), _layer_spec(w_out, l)]
        if not opens:
            args.append(final_g)
            in_specs.append(_full(final_g.shape))
    if opens:
        norm_g, w_in, cos_t, sm_t, l = open_
        args += [norm_g, mod, mod, w_in, cos_t, sm_t]
        per_step = pl.BlockSpec((block_rows // group.n_batch, LANES), lambda i: (i, 0))
        in_specs += [_layer_spec(norm_g, l), _mod_spec(l, group, 1), _mod_spec(l, group, 0), _layer_spec(w_in, l),
                     per_step, per_step]
    if closes or batch_major_in:
        out_specs.append(blocked3 if batch_major_out else rows(D_MODEL))
        out_shape.append(jax.ShapeDtypeStruct((group.n_batch, t_rows // group.n_batch, D_MODEL), F32)
                         if batch_major_out else flat(D_MODEL))
    if opens:
        out_specs += [chunked, rows(KV_W), rows(KV_W), chunked, rows(SSM_W), rows(SSM_W)]
        out_shape += [chunked_shape, flat(KV_W), flat(KV_W), chunked_shape, flat(SSM_W), flat(SSM_W)]
    scratch = [pltpu.VMEM((2, block_rows, LANES), F32)] if opens else []
    outs = pl.pallas_call(
        functools.partial(_boundary_kernel, n_batch=group.n_batch, closes=closes, opens=opens,
                          batch_major=batch_major, block_rows=block_rows, n_parts=BOUNDARY_PARTS),
        grid=(t_rows // block_rows,),
        in_specs=in_specs,
        out_specs=out_specs,
        out_shape=out_shape,
        scratch_shapes=scratch,
        compiler_params=_params(("parallel",)),
        name="layer_boundary",
    )(*args)
    x_rows = outs[0] if (closes or batch_major_in) else x
    return x_rows, (tuple(outs[-6:]) if opens else None)


def _pad_queries(chunks):
    lane = lax.broadcasted_iota(jnp.int32, chunks[0].shape, 1)
    pieces = []
    for chunk in chunks:
        pieces.append(jnp.where(lane < HEAD_DIM, chunk, 0.0))
        pieces.append(jnp.where(lane >= HEAD_DIM, chunk, 0.0))
    return jnp.concatenate(pieces, axis=0).astype(BF16)


def _unpad_outputs(o, t):
    lane = lax.broadcasted_iota(jnp.int32, (t, LANES), 1)
    chunks = []
    for c in range(N_CHUNKS):
        lo = o[(2 * c) * t:(2 * c + 1) * t]
        hi = o[(2 * c + 1) * t:(2 * c + 2) * t]
        chunks.append(jnp.where(lane < HEAD_DIM, lo, hi))
    return chunks


def _gated_store(o_ref, ga_ref, sel, ya_chunks):
    for c, ya in enumerate(ya_chunks):
        o_ref[c, sel, :] = ya * ga_ref[c, sel, :]


def _scores(qp, k, scale=HEAD_DIM ** -0.5):
    return lax.dot_general(qp, k.astype(BF16), (((1,), (1,)), ((), ())), preferred_element_type=F32) * scale


def _prompt_attention_stages(q_ref, kp_ref, kc_ref, vp_ref, vc_ref, za_ref, sink_ref, o_ref, n_batch, blk):
    assert blk == WINDOW
    n = pl.program_id(0)
    qb = blk // 2
    keys = blk + qb
    cols = 2 * LANES
    j = lax.broadcasted_iota(jnp.int32, (keys, cols), 0)
    tq = lax.broadcasted_iota(jnp.int32, (keys, cols), 1) % qb
    masks = [(j >= jnp.maximum(tq, jnp.where(n == 0, blk - h * qb, 0))) & (j <= tq + WINDOW) for h in range(2)]
    low = lax.broadcasted_iota(jnp.int32, (qb, LANES), 1) < HEAD_DIM
    ones = jnp.ones((SUBLANES, keys), F32)
    units = [(h, pr) for h in range(2) for pr in range(N_CHUNKS // 2)]
    sinks = [jnp.concatenate([sink_ref[slot:slot + 1, :qb] for slot in range(4 * pr, 4 * pr + 4)], axis=1) * LOG2_E
             for pr in range(N_CHUNKS // 2)]

    def query_rows(b, h):
        return pl.ds(b + h * qb * n_batch, qb, stride=n_batch)

    def window(prev_ref, cur_ref, b, h):
        if h == 0:
            return jnp.concatenate([prev_ref[pl.ds(b, blk, stride=n_batch), :], cur_ref[query_rows(b, 0), :]], axis=0)
        return jnp.concatenate([prev_ref[query_rows(b, 1), :], cur_ref[pl.ds(b, blk, stride=n_batch), :]], axis=0)

    def score_stage(b):
        sts = []
        for h in range(2):
            k = window(kp_ref, kc_ref, b, h).astype(BF16)
            for pr in range(N_CHUNKS // 2):
                pieces = []
                for c in (2 * pr, 2 * pr + 1):
                    qc = q_ref[c, query_rows(b, h), :]
                    pieces += [jnp.where(low, qc, 0.0), jnp.where(low, 0.0, qc)]
                qp = jnp.concatenate(pieces, axis=0).astype(BF16)
                sts.append(_scores(k, qp, HEAD_DIM ** -0.5 * LOG2_E))
        return sts

    def value_stage(b, sts):
        vt_aug = [jnp.concatenate([window(vp_ref, vc_ref, b, h).T, ones], axis=0).astype(BF16) for h in range(2)]

        def finish(h, pr, ot, m):
            den = ot[KV_W:KV_W + 1, :] + jnp.exp2(sinks[pr] - m)
            o = (ot[:KV_W, :] * (1.0 / den)).T
            sel = query_rows(b, h)
            for i, c in enumerate((2 * pr, 2 * pr + 1)):
                lo = 2 * i * qb
                ya = jnp.where(low, o[lo:lo + qb], o[lo + qb:lo + 2 * qb])
                o_ref[c, sel, :] = ya * za_ref[c, sel, :]

        pending = None
        for (h, pr), st in zip(units, sts):
            st = jnp.where(masks[h], st, NEG_INF)
            m = jnp.maximum(jnp.max(st, axis=0, keepdims=True), sinks[pr])
            p = jnp.exp2(st - m).astype(BF16)
            ot = jnp.dot(vt_aug[h], p, preferred_element_type=F32)
            if pending is not None:
                finish(*pending)
            pending = (h, pr, ot, m)
        finish(*pending)

    return score_stage, value_stage


def _attn_sample_kernel(q_ref, k_ref, v_ref, za_ref, ck_ref, cv_ref, sink_ref, nk_all_ref, nv_all_ref,
                        o_ref, nk_ref, nv_ref, *, n_batch, t_new, group):
    del nk_all_ref, nv_all_ref
    g = pl.program_id(0)
    win = ck_ref.shape[2]
    rows = N_Q_HEADS * t_new
    tq = lax.broadcasted_iota(jnp.int32, (rows, 2 * win), 0) % t_new
    j = lax.broadcasted_iota(jnp.int32, (rows, 2 * win), 1)
    dist_c = tq + win - j
    mask = ((j < win) & (dist_c >= 0) & (dist_c <= WINDOW)) | ((j >= win) & (j - win <= tq) & (j - win < t_new))
    sink = sink_ref[...]
    pad = jnp.zeros((win - t_new, KV_W), F32)
    ones_rows = jnp.ones((LANES, win), F32)
    ones_cols = jnp.ones((win, LANES), F32)
    kept = lax.broadcasted_iota(jnp.int32, (KV_W, win), 1) < win - t_new
    batch = range(group)
    sels = [pl.ds(g * group + i, t_new, stride=n_batch) for i in batch]

    scores = []
    for i in batch:
        qp = _pad_queries([q_ref[c, sels[i], :] for c in range(N_CHUNKS)])
        s_cache = jnp.dot(qp, ck_ref[i].astype(BF16), preferred_element_type=F32) * (HEAD_DIM ** -0.5)
        s_new = _scores(qp, jnp.concatenate([k_ref[sels[i], :], pad], axis=0))
        scores.append(jnp.where(mask, jnp.concatenate([s_cache, s_new], axis=1), NEG_INF))
    outs = []
    for i in batch:
        s = scores[i]
        m = jnp.maximum(jnp.max(s, axis=-1, keepdims=True), sink[:, :1])
        p = jnp.exp(s - m).astype(BF16)
        vt_aug = jnp.concatenate([cv_ref[i], ones_rows], axis=0).astype(BF16)
        v_new = jnp.concatenate([jnp.concatenate([v_ref[sels[i], :], pad], axis=0), ones_cols], axis=1).astype(BF16)
        o = (lax.dot_general(p[:, :win], vt_aug, (((1,), (1,)), ((), ())), preferred_element_type=F32)
             + jnp.dot(p[:, win:], v_new, preferred_element_type=F32))
        outs.append(o[:, :KV_W] / (o[:, KV_W:] + jnp.exp(sink - m)))
    for i in batch:
        _gated_store(o_ref, za_ref, sels[i], _unpad_outputs(outs[i], t_new))
        for new_ref, old_ref, rows_ref in ((nk_ref, ck_ref, k_ref), (nv_ref, cv_ref, v_ref)):
            appended = jnp.concatenate([pad, rows_ref[sels[i], :]], axis=0).T
            new_ref[i] = jnp.where(kept, pltpu.roll(old_ref[i], win - t_new, 1), appended)


def attention_sample(q, k, v, za, cache_kt, cache_vt, sink_rows, new_kt, new_vt, l, n_batch, t_new, group):
    t_rows = k.shape[0]
    win = cache_kt.shape[3]
    window = pl.BlockSpec((None, group, KV_W, win), lambda i: (l, i, 0, 0))
    chunked = _full((N_CHUNKS, t_rows, LANES))
    in_place = pl.BlockSpec(memory_space=pl.ANY)
    return pl.pallas_call(
        functools.partial(_attn_sample_kernel, n_batch=n_batch, t_new=t_new, group=group),
        grid=(n_batch // group,),
        in_specs=[chunked, _full((t_rows, KV_W)), _full((t_rows, KV_W)), chunked,
                  window, window, _layer_spec(sink_rows, l), in_place, in_place],
        out_specs=[chunked, window, window],
        out_shape=[jax.ShapeDtypeStruct((N_CHUNKS, t_rows, LANES), F32),
                   jax.ShapeDtypeStruct(new_kt.shape, F32), jax.ShapeDtypeStruct(new_vt.shape, F32)],
        input_output_aliases={7: 1, 8: 2},
        compiler_params=_params(("arbitrary",)),
        name="attention_sample",
    )(q, k, v, za, cache_kt, cache_vt, sink_rows, new_kt, new_vt)


def _ssm_stages(u_ref, zs_ref, h0re_ref, h0im_ref, are_ref, aim_ref, wb_ref, wc_ref, d_ref, wg_ref, bg_ref,
                o_ref, hre_ref, him_ref, x_scr, n_batch, sub_rows, slab, h0_transposed=False):
    step = pl.program_id(0)

    @pl.when(step == 0)
    def _():
        hre_ref[...] = h0re_ref[...].T if h0_transposed else h0re_ref[...]
        him_ref[...] = h0im_ref[...].T if h0_transposed else h0im_ref[...]

    rows = u_ref.shape[0]
    half_in = SSM_W // 2
    n_sub = rows // sub_rows
    n_tiles = n_batch // SUBLANES
    t_steps = sub_rows // n_batch
    n_buf = x_scr.shape[0]
    items = [(sc, s) for sc in range(n_sub) for s in range(2)]

    def expand(i):
        sc, s = items[i]
        ub = u_ref[sc * sub_rows:(sc + 1) * sub_rows, s * half_in:(s + 1) * half_in].astype(BF16)
        x_scr[i % n_buf] = jnp.dot(ub, wb_ref[s], preferred_element_type=F32)

    def recur(i):
        sc, s = items[i]
        buf = i % n_buf
        for jb in range(HALF_STATE // slab):
            n_lo = s * HALF_STATE + jb * slab
            ar = jnp.broadcast_to(are_ref[:, n_lo:n_lo + slab], (SUBLANES, slab))
            ai = jnp.broadcast_to(aim_ref[:, n_lo:n_lo + slab], (SUBLANES, slab))
            re = slice(jb * slab, (jb + 1) * slab)
            im = slice(HALF_STATE + jb * slab, HALF_STATE + (jb + 1) * slab)
            for tile in range(n_tiles):
                r0 = tile * SUBLANES
                hr = hre_ref[r0:r0 + SUBLANES, n_lo:n_lo + slab]
                hi = him_ref[r0:r0 + SUBLANES, n_lo:n_lo + slab]
                for t in range(t_steps):
                    row = slice(t * n_batch + r0, t * n_batch + r0 + SUBLANES)
                    hr, hi = (ar * hr - ai * hi + x_scr[buf, row, re], ar * hi + ai * hr + x_scr[buf, row, im])
                    x_scr[buf, row, re] = hr
                    x_scr[buf, row, im] = hi
                hre_ref[r0:r0 + SUBLANES, n_lo:n_lo + slab] = hr
                him_ref[r0:r0 + SUBLANES, n_lo:n_lo + slab] = hi

    y_halves = {}

    def contract(i):
        sc, s = items[i]
        y_halves[s] = jnp.dot(x_scr[i % n_buf].astype(BF16), wc_ref[s], preferred_element_type=F32)
        if s == 1:
            r = slice(sc * sub_rows, (sc + 1) * sub_rows)
            y = jnp.concatenate([y_halves[0], y_halves[1]], axis=1) + d_ref[...] * u_ref[r, :]
            y = jax.nn.gelu(y)
            gate = jax.nn.sigmoid(jnp.dot(y.astype(BF16), wg_ref[...], preferred_element_type=F32) + bg_ref[...])
            o_ref[r, :] = (y * gate * zs_ref[r, :]).astype(o_ref.dtype)

    return len(items), expand, recur, contract


def _ssm_kernel(*refs, n_batch, sub_rows, slab):
    inputs, (o_ref, hre_t_ref, him_t_ref, x_scr, hre_ref, him_ref) = refs[:N_SSM_INPUTS], refs[N_SSM_INPUTS:]
    n_items, expand, recur, contract = _ssm_stages(*inputs, o_ref, hre_ref, him_ref, x_scr, n_batch, sub_rows, slab,
                                                   h0_transposed=True)
    expand(0)
    for i in range(n_items + 1):
        if i + 1 < n_items:
            expand(i + 1)
        if i < n_items:
            recur(i)
        if i >= 1:
            contract(i - 1)

    @pl.when(pl.program_id(0) == pl.num_programs(0) - 1)
    def _():
        hre_t_ref[...] = hre_ref[...].T
        him_t_ref[...] = him_ref[...].T


N_ATTN_INPUTS = 7
N_SSM_INPUTS = 11


def _prompt_branches_kernel(*refs, n_batch, blk, sub_rows, slab):
    n_in = N_ATTN_INPUTS + N_SSM_INPUTS
    attn_refs = refs[:N_ATTN_INPUTS] + refs[n_in:n_in + 1]
    ssm_refs = refs[N_ATTN_INPUTS:n_in] + refs[n_in + 1:]
    score_stage, value_stage = _prompt_attention_stages(*attn_refs, n_batch, blk)
    n_items, expand, recur, contract = _ssm_stages(*ssm_refs, n_batch, sub_rows, slab)
    per_item = n_batch // n_items
    assert per_item * n_items == n_batch
    seqs_of = lambda i: range(i * per_item, (i + 1) * per_item)
    expand(0)
    scores = [score_stage(b) for b in seqs_of(0)]
    for i in range(n_items + 1):
        if i < n_items:
            recur(i)
        if i + 1 < n_items:
            expand(i + 1)
        if i < n_items:
            for b, sts in zip(seqs_of(i), scores):
                value_stage(b, sts)
        if i + 1 < n_items:
            scores = [score_stage(b) for b in seqs_of(i + 1)]
        if i >= 1:
            contract(i - 1)


def _ssm_call_parts(u, h0_re, h0_im, abar_re, abar_im, w_b, w_c, d, w_glu, b_glu, l, l_state, n_batch, r):
    t_rows = u.shape[0]
    sub_rows = max(SSM_SUB_ROWS, n_batch)
    rows = pl.BlockSpec((r, SSM_W), lambda i: (i, 0))
    state = _full((n_batch, N_STATE))
    layer = lambda a: _layer_spec(a, l)
    in_specs = [rows, rows, _layer_spec(h0_re, l_state), _layer_spec(h0_im, l_state), layer(abar_re), layer(abar_im),
                layer(w_b), layer(w_c), layer(d), layer(w_glu), layer(b_glu)]
    out_shape = [jax.ShapeDtypeStruct((t_rows, SSM_W), BF16),
                 jax.ShapeDtypeStruct((n_batch, N_STATE), F32), jax.ShapeDtypeStruct((n_batch, N_STATE), F32)]
    scratch = [pltpu.VMEM((SSM_BUFFERS, sub_rows, 2 * HALF_STATE), F32)]
    return in_specs, [rows, state, state], out_shape, scratch, sub_rows


def ssm_branch(u, zs, h0_re, h0_im, abar_re, abar_im, w_b, w_c, d, w_glu, b_glu, l, l_state, n_batch, t_chunk):
    r = t_chunk * n_batch
    in_specs, out_specs, out_shape, scratch, sub_rows = _ssm_call_parts(
        u, h0_re, h0_im, abar_re, abar_im, w_b, w_c, d, w_glu, b_glu, l, l_state, n_batch, r)
    out_specs = out_specs[:1] + [_full((N_STATE, n_batch))] * 2
    out_shape = out_shape[:1] + [jax.ShapeDtypeStruct((N_STATE, n_batch), F32)] * 2
    scratch = scratch + [pltpu.VMEM((n_batch, N_STATE), F32)] * 2
    return pl.pallas_call(
        functools.partial(_ssm_kernel, n_batch=n_batch, sub_rows=sub_rows, slab=SSM_SLAB),
        grid=(u.shape[0] // r,),
        in_specs=in_specs,
        out_specs=out_specs,
        out_shape=out_shape,
        scratch_shapes=scratch,
        compiler_params=_params(("arbitrary",)),
        name="ssm_branch",
    )(u, zs, h0_re, h0_im, abar_re, abar_im, w_b, w_c, d, w_glu, b_glu)


def prompt_branches(q, k, v, za, sinks, u, zs, h0_re, h0_im, abar_re, abar_im, w_b, w_c, d, w_glu, b_glu,
                    l, l_state, n_batch, blk):
    t_rows = k.shape[0]
    r = blk * n_batch
    cur = lambda w: pl.BlockSpec((r, w), lambda i: (i, 0))
    prev = lambda w: pl.BlockSpec((r, w), lambda i: (jnp.maximum(i - 1, 0), 0))
    chunked = pl.BlockSpec((N_CHUNKS, r, LANES), lambda i: (0, i, 0))
    ssm_in, ssm_out, ssm_shape, scratch, sub_rows = _ssm_call_parts(
        u, h0_re, h0_im, abar_re, abar_im, w_b, w_c, d, w_glu, b_glu, l, l_state, n_batch, r)
    attn_in = [chunked, prev(KV_W), cur(KV_W), prev(KV_W), cur(KV_W), chunked, _layer_spec(sinks, l)]
    assert len(attn_in) == N_ATTN_INPUTS and len(ssm_in) == N_SSM_INPUTS
    return pl.pallas_call(
        functools.partial(_prompt_branches_kernel, n_batch=n_batch, blk=blk, sub_rows=sub_rows, slab=SSM_SLAB),
        grid=(t_rows // r,),
        in_specs=attn_in + ssm_in,
        out_specs=[chunked] + ssm_out,
        out_shape=[jax.ShapeDtypeStruct((N_CHUNKS, t_rows, LANES), F32)] + ssm_shape,
        scratch_shapes=scratch,
        compiler_params=_params(("arbitrary",)),
        name="prompt_branches",
    )(q, k, k, v, v, za, sinks, u, zs, h0_re, h0_im, abar_re, abar_im, w_b, w_c, d, w_glu, b_glu)


def _rope_tables(pos):
    half = ROT_DIM // 2
    inv = ROPE_THETA ** (-jnp.arange(half, dtype=F32) / half)
    ang = pos.astype(F32)[:, None] * inv[None, :]
    d = np.arange(LANES) % HEAD_DIM
    idx = d % half
    cos = jnp.where(d < ROT_DIM, jnp.cos(ang)[:, idx], 1.0)
    sin = jnp.sin(ang)[:, idx]
    sm = jnp.where(d < half, -sin, jnp.where(d < ROT_DIM, sin, 0.0))
    return cos, sm


def _keep_diagonal_blocks(w, row_block, col_block):
    rows, cols = w.shape[-2:]
    keep = (np.arange(rows)[:, None] // row_block) == (np.arange(cols)[None, :] // col_block)
    return jnp.where(keep, w, 0.0)


def _expand_weights(bb):
    depth = bb.shape[0]
    hg = N_SSM_GROUPS // 2
    halves = bb.reshape(depth, SSM_GROUP_CH, 2, HALF_STATE).transpose(0, 2, 1, 3)
    tiled = jnp.broadcast_to(halves[:, :, None], (depth, 2, hg, SSM_GROUP_CH, HALF_STATE))
    return _keep_diagonal_blocks(tiled.reshape(depth, 2, hg * SSM_GROUP_CH, HALF_STATE), SSM_GROUP_CH, SSM_STATE)


def _contract_weights(c):
    depth = c.shape[0]
    hg = N_SSM_GROUPS // 2
    rows = c.reshape(depth, 2, hg, SSM_GROUP_CH, SSM_STATE).transpose(0, 1, 2, 4, 3).reshape(
        depth, 2, HALF_STATE, SSM_GROUP_CH)
    return _keep_diagonal_blocks(jnp.tile(rows, (1, 1, 1, hg)), SSM_STATE, SSM_GROUP_CH)


def _prepare_weights(w_in, w_out, attn_sinks, bb_re, bb_im, ssm_c_re, ssm_c_im, t_new):
    depth = w_in.shape[0]
    assert HEAD_PERM == tuple(kv * Q_GROUP + g for g in range(Q_GROUP) for kv in range(N_KV_HEADS))
    w_in_p = w_in.astype(BF16)
    w_out_p = w_out.astype(BF16)
    sinks = jnp.swapaxes(attn_sinks.reshape(depth, N_KV_HEADS, Q_GROUP), 1, 2).reshape(depth, N_Q_HEADS)
    sink_lanes = jnp.broadcast_to(sinks[:, :, None], (depth, N_Q_HEADS, LANES))
    sink_rows = jnp.broadcast_to(jnp.repeat(sinks, t_new, axis=1)[:, :, None], (depth, N_Q_HEADS * t_new, LANES))
    w_b = jnp.concatenate([_expand_weights(bb_re), _expand_weights(bb_im)], axis=3).astype(BF16)
    w_c = jnp.concatenate([_contract_weights(ssm_c_re), -_contract_weights(ssm_c_im)], axis=2).astype(BF16)
    return w_in_p, w_out_p, sink_lanes, sink_rows, w_b, w_c


def kernel(x_prompt, x_sample, cache_k, cache_v, state_ssm_re, state_ssm_im, c_prompt, c_sample, norm_g, w_ada, b_ada, w_in, attn_sinks, ssm_a_re, ssm_a_im, ssm_log_dt, ssm_b_re, ssm_b_im, ssm_c_re, ssm_c_im, ssm_d, w_glu, b_glu, w_out, final_g):
    depth = w_in.shape[0]
    nb_p, seq, _ = x_prompt.shape
    nb_s, t_new, _ = x_sample.shape
    win = cache_k.shape[2]
    blk = WINDOW

    sample = Group(n_batch=nb_s, mod_block=0)
    prompt = Group(n_batch=nb_p, mod_block=nb_s // nb_p)
    mod = adaln_mod(jnp.concatenate([c_sample, c_prompt], axis=0), w_ada, b_ada)
    abar_re, abar_im, bb_re, bb_im = ssm_discretise(ssm_a_re, ssm_a_im, ssm_log_dt, ssm_b_re, ssm_b_im)
    w_in_p, w_out_p, sink_lanes, sink_rows, w_b, w_c = _prepare_weights(
        w_in, w_out, attn_sinks, bb_re, bb_im, ssm_c_re, ssm_c_im, t_new)
    norm_g3 = norm_g.reshape(depth, 1, D_MODEL)
    d3 = ssm_d.reshape(depth, 1, SSM_W)
    w_glu_b = w_glu.astype(BF16)
    b_glu3 = b_glu.reshape(depth, 1, SSM_W)
    final_g2 = final_g.reshape(1, D_MODEL)
    ssm_weights = (abar_re, abar_im, w_b, w_c, d3, w_glu_b, b_glu3)

    cos_p, sm_p = _rope_tables(jnp.arange(seq))
    cos_s, sm_s = _rope_tables(PAST_LEN + jnp.arange(t_new))

    xp = x_prompt
    xs = x_sample.transpose(1, 0, 2).reshape(t_new * nb_s, D_MODEL)
    zeros_state = jnp.zeros((1, nb_p, N_STATE), F32)
    transposed = lambda c: c.transpose(0, 1, 3, 4, 2).reshape(depth, nb_s, KV_W, win)
    cache_kt, cache_vt = transposed(cache_k), transposed(cache_v)
    new_kt, new_vt = jnp.zeros_like(cache_kt), jnp.zeros_like(cache_vt)
    state_re3 = state_ssm_re.transpose(0, 2, 3, 1).reshape(depth, N_STATE, nb_s)
    state_im3 = state_ssm_im.transpose(0, 2, 3, 1).reshape(depth, N_STATE, nb_s)

    outs = {k: [] for k in ("kp", "vp", "rp", "ip", "rs", "is")}
    rows_p, rows_s = BOUNDARY_ROWS, t_new * nb_s
    opening = lambda l, cos_t, sm_t: (norm_g3, w_in_p, cos_t, sm_t, l)
    xp, proj_p = layer_boundary(xp, mod, prompt, rows_p, open_=opening(0, cos_p, sm_p))
    xs, proj_s = layer_boundary(xs, mod, sample, rows_s, open_=opening(0, cos_s, sm_s))
    for l in range(depth):
        last = l == depth - 1

        q, k, v, za, u, zs = proj_p
        ya, ys, hre, him = prompt_branches(q, k, v, za, sink_lanes, u, zs, zeros_state, zeros_state, *ssm_weights,
                                           l, 0, nb_p, blk)
        xp, proj_p = layer_boundary(xp, mod, prompt, rows_p, close=(ya, ys, w_out_p, l),
                                    open_=None if last else opening(l + 1, cos_p, sm_p),
                                    final_g=final_g2, batch_major_out=last)
        outs["kp"].append(k[(seq - win) * nb_p:])
        outs["vp"].append(v[(seq - win) * nb_p:])
        outs["rp"].append(hre)
        outs["ip"].append(him)

        q, k, v, za, u, zs = proj_s
        ya, new_kt, new_vt = attention_sample(q, k, v, za, cache_kt, cache_vt, sink_rows, new_kt, new_vt, l,
                                              nb_s, t_new, SAMPLE_GROUP)
        ys, hre, him = ssm_branch(u, zs, state_re3, state_im3, *ssm_weights, l, l, nb_s, t_new)
        xs, proj_s = layer_boundary(xs, mod, sample, rows_s, close=(ya, ys, w_out_p, l),
                                    open_=None if last else opening(l + 1, cos_s, sm_s), final_g=final_g2)
        outs["rs"].append(hre)
        outs["is"].append(him)

    y_prompt = xp
    y_sample = xs.reshape(t_new, nb_s, D_MODEL).transpose(1, 0, 2)
    st = lambda key: jnp.stack(outs[key])
    window = lambda key: st(key).reshape(depth, win, nb_p, N_KV_HEADS, HEAD_DIM).transpose(0, 2, 1, 3, 4)
    state = lambda key, nb: st(key).reshape(depth, nb, N_SSM_GROUPS, SSM_STATE)
    state_t = lambda key: st(key).reshape(depth, N_SSM_GROUPS, SSM_STATE, nb_s).transpose(0, 3, 1, 2)
    untransposed = lambda w: w.reshape(depth, nb_s, N_KV_HEADS, HEAD_DIM, win).transpose(0, 1, 4, 2, 3)
    return (y_prompt, y_sample, window("kp"), window("vp"), state("rp", nb_p), state("ip", nb_p),
            untransposed(new_kt), untransposed(new_vt), state_t("rs"), state_t("is"))
```

```python
import functools
from typing import NamedTuple

import jax
import jax.numpy as jnp
import numpy as np
from jax import lax
from jax.experimental import pallas as pl
from jax.experimental.pallas import tpu as pltpu

F32 = jnp.float32
BF16 = jnp.bfloat16

D_MODEL = 1024
HEAD_DIM = 64
ATTN_W = 512
N_Q_HEADS = 8
N_KV_HEADS = 2
Q_GROUP = N_Q_HEADS // N_KV_HEADS
KV_W = 128
WINDOW = 128
ROT_DIM = 16
ROPE_THETA = 500000.0
SSM_W = 512
SSM_GROUP_CH = 16
N_SSM_GROUPS = 32
SSM_STATE = 64
N_STATE = N_SSM_GROUPS * SSM_STATE
COL_Q, COL_KV, COL_ZA = 0, ATTN_W, ATTN_W + 2 * KV_W
COL_U, COL_ZS = COL_ZA + ATTN_W, COL_ZA + ATTN_W + SSM_W
HALF_STATE = N_STATE // 2
EPS = 1e-6
NEG_INF = -1e30
LOG2_E = 1.4426950408889634
PAST_LEN = 8192

MOD_PARTS = 3
LANES = 128
N_CHUNKS = ATTN_W // LANES
BOUNDARY_ROWS = 1024
BOUNDARY_PARTS = 2
SSM_SLAB = 4 * LANES
SAMPLE_GROUP = 32
SSM_SUB_ROWS = 256
SSM_BUFFERS = 3
SUBLANES = 8
VMEM_LIMIT = 56 * 1024 * 1024

HEAD_PERM = (0, 4, 1, 5, 2, 6, 3, 7)


class Group(NamedTuple):
    n_batch: int
    mod_block: int


def _params(sem):
    return pltpu.CompilerParams(dimension_semantics=sem, vmem_limit_bytes=VMEM_LIMIT)


def _full(shape):
    return pl.BlockSpec(shape, lambda *_: (0,) * len(shape))


def _mod_kernel(c_ref, w_ref, b_ref, o_ref):
    c = c_ref[...]
    a = (c * jax.nn.sigmoid(c)).astype(BF16)
    o_ref[...] = jnp.dot(a, w_ref[...].astype(BF16), preferred_element_type=F32) + b_ref[...]


def adaln_mod(c_all, w_ada, b_ada):
    depth = w_ada.shape[0]
    n = c_all.shape[0]
    width = MOD_PARTS * D_MODEL
    return pl.pallas_call(
        _mod_kernel,
        grid=(depth,),
        in_specs=[
            pl.BlockSpec((n, D_MODEL), lambda l: (0, 0)),
            pl.BlockSpec((None, D_MODEL, width), lambda l: (l, 0, 0)),
            pl.BlockSpec((None, 1, width), lambda l: (l, 0, 0)),
        ],
        out_specs=pl.BlockSpec((None, n, width), lambda l: (l, 0, 0)),
        out_shape=jax.ShapeDtypeStruct((depth, n, width), F32),
        compiler_params=_params(("arbitrary",)),
        name="adaln_mod",
    )(c_all, w_ada, b_ada.reshape(depth, 1, width))


def _disc_kernel(are_ref, aim_ref, ldt_ref, bre_ref, bim_ref, abre_ref, abim_ref, bbre_ref, bbim_ref):
    a_re = are_ref[...]
    a_im = aim_ref[...]
    dt = jnp.exp(ldt_ref[...])
    mag = jnp.exp(a_re * dt)
    abar_re = mag * jnp.cos(a_im * dt)
    abar_im = mag * jnp.sin(a_im * dt)
    den = a_re * a_re + a_im * a_im
    nr = abar_re - 1.0
    coef_re = (nr * a_re + abar_im * a_im) / den
    coef_im = (abar_im * a_re - nr * a_im) / den
    br = bre_ref[...]
    bi = bim_ref[...]
    abre_ref[...] = abar_re
    abim_ref[...] = abar_im
    bbre_ref[...] = coef_re * br - coef_im * bi
    bbim_ref[...] = coef_re * bi + coef_im * br


def ssm_discretise(a_re, a_im, log_dt, b_re, b_im):
    depth = a_re.shape[0]
    a_re = a_re.reshape(depth, 1, N_STATE)
    a_im = a_im.reshape(depth, 1, N_STATE)
    ldt = jnp.repeat(log_dt, SSM_STATE, axis=1).reshape(depth, 1, N_STATE)
    bt_re = b_re.reshape(depth, N_STATE, SSM_GROUP_CH).transpose(0, 2, 1)
    bt_im = b_im.reshape(depth, N_STATE, SSM_GROUP_CH).transpose(0, 2, 1)
    row = pl.BlockSpec((None, 1, N_STATE), lambda l: (l, 0, 0))
    mat = pl.BlockSpec((None, SSM_GROUP_CH, N_STATE), lambda l: (l, 0, 0))
    return pl.pallas_call(
        _disc_kernel,
        grid=(depth,),
        in_specs=[row, row, row, mat, mat],
        out_specs=[row, row, mat, mat],
        out_shape=[jax.ShapeDtypeStruct((depth, 1, N_STATE), F32)] * 2
        + [jax.ShapeDtypeStruct((depth, SSM_GROUP_CH, N_STATE), F32)] * 2,
        compiler_params=_params(("arbitrary",)),
        name="ssm_discretise",
    )(a_re, a_im, ldt, bt_re, bt_im)


def _rope(x, cos, sm):
    lane = lax.broadcasted_iota(jnp.int32, x.shape, 1) % HEAD_DIM
    partner = jnp.where(lane < ROT_DIM // 2, pltpu.roll(x, LANES - ROT_DIM // 2, 1), pltpu.roll(x, ROT_DIM // 2, 1))
    return x * cos + partner * sm


def _to_time_major(x_ref):
    n_batch, steps, width = x_ref.shape
    return jnp.swapaxes(x_ref[...], 0, 1).reshape(steps * n_batch, width)


def _from_time_major(x, o_ref, n_batch):
    rows, width = x.shape
    o_ref[...] = jnp.swapaxes(x.reshape(rows // n_batch, n_batch, width), 0, 1)


def _rms(x, g_ref):
    ms = jnp.mean(x * x, axis=-1, keepdims=True)
    return x * lax.rsqrt(ms + EPS) * g_ref[...]


def _per_sequence(y, n_batch, fn):
    rows = y.shape[0]
    return fn(y.reshape(rows // n_batch, n_batch, D_MODEL)).reshape(rows, D_MODEL)


def _pair_heads(chunks):
    low = lax.broadcasted_iota(jnp.int32, chunks[0].shape, 1) < HEAD_DIM
    swap = lambda x: pltpu.roll(x, HEAD_DIM, 1)
    n0, n1, n2, n3 = chunks
    return [jnp.where(low, n0, swap(n2)), jnp.where(low, swap(n0), n2),
            jnp.where(low, n1, swap(n3)), jnp.where(low, swap(n1), n3)]


def _unpair_heads(chunks):
    low = lax.broadcasted_iota(jnp.int32, chunks[0].shape, 1) < HEAD_DIM
    swap = lambda x: pltpu.roll(x, HEAD_DIM, 1)
    p0, p1, p2, p3 = chunks
    return [jnp.where(low, p0, swap(p1)), jnp.where(low, p2, swap(p3)),
            jnp.where(low, swap(p0), p1), jnp.where(low, swap(p2), p3)]


def _residual_update(x, rows, ya_ref, ys_ref, gate_ref, w_ref, n_batch):
    ya = _unpair_heads([ya_ref[c, rows, :] for c in range(N_CHUNKS)])
    mixed = jnp.concatenate([y.astype(BF16) for y in ya] + [ys_ref[rows, :]], axis=1)
    mix = jnp.dot(mixed, w_ref[...], preferred_element_type=F32)
    return x + _per_sequence(mix, n_batch, lambda m: m * gate_ref[...][None])


def _modulated_norm(x, g_ref, scale_ref, shift_ref, n_batch):
    y = _per_sequence(_rms(x, g_ref), n_batch, lambda y3: y3 * (1.0 + scale_ref[...])[None] + shift_ref[...][None])
    return y.astype(BF16)


def _silu(z):
    return z * jax.nn.sigmoid(z)


def _project(h, rows, w_ref, rope_scr, q_ref, k_ref, v_ref, ga_ref, u_ref, gs_ref):
    cos = rope_scr[0, rows, :]
    sm = rope_scr[1, rows, :]

    def proj(lo, width):
        return jnp.dot(h, w_ref[:, lo:lo + width], preferred_element_type=F32)

    chunks_of = lambda a: [a[:, c * LANES:(c + 1) * LANES] for c in range(N_CHUNKS)]
    q = _pair_heads(chunks_of(proj(COL_Q, ATTN_W)))
    ga = _pair_heads(chunks_of(_silu(proj(COL_ZA, ATTN_W))))
    for c in range(N_CHUNKS):
        q_ref[c, rows, :] = _rope(q[c], cos, sm)
        ga_ref[c, rows, :] = ga[c]
    u_ref[rows, :] = proj(COL_U, SSM_W)
    gs_ref[rows, :] = _silu(proj(COL_ZS, SSM_W))
    kv = proj(COL_KV, 2 * KV_W)
    k_ref[rows, :] = _rope(kv[:, :KV_W], cos, sm)
    v_ref[rows, :] = kv[:, KV_W:]


def _boundary_kernel(*refs, n_batch, closes, opens, batch_major, block_rows, n_parts):
    refs = list(refs)
    x_ref = refs.pop(0)
    close_refs = [refs.pop(0) for _ in range(4)] if closes else None
    final_g_ref = refs.pop(0) if closes and not opens else None
    open_refs = [refs.pop(0) for _ in range(6)] if opens else None
    x_out_ref = refs.pop(0) if closes or batch_major else None
    proj_out_refs = [refs.pop(0) for _ in range(6)] if opens else None
    rope_scr = refs.pop(0) if opens else None

    if opens:
        g_ref, scale_ref, shift_ref, w_in_ref, cos_ref, sm_ref = open_refs
        steps = cos_ref.shape[0]
        for b in range(n_batch):
            rope_scr[0, pl.ds(b, steps, stride=n_batch), :] = cos_ref[...]
            rope_scr[1, pl.ds(b, steps, stride=n_batch), :] = sm_ref[...]

    part = block_rows // n_parts
    groups = [slice(i * part, (i + 1) * part) for i in range(n_parts)]
    x_full = _to_time_major(x_ref) if (batch_major and not closes) else None

    def updated(rows):
        x = x_full[rows, :] if x_full is not None else x_ref[rows, :]
        return _residual_update(x, rows, *close_refs, n_batch) if closes else x

    def normed(x, rows):
        if opens:
            if x_out_ref is not None:
                x_out_ref[rows, :] = x
            return _modulated_norm(x, g_ref, scale_ref, shift_ref, n_batch)
        return _rms(x, final_g_ref)

    xs = [updated(rows) for rows in groups]
    hs = [None] * n_parts
    hs[0] = normed(xs[0], groups[0])
    for i, rows in enumerate(groups):
        if opens:
            _project(hs[i], rows, w_in_ref, rope_scr, *proj_out_refs)
        if i + 1 < n_parts:
            hs[i + 1] = normed(xs[i + 1], groups[i + 1])
    if not opens:
        y = jnp.concatenate(hs, axis=0)
        if batch_major:
            _from_time_major(y, x_out_ref, n_batch)
        else:
            x_out_ref[...] = y


def _layer_spec(arr, l):
    tail = arr.shape[1:]
    return pl.BlockSpec((None,) + tail, lambda *_: (l,) + (0,) * len(tail))


def _mod_spec(l, group, part):
    return pl.BlockSpec((None, group.n_batch, D_MODEL), lambda *_: (l, group.mod_block, part))


def layer_boundary(x, mod, group, block_rows, close=None, open_=None, final_g=None, batch_major_out=False):
    closes, opens = close is not None, open_ is not None
    batch_major_in = x.ndim == 3
    assert not (batch_major_in and closes) and not (batch_major_out and opens)
    batch_major = batch_major_in or batch_major_out
    t_rows = x.shape[0] * x.shape[1] if batch_major_in else x.shape[0]
    rows = lambda w: pl.BlockSpec((block_rows, w), lambda i: (i, 0))
    chunked = pl.BlockSpec((N_CHUNKS, block_rows, LANES), lambda i: (0, i, 0))
    blocked3 = pl.BlockSpec((group.n_batch, block_rows // group.n_batch, D_MODEL), lambda i: (0, i, 0))
    flat = lambda w: jax.ShapeDtypeStruct((t_rows, w), F32)
    chunked_shape = jax.ShapeDtypeStruct((N_CHUNKS, t_rows, LANES), F32)

    args, in_specs, out_specs, out_shape = [x], [blocked3 if batch_major_in else rows(D_MODEL)], [], []
    if closes:
        ya, ys, w_out, l = close
        args += [ya, ys, mod, w_out]
        in_specs += [chunked, rows(SSM_W), _mod_spec(l, group, 2), _layer_spec(w_out, l)]
        if not opens:
            args.append(final_g)
            in_specs.append(_full(final_g.shape))
    if opens:
        norm_g, w_in, cos_t, sm_t, l = open_
        args += [norm_g, mod, mod, w_in, cos_t, sm_t]
        per_step = pl.BlockSpec((block_rows // group.n_batch, LANES), lambda i: (i, 0))
        in_specs += [_layer_spec(norm_g, l), _mod_spec(l, group, 1), _mod_spec(l, group, 0), _layer_spec(w_in, l),
                     per_step, per_step]
    if closes or batch_major_in:
        out_specs.append(blocked3 if batch_major_out else rows(D_MODEL))
        out_shape.append(jax.ShapeDtypeStruct((group.n_batch, t_rows // group.n_batch, D_MODEL), F32)
                         if batch_major_out else flat(D_MODEL))
    if opens:
        out_specs += [chunked, rows(KV_W), rows(KV_W), chunked, rows(SSM_W), rows(SSM_W)]
        out_shape += [chunked_shape, flat(KV_W), flat(KV_W), chunked_shape, flat(SSM_W), flat(SSM_W)]
    scratch = [pltpu.VMEM((2, block_rows, LANES), F32)] if opens else []
    outs = pl.pallas_call(
        functools.partial(_boundary_kernel, n_batch=group.n_batch, closes=closes, opens=opens,
                          batch_major=batch_major, block_rows=block_rows, n_parts=BOUNDARY_PARTS),
        grid=(t_rows // block_rows,),
        in_specs=in_specs,
        out_specs=out_specs,
        out_shape=out_shape,
        scratch_shapes=scratch,
        compiler_params=_params(("parallel",)),
        name="layer_boundary",
    )(*args)
    x_rows = outs[0] if (closes or batch_major_in) else x
    return x_rows, (tuple(outs[-6:]) if opens else None)


def _pad_queries(chunks):
    lane = lax.broadcasted_iota(jnp.int32, chunks[0].shape, 1)
    pieces = []
    for chunk in chunks:
        pieces.append(jnp.where(lane < HEAD_DIM, chunk, 0.0))
        pieces.append(jnp.where(lane >= HEAD_DIM, chunk, 0.0))
    return jnp.concatenate(pieces, axis=0).astype(BF16)


def _unpad_outputs(o, t):
    lane = lax.broadcasted_iota(jnp.int32, (t, LANES), 1)
    chunks = []
    for c in range(N_CHUNKS):
        lo = o[(2 * c) * t:(2 * c + 1) * t]
        hi = o[(2 * c + 1) * t:(2 * c + 2) * t]
        chunks.append(jnp.where(lane < HEAD_DIM, lo, hi))
    return chunks


def _gated_store(o_ref, ga_ref, sel, ya_chunks):
    for c, ya in enumerate(ya_chunks):
        o_ref[c, sel, :] = ya * ga_ref[c, sel, :]


def _scores(qp, k, scale=HEAD_DIM ** -0.5):
    return lax.dot_general(qp, k.astype(BF16), (((1,), (1,)), ((), ())), preferred_element_type=F32) * scale


def _prompt_attention_stages(q_ref, kp_ref, kc_ref, vp_ref, vc_ref, za_ref, sink_ref, o_ref, n_batch, blk):
    assert blk == WINDOW
    n = pl.program_id(0)
    qb = blk // 2
    keys = blk + qb
    cols = 2 * LANES
    j = lax.broadcasted_iota(jnp.int32, (keys, cols), 0)
    tq = lax.broadcasted_iota(jnp.int32, (keys, cols), 1) % qb
    masks = [(j >= jnp.maximum(tq, jnp.where(n == 0, blk - h * qb, 0))) & (j <= tq + WINDOW) for h in range(2)]
    low = lax.broadcasted_iota(jnp.int32, (qb, LANES), 1) < HEAD_DIM
    ones = jnp.ones((SUBLANES, keys), F32)
    units = [(h, pr) for h in range(2) for pr in range(N_CHUNKS // 2)]
    sinks = [jnp.concatenate([sink_ref[slot:slot + 1, :qb] for slot in range(4 * pr, 4 * pr + 4)], axis=1) * LOG2_E
             for pr in range(N_CHUNKS // 2)]

    def query_rows(b, h):
        return pl.ds(b + h * qb * n_batch, qb, stride=n_batch)

    def window(prev_ref, cur_ref, b, h):
        if h == 0:
            return jnp.concatenate([prev_ref[pl.ds(b, blk, stride=n_batch), :], cur_ref[query_rows(b, 0), :]], axis=0)
        return jnp.concatenate([prev_ref[query_rows(b, 1), :], cur_ref[pl.ds(b, blk, stride=n_batch), :]], axis=0)

    def score_stage(b):
        sts = []
        for h in range(2):
            k = window(kp_ref, kc_ref, b, h).astype(BF16)
            for pr in range(N_CHUNKS // 2):
                pieces = []
                for c in (2 * pr, 2 * pr + 1):
                    qc = q_ref[c, query_rows(b, h), :]
                    pieces += [jnp.where(low, qc, 0.0), jnp.where(low, 0.0, qc)]
                qp = jnp.concatenate(pieces, axis=0).astype(BF16)
                sts.append(_scores(k, qp, HEAD_DIM ** -0.5 * LOG2_E))
        return sts

    def value_stage(b, sts):
        vt_aug = [jnp.concatenate([window(vp_ref, vc_ref, b, h).T, ones], axis=0).astype(BF16) for h in range(2)]

        def finish(h, pr, ot, m):
            den = ot[KV_W:KV_W + 1, :] + jnp.exp2(sinks[pr] - m)
            o = (ot[:KV_W, :] * (1.0 / den)).T
            sel = query_rows(b, h)
            for i, c in enumerate((2 * pr, 2 * pr + 1)):
                lo = 2 * i * qb
                ya = jnp.where(low, o[lo:lo + qb], o[lo + qb:lo + 2 * qb])
                o_ref[c, sel, :] = ya * za_ref[c, sel, :]

        pending = None
        for (h, pr), st in zip(units, sts):
            st = jnp.where(masks[h], st, NEG_INF)
            m = jnp.maximum(jnp.max(st, axis=0, keepdims=True), sinks[pr])
            p = jnp.exp2(st - m).astype(BF16)
            ot = jnp.dot(vt_aug[h], p, preferred_element_type=F32)
            if pending is not None:
                finish(*pending)
            pending = (h, pr, ot, m)
        finish(*pending)

    return score_stage, value_stage


def _attn_sample_kernel(q_ref, k_ref, v_ref, za_ref, ck_ref, cv_ref, sink_ref, nk_all_ref, nv_all_ref,
                        o_ref, nk_ref, nv_ref, *, n_batch, t_new, group):
    del nk_all_ref, nv_all_ref
    g = pl.program_id(0)
    win = ck_ref.shape[2]
    rows = N_Q_HEADS * t_new
    tq = lax.broadcasted_iota(jnp.int32, (rows, 2 * win), 0) % t_new
    j = lax.broadcasted_iota(jnp.int32, (rows, 2 * win), 1)
    dist_c = tq + win - j
    mask = ((j < win) & (dist_c >= 0) & (dist_c <= WINDOW)) | ((j >= win) & (j - win <= tq) & (j - win < t_new))
    sink = sink_ref[...]
    pad = jnp.zeros((win - t_new, KV_W), F32)
    ones_rows = jnp.ones((LANES, win), F32)
    ones_cols = jnp.ones((win, LANES), F32)
    kept = lax.broadcasted_iota(jnp.int32, (KV_W, win), 1) < win - t_new
    batch = range(group)
    sels = [pl.ds(g * group + i, t_new, stride=n_batch) for i in batch]

    scores = []
    for i in batch:
        qp = _pad_queries([q_ref[c, sels[i], :] for c in range(N_CHUNKS)])
        s_cache = jnp.dot(qp, ck_ref[i].astype(BF16), preferred_element_type=F32) * (HEAD_DIM ** -0.5)
        s_new = _scores(qp, jnp.concatenate([k_ref[sels[i], :], pad], axis=0))
        scores.append(jnp.where(mask, jnp.concatenate([s_cache, s_new], axis=1), NEG_INF))
    outs = []
    for i in batch:
        s = scores[i]
        m = jnp.maximum(jnp.max(s, axis=-1, keepdims=True), sink[:, :1])
        p = jnp.exp(s - m).astype(BF16)
        vt_aug = jnp.concatenate([cv_ref[i], ones_rows], axis=0).astype(BF16)
        v_new = jnp.concatenate([jnp.concatenate([v_ref[sels[i], :], pad], axis=0), ones_cols], axis=1).astype(BF16)
        o = (lax.dot_general(p[:, :win], vt_aug, (((1,), (1,)), ((), ())), preferred_element_type=F32)
             + jnp.dot(p[:, win:], v_new, preferred_element_type=F32))
        outs.append(o[:, :KV_W] / (o[:, KV_W:] + jnp.exp(sink - m)))
    for i in batch:
        _gated_store(o_ref, za_ref, sels[i], _unpad_outputs(outs[i], t_new))
        for new_ref, old_ref, rows_ref in ((nk_ref, ck_ref, k_ref), (nv_ref, cv_ref, v_ref)):
            appended = jnp.concatenate([pad, rows_ref[sels[i], :]], axis=0).T
            new_ref[i] = jnp.where(kept, pltpu.roll(old_ref[i], win - t_new, 1), appended)


def attention_sample(q, k, v, za, cache_kt, cache_vt, sink_rows, new_kt, new_vt, l, n_batch, t_new, group):
    t_rows = k.shape[0]
    win = cache_kt.shape[3]
    window = pl.BlockSpec((None, group, KV_W, win), lambda i: (l, i, 0, 0))
    chunked = _full((N_CHUNKS, t_rows, LANES))
    in_place = pl.BlockSpec(memory_space=pl.ANY)
    return pl.pallas_call(
        functools.partial(_attn_sample_kernel, n_batch=n_batch, t_new=t_new, group=group),
        grid=(n_batch // group,),
        in_specs=[chunked, _full((t_rows, KV_W)), _full((t_rows, KV_W)), chunked,
                  window, window, _layer_spec(sink_rows, l), in_place, in_place],
        out_specs=[chunked, window, window],
        out_shape=[jax.ShapeDtypeStruct((N_CHUNKS, t_rows, LANES), F32),
                   jax.ShapeDtypeStruct(new_kt.shape, F32), jax.ShapeDtypeStruct(new_vt.shape, F32)],
        input_output_aliases={7: 1, 8: 2},
        compiler_params=_params(("arbitrary",)),
        name="attention_sample",
    )(q, k, v, za, cache_kt, cache_vt, sink_rows, new_kt, new_vt)


def _ssm_stages(u_ref, zs_ref, h0re_ref, h0im_ref, are_ref, aim_ref, wb_ref, wc_ref, d_ref, wg_ref, bg_ref,
                o_ref, hre_ref, him_ref, x_scr, n_batch, sub_rows, slab, h0_transposed=False):
    step = pl.program_id(0)

    @pl.when(step == 0)
    def _():
        hre_ref[...] = h0re_ref[...].T if h0_transposed else h0re_ref[...]
        him_ref[...] = h0im_ref[...].T if h0_transposed else h0im_ref[...]

    rows = u_ref.shape[0]
    half_in = SSM_W // 2
    n_sub = rows // sub_rows
    n_tiles = n_batch // SUBLANES
    t_steps = sub_rows // n_batch
    n_buf = x_scr.shape[0]
    items = [(sc, s) for sc in range(n_sub) for s in range(2)]

    def expand(i):
        sc, s = items[i]
        ub = u_ref[sc * sub_rows:(sc + 1) * sub_rows, s * half_in:(s + 1) * half_in].astype(BF16)
        x_scr[i % n_buf] = jnp.dot(ub, wb_ref[s], preferred_element_type=F32)

    def recur(i):
        sc, s = items[i]
        buf = i % n_buf
        for jb in range(HALF_STATE // slab):
            n_lo = s * HALF_STATE + jb * slab
            ar = jnp.broadcast_to(are_ref[:, n_lo:n_lo + slab], (SUBLANES, slab))
            ai = jnp.broadcast_to(aim_ref[:, n_lo:n_lo + slab], (SUBLANES, slab))
            re = slice(jb * slab, (jb + 1) * slab)
            im = slice(HALF_STATE + jb * slab, HALF_STATE + (jb + 1) * slab)
            for tile in range(n_tiles):
                r0 = tile * SUBLANES
                hr = hre_ref[r0:r0 + SUBLANES, n_lo:n_lo + slab]
                hi = him_ref[r0:r0 + SUBLANES, n_lo:n_lo + slab]
                for t in range(t_steps):
                    row = slice(t * n_batch + r0, t * n_batch + r0 + SUBLANES)
                    hr, hi = (ar * hr - ai * hi + x_scr[buf, row, re], ar * hi + ai * hr + x_scr[buf, row, im])
                    x_scr[buf, row, re] = hr
                    x_scr[buf, row, im] = hi
                hre_ref[r0:r0 + SUBLANES, n_lo:n_lo + slab] = hr
                him_ref[r0:r0 + SUBLANES, n_lo:n_lo + slab] = hi

    y_halves = {}

    def contract(i):
        sc, s = items[i]
        y_halves[s] = jnp.dot(x_scr[i % n_buf].astype(BF16), wc_ref[s], preferred_element_type=F32)
        if s == 1:
            r = slice(sc * sub_rows, (sc + 1) * sub_rows)
            y = jnp.concatenate([y_halves[0], y_halves[1]], axis=1) + d_ref[...] * u_ref[r, :]
            y = jax.nn.gelu(y)
            gate = jax.nn.sigmoid(jnp.dot(y.astype(BF16), wg_ref[...], preferred_element_type=F32) + bg_ref[...])
            o_ref[r, :] = (y * gate * zs_ref[r, :]).astype(o_ref.dtype)

    return len(items), expand, recur, contract


def _ssm_kernel(*refs, n_batch, sub_rows, slab):
    inputs, (o_ref, hre_t_ref, him_t_ref, x_scr, hre_ref, him_ref) = refs[:N_SSM_INPUTS], refs[N_SSM_INPUTS:]
    n_items, expand, recur, contract = _ssm_stages(*inputs, o_ref, hre_ref, him_ref, x_scr, n_batch, sub_rows, slab,
                                                   h0_transposed=True)
    expand(0)
    for i in range(n_items + 1):
        if i + 1 < n_items:
            expand(i + 1)
        if i < n_items:
            recur(i)
        if i >= 1:
            contract(i - 1)

    @pl.when(pl.program_id(0) == pl.num_programs(0) - 1)
    def _():
        hre_t_ref[...] = hre_ref[...].T
        him_t_ref[...] = him_ref[...].T


N_ATTN_INPUTS = 7
N_SSM_INPUTS = 11


def _prompt_branches_kernel(*refs, n_batch, blk, sub_rows, slab):
    n_in = N_ATTN_INPUTS + N_SSM_INPUTS
    attn_refs = refs[:N_ATTN_INPUTS] + refs[n_in:n_in + 1]
    ssm_refs = refs[N_ATTN_INPUTS:n_in] + refs[n_in + 1:]
    score_stage, value_stage = _prompt_attention_stages(*attn_refs, n_batch, blk)
    n_items, expand, recur, contract = _ssm_stages(*ssm_refs, n_batch, sub_rows, slab)
    per_item = n_batch // n_items
    assert per_item * n_items == n_batch
    seqs_of = lambda i: range(i * per_item, (i + 1) * per_item)
    expand(0)
    scores = [score_stage(b) for b in seqs_of(0)]
    for i in range(n_items + 1):
        if i < n_items:
            recur(i)
        if i + 1 < n_items:
            expand(i + 1)
        if i < n_items:
            for b, sts in zip(seqs_of(i), scores):
                value_stage(b, sts)
        if i + 1 < n_items:
            scores = [score_stage(b) for b in seqs_of(i + 1)]
        if i >= 1:
            contract(i - 1)


def _ssm_call_parts(u, h0_re, h0_im, abar_re, abar_im, w_b, w_c, d, w_glu, b_glu, l, l_state, n_batch, r):
    t_rows = u.shape[0]
    sub_rows = max(SSM_SUB_ROWS, n_batch)
    rows = pl.BlockSpec((r, SSM_W), lambda i: (i, 0))
    state = _full((n_batch, N_STATE))
    layer = lambda a: _layer_spec(a, l)
    in_specs = [rows, rows, _layer_spec(h0_re, l_state), _layer_spec(h0_im, l_state), layer(abar_re), layer(abar_im),
                layer(w_b), layer(w_c), layer(d), layer(w_glu), layer(b_glu)]
    out_shape = [jax.ShapeDtypeStruct((t_rows, SSM_W), BF16),
                 jax.ShapeDtypeStruct((n_batch, N_STATE), F32), jax.ShapeDtypeStruct((n_batch, N_STATE), F32)]
    scratch = [pltpu.VMEM((SSM_BUFFERS, sub_rows, 2 * HALF_STATE), F32)]
    return in_specs, [rows, state, state], out_shape, scratch, sub_rows


def ssm_branch(u, zs, h0_re, h0_im, abar_re, abar_im, w_b, w_c, d, w_glu, b_glu, l, l_state, n_batch, t_chunk):
    r = t_chunk * n_batch
    in_specs, out_specs, out_shape, scratch, sub_rows = _ssm_call_parts(
        u, h0_re, h0_im, abar_re, abar_im, w_b, w_c, d, w_glu, b_glu, l, l_state, n_batch, r)
    out_specs = out_specs[:1] + [_full((N_STATE, n_batch))] * 2
    out_shape = out_shape[:1] + [jax.ShapeDtypeStruct((N_STATE, n_batch), F32)] * 2
    scratch = scratch + [pltpu.VMEM((n_batch, N_STATE), F32)] * 2
    return pl.pallas_call(
        functools.partial(_ssm_kernel, n_batch=n_batch, sub_rows=sub_rows, slab=SSM_SLAB),
        grid=(u.shape[0] // r,),
        in_specs=in_specs,
        out_specs=out_specs,
        out_shape=out_shape,
        scratch_shapes=scratch,
        compiler_params=_params(("arbitrary",)),
        name="ssm_branch",
    )(u, zs, h0_re, h0_im, abar_re, abar_im, w_b, w_c, d, w_glu, b_glu)


def prompt_branches(q, k, v, za, sinks, u, zs, h0_re, h0_im, abar_re, abar_im, w_b, w_c, d, w_glu, b_glu,
                    l, l_state, n_batch, blk):
    t_rows = k.shape[0]
    r = blk * n_batch
    cur = lambda w: pl.BlockSpec((r, w), lambda i: (i, 0))
    prev = lambda w: pl.BlockSpec((r, w), lambda i: (jnp.maximum(i - 1, 0), 0))
    chunked = pl.BlockSpec((N_CHUNKS, r, LANES), lambda i: (0, i, 0))
    ssm_in, ssm_out, ssm_shape, scratch, sub_rows = _ssm_call_parts(
        u, h0_re, h0_im, abar_re, abar_im, w_b, w_c, d, w_glu, b_glu, l, l_state, n_batch, r)
    attn_in = [chunked, prev(KV_W), cur(KV_W), prev(KV_W), cur(KV_W), chunked, _layer_spec(sinks, l)]
    assert len(attn_in) == N_ATTN_INPUTS and len(ssm_in) == N_SSM_INPUTS
    return pl.pallas_call(
        functools.partial(_prompt_branches_kernel, n_batch=n_batch, blk=blk, sub_rows=sub_rows, slab=SSM_SLAB),
        grid=(t_rows // r,),
        in_specs=attn_in + ssm_in,
        out_specs=[chunked] + ssm_out,
        out_shape=[jax.ShapeDtypeStruct((N_CHUNKS, t_rows, LANES), F32)] + ssm_shape,
        scratch_shapes=scratch,
        compiler_params=_params(("arbitrary",)),
        name="prompt_branches",
    )(q, k, k, v, v, za, sinks, u, zs, h0_re, h0_im, abar_re, abar_im, w_b, w_c, d, w_glu, b_glu)


def _rope_tables(pos):
    half = ROT_DIM // 2
    inv = ROPE_THETA ** (-jnp.arange(half, dtype=F32) / half)
    ang = pos.astype(F32)[:, None] * inv[None, :]
    d = np.arange(LANES) % HEAD_DIM
    idx = d % half
    cos = jnp.where(d < ROT_DIM, jnp.cos(ang)[:, idx], 1.0)
    sin = jnp.sin(ang)[:, idx]
    sm = jnp.where(d < half, -sin, jnp.where(d < ROT_DIM, sin, 0.0))
    return cos, sm


def _keep_diagonal_blocks(w, row_block, col_block):
    rows, cols = w.shape[-2:]
    keep = (np.arange(rows)[:, None] // row_block) == (np.arange(cols)[None, :] // col_block)
    return jnp.where(keep, w, 0.0)


def _expand_weights(bb):
    depth = bb.shape[0]
    hg = N_SSM_GROUPS // 2
    halves = bb.reshape(depth, SSM_GROUP_CH, 2, HALF_STATE).transpose(0, 2, 1, 3)
    tiled = jnp.broadcast_to(halves[:, :, None], (depth, 2, hg, SSM_GROUP_CH, HALF_STATE))
    return _keep_diagonal_blocks(tiled.reshape(depth, 2, hg * SSM_GROUP_CH, HALF_STATE), SSM_GROUP_CH, SSM_STATE)


def _contract_weights(c):
    depth = c.shape[0]
    hg = N_SSM_GROUPS // 2
    rows = c.reshape(depth, 2, hg, SSM_GROUP_CH, SSM_STATE).transpose(0, 1, 2, 4, 3).reshape(
        depth, 2, HALF_STATE, SSM_GROUP_CH)
    return _keep_diagonal_blocks(jnp.tile(rows, (1, 1, 1, hg)), SSM_STATE, SSM_GROUP_CH)


def _prepare_weights(w_in, w_out, attn_sinks, bb_re, bb_im, ssm_c_re, ssm_c_im, t_new):
    depth = w_in.shape[0]
    assert HEAD_PERM == tuple(kv * Q_GROUP + g for g in range(Q_GROUP) for kv in range(N_KV_HEADS))
    w_in_p = w_in.astype(BF16)
    w_out_p = w_out.astype(BF16)
    sinks = jnp.swapaxes(attn_sinks.reshape(depth, N_KV_HEADS, Q_GROUP), 1, 2).reshape(depth, N_Q_HEADS)
    sink_lanes = jnp.broadcast_to(sinks[:, :, None], (depth, N_Q_HEADS, LANES))
    sink_rows = jnp.broadcast_to(jnp.repeat(sinks, t_new, axis=1)[:, :, None], (depth, N_Q_HEADS * t_new, LANES))
    w_b = jnp.concatenate([_expand_weights(bb_re), _expand_weights(bb_im)], axis=3).astype(BF16)
    w_c = jnp.concatenate([_contract_weights(ssm_c_re), -_contract_weights(ssm_c_im)], axis=2).astype(BF16)
    return w_in_p, w_out_p, sink_lanes, sink_rows, w_b, w_c


def kernel(x_prompt, x_sample, cache_k, cache_v, state_ssm_re, state_ssm_im, c_prompt, c_sample, norm_g, w_ada, b_ada, w_in, attn_sinks, ssm_a_re, ssm_a_im, ssm_log_dt, ssm_b_re, ssm_b_im, ssm_c_re, ssm_c_im, ssm_d, w_glu, b_glu, w_out, final_g):
    depth = w_in.shape[0]
    nb_p, seq, _ = x_prompt.shape
    nb_s, t_new, _ = x_sample.shape
    win = cache_k.shape[2]
    blk = WINDOW

    sample = Group(n_batch=nb_s, mod_block=0)
    prompt = Group(n_batch=nb_p, mod_block=nb_s // nb_p)
    mod = adaln_mod(jnp.concatenate([c_sample, c_prompt], axis=0), w_ada, b_ada)
    abar_re, abar_im, bb_re, bb_im = ssm_discretise(ssm_a_re, ssm_a_im, ssm_log_dt, ssm_b_re, ssm_b_im)
    w_in_p, w_out_p, sink_lanes, sink_rows, w_b, w_c = _prepare_weights(
        w_in, w_out, attn_sinks, bb_re, bb_im, ssm_c_re, ssm_c_im, t_new)
    norm_g3 = norm_g.reshape(depth, 1, D_MODEL)
    d3 = ssm_d.reshape(depth, 1, SSM_W)
    w_glu_b = w_glu.astype(BF16)
    b_glu3 = b_glu.reshape(depth, 1, SSM_W)
    final_g2 = final_g.reshape(1, D_MODEL)
    ssm_weights = (abar_re, abar_im, w_b, w_c, d3, w_glu_b, b_glu3)

    cos_p, sm_p = _rope_tables(jnp.arange(seq))
    cos_s, sm_s = _rope_tables(PAST_LEN + jnp.arange(t_new))

    xp = x_prompt
    xs = x_sample.transpose(1, 0, 2).reshape(t_new * nb_s, D_MODEL)
    zeros_state = jnp.zeros((1, nb_p, N_STATE), F32)
    transposed = lambda c: c.transpose(0, 1, 3, 4, 2).reshape(depth, nb_s, KV_W, win)
    cache_kt, cache_vt = transposed(cache_k), transposed(cache_v)
    new_kt, new_vt = jnp.zeros_like(cache_kt), jnp.zeros_like(cache_vt)
    state_re3 = state_ssm_re.transpose(0, 2, 3, 1).reshape(depth, N_STATE, nb_s)
    state_im3 = state_ssm_im.transpose(0, 2, 3, 1).reshape(depth, N_STATE, nb_s)

    outs = {k: [] for k in ("kp", "vp", "rp", "ip", "rs", "is")}
    rows_p, rows_s = BOUNDARY_ROWS, t_new * nb_s
    opening = lambda l, cos_t, sm_t: (norm_g3, w_in_p, cos_t, sm_t, l)
    xp, proj_p = layer_boundary(xp, mod, prompt, rows_p, open_=opening(0, cos_p, sm_p))
    xs, proj_s = layer_boundary(xs, mod, sample, rows_s, open_=opening(0, cos_s, sm_s))
    for l in range(depth):
        last = l == depth - 1

        q, k, v, za, u, zs = proj_p
        ya, ys, hre, him = prompt_branches(q, k, v, za, sink_lanes, u, zs, zeros_state, zeros_state, *ssm_weights,
                                           l, 0, nb_p, blk)
        xp, proj_p = layer_boundary(xp, mod, prompt, rows_p, close=(ya, ys, w_out_p, l),
                                    open_=None if last else opening(l + 1, cos_p, sm_p),
                                    final_g=final_g2, batch_major_out=last)
        outs["kp"].append(k[(seq - win) * nb_p:])
        outs["vp"].append(v[(seq - win) * nb_p:])
        outs["rp"].append(hre)
        outs["ip"].append(him)

        q, k, v, za, u, zs = proj_s
        ya, new_kt, new_vt = attention_sample(q, k, v, za, cache_kt, cache_vt, sink_rows, new_kt, new_vt, l,
                                              nb_s, t_new, SAMPLE_GROUP)
        ys, hre, him = ssm_branch(u, zs, state_re3, state_im3, *ssm_weights, l, l, nb_s, t_new)
        xs, proj_s = layer_boundary(xs, mod, sample, rows_s, close=(ya, ys, w_out_p, l),
                                    open_=None if last else opening(l + 1, cos_s, sm_s), final_g=final_g2)
        outs["rs"].append(hre)
        outs["is"].append(him)

    y_prompt = xp
    y_sample = xs.reshape(t_new, nb_s, D_MODEL).transpose(1, 0, 2)
    st = lambda key: jnp.stack(outs[key])
    window = lambda key: st(key).reshape(depth, win, nb_p, N_KV_HEADS, HEAD_DIM).transpose(0, 2, 1, 3, 4)
    state = lambda key, nb: st(key).reshape(depth, nb, N_SSM_GROUPS, SSM_STATE)
    state_t = lambda key: st(key).reshape(depth, N_SSM_GROUPS, SSM_STATE, nb_s).transpose(0, 3, 1, 2)
    untransposed = lambda w: w.reshape(depth, nb_s, N_KV_HEADS, HEAD_DIM, win).transpose(0, 1, 4, 2, 3)
    return (y_prompt, y_sample, window("kp"), window("vp"), state("rp", nb_p), state("ip", nb_p),
            untransposed(new_kt), untransposed(new_vt), state_t("rs"), state_t("is"))
```
